```python
import math
import jax, jax.numpy as jnp
from jax import lax
import numpy as np

D_MODEL = 1024
BATCH = 8
SEQ = 4096
DEPTH = 1
DEC_BATCH = 32
DEC_SEQ = 1
PAST_LEN = 16384
PAGE_SIZE = 128

DK_A = 128
DV_A = 128
H_A = D_MODEL // 256
CONV_W = 4
CONV_CH = H_A * (2 * DK_A + DV_A)
CHUNK = 64
HD_B = 64
H_B = D_MODEL // 128
N_KV_B = 2
H_IDX = 8
D_IDX = 64
TOPK_MAX = 256
Q_BLOCK = 128
ROPE_THETA = 10000.0
D_FF = 4 * D_MODEL
EPS = 1e-6
POOL_NUM = 5
POOL_DEN = 4

IN_SIZES = (H_A * DK_A, H_A * DK_A, H_A * DV_A, H_A * DV_A, H_A, H_A,
            H_B * HD_B, N_KV_B * HD_B, N_KV_B * HD_B, H_IDX * D_IDX, D_IDX, H_IDX,
            D_MODEL, D_MODEL)
N_IN = sum(IN_SIZES)

kernel_name = 'hybrid_gdn_dsa_parallel_decode_step'


def rms_norm(x, g):
    xf = x.astype(jnp.float32)
    y = xf * lax.rsqrt(jnp.mean(xf * xf, axis=-1, keepdims=True) + EPS)
    return (y * g.astype(jnp.float32)).astype(x.dtype)


def l2_norm(x):
    return x * lax.rsqrt(jnp.sum(x * x, axis=-1, keepdims=True) + EPS)


def rope(x, pos):
    half = x.shape[-1] // 2
    inv = ROPE_THETA ** (-jnp.arange(half, dtype=jnp.float32) / half)
    ang = pos.astype(jnp.float32)[:, None] * inv[None, :]
    cos = jnp.cos(ang)[None, :, None, :]
    sin = jnp.sin(ang)[None, :, None, :]
    xf = x.astype(jnp.float32)
    x1, x2 = xf[..., :half], xf[..., half:]
    return jnp.concatenate([x1 * cos - x2 * sin, x2 * cos + x1 * sin], axis=-1).astype(x.dtype)


def causal_conv(u, prev, conv_w):
    t = u.shape[1]
    up = jnp.concatenate([prev.astype(u.dtype), u], axis=1)
    acc = up[:, 0:t] * conv_w[0]
    for j in range(1, CONV_W):
        acc = acc + up[:, j:j + t] * conv_w[j]
    return jax.nn.silu(acc), up[:, t:]


def gated_delta_chunked(q, k, v, g, beta, s0):
    b, t, h, _ = q.shape
    dv = v.shape[-1]
    nc = t // CHUNK

    def chunks(a):
        return a.reshape(b, nc, CHUNK, *a.shape[2:]).swapaxes(2, 3)

    q, k, v, g, beta = (chunks(a) for a in (q, k, v, g, beta))
    gc = jnp.cumsum(g, axis=-1)
    diff = gc[..., :, None] - gc[..., None, :]
    strict = jnp.tril(jnp.ones((CHUNK, CHUNK), dtype=bool), -1)
    incl = jnp.tril(jnp.ones((CHUNK, CHUNK), dtype=bool))
    dec_strict = jnp.where(strict, jnp.exp(jnp.where(strict, diff, 0.0)), 0.0)
    dec_incl = jnp.where(incl, jnp.exp(jnp.where(incl, diff, 0.0)), 0.0)
    kb = k * beta[..., None]
    a_mat = jnp.einsum('bnhid,bnhjd->bnhij', kb, k) * dec_strict + jnp.eye(CHUNK, dtype=jnp.float32)
    rhs = jnp.concatenate([v * beta[..., None], kb * jnp.exp(gc)[..., None]], axis=-1)
    sol = lax.linalg.triangular_solve(a_mat, rhs, left_side=True, lower=True, unit_diagonal=True)
    u_c, w_c = sol[..., :dv], sol[..., dv:]
    qk = jnp.einsum('bnhid,bnhjd->bnhij', q, k) * dec_incl
    g_last = gc[..., -1:]
    q_dec = q * jnp.exp(gc)[..., None]
    k_dec = k * jnp.exp(g_last - gc)[..., None]
    c_dec = jnp.exp(g_last[..., 0])

    def step(s, inp):
        uu, ww, qd, kd, qkc, cd = inp
        vn = uu - jnp.einsum('bhcd,bhde->bhce', ww, s)
        o = jnp.einsum('bhcd,bhde->bhce', qd, s) + jnp.einsum('bhij,bhje->bhie', qkc, vn)
        s = cd[..., None, None] * s + jnp.einsum('bhcd,bhce->bhde', kd, vn)
        return s, o

    xs = tuple(jnp.moveaxis(a, 1, 0) for a in (u_c, w_c, q_dec, k_dec, qk, c_dec))
    s_fin, o = lax.scan(step, s0.astype(jnp.float32), xs)
    o = o.transpose(1, 0, 3, 2, 4).reshape(b, t, h, dv)
    return o, s_fin


def gated_delta_recurrent(q, k, v, g, beta, s0):
    def step(s, inp):
        qt, kt, vt, gt, bt = inp
        s = s * jnp.exp(gt)[..., None, None]
        vn = bt[..., None] * (vt - jnp.einsum('bhd,bhde->bhe', kt, s))
        s = s + jnp.einsum('bhd,bhe->bhde', kt, vn)
        return s, jnp.einsum('bhd,bhde->bhe', qt, s)

    xs = tuple(a.swapaxes(0, 1) for a in (q, k, v, g, beta))
    s_fin, o = lax.scan(step, s0.astype(jnp.float32), xs)
    return o.swapaxes(0, 1), s_fin


def dsa_attend(q, qi, wi, q_pos, k, v, ki, k_top):
    b, tq = q.shape[:2]
    k_pos = jnp.arange(k.shape[1])
    causal = k_pos[None, :] <= q_pos[:, None]
    s = jnp.einsum('bqhd,bsd->bqhs', qi, ki, preferred_element_type=jnp.float32) * (D_IDX ** -0.5)
    score = jnp.einsum('bqhs,bqh->bqs', jax.nn.relu(s), wi.astype(jnp.float32))
    score = jnp.where(causal[None], score, -jnp.inf)
    _, idx = lax.top_k(score, k_top)
    valid = idx <= q_pos[None, :, None]
    gather = jax.vmap(lambda rows, ii: rows[ii])
    k_sel = gather(k, idx)
    v_sel = gather(v, idx)
    qg = q.reshape(b, tq, N_KV_B, H_B // N_KV_B, HD_B)
    logits = jnp.einsum('bqngd,bqknd->bqngk', qg, k_sel, preferred_element_type=jnp.float32) * (HD_B ** -0.5)
    logits = jnp.where(valid[:, :, None, None, :], logits, -jnp.inf)
    p = jax.nn.softmax(logits, axis=-1)
    o = jnp.einsum('bqngk,bqknd->bqngd', p.astype(v.dtype), v_sel)
    return o.reshape(b, tq, H_B * HD_B)


def hybrid_layer(x, pos, ssm0, conv0, past_kv, past_ki, norm1_g, w_in, conv_w, a_log, dt_bias,
                 gdn_norm_g, q_norm_g, k_norm_g, w_out_a, w_out_b, w_o, norm2_g, w_up, w_down):
    b, t, _ = x.shape
    f32 = jnp.float32
    xn = rms_norm(x, norm1_g)
    split_pts = np.cumsum(IN_SIZES)[:-1].tolist()
    (aq, ak, av, az, ab, aa, bq, bk, bv, iq, ik, iw, ga, gb) = jnp.split(xn @ w_in, split_pts, axis=-1)

    conv_out, conv_new = causal_conv(jnp.concatenate([aq, ak, av], axis=-1), conv0, conv_w)
    cq, ck, cv = jnp.split(conv_out.astype(f32), [H_A * DK_A, 2 * H_A * DK_A], axis=-1)
    qa = l2_norm(cq.reshape(b, t, H_A, DK_A)) * (DK_A ** -0.5)
    ka = l2_norm(ck.reshape(b, t, H_A, DK_A))
    va = cv.reshape(b, t, H_A, DV_A)
    beta = jax.nn.sigmoid(ab.astype(f32))
    g = -jnp.exp(a_log.astype(f32)) * jax.nn.softplus(aa.astype(f32) + dt_bias.astype(f32))
    if past_kv is None:
        oa, ssm_new = gated_delta_chunked(qa, ka, va, g, beta, ssm0)
    else:
        oa, ssm_new = gated_delta_recurrent(qa, ka, va, g, beta, ssm0)
    oa = rms_norm(oa, gdn_norm_g) * jax.nn.silu(az.reshape(b, t, H_A, DV_A).astype(f32))
    oa = oa.astype(x.dtype).reshape(b, t, H_A * DV_A)

    qb = rope(rms_norm(bq.reshape(b, t, H_B, HD_B), q_norm_g), pos)
    kb = rope(rms_norm(bk.reshape(b, t, N_KV_B, HD_B), k_norm_g), pos)
    vb = bv.reshape(b, t, N_KV_B, HD_B)
    qi = rope(iq.reshape(b, t, H_IDX, D_IDX), pos)
    ki = rope(ik.reshape(b, t, 1, D_IDX), pos)[:, :, 0]
    wi = iw * (H_IDX ** -0.5)
    kv_new = jnp.stack([kb, vb], axis=2)
    if past_kv is None:
        k_top = min(TOPK_MAX, t // 4)
        nb = t // Q_BLOCK

        def blk(a):
            return a.reshape(b, nb, Q_BLOCK, *a.shape[2:]).swapaxes(0, 1)

        ob = lax.map(lambda a: dsa_attend(a[0], a[1], a[2], a[3], kb, vb, ki, k_top),
                     (blk(qb), blk(qi), blk(wi), pos.reshape(nb, Q_BLOCK)))
        ob = ob.swapaxes(0, 1).reshape(b, t, H_B * HD_B)
    else:
        k_all = jnp.concatenate([past_kv[:, :, 0].astype(kb.dtype), kb], axis=1)
        v_all = jnp.concatenate([past_kv[:, :, 1].astype(vb.dtype), vb], axis=1)
        ki_all = jnp.concatenate([past_ki.astype(ki.dtype), ki], axis=1)
        k_top = min(TOPK_MAX, k_all.shape[1] // 4)
        ob = dsa_attend(qb, qi, wi, pos, k_all, v_all, ki_all, k_top)

    mix = jax.nn.sigmoid(ga) * (oa @ w_out_a) + jax.nn.sigmoid(gb) * (ob @ w_out_b)
    h = x + mix @ w_o
    hn = rms_norm(h, norm2_g)
    y = h + jnp.square(jax.nn.relu(hn @ w_up)) @ w_down
    return y, kv_new, ki, ssm_new.astype(ssm0.dtype), conv_new


def setup_inputs(seed: int = 0) -> dict:
    key = jax.random.key(seed)
    ks = jax.random.split(key, 24)
    f32 = jnp.float32
    n_pages = PAST_LEN // PAGE_SIZE
    n_used = DEC_BATCH * n_pages
    n_pool = (n_used * POOL_NUM) // POOL_DEN

    def nrm(k, shape, scale):
        return jax.random.normal(k, shape, f32) * scale

    def gain(k, shape):
        return 1.0 + 0.02 * jax.random.normal(k, shape, f32)

    page_table = jax.random.permutation(ks[4], n_pool)[:n_used].reshape(DEC_BATCH, n_pages).astype(jnp.int32)
    dt = jnp.exp(jax.random.uniform(ks[10], (DEPTH, H_A), f32, math.log(1e-3), math.log(1e-1)))
    dt_bias = dt + jnp.log(-jnp.expm1(-dt))
    a_log = jnp.log(jax.random.uniform(ks[9], (DEPTH, H_A), f32, 1.0, 16.0))
    return {
        'x_prompt': nrm(ks[0], (BATCH, SEQ, D_MODEL), 1.0),
        'x_sample': nrm(ks[1], (DEC_BATCH, DEC_SEQ, D_MODEL), 1.0),
        'cache_kv': nrm(ks[2], (DEPTH, n_pool, PAGE_SIZE, 2, N_KV_B, HD_B), 1.0),
        'cache_kidx': nrm(ks[3], (DEPTH, n_pool, PAGE_SIZE, D_IDX), 1.0),
        'page_table': page_table,
        'state_ssm': nrm(ks[5], (DEPTH, DEC_BATCH, H_A, DK_A, DV_A), 0.3),
        'state_conv': nrm(ks[6], (DEPTH, DEC_BATCH, CONV_W - 1, CONV_CH), 1.0),
        'norm1_g': gain(ks[7], (DEPTH, D_MODEL)),
        'w_in': nrm(ks[8], (DEPTH, D_MODEL, N_IN), D_MODEL ** -0.5),
        'conv_w': nrm(ks[11], (DEPTH, CONV_W, CONV_CH), CONV_W ** -0.5),
        'a_log': a_log,
        'dt_bias': dt_bias,
        'gdn_norm_g': gain(ks[12], (DEPTH, DV_A)),
        'q_norm_g': gain(ks[13], (DEPTH, HD_B)),
        'k_norm_g': gain(ks[14], (DEPTH, HD_B)),
        'w_out_a': nrm(ks[15], (DEPTH, H_A * DV_A, D_MODEL), (H_A * DV_A) ** -0.5),
        'w_out_b': nrm(ks[16], (DEPTH, H_B * HD_B, D_MODEL), (H_B * HD_B) ** -0.5),
        'w_o': nrm(ks[17], (DEPTH, D_MODEL, D_MODEL), D_MODEL ** -0.5),
        'norm2_g': gain(ks[18], (DEPTH, D_MODEL)),
        'w_up': nrm(ks[19], (DEPTH, D_MODEL, D_FF), D_MODEL ** -0.5),
        'w_down': nrm(ks[20], (DEPTH, D_FF, D_MODEL), D_FF ** -0.5),
    }


def reference(x_prompt, x_sample, cache_kv, cache_kidx, page_table, state_ssm, state_conv,
              norm1_g, w_in, conv_w, a_log, dt_bias, gdn_norm_g, q_norm_g, k_norm_g,
              w_out_a, w_out_b, w_o, norm2_g, w_up, w_down):
    b_p, t_p, _ = x_prompt.shape
    b_s, t_s, _ = x_sample.shape
    n_pages = page_table.shape[1]
    past_len = n_pages * PAGE_SIZE
    pos_p = jnp.arange(t_p)
    pos_s = past_len + jnp.arange(t_s)
    hp, hs = x_prompt, x_sample
    outs_p, outs_s = [], []
    for l in range(DEPTH):
        w = (norm1_g[l], w_in[l], conv_w[l], a_log[l], dt_bias[l], gdn_norm_g[l], q_norm_g[l],
             k_norm_g[l], w_out_a[l], w_out_b[l], w_o[l], norm2_g[l], w_up[l], w_down[l])
        ssm0_p = jnp.zeros((b_p, H_A, DK_A, DV_A), x_prompt.dtype)
        conv0_p = jnp.zeros((b_p, CONV_W - 1, CONV_CH), x_prompt.dtype)
        hp, kv_p, ki_p, ssm_p, cv_p = hybrid_layer(hp, pos_p, ssm0_p, conv0_p, None, None, *w)
        past_kv = cache_kv[l][page_table].reshape(b_s, past_len, 2, N_KV_B, HD_B)
        past_ki = cache_kidx[l][page_table].reshape(b_s, past_len, D_IDX)
        hs, kv_s, ki_s, ssm_s, cv_s = hybrid_layer(hs, pos_s, state_ssm[l], state_conv[l], past_kv, past_ki, *w)
        outs_p.append((kv_p, ki_p, ssm_p, cv_p))
        outs_s.append((kv_s, ki_s, ssm_s, cv_s))
    kv_prompt, kidx_prompt, ssm_prompt, conv_prompt = (jnp.stack(a) for a in zip(*outs_p))
    kv_sample, kidx_sample, ssm_sample, conv_sample = (jnp.stack(a) for a in zip(*outs_s))
    return (hp, hs, kv_prompt, kidx_prompt, ssm_prompt, conv_prompt, kv_sample, kidx_sample, ssm_sample, conv_sample)
```

```python
import functools
import math

import jax
import jax.numpy as jnp
import numpy as np
from jax import lax
from jax.experimental import pallas as pl
from jax.experimental.pallas import tpu as pltpu

F32 = jnp.float32
BF16 = jnp.bfloat16
I32 = jnp.int32

D_MODEL = 1024
H_A, DK_A, DV_A = 4, 128, 128
CONV_W = 4
CONV_CH = H_A * (2 * DK_A + DV_A)
CHUNK = 64
HD_B, H_B, N_KV_B = 64, 8, 2
H_IDX, D_IDX = 8, 64
TOPK_MAX = 256
PAGE_SIZE = 128
ROPE_THETA = 10000.0
D_FF = 4 * D_MODEL
EPS = 1e-6

_OFF = dict(aq=0, ak=512, av=1024, az=1536, ab=2048, aa=2052, bq=2056, bk=2568, bv=2696,
            iq=2824, ik=3336, iw=3400, ga=3408, gb=4432, end=5456)
_SM_BETA, _SM_G, _SM_WI = 64, 68, 72

NEG_BIG = -1e30
INT_MIN = -(2 ** 31)
NEG_INF_KEY = (0xFF800000 ^ 0x7FFFFFFF) - 2 ** 32

VMEM_LIMIT = 56 * 1024 * 1024
HIGHEST = lax.Precision.HIGHEST


def _cparams(n_axes):
    return pltpu.CompilerParams(dimension_semantics=("arbitrary",) * n_axes,
                                vmem_limit_bytes=VMEM_LIMIT)


def _const_spec(shape):
    nd = len(shape)
    return pl.BlockSpec(shape, lambda *a: (0,) * nd, pipeline_mode=pl.Buffered(1))


def _mm(a, b):
    return jnp.dot(a.astype(BF16), b.astype(BF16), preferred_element_type=F32)


def _mm_nt(a, b):
    return lax.dot_general(a.astype(BF16), b.astype(BF16), (((1,), (1,)), ((), ())),
                           preferred_element_type=F32)


def _mm_tn(a, b):
    return lax.dot_general(a.astype(BF16), b.astype(BF16), (((0,), (0,)), ((), ())),
                           preferred_element_type=F32)


def _mm_f32(a, b):
    return jnp.dot(a, b, preferred_element_type=F32, precision=HIGHEST)


def _rms_rows(x, g):
    return x * lax.rsqrt(jnp.mean(x * x, axis=-1, keepdims=True) + EPS) * g


def _silu(x):
    return x * jax.nn.sigmoid(x)


def _softplus(x):
    return jnp.maximum(x, 0.0) + jnp.log1p(jnp.exp(-jnp.abs(x)))


def _rope_rows(x, cos, sin_signed):
    n = x.shape[-1]
    reps = n // 128
    if reps > 1:
        cos = jnp.concatenate([cos] * reps, axis=-1)
        sin_signed = jnp.concatenate([sin_signed] * reps, axis=-1)
    lane = lax.broadcasted_iota(I32, x.shape, x.ndim - 1)
    first_half = (lane % HD_B) < (HD_B // 2)
    partner = jnp.where(first_half, pltpu.roll(x, n - HD_B // 2, x.ndim - 1),
                        pltpu.roll(x, HD_B // 2, x.ndim - 1))
    return x * cos + partner * sin_signed


def _rope_cols(x, cos_t, sin_t):
    x1, x2 = x[:32], x[32:]
    return jnp.concatenate([x1 * cos_t - x2 * sin_t, x2 * cos_t + x1 * sin_t], axis=0)


def _seg_rms(x, ones_bd, g):
    sq = x * x
    hi = sq.astype(BF16)
    lo = (sq - hi.astype(F32)).astype(BF16)
    ss = (jnp.dot(hi, ones_bd, preferred_element_type=F32)
          + jnp.dot(lo, ones_bd, preferred_element_type=F32))
    return x * lax.rsqrt(ss * (1.0 / HD_B) + EPS) * g


def _small_block(sm, cos, sin_signed, alog_row, dtb_row):
    lane = lax.broadcasted_iota(I32, sm.shape, 1)
    roped = _rope_rows(sm, cos, sin_signed)
    beta = jax.nn.sigmoid(sm)
    g = -jnp.exp(alog_row) * _softplus(sm + dtb_row)
    wi = sm * (H_IDX ** -0.5)
    out = jnp.where(lane < _SM_BETA, roped,
                    jnp.where(lane < _SM_G, beta, jnp.where(lane < _SM_WI, g, wi)))
    return out


def _proj_prompt_body(x_ref, g1_ref, wqkv_ref, wz_ref, wbq_ref, wkv_ref, wiq_ref, wsm_ref,
                      wkt_ref, wsmt_ref, convw_ref, qng_ref, kng_ref, kngt_ref, alog_ref, dtb_ref,
                      alogt_ref, dtbt_ref, ones_ref, cos_ref, sin_ref, cost_ref, sint_ref,
                      q_out, k_out, v_out, z_out, qb_out, kv_out, vbf_out, qi_out, sm_out, kidx_out,
                      ktp_out, kitp_out, gt_out, conv_out, ubuf, *, tm, kc):
    t = pl.program_id(1)
    x = x_ref[0]
    xn = _rms_rows(x, g1_ref[...]).astype(BF16)

    u = jnp.dot(xn, wqkv_ref[...], preferred_element_type=F32)

    @pl.when(t == 0)
    def _():
        ubuf[0:8, :] = jnp.zeros((8, CONV_CH), F32)

    ubuf[8:tm + 8, :] = u
    cw = convw_ref[...]
    acc = (ubuf[5:5 + tm, :] * cw[0:1] + ubuf[6:6 + tm, :] * cw[1:2]
           + ubuf[7:7 + tm, :] * cw[2:3] + u * cw[3:4])
    conv_out[0] = ubuf[tm + 5:tm + 8, :]
    ubuf[0:8, :] = ubuf[tm:tm + 8, :]
    c = _silu(acc)
    for h in range(H_A):
        qh = c[:, h * DK_A:(h + 1) * DK_A]
        kh = c[:, (H_A + h) * DK_A:(H_A + h + 1) * DK_A]
        q_out[:, h * DK_A:(h + 1) * DK_A] = qh * (lax.rsqrt(jnp.sum(qh * qh, -1, keepdims=True) + EPS)
                                                 * (DK_A ** -0.5))
        k_out[:, h * DK_A:(h + 1) * DK_A] = kh * lax.rsqrt(jnp.sum(kh * kh, -1, keepdims=True) + EPS)
    v_out[...] = c[:, 2 * H_A * DK_A:]
    z_out[...] = jnp.dot(xn, wz_ref[...], preferred_element_type=F32)

    cos, sin = cos_ref[...], sin_ref[...]
    ones_bd = ones_ref[...]

    bq = jnp.dot(xn, wbq_ref[...], preferred_element_type=F32)
    qb = _rope_rows(_seg_rms(bq, ones_bd, qng_ref[...]), cos, sin)
    qb_out[...] = (qb * (HD_B ** -0.5)).astype(BF16)
    kv = jnp.dot(xn, wkv_ref[...], preferred_element_type=F32)
    kb = _rope_rows(_seg_rms(kv[:, :128], ones_bd[:128, :128], kng_ref[...]), cos, sin)
    kv_out[:, 0:128] = kb
    kv_out[:, 128:256] = kv[:, 128:]
    vbf_out[...] = kv[:, 128:].astype(BF16)

    iq = jnp.dot(xn, wiq_ref[...], preferred_element_type=F32)
    qi_out[...] = (_rope_rows(iq, cos, sin) * (D_IDX ** -0.5)).astype(BF16)
    sm = jnp.dot(xn, wsm_ref[...], preferred_element_type=F32)
    smo = _small_block(sm, cos, sin, alog_ref[...], dtb_ref[...])
    sm_out[...] = smo
    kidx_out[...] = smo[:, :D_IDX]

    cos_t, sin_t = cost_ref[...], sint_ref[...]
    kt = lax.dot_general(wkt_ref[...], xn, (((1,), (1,)), ((), ())), preferred_element_type=F32)
    smt = lax.dot_general(wsmt_ref[...], xn, (((1,), (1,)), ((), ())), preferred_element_type=F32)
    kngt = kngt_ref[...]
    zero64 = jnp.zeros((64, kc), BF16)
    kts = []
    for n in range(N_KV_B):
        kn = kt[64 * n:64 * (n + 1)]
        kn = kn * lax.rsqrt(jnp.mean(kn * kn, axis=0, keepdims=True) + EPS) * kngt
        kts.append(_rope_cols(kn, cos_t, sin_t).astype(BF16))
    kit = _rope_cols(smt[:64], cos_t, sin_t).astype(BF16)
    for j in range(tm // kc):
        sl = slice(j * kc, (j + 1) * kc)
        for n in range(N_KV_B):
            ktp_out[0, j, 256 * n:256 * n + 64, :] = kts[n][:, sl]
            ktp_out[0, j, 256 * n + 64:256 * n + 128, :] = zero64
            ktp_out[0, j, 256 * n + 128:256 * n + 192, :] = zero64
            ktp_out[0, j, 256 * n + 192:256 * n + 256, :] = kts[n][:, sl]
        kitp_out[0, j, 0:64, :] = kit[:, sl]
        kitp_out[0, j, 64:128, :] = zero64
        kitp_out[0, j, 128:192, :] = zero64
        kitp_out[0, j, 192:256, :] = kit[:, sl]
    sm8 = smt[64:72]
    row = lax.broadcasted_iota(I32, sm8.shape, 0)
    g8 = -jnp.exp(alogt_ref[...]) * _softplus(sm8 + dtbt_ref[...])
    gt_out[0] = jnp.where(row < 4, jax.nn.sigmoid(sm8), g8)


def _proj_prompt(x, w, tabs, *, tm, kc):
    b, t, d = x.shape
    nt = t // tm
    m = b * t
    row = lambda n: pl.BlockSpec((tm, n), lambda i, j: (i * nt + j, 0))
    ins = [
        pl.BlockSpec((1, tm, d), lambda i, j: (i, j, 0)),
        _const_spec((1, d)),
        _const_spec((d, CONV_CH)), _const_spec((d, 512)), _const_spec((d, 512)), _const_spec((d, 256)),
        _const_spec((d, 512)), _const_spec((d, 128)), _const_spec((128, d)), _const_spec((128, d)),
        _const_spec((CONV_W, CONV_CH)), _const_spec((1, 512)), _const_spec((1, 128)), _const_spec((64, 1)),
        _const_spec((1, 128)), _const_spec((1, 128)), _const_spec((8, 1)), _const_spec((8, 1)),
        _const_spec((512, 512)),
        pl.BlockSpec((tm, 128), lambda i, j: (j, 0)), pl.BlockSpec((tm, 128), lambda i, j: (j, 0)),
        pl.BlockSpec((32, tm), lambda i, j: (0, j)), pl.BlockSpec((32, tm), lambda i, j: (0, j)),
    ]
    nkc = t // kc
    outs = [
        (jax.ShapeDtypeStruct((m, 512), F32), row(512)),
        (jax.ShapeDtypeStruct((m, 512), F32), row(512)),
        (jax.ShapeDtypeStruct((m, 512), F32), row(512)),
        (jax.ShapeDtypeStruct((m, 512), F32), row(512)),
        (jax.ShapeDtypeStruct((m, 512), BF16), row(512)),
        (jax.ShapeDtypeStruct((m, 256), F32), row(256)),
        (jax.ShapeDtypeStruct((m, 128), BF16), row(128)),
        (jax.ShapeDtypeStruct((m, 512), BF16), row(512)),
        (jax.ShapeDtypeStruct((m, 128), F32), row(128)),
        (jax.ShapeDtypeStruct((m, D_IDX), F32), row(D_IDX)),
        (jax.ShapeDtypeStruct((b, nkc, 512, kc), BF16),
         pl.BlockSpec((1, tm // kc, 512, kc), lambda i, j: (i, j, 0, 0))),
        (jax.ShapeDtypeStruct((b, nkc, 256, kc), BF16),
         pl.BlockSpec((1, tm // kc, 256, kc), lambda i, j: (i, j, 0, 0))),
        (jax.ShapeDtypeStruct((b, 8, t), F32), pl.BlockSpec((1, 8, tm), lambda i, j: (i, 0, j))),
        (jax.ShapeDtypeStruct((b, CONV_W - 1, CONV_CH), F32),
         pl.BlockSpec((1, CONV_W - 1, CONV_CH), lambda i, j: (i, 0, 0))),
    ]
    return pl.pallas_call(
        functools.partial(_proj_prompt_body, tm=tm, kc=kc),
        out_shape=[o[0] for o in outs],
        grid=(b, nt),
        in_specs=ins,
        out_specs=[o[1] for o in outs],
        scratch_shapes=[pltpu.VMEM((tm + 8, CONV_CH), F32)],
        compiler_params=_cparams(2),
        name="proj_prompt",
    )(x, w["g1"], w["wqkv"], w["wz"], w["wbq"], w["wkv"], w["wiq"], w["wsm"], w["wkt"], w["wsmt"],
      w["convw"], w["qng"], w["kng"], w["kngt"], w["alog_row"], w["dtb_row"], w["alog_col"], w["dtb_col"],
      w["ones_bd"], tabs["cos"], tabs["sin"], tabs["cos_t"], tabs["sin_t"])


def _split3(x):
    hi = x.astype(BF16)
    r1 = x - hi.astype(F32)
    mid = r1.astype(BF16)
    lo = (r1 - mid.astype(F32)).astype(BF16)
    return hi, mid, lo


def _gdn_body(q_ref, k_ref, v_ref, z_ref, sm_ref, gt_ref, gg_ref, bdk_ref, bdp_ref, o_ref, s_ref, *, nb):
    c = pl.program_id(1)

    @pl.when(c == 0)
    def _():
        s_ref[...] = jnp.zeros_like(s_ref)

    n, nh = CHUNK, H_A
    w4 = nh * n
    ri4 = lax.broadcasted_iota(I32, (n, w4), 0)
    ci4 = lax.broadcasted_iota(I32, (n, w4), 1) % n
    strict4, incl4 = ri4 > ci4, ri4 >= ci4
    eye4 = jnp.where(ri4 == ci4, 1.0, 0.0)
    triu4 = jnp.where(ri4 <= ci4, 1.0, 0.0).astype(BF16)
    ri = lax.broadcasted_iota(I32, (n, n), 0)
    tril = jnp.where(ri >= lax.broadcasted_iota(I32, (n, n), 1), 1.0, 0.0).astype(BF16)
    blk_row = lax.broadcasted_iota(I32, (1, w4), 1) // n
    low_half = lax.broadcasted_iota(I32, (n, 128), 1) < n
    zeros_pad = jnp.zeros((n, 2 * DV_A), BF16)
    gg = gg_ref[...]
    dot = functools.partial(jnp.dot, preferred_element_type=F32)

    def pad_rows(x, e):
        z = zeros_pad[:, :x.shape[1]]
        return jnp.concatenate([x, z] if e == 0 else [z, x], axis=0)

    seqs = range(nb)
    st = [dict() for _ in seqs]

    for bi in seqs:
        hi, mid, lo = _split3(sm_ref[bi])
        st[bi]["gc_all"] = dot(tril, hi) + dot(tril, mid) + dot(tril, lo)
        hi, mid, lo = _split3(gt_ref[bi, 0])
        st[bi]["grows"] = dot(hi, triu4) + dot(mid, triu4) + dot(lo, triu4)

    for bi in seqs:
        d = st[bi]
        sm, k, grows = sm_ref[bi], k_ref[bi], d["grows"]
        grow = jnp.where(blk_row == 0, grows[4:5], jnp.where(blk_row == 1, grows[5:6],
                         jnp.where(blk_row == 2, grows[6:7], grows[7:8])))
        gcols = [jnp.broadcast_to(d["gc_all"][:, _SM_G + h:_SM_G + h + 1], (n, 128)) for h in range(nh)]
        betas = [jnp.broadcast_to(sm[:, _SM_BETA + h:_SM_BETA + h + 1], (n, 128)) for h in range(nh)]
        d["gc512"] = jnp.concatenate(gcols, axis=1)
        d["beta512"] = jnp.concatenate(betas, axis=1)
        gc256 = jnp.concatenate([jnp.where(low_half, gcols[0], gcols[1]),
                                 jnp.where(low_half, gcols[2], gcols[3])], axis=1)
        d["dec"] = jnp.exp(jnp.minimum(gc256 - grow, 0.0))
        d["kb"] = k * d["beta512"]
        kbd = jnp.concatenate([k.astype(BF16)] * nh, axis=0) * bdk_ref[...]
        d["aq"] = lax.dot_general(jnp.concatenate([d["kb"], q_ref[bi]], axis=0).astype(BF16), kbd,
                                  (((1,), (1,)), ((), ())), preferred_element_type=F32)

    for bi in seqs:
        d = st[bi]
        x = jnp.where(strict4, -d["aq"][:n] * d["dec"], 0.0)
        d["qkm"] = jnp.where(incl4, d["aq"][n:] * d["dec"], 0.0).astype(BF16)
        d["inv"], d["p"] = eye4 + x, x
    for lvl in range(6):
        for bi in seqs:
            d = st[bi]
            p, inv = d["p"], d["inv"]
            wbd = jnp.concatenate([p.astype(BF16)] * nh, axis=0) * bdp_ref[...]
            if lvl == 0:
                d["p"] = dot(p.astype(BF16), wbd)
            elif lvl < 5:
                r = dot(jnp.concatenate([p, inv], axis=0).astype(BF16), wbd)
                d["p"], d["inv"] = r[:n], inv + r[n:]
            else:
                d["inv"] = inv + dot(inv.astype(BF16), wbd)

    heads = [(bi, h) for bi in seqs for h in range(nh)]
    hsl = lambda h: slice(h * DK_A, (h + 1) * DK_A)
    psl = lambda h: slice(128 * (h // 2), 128 * (h // 2 + 1))
    for bi in seqs:
        d = st[bi]
        k, gc512 = k_ref[bi], d["gc512"]
        d["tinv"] = d["inv"].astype(BF16)
        egc = jnp.exp(gc512)
        d["rhs_v"] = (v_ref[bi] * d["beta512"]).astype(BF16)
        d["rhs_k"] = (d["kb"] * egc).astype(BF16)
        g_last = gc512[n - 1:n, :]
        d["q_dec"] = q_ref[bi] * egc
        d["k_dec"] = (k * jnp.exp(g_last - gc512)).astype(BF16)
        d["c_dec"] = jnp.exp(g_last)
    sol, ws, vn = {}, {}, {}
    for bi, h in heads:
        d = st[bi]
        rhs = pad_rows(jnp.concatenate([d["rhs_v"][:, hsl(h)], d["rhs_k"][:, hsl(h)]], axis=1), h % 2)
        sol[bi, h] = dot(d["tinv"][:, psl(h)], rhs)
    for bi, h in heads:
        lhs = jnp.concatenate([sol[bi, h][:, DV_A:], st[bi]["q_dec"][:, hsl(h)]], axis=0)
        ws[bi, h] = dot(lhs.astype(BF16), s_ref[bi, h].astype(BF16))
    for bi, h in heads:
        d = st[bi]
        vn[bi, h] = (sol[bi, h][:, :DV_A] - ws[bi, h][:n]).astype(BF16)
        s_ref[bi, h] = d["c_dec"][:, hsl(h)] * s_ref[bi, h] + lax.dot_general(
            d["k_dec"][:, hsl(h)], vn[bi, h], (((0,), (0,)), ((), ())), preferred_element_type=F32)
    for bi, h in heads:
        o = ws[bi, h][n:] + dot(st[bi]["qkm"][:, psl(h)], pad_rows(vn[bi, h], h % 2))
        o_ref[bi, :, hsl(h)] = (_rms_rows(o, gg) * _silu(z_ref[bi, :, hsl(h)])).astype(BF16)


def _gdn_prompt(q, k, v, z, sm, gt, gg, *, b, t, nb):
    nc = t // CHUNK
    r3 = lambda a, n: a.reshape(b, t, n)
    blk = lambda n: pl.BlockSpec((nb, CHUNK, n), lambda i, j: (i, j, 0))
    w4 = H_A * CHUNK
    bdk = np.kron(np.eye(H_A, dtype=np.float32), np.ones((CHUNK, DK_A), np.float32))
    bdp = np.kron(np.eye(H_A, dtype=np.float32), np.ones((CHUNK, CHUNK), np.float32))
    return pl.pallas_call(
        functools.partial(_gdn_body, nb=nb),
        out_shape=[jax.ShapeDtypeStruct((b, t, 512), BF16),
                   jax.ShapeDtypeStruct((b, H_A, DK_A, DV_A), F32)],
        grid=(b // nb, nc),
        in_specs=[blk(512), blk(512), blk(512), blk(512), blk(128),
                  pl.BlockSpec((nb, 1, 8, CHUNK), lambda i, j: (i, j, 0, 0)),
                  _const_spec((1, DV_A)), _const_spec((w4, H_A * DK_A)), _const_spec((w4, w4))],
        out_specs=[blk(512), pl.BlockSpec((nb, H_A, DK_A, DV_A), lambda i, j: (i, 0, 0, 0))],
        compiler_params=_cparams(2),
        name="gdn_prompt",
    )(r3(q, 512), r3(k, 512), r3(v, 512), r3(z, 512), r3(sm, 128),
      jnp.swapaxes(gt.reshape(b, 8, nc, CHUNK), 1, 2), gg, jnp.asarray(bdk, BF16), jnp.asarray(bdp, BF16))


def _ordinal_to_float(k):
    k = jnp.maximum(k, NEG_INF_KEY)
    return pltpu.bitcast(k ^ ((k >> 31) & 0x7FFFFFFF), F32)


def _topk_threshold(count_ge, count_gt_eq, count_eq_below, shape, k_top, idx_bits):
    def bit_step(i, o):
        cand = o + jnp.left_shift(jnp.int32(1), 31 - i)
        return jnp.where(count_ge(_ordinal_to_float(cand)) >= k_top, cand, o)

    o = lax.fori_loop(0, 32, bit_step, jnp.full(shape, INT_MIN, I32))
    t = _ordinal_to_float(o)
    c_gt, c_eq = count_gt_eq(t)
    need = k_top - c_gt
    untied = jnp.logical_or(c_eq == need, o <= NEG_INF_KEY)
    all_untied = jnp.min(jnp.where(untied, 1.0, 0.0)) > 0.5

    def tie_search(_):
        def idx_step(i, x):
            cand = x + jnp.left_shift(jnp.int32(1), idx_bits - 1 - i)
            return jnp.where(count_eq_below(t, cand) < need, cand, x)
        return lax.fori_loop(0, idx_bits, idx_step, jnp.zeros(shape, I32))

    x = lax.cond(all_untied, lambda _: jnp.full(shape, 2 ** 30, I32), tie_search, 0)
    return t, x


def _dsa_body(qi_ref, qb_ref, sm_ref, kitp_ref, ktp_ref, v_ref, o_ref,
              key_ref, bias_ref, wib_ref, m_ref, acc_ref, *, tq, kc, rg, k_top):
    i = pl.program_id(1)
    nkc = i + 1

    sm = sm_ref[...]
    for h in range(H_IDX):
        wib_ref[h] = jnp.broadcast_to(sm[:, _SM_WI + h:_SM_WI + h + 1], (tq, 128))

    def score_chunk(c, carry):
        kit = kitp_ref[0, c]
        halves = []
        for half in range(kc // 128):
            halves.append(jnp.zeros((tq, 128), F32))
        for p in range(H_IDX // 2):
            qp = qi_ref[:, 128 * p:128 * (p + 1)]
            for e in range(2):
                s = jnp.dot(qp, kit[128 * e:128 * (e + 1)], preferred_element_type=F32)
                w = wib_ref[2 * p + e]
                for half in range(kc // 128):
                    halves[half] = halves[half] + jnp.maximum(s[:, 128 * half:128 * (half + 1)], 0.0) * w
        score = jnp.concatenate(halves, axis=-1)
        rowi = lax.broadcasted_iota(I32, (tq, kc), 0)
        coli = lax.broadcasted_iota(I32, (tq, kc), 1)
        causal = coli <= rowi + (i - c) * kc
        key_ref[c] = jnp.where(causal, score, -jnp.inf)
        return carry

    lax.fori_loop(0, nkc, score_chunk, 0)

    for r in range(tq // rg):
        rs = slice(r * rg, (r + 1) * rg)

        def lane_sum(f):
            def body(c, acc):
                kk = key_ref[c, rs, :]
                for half in range(kc // 128):
                    acc = acc + f(kk[:, 128 * half:128 * (half + 1)], c * kc + 128 * half)
                return acc
            acc = lax.fori_loop(0, nkc, body, jnp.zeros((rg, 128), F32))
            return jnp.sum(acc, axis=-1, keepdims=True).astype(I32)

        def count_ge(cand):
            cb = jnp.broadcast_to(cand, (rg, 128))
            return lane_sum(lambda kk, off: jnp.where(kk >= cb, 1.0, 0.0))

        def count_gt_eq(t):
            tb = jnp.broadcast_to(t, (rg, 128))
            return (lane_sum(lambda kk, off: jnp.where(kk > tb, 1.0, 0.0)),
                    lane_sum(lambda kk, off: jnp.where(kk == tb, 1.0, 0.0)))

        def count_eq_below(t, j):
            tb = jnp.broadcast_to(t, (rg, 128))
            jb = jnp.broadcast_to(j, (rg, 128))
            lane = lax.broadcasted_iota(I32, (rg, 128), 1)
            return lane_sum(lambda kk, off: jnp.where(kk == tb, jnp.where(lane + off < jb, 1.0, 0.0), 0.0))

        t, x = _topk_threshold(count_ge, count_gt_eq, count_eq_below, (rg, 1), k_top, 12)
        tb = jnp.broadcast_to(t, (rg, kc))
        xb = jnp.broadcast_to(x, (rg, kc))

        def bias_chunk(c, carry):
            kk = key_ref[c, rs, :]
            idx = lax.broadcasted_iota(I32, (rg, kc), 1) + c * kc
            tie = jnp.where(kk == tb, jnp.where(idx <= xb, 0.0, NEG_BIG), NEG_BIG)
            bias = jnp.where(kk > tb, 0.0, tie)
            bias_ref[c, rs, :] = jnp.where(kk > -jnp.inf, bias, NEG_BIG)
            return carry

        lax.fori_loop(0, nkc, bias_chunk, 0)

    m_ref[...] = jnp.full(m_ref.shape, NEG_BIG, F32)
    acc_ref[...] = jnp.zeros(acc_ref.shape, F32)

    def attn_chunk(c, carry):
        ktp = ktp_ref[0, c]
        bias = bias_ref[c]
        vv = v_ref[0, c]
        lane = lax.broadcasted_iota(I32, (kc, 128), 1)
        vf = vv.astype(F32)
        vaug = [jnp.where(lane < HD_B, vf, 1.0).astype(BF16),
                jnp.where(lane < HD_B, pltpu.roll(vf, HD_B, 1), 1.0).astype(BF16)]
        for h in range(H_B):
            n, p, e = h // 4, h // 2, h % 2
            qp = qb_ref[:, 128 * p:128 * (p + 1)]
            s = jnp.dot(qp, ktp[256 * n + 128 * e:256 * n + 128 * (e + 1)],
                        preferred_element_type=F32) + bias
            m_old = m_ref[h]
            m_new = jnp.maximum(m_old, jnp.max(s, axis=-1, keepdims=True))
            alpha = jnp.exp(m_old - m_new)
            pexp = jnp.exp(s - m_new).astype(BF16)
            acc_ref[h] = acc_ref[h] * alpha + jnp.dot(pexp, vaug[n], preferred_element_type=F32)
            m_ref[h] = m_new
        return carry

    lax.fori_loop(0, nkc, attn_chunk, 0)
    for h in range(H_B):
        acc = acc_ref[h]
        o_ref[:, 128 * h:128 * (h + 1)] = (acc / acc[:, HD_B:HD_B + 1]).astype(BF16)


def _dsa_prompt(qi, qb, sm, kitp, ktp, vbf, *, b, t, tq, k_top):
    kc = tq
    nq = t // tq
    nkc = t // kc
    row = lambda n: pl.BlockSpec((tq, n), lambda i, j: (i * nq + j, 0))
    return pl.pallas_call(
        functools.partial(_dsa_body, tq=tq, kc=kc, rg=128, k_top=k_top),
        out_shape=jax.ShapeDtypeStruct((b * t, H_B * 128), BF16),
        grid=(b, nq),
        in_specs=[row(512), row(512), row(128),
                  pl.BlockSpec((1, nkc, 256, kc), lambda i, j: (i, 0, 0, 0)),
                  pl.BlockSpec((1, nkc, 512, kc), lambda i, j: (i, 0, 0, 0)),
                  pl.BlockSpec((1, nkc, kc, 128), lambda i, j: (i, 0, 0, 0))],
        out_specs=row(H_B * 128),
        scratch_shapes=[pltpu.VMEM((nkc, tq, kc), F32), pltpu.VMEM((nkc, tq, kc), F32),
                        pltpu.VMEM((H_IDX, tq, 128), F32), pltpu.VMEM((H_B, tq, 1), F32),
                        pltpu.VMEM((H_B, tq, 128), F32)],
        compiler_params=_cparams(2),
        name="dsa_prompt",
    )(qi, qb, sm, kitp, ktp, vbf.reshape(b, nkc, kc, 128))


def _merge_body(x_ref, oa_ref, ob_ref, g1_ref, wg_ref, woa_ref, wob_ref, wo_ref, h_ref):
    x = x_ref[...]
    xn = _rms_rows(x, g1_ref[...]).astype(BF16)
    gates = jax.nn.sigmoid(jnp.dot(xn, wg_ref[...], preferred_element_type=F32))
    a = jnp.dot(oa_ref[...], woa_ref[...], preferred_element_type=F32)
    bb = jnp.dot(ob_ref[...], wob_ref[...], preferred_element_type=F32)
    mix = gates[:, :D_MODEL] * a + gates[:, D_MODEL:] * bb
    h_ref[...] = x + jnp.dot(mix.astype(BF16), wo_ref[...], preferred_element_type=F32)


def _merge(x, oa, ob, w, *, tm):
    m, d = x.shape
    row = lambda n: pl.BlockSpec((tm, n), lambda i: (i, 0))
    return pl.pallas_call(
        _merge_body,
        out_shape=jax.ShapeDtypeStruct((m, d), F32),
        grid=(m // tm,),
        in_specs=[row(d), row(512), row(1024), _const_spec((1, d)), _const_spec((d, 2 * d)),
                  _const_spec((512, d)), _const_spec((1024, d)), _const_spec((d, d))],
        out_specs=row(d),
        compiler_params=_cparams(1),
        name="merge",
    )(x, oa, ob, w["g1"], w["wg"], w["woa"], w["wob"], w["wo"])


def _mlp_body(h_ref, g2_ref, wup_ref, wdn_ref, y_ref):
    h = h_ref[...]
    hn = _rms_rows(h, g2_ref[...]).astype(BF16)
    up = jnp.dot(hn, wup_ref[...], preferred_element_type=F32)
    act = jnp.square(jnp.maximum(up, 0.0)).astype(BF16)
    y_ref[...] = h + jnp.dot(act, wdn_ref[...], preferred_element_type=F32)


def _mlp(h, w, *, tm):
    m, d = h.shape
    row = pl.BlockSpec((tm, d), lambda i: (i, 0))
    return pl.pallas_call(
        _mlp_body,
        out_shape=jax.ShapeDtypeStruct((m, d), F32),
        grid=(m // tm,),
        in_specs=[row, _const_spec((1, d)), _const_spec((d, D_FF)), _const_spec((D_FF, d))],
        out_specs=row,
        compiler_params=_cparams(1),
        name="mlp",
    )(h, w["g2"], w["wup"], w["wdn"])


def _proj_decode_body(x_ref, st_ref, g1_ref, wqkv_ref, wz_ref, wbq_ref, wkv_ref, wiq_ref, wsm_ref,
                      convw_ref, qng_ref, kng_ref, alog_ref, dtb_ref, ones_ref, seg_ref, cos_ref, sin_ref,
                      u_out, q_out, k_out, v_out, z_out, qb_out, kv_out, qi_out, sm_out, ss_out):
    x = x_ref[...]
    xn = _rms_rows(x, g1_ref[...]).astype(BF16)
    u = jnp.dot(xn, wqkv_ref[...], preferred_element_type=F32)
    u_out[...] = u
    cw = convw_ref[...]
    acc = st_ref[0] * cw[0:1] + st_ref[1] * cw[1:2] + st_ref[2] * cw[2:3] + u * cw[3:4]
    c = _silu(acc)
    for h in range(H_A):
        qh = c[:, h * DK_A:(h + 1) * DK_A]
        kh = c[:, (H_A + h) * DK_A:(H_A + h + 1) * DK_A]
        q_out[:, h * DK_A:(h + 1) * DK_A] = qh * (lax.rsqrt(jnp.sum(qh * qh, -1, keepdims=True) + EPS)
                                                 * (DK_A ** -0.5))
        k_out[:, h * DK_A:(h + 1) * DK_A] = kh * lax.rsqrt(jnp.sum(kh * kh, -1, keepdims=True) + EPS)
    v_out[...] = c[:, 2 * H_A * DK_A:]
    z_out[...] = jnp.dot(xn, wz_ref[...], preferred_element_type=F32)
    cos, sin = cos_ref[...], sin_ref[...]
    ones_bd = ones_ref[...]
    bq = jnp.dot(xn, wbq_ref[...], preferred_element_type=F32)
    qb = _rope_rows(_seg_rms(bq, ones_bd, qng_ref[...]), cos, sin)
    qb_out[...] = (qb * (HD_B ** -0.5)).astype(BF16)
    kv = jnp.dot(xn, wkv_ref[...], preferred_element_type=F32)
    kb = _rope_rows(_seg_rms(kv[:, :128], ones_bd[:128, :128], kng_ref[...]), cos, sin)
    kv_out[:, 0:128] = kb
    kv_out[:, 128:256] = kv[:, 128:]
    iq = jnp.dot(xn, wiq_ref[...], preferred_element_type=F32)
    qi = (_rope_rows(iq, cos, sin) * (D_IDX ** -0.5)).astype(BF16)
    qi_out[...] = qi
    sm = jnp.dot(xn, wsm_ref[...], preferred_element_type=F32)
    smo = _small_block(sm, cos, sin, alog_ref[...], dtb_ref[...])
    sm_out[...] = smo
    lane = lax.broadcasted_iota(I32, smo.shape, 1)
    ki = smo.astype(BF16).astype(F32)
    ki2 = jnp.where(lane < D_IDX, ki, pltpu.roll(ki, D_IDX, 1))
    prod = qi.astype(F32) * jnp.concatenate([ki2] * 4, axis=-1)
    sh = _mm_f32(prod, seg_ref[...])
    contrib = jnp.where(jnp.logical_and(lane >= _SM_WI, lane < _SM_WI + H_IDX),
                        jnp.maximum(sh, 0.0) * smo, 0.0)
    ss_out[...] = jnp.broadcast_to(jnp.sum(contrib, axis=-1, keepdims=True), ss_out.shape)


def _proj_decode(x, st, w, tabs):
    r, d = x.shape
    outs = [(r, CONV_CH, F32), (r, 512, F32), (r, 512, F32), (r, 512, F32), (r, 512, F32), (r, 512, BF16),
            (r, 256, F32), (r, 512, BF16), (r, 128, F32), (r, 128, F32)]
    args = (x, st, w["g1"], w["wqkv"], w["wz"], w["wbq"], w["wkv"], w["wiq"], w["wsm"], w["convw"],
            w["qng"], w["kng"], w["alog_row"], w["dtb_row"], w["ones_bd"], w["seg_sum"], tabs["cos"], tabs["sin"])
    return pl.pallas_call(
        _proj_decode_body,
        out_shape=[jax.ShapeDtypeStruct((a, n), dt) for a, n, dt in outs],
        compiler_params=pltpu.CompilerParams(vmem_limit_bytes=VMEM_LIMIT),
        name="proj_decode",
    )(*args)


def _gdn_decode_body(qc_ref, kc_ref, v_ref, z_ref, sm_ref, gg_ref, s_ref, o_ref, so_ref):
    sm = sm_ref[0]
    gg = gg_ref[...]
    for h in range(H_A):
        g = sm[:, _SM_G + h:_SM_G + h + 1]
        beta = sm[:, _SM_BETA + h:_SM_BETA + h + 1]
        kcol, qcol = kc_ref[0, h], qc_ref[0, h]
        v = v_ref[0, h]
        s = s_ref[0, h] * jnp.exp(g)
        vn = beta * (v - jnp.sum(s * kcol, axis=0, keepdims=True))
        s = s + kcol * vn
        so_ref[0, h] = s
        o = jnp.sum(s * qcol, axis=0, keepdims=True)
        o_ref[0, h] = (_rms_rows(o, gg) * _silu(z_ref[0, h])).astype(BF16)


def _gdn_decode(q, k, v, z, sm, gg, state):
    r = q.shape[0]
    col = lambda a: a.reshape(r, H_A, DK_A, 1)
    rowv = lambda a: a.reshape(r, H_A, 1, DV_A)
    cspec = pl.BlockSpec((1, H_A, DK_A, 1), lambda i: (i, 0, 0, 0))
    rspec = pl.BlockSpec((1, H_A, 1, DV_A), lambda i: (i, 0, 0, 0))
    sspec = pl.BlockSpec((1, H_A, DK_A, DV_A), lambda i: (i, 0, 0, 0))
    o, s_new = pl.pallas_call(
        _gdn_decode_body,
        out_shape=[jax.ShapeDtypeStruct((r, H_A, 1, DV_A), BF16),
                   jax.ShapeDtypeStruct((r, H_A, DK_A, DV_A), F32)],
        grid=(r,),
        in_specs=[cspec, cspec, rspec, rspec, pl.BlockSpec((1, 1, 128), lambda i: (i, 0, 0)),
                  _const_spec((1, DV_A)), sspec],
        out_specs=[rspec, sspec],
        compiler_params=_cparams(1),
        name="gdn_decode",
    )(col(q), col(k), rowv(v), rowv(z), sm.reshape(r, 1, 128), gg, state)
    return o.reshape(r, H_A * DV_A), s_new


def _idx_decode_body(pt_ref, qi_ref, w_ref, *refs, npg):
    pages, out_ref = refs[:npg], refs[npg]
    qi = qi_ref[0]
    w = w_ref[0]
    for p in range(npg):
        s = _mm(qi, pages[p][0])
        out_ref[0, :, PAGE_SIZE * p:PAGE_SIZE * (p + 1)] = jnp.sum(jnp.maximum(s, 0.0) * w, axis=0,
                                                                   keepdims=True)


def _idx_decode(page_table, qi, wi_col, cache_kidx, *, npg):
    r, n_pages = page_table.shape
    page_spec = lambda p: pl.BlockSpec((1, D_IDX, PAGE_SIZE), lambda i, j, pt: (pt[i, j * npg + p], 0, 0))
    grid_spec = pltpu.PrefetchScalarGridSpec(
        num_scalar_prefetch=1,
        grid=(r, n_pages // npg),
        in_specs=[pl.BlockSpec((1, H_IDX, D_IDX), lambda i, j, pt: (i, 0, 0)),
                  pl.BlockSpec((1, H_IDX, 1), lambda i, j, pt: (i, 0, 0))]
                 + [page_spec(p) for p in range(npg)],
        out_specs=pl.BlockSpec((1, 1, PAGE_SIZE * npg), lambda i, j, pt: (i, 0, j)),
    )
    return pl.pallas_call(
        functools.partial(_idx_decode_body, npg=npg),
        out_shape=jax.ShapeDtypeStruct((r, 1, n_pages * PAGE_SIZE), F32),
        grid_spec=grid_spec,
        compiler_params=_cparams(2),
        name="idx_decode",
    )(page_table, qi.reshape(r, H_IDX, D_IDX), wi_col, *([cache_kidx] * npg))


def _select_decode_body(sc_ref, ss_ref, bias_ref, bself_ref, *, k_top, cw):
    r, n = sc_ref.shape
    nch = n // cw
    key_ref = sc_ref
    skey = ss_ref[:, 0:1]

    def lane_sum(f):
        def body(c, acc):
            off = pl.multiple_of(c * cw, cw)
            kk = key_ref[:, pl.ds(off, cw)]
            for j in range(cw // 128):
                acc = acc + f(kk[:, 128 * j:128 * (j + 1)], off + 128 * j)
            return acc
        acc = lax.fori_loop(0, nch, body, jnp.zeros((r, 128), F32))
        return jnp.sum(acc, axis=-1, keepdims=True).astype(I32)

    def count_ge(cand):
        cb = jnp.broadcast_to(cand, (r, 128))
        return lane_sum(lambda kk, off: jnp.where(kk >= cb, 1.0, 0.0)) + (skey >= cand).astype(I32)

    def count_gt_eq(t):
        tb = jnp.broadcast_to(t, (r, 128))
        return (lane_sum(lambda kk, off: jnp.where(kk > tb, 1.0, 0.0)) + (skey > t).astype(I32),
                lane_sum(lambda kk, off: jnp.where(kk == tb, 1.0, 0.0)) + (skey == t).astype(I32))

    def count_eq_below(t, j):
        tb = jnp.broadcast_to(t, (r, 128))
        jb = jnp.broadcast_to(j, (r, 128))
        lane = lax.broadcasted_iota(I32, (r, 128), 1)
        return (lane_sum(lambda kk, off: jnp.where(kk == tb, jnp.where(lane + off < jb, 1.0, 0.0), 0.0))
                + jnp.logical_and(skey == t, j > n).astype(I32))

    t, x = _topk_threshold(count_ge, count_gt_eq, count_eq_below, (r, 1), k_top, 15)
    tb = jnp.broadcast_to(t, (r, cw))
    xb = jnp.broadcast_to(x, (r, cw))

    def bias_chunk(c, carry):
        off = pl.multiple_of(c * cw, cw)
        kk = key_ref[:, pl.ds(off, cw)]
        idx = lax.broadcasted_iota(I32, (r, cw), 1) + off
        tie = jnp.where(kk == tb, jnp.where(idx <= xb, 0.0, NEG_BIG), NEG_BIG)
        bias_ref[:, pl.ds(off, cw)] = jnp.where(kk > tb, 0.0, tie)
        return carry

    lax.fori_loop(0, nch, bias_chunk, 0)
    tie_self = jnp.where(skey == t, jnp.where(x >= n, 0.0, NEG_BIG), NEG_BIG)
    bself_ref[...] = jnp.broadcast_to(jnp.where(skey > t, 0.0, tie_self), bself_ref.shape)


def _select_decode(scores, sself, *, k_top):
    r, n = scores.shape
    return pl.pallas_call(
        functools.partial(_select_decode_body, k_top=k_top, cw=512),
        out_shape=[jax.ShapeDtypeStruct((r, n), F32), jax.ShapeDtypeStruct((r, 128), F32)],
        compiler_params=pltpu.CompilerParams(vmem_limit_bytes=VMEM_LIMIT),
        name="select_decode",
    )(scores, sself)


def _attn_decode_body(pt_ref, qp_ref, bias_ref, kvs_ref, bself_ref, *refs, npg, nsteps):
    pages, o_ref, m_ref, l_ref, acc_ref = refs[:npg], refs[npg], refs[npg + 1], refs[npg + 2], refs[npg + 3]
    j = pl.program_id(1)

    @pl.when(j == 0)
    def _():
        m_ref[...] = jnp.full(m_ref.shape, NEG_BIG, F32)
        l_ref[...] = jnp.zeros(l_ref.shape, F32)
        acc_ref[...] = jnp.zeros(acc_ref.shape, F32)

    qp = qp_ref[0]

    ss = [_mm(qp, pages[p][0, 0:128, :]) + bias_ref[0, :, PAGE_SIZE * p:PAGE_SIZE * (p + 1)]
          for p in range(npg)]
    m_old = m_ref[...]
    m_new = m_old
    for s in ss:
        m_new = jnp.maximum(m_new, jnp.max(s, axis=-1, keepdims=True))
    alpha = jnp.exp(m_old - m_new)
    l_new = l_ref[...] * alpha
    acc_new = acc_ref[...] * alpha
    for p in range(npg):
        pe = jnp.exp(ss[p] - m_new)
        l_new = l_new + jnp.sum(pe, axis=-1, keepdims=True)
        acc_new = acc_new + _mm_nt(pe, pages[p][0, 128:256, :])
    m_ref[...] = m_new
    l_ref[...] = l_new
    acc_ref[...] = acc_new

    @pl.when(j == nsteps - 1)
    def _():
        kvs = kvs_ref[0]
        krow = kvs[:, :128].astype(BF16).astype(F32)
        s = jnp.sum(qp.astype(F32) * krow, axis=-1, keepdims=True) + bself_ref[0][:, 0:1]
        m_old = m_ref[...]
        m_new = jnp.maximum(m_old, s)
        alpha = jnp.exp(m_old - m_new)
        p = jnp.exp(s - m_new)
        l = l_ref[...] * alpha + p
        vrow = kvs[:, 128:].astype(BF16).astype(F32)
        acc = acc_ref[...] * alpha + p.astype(BF16).astype(F32) * vrow
        out = acc / l
        rowi = lax.broadcasted_iota(I32, out.shape, 0)
        out = jnp.where(rowi < 4, out, pltpu.roll(out, HD_B, 1))
        o_ref[0] = out.astype(BF16)


def _attn_decode(page_table, qpad, bias, kv_self, bself, cache_kv, *, npg):
    r, n_pages = page_table.shape
    nsteps = n_pages // npg
    page_spec = lambda p: pl.BlockSpec((1, 256, PAGE_SIZE), lambda i, j, pt: (pt[i, j * npg + p], 0, 0))
    grid_spec = pltpu.PrefetchScalarGridSpec(
        num_scalar_prefetch=1,
        grid=(r, nsteps),
        in_specs=[pl.BlockSpec((1, H_B, 128), lambda i, j, pt: (i, 0, 0)),
                  pl.BlockSpec((1, 1, PAGE_SIZE * npg), lambda i, j, pt: (i, 0, j)),
                  pl.BlockSpec((1, 1, 256), lambda i, j, pt: (i, 0, 0)),
                  pl.BlockSpec((1, 1, 128), lambda i, j, pt: (i, 0, 0))]
                 + [page_spec(p) for p in range(npg)],
        out_specs=pl.BlockSpec((1, H_B, 128), lambda i, j, pt: (i, 0, 0)),
        scratch_shapes=[pltpu.VMEM((H_B, 1), F32), pltpu.VMEM((H_B, 1), F32), pltpu.VMEM((H_B, 128), F32)],
    )
    return pl.pallas_call(
        functools.partial(_attn_decode_body, npg=npg, nsteps=nsteps),
        out_shape=jax.ShapeDtypeStruct((r, H_B, 128), BF16),
        grid_spec=grid_spec,
        compiler_params=_cparams(2),
        name="attn_decode",
    )(page_table, qpad, bias.reshape(r, 1, -1), kv_self.reshape(r, 1, 256), bself.reshape(r, 1, 128),
      *([cache_kv] * npg))


def _prep_weights(norm1_g, w_in, conv_w, a_log, dt_bias, gdn_norm_g, q_norm_g, k_norm_g,
                  w_out_a, w_out_b, w_o, norm2_g, w_up, w_down):
    o = _OFF
    wb = w_in.astype(BF16)
    d = w_in.shape[0]
    wsm = jnp.concatenate([wb[:, o["ik"]:o["iw"]], wb[:, o["ab"]:o["bq"]], wb[:, o["iw"]:o["ga"]],
                           jnp.zeros((d, 128 - 80), BF16)], axis=1)

    def lanes(vals, start):
        return jnp.zeros((1, 128), F32).at[0, start:start + vals.shape[0]].set(vals.astype(F32))

    seg = np.zeros((512, 128), np.float32)
    for h in range(H_IDX):
        seg[64 * h:64 * (h + 1), _SM_WI + h] = 1.0
    ones_bd = np.kron(np.eye(8, dtype=np.float32), np.ones((64, 64), np.float32))
    wob = w_out_b.astype(BF16).reshape(H_B, HD_B, d)
    wob_pad = jnp.concatenate([wob, jnp.zeros_like(wob)], axis=1).reshape(H_B * 128, d)
    col8 = lambda v: jnp.concatenate([jnp.zeros((4,), F32), v.astype(F32)]).reshape(8, 1)
    return dict(
        g1=norm1_g.reshape(1, d).astype(F32),
        wqkv=wb[:, o["aq"]:o["az"]], wz=wb[:, o["az"]:o["ab"]], wbq=wb[:, o["bq"]:o["bk"]],
        wkv=wb[:, o["bk"]:o["iq"]], wiq=wb[:, o["iq"]:o["ik"]], wsm=wsm,
        wkt=wb[:, o["bk"]:o["bv"]].T, wsmt=wsm.T, wg=wb[:, o["ga"]:o["end"]],
        convw=conv_w.astype(F32),
        qng=jnp.tile(q_norm_g.astype(F32), H_B).reshape(1, 512),
        kng=jnp.tile(k_norm_g.astype(F32), N_KV_B).reshape(1, 128),
        kngt=k_norm_g.astype(F32).reshape(HD_B, 1),
        alog_row=lanes(a_log, _SM_G), dtb_row=lanes(dt_bias, _SM_G),
        alog_col=col8(a_log), dtb_col=col8(dt_bias),
        ones_bd=jnp.asarray(ones_bd, BF16), seg_sum=jnp.asarray(seg),
        gg=gdn_norm_g.reshape(1, DV_A).astype(F32),
        woa=w_out_a.astype(BF16), wob=wob_pad, wo=w_o.astype(BF16),
        g2=norm2_g.reshape(1, d).astype(F32), wup=w_up.astype(BF16), wdn=w_down.astype(BF16),
    )


def _rope_tables(pos):
    half = HD_B // 2
    inv = ROPE_THETA ** (-jnp.arange(half, dtype=F32) / half)
    ang = pos.astype(F32)[:, None] * inv[None, :]
    cos, sin = jnp.cos(ang), jnp.sin(ang)
    return dict(cos=jnp.tile(cos, (1, 4)), sin=jnp.tile(jnp.concatenate([-sin, sin], axis=1), (1, 2)),
                cos_t=cos.T, sin_t=sin.T)


def _layer_prompt(x, w, *, tm=512, tq=256, nb=4):
    b, t, d = x.shape
    tabs = _rope_tables(jnp.arange(t))
    (q, k, v, z, qb, kv, vbf, qi, sm, kidx, ktp, kitp, gt, conv_new) = _proj_prompt(x, w, tabs, tm=tm, kc=tq)
    oa, ssm = _gdn_prompt(q, k, v, z, sm, gt, w["gg"], b=b, t=t, nb=nb)
    ob = _dsa_prompt(qi, qb, sm, kitp, ktp, vbf, b=b, t=t, tq=tq, k_top=min(TOPK_MAX, t // 4))
    h = _merge(x.reshape(b * t, d), oa.reshape(b * t, 512), ob, w, tm=tm)
    y = _mlp(h, w, tm=tm)
    return (y.reshape(b, t, d), kv.reshape(b, t, 2, N_KV_B, HD_B), kidx.reshape(b, t, D_IDX), ssm, conv_new)


def _layer_decode(x, w, state_ssm, state_conv, cache_kv, cache_kidx, page_table, *, npg=16):
    r, d = x.shape
    n_pages = page_table.shape[1]
    past = n_pages * PAGE_SIZE
    tabs = _rope_tables(jnp.full((1,), past))
    st = jnp.swapaxes(state_conv, 0, 1)
    (u, q, k, v, z, qb, kv, qi, sm, sself) = _proj_decode(x, st, w, tabs)
    oa, ssm = _gdn_decode(q, k, v, z, sm, w["gg"], state_ssm)
    wi_col = sm[:, _SM_WI:_SM_WI + H_IDX].reshape(r, H_IDX, 1)
    kidx_pages = jnp.swapaxes(cache_kidx, 1, 2)
    kv_pages = jnp.transpose(cache_kv, (0, 2, 3, 4, 1)).reshape(-1, 2 * N_KV_B * HD_B, PAGE_SIZE)
    scores = _idx_decode(page_table, qi, wi_col, kidx_pages, npg=npg).reshape(r, past)
    bias, bself = _select_decode(scores, sself, k_top=min(TOPK_MAX, (past + 1) // 4))
    qh = qb.reshape(r, H_B, HD_B)
    zq = jnp.zeros_like(qh[:, :4])
    qpad = jnp.concatenate([jnp.concatenate([qh[:, :4], zq], axis=-1),
                            jnp.concatenate([zq, qh[:, 4:]], axis=-1)], axis=1)
    ob = _attn_decode(page_table, qpad, bias, kv, bself, kv_pages, npg=npg)
    h = _merge(x, oa, ob.reshape(r, H_B * 128), w, tm=r)
    y = _mlp(h, w, tm=r)
    conv_new = jnp.concatenate([state_conv[:, 1:], u[:, None, :]], axis=1)
    return (y, kv.reshape(r, 1, 2, N_KV_B, HD_B), sm[:, :D_IDX].reshape(r, 1, D_IDX), ssm, conv_new)


def kernel(x_prompt, x_sample, cache_kv, cache_kidx, page_table, state_ssm, state_conv, norm1_g, w_in, conv_w,
           a_log, dt_bias, gdn_norm_g, q_norm_g, k_norm_g, w_out_a, w_out_b, w_o, norm2_g, w_up, w_down):
    assert w_in.shape[0] == 1, "single-layer trunk"
    w = _prep_weights(norm1_g[0], w_in[0], conv_w[0], a_log[0], dt_bias[0], gdn_norm_g[0], q_norm_g[0],
                      k_norm_g[0], w_out_a[0], w_out_b[0], w_o[0], norm2_g[0], w_up[0], w_down[0])
    yp, kv_p, ki_p, ssm_p, cv_p = _layer_prompt(x_prompt, w)
    ys, kv_s, ki_s, ssm_s, cv_s = _layer_decode(x_sample[:, 0], w, state_ssm[0], state_conv[0],
                                                cache_kv[0], cache_kidx[0], page_table)
    return (yp, ys[:, None, :], kv_p[None], ki_p[None], ssm_p[None], cv_p[None],
            kv_s[None], ki_s[None], ssm_s[None], cv_s[None])
```

```python
import functools
import math

import jax
import jax.numpy as jnp
import numpy as np
from jax import lax
from jax.experimental import pallas as pl
from jax.experimental.pallas import tpu as pltpu

F32 = jnp.float32
BF16 = jnp.bfloat16
I32 = jnp.int32

D_MODEL = 1024
H_A, DK_A, DV_A = 4, 128, 128
CONV_W = 4
CONV_CH = H_A * (2 * DK_A + DV_A)
CHUNK = 64
HD_B, H_B, N_KV_B = 64, 8, 2
H_IDX, D_IDX = 8, 64
TOPK_MAX = 256
PAGE_SIZE = 128
ROPE_THETA = 10000.0
D_FF = 4 * D_MODEL
EPS = 1e-6

_OFF = dict(aq=0, ak=512, av=1024, az=1536, ab=2048, aa=2052, bq=2056, bk=2568, bv=2696,
            iq=2824, ik=3336, iw=3400, ga=3408, gb=4432, end=5456)
_SM_BETA, _SM_G, _SM_WI = 64, 68, 72

NEG_BIG = -1e30
INT_MIN = -(2 ** 31)
NEG_INF_KEY = (0xFF800000 ^ 0x7FFFFFFF) - 2 ** 32

VMEM_LIMIT = 56 * 1024 * 1024
HIGHEST = lax.Precision.HIGHEST


def _cparams(n_axes):
    return pltpu.CompilerParams(dimension_semantics=("arbitrary",) * n_axes,
                                vmem_limit_bytes=VMEM_LIMIT)


def _const_spec(shape):
    nd = len(shape)
    return pl.BlockSpec(shape, lambda *a: (0,) * nd, pipeline_mode=pl.Buffered(1))


def _mm(a, b):
    return jnp.dot(a.astype(BF16), b.astype(BF16), preferred_element_type=F32)


def _mm_nt(a, b):
    return lax.dot_general(a.astype(BF16), b.astype(BF16), (((1,), (1,)), ((), ())),
                           preferred_element_type=F32)


def _mm_tn(a, b):
    return lax.dot_general(a.astype(BF16), b.astype(BF16), (((0,), (0,)), ((), ())),
                           preferred_element_type=F32)


def _mm_f32(a, b):
    return jnp.dot(a, b, preferred_element_type=F32, precision=HIGHEST)


def _rms_rows(x, g):
    return x * lax.rsqrt(jnp.mean(x * x, axis=-1, keepdims=True) + EPS) * g


def _silu(x):
    return x * jax.nn.sigmoid(x)


def _softplus(x):
    return jnp.maximum(x, 0.0) + jnp.log1p(jnp.exp(-jnp.abs(x)))


def _rope_rows(x, cos, sin_signed):
    n = x.shape[-1]
    reps = n // 128
    if reps > 1:
        cos = jnp.concatenate([cos] * reps, axis=-1)
        sin_signed = jnp.concatenate([sin_signed] * reps, axis=-1)
    lane = lax.broadcasted_iota(I32, x.shape, x.ndim - 1)
    first_half = (lane % HD_B) < (HD_B // 2)
    partner = jnp.where(first_half, pltpu.roll(x, n - HD_B // 2, x.ndim - 1),
                        pltpu.roll(x, HD_B // 2, x.ndim - 1))
    return x * cos + partner * sin_signed


def _rope_cols(x, cos_t, sin_t):
    x1, x2 = x[:32], x[32:]
    return jnp.concatenate([x1 * cos_t - x2 * sin_t, x2 * cos_t + x1 * sin_t], axis=0)


def _seg_rms(x, ones_bd, g):
    sq = x * x
    hi = sq.astype(BF16)
    lo = (sq - hi.astype(F32)).astype(BF16)
    ss = (jnp.dot(hi, ones_bd, preferred_element_type=F32)
          + jnp.dot(lo, ones_bd, preferred_element_type=F32))
    return x * lax.rsqrt(ss * (1.0 / HD_B) + EPS) * g


def _small_block(sm, cos, sin_signed, alog_row, dtb_row):
    lane = lax.broadcasted_iota(I32, sm.shape, 1)
    roped = _rope_rows(sm, cos, sin_signed)
    beta = jax.nn.sigmoid(sm)
    g = -jnp.exp(alog_row) * _softplus(sm + dtb_row)
    wi = sm * (H_IDX ** -0.5)
    out = jnp.where(lane < _SM_BETA, roped,
                    jnp.where(lane < _SM_G, beta, jnp.where(lane < _SM_WI, g, wi)))
    return out


def _proj_prompt_body(x_ref, g1_ref, wqkv_ref, wz_ref, wkv_ref, wsm_ref,
                      wbqt_ref, wiqt_ref, wvt_ref, wsmt_ref, convw_ref, kng_ref, qngt_ref, alog_ref, dtb_ref,
                      alogt_ref, dtbt_ref, ones_ref, cos_ref, sin_ref, cost_ref, sint_ref,
                      q_out, k_out, v_out, z_out, kv_out, kbf_out, kibf_out, sm_out, kidx_out,
                      qtp_out, qit_out, vt_out, gt_out, wit_out, conv_out, ubuf, *, tm, kc):
    t = pl.program_id(1)
    x = x_ref[0]
    xn = _rms_rows(x, g1_ref[...]).astype(BF16)

    u = jnp.dot(xn, wqkv_ref[...], preferred_element_type=F32)

    @pl.when(t == 0)
    def _():
        ubuf[0:8, :] = jnp.zeros((8, CONV_CH), F32)

    ubuf[8:tm + 8, :] = u
    cw = convw_ref[...]
    acc = (ubuf[5:5 + tm, :] * cw[0:1] + ubuf[6:6 + tm, :] * cw[1:2]
           + ubuf[7:7 + tm, :] * cw[2:3] + u * cw[3:4])
    conv_out[0] = ubuf[tm + 5:tm + 8, :]
    ubuf[0:8, :] = ubuf[tm:tm + 8, :]
    c = _silu(acc)
    for h in range(H_A):
        qh = c[:, h * DK_A:(h + 1) * DK_A]
        kh = c[:, (H_A + h) * DK_A:(H_A + h + 1) * DK_A]
        q_out[:, h * DK_A:(h + 1) * DK_A] = qh * (lax.rsqrt(jnp.sum(qh * qh, -1, keepdims=True) + EPS)
                                                 * (DK_A ** -0.5))
        k_out[:, h * DK_A:(h + 1) * DK_A] = kh * lax.rsqrt(jnp.sum(kh * kh, -1, keepdims=True) + EPS)
    v_out[...] = c[:, 2 * H_A * DK_A:]
    z_out[...] = jnp.dot(xn, wz_ref[...], preferred_element_type=F32)

    cos, sin = cos_ref[...], sin_ref[...]
    ones_bd = ones_ref[...]

    kv = jnp.dot(xn, wkv_ref[...], preferred_element_type=F32)
    kb = _rope_rows(_seg_rms(kv[:, :128], ones_bd, kng_ref[...]), cos, sin)
    kv_out[:, 0:128] = kb
    kv_out[:, 128:256] = kv[:, 128:]
    kbf_out[...] = kb.astype(BF16)

    sm = jnp.dot(xn, wsm_ref[...], preferred_element_type=F32)
    smo = _small_block(sm, cos, sin, alog_ref[...], dtb_ref[...])
    sm_out[...] = smo
    kidx_out[...] = smo[:, :D_IDX]
    kibf_out[...] = smo[:, :D_IDX].astype(BF16)

    nt_dims = (((1,), (1,)), ((), ()))
    cos_t, sin_t = cost_ref[...], sint_ref[...]
    qt = lax.dot_general(wbqt_ref[...], xn, nt_dims, preferred_element_type=F32)
    qit = lax.dot_general(wiqt_ref[...], xn, nt_dims, preferred_element_type=F32)
    vt = lax.dot_general(wvt_ref[...], xn, nt_dims, preferred_element_type=F32)
    smt = lax.dot_general(wsmt_ref[...], xn, nt_dims, preferred_element_type=F32)
    qg = jnp.concatenate([qngt_ref[...]] * (tm // 128), axis=1)
    zero64 = jnp.zeros((HD_B, tm), BF16)
    for h in range(H_B):
        qh = qt[HD_B * h:HD_B * (h + 1)]
        qh = qh * lax.rsqrt(jnp.mean(qh * qh, axis=0, keepdims=True) + EPS) * qg
        qh = (_rope_cols(qh, cos_t, sin_t) * (HD_B ** -0.5)).astype(BF16)
        n = h // (H_B // N_KV_B)
        qtp_out[0, h, HD_B * n:HD_B * (n + 1), :] = qh
        qtp_out[0, h, HD_B * (1 - n):HD_B * (2 - n), :] = zero64
        qit_out[0, h] = (_rope_cols(qit[D_IDX * h:D_IDX * (h + 1)], cos_t, sin_t) * (D_IDX ** -0.5)).astype(BF16)
    vtb = vt.astype(BF16)
    for j in range(tm // kc):
        vt_out[0, j] = vtb[:, j * kc:(j + 1) * kc]
    sm8 = smt[64:72]
    row = lax.broadcasted_iota(I32, sm8.shape, 0)
    g8 = -jnp.exp(alogt_ref[...]) * _softplus(sm8 + dtbt_ref[...])
    gt_out[0] = jnp.where(row < 4, jax.nn.sigmoid(sm8), g8)
    wit_out[0] = smt[72:80] * (H_IDX ** -0.5)


def _proj_prompt(x, w, tabs, *, tm, kc):
    b, t, d = x.shape
    nt = t // tm
    m = b * t
    row = lambda n: pl.BlockSpec((tm, n), lambda i, j: (i * nt + j, 0))
    ins = [
        pl.BlockSpec((1, tm, d), lambda i, j: (i, j, 0)),
        _const_spec((1, d)),
        _const_spec((d, CONV_CH)), _const_spec((d, 512)), _const_spec((d, 256)), _const_spec((d, 128)),
        _const_spec((512, d)), _const_spec((512, d)), _const_spec((128, d)), _const_spec((128, d)),
        _const_spec((CONV_W, CONV_CH)), _const_spec((1, 128)), _const_spec((HD_B, 128)),
        _const_spec((1, 128)), _const_spec((1, 128)), _const_spec((8, 1)), _const_spec((8, 1)),
        _const_spec((128, 128)),
        pl.BlockSpec((tm, 128), lambda i, j: (j, 0)), pl.BlockSpec((tm, 128), lambda i, j: (j, 0)),
        pl.BlockSpec((32, tm), lambda i, j: (0, j)), pl.BlockSpec((32, tm), lambda i, j: (0, j)),
    ]
    nkc = t // kc
    outs = [
        (jax.ShapeDtypeStruct((m, 512), F32), row(512)),
        (jax.ShapeDtypeStruct((m, 512), F32), row(512)),
        (jax.ShapeDtypeStruct((m, 512), F32), row(512)),
        (jax.ShapeDtypeStruct((m, 512), F32), row(512)),
        (jax.ShapeDtypeStruct((m, 256), F32), row(256)),
        (jax.ShapeDtypeStruct((m, 128), BF16), row(128)),
        (jax.ShapeDtypeStruct((m, D_IDX), BF16), row(D_IDX)),
        (jax.ShapeDtypeStruct((m, 128), F32), row(128)),
        (jax.ShapeDtypeStruct((m, D_IDX), F32), row(D_IDX)),
        (jax.ShapeDtypeStruct((b, H_B, 128, t), BF16),
         pl.BlockSpec((1, H_B, 128, tm), lambda i, j: (i, 0, 0, j))),
        (jax.ShapeDtypeStruct((b, H_IDX, D_IDX, t), BF16),
         pl.BlockSpec((1, H_IDX, D_IDX, tm), lambda i, j: (i, 0, 0, j))),
        (jax.ShapeDtypeStruct((b, nkc, 128, kc), BF16),
         pl.BlockSpec((1, tm // kc, 128, kc), lambda i, j: (i, j, 0, 0))),
        (jax.ShapeDtypeStruct((b, 8, t), F32), pl.BlockSpec((1, 8, tm), lambda i, j: (i, 0, j))),
        (jax.ShapeDtypeStruct((b, H_IDX, t), F32), pl.BlockSpec((1, H_IDX, tm), lambda i, j: (i, 0, j))),
        (jax.ShapeDtypeStruct((b, CONV_W - 1, CONV_CH), F32),
         pl.BlockSpec((1, CONV_W - 1, CONV_CH), lambda i, j: (i, 0, 0))),
    ]
    return pl.pallas_call(
        functools.partial(_proj_prompt_body, tm=tm, kc=kc),
        out_shape=[o[0] for o in outs],
        grid=(b, nt),
        in_specs=ins,
        out_specs=[o[1] for o in outs],
        scratch_shapes=[pltpu.VMEM((tm + 8, CONV_CH), F32)],
        compiler_params=_cparams(2),
        name="proj_prompt",
    )(x, w["g1"], w["wqkv"], w["wz"], w["wkv"], w["wsm"], w["wbqt"], w["wiqt"], w["wvt"], w["wsmt"],
      w["convw"], w["kng"], w["qngt"], w["alog_row"], w["dtb_row"], w["alog_col"], w["dtb_col"],
      w["ones_bd128"], tabs["cos"], tabs["sin"], tabs["cos_t"], tabs["sin_t"])


def _split3(x):
    hi = x.astype(BF16)
    r1 = x - hi.astype(F32)
    mid = r1.astype(BF16)
    lo = (r1 - mid.astype(F32)).astype(BF16)
    return hi, mid, lo


def _gdn_body(q_ref, k_ref, v_ref, z_ref, sm_ref, gt_ref, gg_ref, bdk_ref, bdp_ref, o_ref, s_ref, *, nb):
    c = pl.program_id(1)

    @pl.when(c == 0)
    def _():
        s_ref[...] = jnp.zeros_like(s_ref)

    n, nh = CHUNK, H_A
    w4 = nh * n
    ri4 = lax.broadcasted_iota(I32, (n, w4), 0)
    ci4 = lax.broadcasted_iota(I32, (n, w4), 1) % n
    strict4, incl4 = ri4 > ci4, ri4 >= ci4
    eye4 = jnp.where(ri4 == ci4, 1.0, 0.0)
    triu4 = jnp.where(ri4 <= ci4, 1.0, 0.0).astype(BF16)
    ri = lax.broadcasted_iota(I32, (n, n), 0)
    tril = jnp.where(ri >= lax.broadcasted_iota(I32, (n, n), 1), 1.0, 0.0).astype(BF16)
    blk_row = lax.broadcasted_iota(I32, (1, w4), 1) // n
    low_half = lax.broadcasted_iota(I32, (n, 128), 1) < n
    zeros_pad = jnp.zeros((n, 2 * DV_A), BF16)
    gg = gg_ref[...]
    dot = functools.partial(jnp.dot, preferred_element_type=F32)

    def pad_rows(x, e):
        z = zeros_pad[:, :x.shape[1]]
        return jnp.concatenate([x, z] if e == 0 else [z, x], axis=0)

    seqs = range(nb)
    st = [dict() for _ in seqs]

    for bi in seqs:
        hi, mid, lo = _split3(sm_ref[bi])
        st[bi]["gc_all"] = dot(tril, hi) + dot(tril, mid) + dot(tril, lo)
        hi, mid, lo = _split3(gt_ref[bi, 0])
        st[bi]["grows"] = dot(hi, triu4) + dot(mid, triu4) + dot(lo, triu4)

    for bi in seqs:
        d = st[bi]
        sm, k, grows = sm_ref[bi], k_ref[bi], d["grows"]
        grow = jnp.where(blk_row == 0, grows[4:5], jnp.where(blk_row == 1, grows[5:6],
                         jnp.where(blk_row == 2, grows[6:7], grows[7:8])))
        gcols = [jnp.broadcast_to(d["gc_all"][:, _SM_G + h:_SM_G + h + 1], (n, 128)) for h in range(nh)]
        betas = [jnp.broadcast_to(sm[:, _SM_BETA + h:_SM_BETA + h + 1], (n, 128)) for h in range(nh)]
        d["gc512"] = jnp.concatenate(gcols, axis=1)
        d["beta512"] = jnp.concatenate(betas, axis=1)
        gc256 = jnp.concatenate([jnp.where(low_half, gcols[0], gcols[1]),
                                 jnp.where(low_half, gcols[2], gcols[3])], axis=1)
        d["dec"] = jnp.exp(jnp.minimum(gc256 - grow, 0.0))
        d["kb"] = k * d["beta512"]
        kbd = jnp.concatenate([k.astype(BF16)] * nh, axis=0) * bdk_ref[...]
        d["aq"] = lax.dot_general(jnp.concatenate([d["kb"], q_ref[bi]], axis=0).astype(BF16), kbd,
                                  (((1,), (1,)), ((), ())), preferred_element_type=F32)

    for bi in seqs:
        d = st[bi]
        x = jnp.where(strict4, -d["aq"][:n] * d["dec"], 0.0)
        d["qkm"] = jnp.where(incl4, d["aq"][n:] * d["dec"], 0.0).astype(BF16)
        d["inv"], d["p"] = eye4 + x, x
    for lvl in range(6):
        for bi in seqs:
            d = st[bi]
            p, inv = d["p"], d["inv"]
            wbd = jnp.concatenate([p.astype(BF16)] * nh, axis=0) * bdp_ref[...]
            if lvl == 0:
                d["p"] = dot(p.astype(BF16), wbd)
            elif lvl < 5:
                r = dot(jnp.concatenate([p, inv], axis=0).astype(BF16), wbd)
                d["p"], d["inv"] = r[:n], inv + r[n:]
            else:
                d["inv"] = inv + dot(inv.astype(BF16), wbd)

    heads = [(bi, h) for bi in seqs for h in range(nh)]
    hsl = lambda h: slice(h * DK_A, (h + 1) * DK_A)
    psl = lambda h: slice(128 * (h // 2), 128 * (h // 2 + 1))
    for bi in seqs:
        d = st[bi]
        k, gc512 = k_ref[bi], d["gc512"]
        d["tinv"] = d["inv"].astype(BF16)
        egc = jnp.exp(gc512)
        d["rhs_v"] = (v_ref[bi] * d["beta512"]).astype(BF16)
        d["rhs_k"] = (d["kb"] * egc).astype(BF16)
        g_last = gc512[n - 1:n, :]
        d["q_dec"] = q_ref[bi] * egc
        d["k_dec"] = (k * jnp.exp(g_last - gc512)).astype(BF16)
        d["c_dec"] = jnp.exp(g_last)
    sol, ws, vn = {}, {}, {}
    for bi, h in heads:
        d = st[bi]
        rhs = pad_rows(jnp.concatenate([d["rhs_v"][:, hsl(h)], d["rhs_k"][:, hsl(h)]], axis=1), h % 2)
        sol[bi, h] = dot(d["tinv"][:, psl(h)], rhs)
    for bi, h in heads:
        lhs = jnp.concatenate([sol[bi, h][:, DV_A:], st[bi]["q_dec"][:, hsl(h)]], axis=0)
        ws[bi, h] = dot(lhs.astype(BF16), s_ref[bi, h].astype(BF16))
    for bi, h in heads:
        d = st[bi]
        vn[bi, h] = (sol[bi, h][:, :DV_A] - ws[bi, h][:n]).astype(BF16)
        s_ref[bi, h] = d["c_dec"][:, hsl(h)] * s_ref[bi, h] + lax.dot_general(
            d["k_dec"][:, hsl(h)], vn[bi, h], (((0,), (0,)), ((), ())), preferred_element_type=F32)
    for bi, h in heads:
        o = ws[bi, h][n:] + dot(st[bi]["qkm"][:, psl(h)], pad_rows(vn[bi, h], h % 2))
        o_ref[bi, :, hsl(h)] = (_rms_rows(o, gg) * _silu(z_ref[bi, :, hsl(h)])).astype(BF16)


def _gdn_prompt(q, k, v, z, sm, gt, gg, *, b, t, nb):
    nc = t // CHUNK
    r3 = lambda a, n: a.reshape(b, t, n)
    blk = lambda n: pl.BlockSpec((nb, CHUNK, n), lambda i, j: (i, j, 0))
    w4 = H_A * CHUNK
    bdk = np.kron(np.eye(H_A, dtype=np.float32), np.ones((CHUNK, DK_A), np.float32))
    bdp = np.kron(np.eye(H_A, dtype=np.float32), np.ones((CHUNK, CHUNK), np.float32))
    return pl.pallas_call(
        functools.partial(_gdn_body, nb=nb),
        out_shape=[jax.ShapeDtypeStruct((b, t, 512), BF16),
                   jax.ShapeDtypeStruct((b, H_A, DK_A, DV_A), F32)],
        grid=(b // nb, nc),
        in_specs=[blk(512), blk(512), blk(512), blk(512), blk(128),
                  pl.BlockSpec((nb, 1, 8, CHUNK), lambda i, j: (i, j, 0, 0)),
                  _const_spec((1, DV_A)), _const_spec((w4, H_A * DK_A)), _const_spec((w4, w4))],
        out_specs=[blk(512), pl.BlockSpec((nb, H_A, DK_A, DV_A), lambda i, j: (i, 0, 0, 0))],
        compiler_params=_cparams(2),
        name="gdn_prompt",
    )(r3(q, 512), r3(k, 512), r3(v, 512), r3(z, 512), r3(sm, 128),
      jnp.swapaxes(gt.reshape(b, 8, nc, CHUNK), 1, 2), gg, jnp.asarray(bdk, BF16), jnp.asarray(bdp, BF16))


def _ordinal_to_float(k):
    k = jnp.maximum(k, NEG_INF_KEY)
    return pltpu.bitcast(k ^ ((k >> 31) & 0x7FFFFFFF), F32)


def _topk_threshold(count_ge, count_gt_eq, count_eq_below, shape, k_top, idx_bits):
    def bit_step(i, o):
        cand = o + jnp.left_shift(jnp.int32(1), 31 - i)
        return jnp.where(count_ge(_ordinal_to_float(cand)) >= k_top, cand, o)

    o = lax.fori_loop(0, 32, bit_step, jnp.full(shape, INT_MIN, I32))
    t = _ordinal_to_float(o)
    c_gt, c_eq = count_gt_eq(t)
    need = k_top - c_gt
    untied = jnp.logical_or(c_eq == need, o <= NEG_INF_KEY)
    all_untied = jnp.min(jnp.where(untied, 1.0, 0.0)) > 0.5

    def tie_search(_):
        def idx_step(i, x):
            cand = x + jnp.left_shift(jnp.int32(1), idx_bits - 1 - i)
            return jnp.where(count_eq_below(t, cand) < need, cand, x)
        return lax.fori_loop(0, idx_bits, idx_step, jnp.zeros(shape, I32))

    x = lax.cond(all_untied, lambda _: jnp.full(shape, 2 ** 30, I32), tie_search, 0)
    return t, x


_ACC_ROWS = HD_B + 16


def _dsa_body(qit_ref, wit_ref, qtp_ref, ki_ref, k_ref, vt_ref, o_ref,
              sc_ref, bias_ref, m_ref, acc_ref, *, tq, kc, k_top):
    i = pl.program_id(1)
    nkc = i + 1
    rowi = lax.broadcasted_iota(I32, (kc, tq), 0)
    coli = lax.broadcasted_iota(I32, (kc, tq), 1)
    ng = kc // 32

    wit = wit_ref[0]

    def score_chunk(c, carry):
        kic = ki_ref[0, c]
        score = jnp.zeros((kc, tq), F32)
        for h in range(H_IDX):
            s = jnp.dot(kic, qit_ref[0, h], preferred_element_type=F32)
            score = score + jnp.maximum(s, 0.0) * wit[h:h + 1, :]
        causal = rowi + (c - i) * kc <= coli
        sc_ref[c] = jnp.where(causal, score, -jnp.inf)
        return carry

    lax.fori_loop(0, nkc, score_chunk, 0)

    def key_sum(f):
        def body(c, acc):
            ind = f(sc_ref[c], c * kc)
            return acc + jnp.sum(ind.reshape(ng, 32, tq), axis=0)
        acc = lax.fori_loop(0, nkc, body, jnp.zeros((32, tq), F32))
        return jnp.sum(acc, axis=0, keepdims=True).astype(I32)

    def count_ge(cand):
        return key_sum(lambda s, off: jnp.where(s >= cand, 1.0, 0.0))

    def count_gt_eq(t):
        return (key_sum(lambda s, off: jnp.where(s > t, 1.0, 0.0)),
                key_sum(lambda s, off: jnp.where(s == t, 1.0, 0.0)))

    def count_eq_below(t, j):
        return key_sum(lambda s, off: jnp.where(s == t, jnp.where(rowi + off < j, 1.0, 0.0), 0.0))

    t, x = _topk_threshold(count_ge, count_gt_eq, count_eq_below, (1, tq), k_top, 12)

    def bias_chunk(c, carry):
        s = sc_ref[c]
        tie = jnp.where(s == t, jnp.where(rowi + c * kc <= x, 0.0, NEG_BIG), NEG_BIG)
        bias = jnp.where(s > t, 0.0, tie)
        bias_ref[c] = jnp.where(s > -jnp.inf, bias, NEG_BIG)
        return carry

    lax.fori_loop(0, nkc, bias_chunk, 0)

    m_ref[...] = jnp.full(m_ref.shape, NEG_BIG, F32)
    acc_ref[...] = jnp.zeros(acc_ref.shape, F32)
    ones_rows = jnp.ones((_ACC_ROWS - HD_B, kc), BF16)

    def attn_chunk(c, carry):
        kk = k_ref[0, c]
        bias = bias_ref[c]
        vt = vt_ref[0, c]
        vaug = [jnp.concatenate([vt[HD_B * n:HD_B * (n + 1)], ones_rows], axis=0) for n in range(N_KV_B)]
        logits = [jnp.dot(kk, qtp_ref[0, h], preferred_element_type=F32) for h in range(H_B)]
        for h in range(H_B):
            n = h // (H_B // N_KV_B)
            s = bias + logits[h]
            m_old = m_ref[h:h + 1, :]
            m_new = jnp.maximum(m_old, jnp.max(s, axis=0, keepdims=True))
            alpha = jnp.exp(m_old - m_new)
            pexp = jnp.exp(s - m_new).astype(BF16)
            acc_ref[h] = acc_ref[h] * alpha + jnp.dot(vaug[n], pexp, preferred_element_type=F32)
            m_ref[h:h + 1, :] = m_new
        return carry

    lax.fori_loop(0, nkc, attn_chunk, 0)
    for p in range(H_B // 2):
        halves = []
        for h in (2 * p, 2 * p + 1):
            acc = acc_ref[h]
            halves.append(acc[:HD_B] / acc[HD_B:HD_B + 1])
        o_ref[:, 128 * p:128 * (p + 1)] = jnp.concatenate(halves, axis=0).T.astype(BF16)


def _dsa_prompt(qit, wit, qtp, kibf, kbf, vt, *, b, t, tq, k_top):
    kc = tq
    nq = t // tq
    nkc = t // kc
    whole = lambda *s: pl.BlockSpec((1,) + s, lambda i, j: (i,) + (0,) * len(s))
    return pl.pallas_call(
        functools.partial(_dsa_body, tq=tq, kc=kc, k_top=k_top),
        out_shape=jax.ShapeDtypeStruct((b * t, H_B * HD_B), BF16),
        grid=(b, nq),
        in_specs=[pl.BlockSpec((1, H_IDX, D_IDX, tq), lambda i, j: (i, 0, 0, j)),
                  pl.BlockSpec((1, H_IDX, tq), lambda i, j: (i, 0, j)),
                  pl.BlockSpec((1, H_B, 128, tq), lambda i, j: (i, 0, 0, j)),
                  whole(nkc, kc, D_IDX), whole(nkc, kc, 128), whole(nkc, 128, kc)],
        out_specs=pl.BlockSpec((tq, H_B * HD_B), lambda i, j: (i * nq + j, 0)),
        scratch_shapes=[pltpu.VMEM((nkc, kc, tq), F32), pltpu.VMEM((nkc, kc, tq), F32),
                        pltpu.VMEM((H_B, tq), F32), pltpu.VMEM((H_B, _ACC_ROWS, tq), F32)],
        compiler_params=_cparams(2),
        name="dsa_prompt",
    )(qit, wit, qtp, kibf.reshape(b, nkc, kc, D_IDX), kbf.reshape(b, nkc, kc, 128), vt)


def _merge_body(x_ref, oa_ref, ob_ref, g1_ref, wg_ref, woa_ref, wob_ref, wo_ref, h_ref):
    x = x_ref[...]
    xn = _rms_rows(x, g1_ref[...]).astype(BF16)
    gates = jax.nn.sigmoid(jnp.dot(xn, wg_ref[...], preferred_element_type=F32))
    a = jnp.dot(oa_ref[...], woa_ref[...], preferred_element_type=F32)
    bb = jnp.dot(ob_ref[...], wob_ref[...], preferred_element_type=F32)
    mix = gates[:, :D_MODEL] * a + gates[:, D_MODEL:] * bb
    h_ref[...] = x + jnp.dot(mix.astype(BF16), wo_ref[...], preferred_element_type=F32)


def _merge(x, oa, ob, wob, w, *, tm):
    m, d = x.shape
    nb = ob.shape[1]
    row = lambda n: pl.BlockSpec((tm, n), lambda i: (i, 0))
    return pl.pallas_call(
        _merge_body,
        out_shape=jax.ShapeDtypeStruct((m, d), F32),
        grid=(m // tm,),
        in_specs=[row(d), row(512), row(nb), _const_spec((1, d)), _const_spec((d, 2 * d)),
                  _const_spec((512, d)), _const_spec((nb, d)), _const_spec((d, d))],
        out_specs=row(d),
        compiler_params=_cparams(1),
        name="merge",
    )(x, oa, ob, w["g1"], w["wg"], w["woa"], wob, w["wo"])


def _mlp_body(h_ref, g2_ref, wup_ref, wdn_ref, y_ref):
    h = h_ref[...]
    hn = _rms_rows(h, g2_ref[...]).astype(BF16)
    up = jnp.dot(hn, wup_ref[...], preferred_element_type=F32)
    act = jnp.square(jnp.maximum(up, 0.0)).astype(BF16)
    y_ref[...] = h + jnp.dot(act, wdn_ref[...], preferred_element_type=F32)


def _mlp(h, w, *, tm):
    m, d = h.shape
    row = pl.BlockSpec((tm, d), lambda i: (i, 0))
    return pl.pallas_call(
        _mlp_body,
        out_shape=jax.ShapeDtypeStruct((m, d), F32),
        grid=(m // tm,),
        in_specs=[row, _const_spec((1, d)), _const_spec((d, D_FF)), _const_spec((D_FF, d))],
        out_specs=row,
        compiler_params=_cparams(1),
        name="mlp",
    )(h, w["g2"], w["wup"], w["wdn"])


def _proj_decode_body(x_ref, st_ref, g1_ref, wqkv_ref, wz_ref, wbq_ref, wkv_ref, wiq_ref, wsm_ref,
                      convw_ref, qng_ref, kng_ref, alog_ref, dtb_ref, ones_ref, seg_ref, cos_ref, sin_ref,
                      u_out, q_out, k_out, v_out, z_out, qb_out, kv_out, qi_out, sm_out, ss_out):
    x = x_ref[...]
    xn = _rms_rows(x, g1_ref[...]).astype(BF16)
    u = jnp.dot(xn, wqkv_ref[...], preferred_element_type=F32)
    u_out[...] = u
    cw = convw_ref[...]
    acc = st_ref[0] * cw[0:1] + st_ref[1] * cw[1:2] + st_ref[2] * cw[2:3] + u * cw[3:4]
    c = _silu(acc)
    for h in range(H_A):
        qh = c[:, h * DK_A:(h + 1) * DK_A]
        kh = c[:, (H_A + h) * DK_A:(H_A + h + 1) * DK_A]
        q_out[:, h * DK_A:(h + 1) * DK_A] = qh * (lax.rsqrt(jnp.sum(qh * qh, -1, keepdims=True) + EPS)
                                                 * (DK_A ** -0.5))
        k_out[:, h * DK_A:(h + 1) * DK_A] = kh * lax.rsqrt(jnp.sum(kh * kh, -1, keepdims=True) + EPS)
    v_out[...] = c[:, 2 * H_A * DK_A:]
    z_out[...] = jnp.dot(xn, wz_ref[...], preferred_element_type=F32)
    cos, sin = cos_ref[...], sin_ref[...]
    ones_bd = ones_ref[...]
    bq = jnp.dot(xn, wbq_ref[...], preferred_element_type=F32)
    qb = _rope_rows(_seg_rms(bq, ones_bd, qng_ref[...]), cos, sin)
    qb_out[...] = (qb * (HD_B ** -0.5)).astype(BF16)
    kv = jnp.dot(xn, wkv_ref[...], preferred_element_type=F32)
    kb = _rope_rows(_seg_rms(kv[:, :128], ones_bd[:128, :128], kng_ref[...]), cos, sin)
    kv_out[:, 0:128] = kb
    kv_out[:, 128:256] = kv[:, 128:]
    iq = jnp.dot(xn, wiq_ref[...], preferred_element_type=F32)
    qi = (_rope_rows(iq, cos, sin) * (D_IDX ** -0.5)).astype(BF16)
    qi_out[...] = qi
    sm = jnp.dot(xn, wsm_ref[...], preferred_element_type=F32)
    smo = _small_block(sm, cos, sin, alog_ref[...], dtb_ref[...])
    sm_out[...] = smo
    lane = lax.broadcasted_iota(I32, smo.shape, 1)
    ki = smo.astype(BF16).astype(F32)
    ki2 = jnp.where(lane < D_IDX, ki, pltpu.roll(ki, D_IDX, 1))
    prod = qi.astype(F32) * jnp.concatenate([ki2] * 4, axis=-1)
    sh = _mm_f32(prod, seg_ref[...])
    contrib = jnp.where(jnp.logical_and(lane >= _SM_WI, lane < _SM_WI + H_IDX),
                        jnp.maximum(sh, 0.0) * smo, 0.0)
    ss_out[...] = jnp.broadcast_to(jnp.sum(contrib, axis=-1, keepdims=True), ss_out.shape)


def _proj_decode(x, st, w, tabs):
    r, d = x.shape
    outs = [(r, CONV_CH, F32), (r, 512, F32), (r, 512, F32), (r, 512, F32), (r, 512, F32), (r, 512, BF16),
            (r, 256, F32), (r, 512, BF16), (r, 128, F32), (r, 128, F32)]
    args = (x, st, w["g1"], w["wqkv"], w["wz"], w["wbq"], w["wkv"], w["wiq"], w["wsm"], w["convw"],
            w["qng"], w["kng"], w["alog_row"], w["dtb_row"], w["ones_bd"], w["seg_sum"], tabs["cos"], tabs["sin"])
    return pl.pallas_call(
        _proj_decode_body,
        out_shape=[jax.ShapeDtypeStruct((a, n), dt) for a, n, dt in outs],
        compiler_params=pltpu.CompilerParams(vmem_limit_bytes=VMEM_LIMIT),
        name="proj_decode",
    )(*args)


def _gdn_decode_body(qc_ref, kc_ref, v_ref, z_ref, sm_ref, gg_ref, s_ref, o_ref, so_ref):
    sm = sm_ref[0]
    gg = gg_ref[...]
    for h in range(H_A):
        g = sm[:, _SM_G + h:_SM_G + h + 1]
        beta = sm[:, _SM_BETA + h:_SM_BETA + h + 1]
        kcol, qcol = kc_ref[0, h], qc_ref[0, h]
        v = v_ref[0, h]
        s = s_ref[0, h] * jnp.exp(g)
        vn = beta * (v - jnp.sum(s * kcol, axis=0, keepdims=True))
        s = s + kcol * vn
        so_ref[0, h] = s
        o = jnp.sum(s * qcol, axis=0, keepdims=True)
        o_ref[0, h] = (_rms_rows(o, gg) * _silu(z_ref[0, h])).astype(BF16)


def _gdn_decode(q, k, v, z, sm, gg, state):
    r = q.shape[0]
    col = lambda a: a.reshape(r, H_A, DK_A, 1)
    rowv = lambda a: a.reshape(r, H_A, 1, DV_A)
    cspec = pl.BlockSpec((1, H_A, DK_A, 1), lambda i: (i, 0, 0, 0))
    rspec = pl.BlockSpec((1, H_A, 1, DV_A), lambda i: (i, 0, 0, 0))
    sspec = pl.BlockSpec((1, H_A, DK_A, DV_A), lambda i: (i, 0, 0, 0))
    o, s_new = pl.pallas_call(
        _gdn_decode_body,
        out_shape=[jax.ShapeDtypeStruct((r, H_A, 1, DV_A), BF16),
                   jax.ShapeDtypeStruct((r, H_A, DK_A, DV_A), F32)],
        grid=(r,),
        in_specs=[cspec, cspec, rspec, rspec, pl.BlockSpec((1, 1, 128), lambda i: (i, 0, 0)),
                  _const_spec((1, DV_A)), sspec],
        out_specs=[rspec, sspec],
        compiler_params=_cparams(1),
        name="gdn_decode",
    )(col(q), col(k), rowv(v), rowv(z), sm.reshape(r, 1, 128), gg, state)
    return o.reshape(r, H_A * DV_A), s_new


def _idx_decode_body(pt_ref, qi_ref, w_ref, *refs, npg):
    pages, out_ref = refs[:npg], refs[npg]
    qi = qi_ref[0]
    w = w_ref[0]
    for p in range(npg):
        s = _mm(qi, pages[p][0])
        out_ref[0, :, PAGE_SIZE * p:PAGE_SIZE * (p + 1)] = jnp.sum(jnp.maximum(s, 0.0) * w, axis=0,
                                                                   keepdims=True)


def _idx_decode(page_table, qi, wi_col, cache_kidx, *, npg):
    r, n_pages = page_table.shape
    page_spec = lambda p: pl.BlockSpec((1, D_IDX, PAGE_SIZE), lambda i, j, pt: (pt[i, j * npg + p], 0, 0))
    grid_spec = pltpu.PrefetchScalarGridSpec(
        num_scalar_prefetch=1,
        grid=(r, n_pages // npg),
        in_specs=[pl.BlockSpec((1, H_IDX, D_IDX), lambda i, j, pt: (i, 0, 0)),
                  pl.BlockSpec((1, H_IDX, 1), lambda i, j, pt: (i, 0, 0))]
                 + [page_spec(p) for p in range(npg)],
        out_specs=pl.BlockSpec((1, 1, PAGE_SIZE * npg), lambda i, j, pt: (i, 0, j)),
    )
    return pl.pallas_call(
        functools.partial(_idx_decode_body, npg=npg),
        out_shape=jax.ShapeDtypeStruct((r, 1, n_pages * PAGE_SIZE), F32),
        grid_spec=grid_spec,
        compiler_params=_cparams(2),
        name="idx_decode",
    )(page_table, qi.reshape(r, H_IDX, D_IDX), wi_col, *([cache_kidx] * npg))


def _select_decode_body(sc_ref, ss_ref, bias_ref, bself_ref, *, k_top, cw):
    r, n = sc_ref.shape
    nch = n // cw
    key_ref = sc_ref
    skey = ss_ref[:, 0:1]

    def lane_sum(f):
        def body(c, acc):
            off = pl.multiple_of(c * cw, cw)
            kk = key_ref[:, pl.ds(off, cw)]
            for j in range(cw // 128):
                acc = acc + f(kk[:, 128 * j:128 * (j + 1)], off + 128 * j)
            return acc
        acc = lax.fori_loop(0, nch, body, jnp.zeros((r, 128), F32))
        return jnp.sum(acc, axis=-1, keepdims=True).astype(I32)

    def count_ge(cand):
        cb = jnp.broadcast_to(cand, (r, 128))
        return lane_sum(lambda kk, off: jnp.where(kk >= cb, 1.0, 0.0)) + (skey >= cand).astype(I32)

    def count_gt_eq(t):
        tb = jnp.broadcast_to(t, (r, 128))
        return (lane_sum(lambda kk, off: jnp.where(kk > tb, 1.0, 0.0)) + (skey > t).astype(I32),
                lane_sum(lambda kk, off: jnp.where(kk == tb, 1.0, 0.0)) + (skey == t).astype(I32))

    def count_eq_below(t, j):
        tb = jnp.broadcast_to(t, (r, 128))
        jb = jnp.broadcast_to(j, (r, 128))
        lane = lax.broadcasted_iota(I32, (r, 128), 1)
        return (lane_sum(lambda kk, off: jnp.where(kk == tb, jnp.where(lane + off < jb, 1.0, 0.0), 0.0))
                + jnp.logical_and(skey == t, j > n).astype(I32))

    t, x = _topk_threshold(count_ge, count_gt_eq, count_eq_below, (r, 1), k_top, 15)
    tb = jnp.broadcast_to(t, (r, cw))
    xb = jnp.broadcast_to(x, (r, cw))

    def bias_chunk(c, carry):
        off = pl.multiple_of(c * cw, cw)
        kk = key_ref[:, pl.ds(off, cw)]
        idx = lax.broadcasted_iota(I32, (r, cw), 1) + off
        tie = jnp.where(kk == tb, jnp.where(idx <= xb, 0.0, NEG_BIG), NEG_BIG)
        bias_ref[:, pl.ds(off, cw)] = jnp.where(kk > tb, 0.0, tie)
        return carry

    lax.fori_loop(0, nch, bias_chunk, 0)
    tie_self = jnp.where(skey == t, jnp.where(x >= n, 0.0, NEG_BIG), NEG_BIG)
    bself_ref[...] = jnp.broadcast_to(jnp.where(skey > t, 0.0, tie_self), bself_ref.shape)


def _select_decode(scores, sself, *, k_top):
    r, n = scores.shape
    return pl.pallas_call(
        functools.partial(_select_decode_body, k_top=k_top, cw=512),
        out_shape=[jax.ShapeDtypeStruct((r, n), F32), jax.ShapeDtypeStruct((r, 128), F32)],
        compiler_params=pltpu.CompilerParams(vmem_limit_bytes=VMEM_LIMIT),
        name="select_decode",
    )(scores, sself)


def _attn_decode_body(pt_ref, qp_ref, bias_ref, kvs_ref, bself_ref, *refs, npg, nsteps):
    pages, o_ref, m_ref, l_ref, acc_ref = refs[:npg], refs[npg], refs[npg + 1], refs[npg + 2], refs[npg + 3]
    j = pl.program_id(1)

    @pl.when(j == 0)
    def _():
        m_ref[...] = jnp.full(m_ref.shape, NEG_BIG, F32)
        l_ref[...] = jnp.zeros(l_ref.shape, F32)
        acc_ref[...] = jnp.zeros(acc_ref.shape, F32)

    qp = qp_ref[0]

    ss = [_mm(qp, pages[p][0, 0:128, :]) + bias_ref[0, :, PAGE_SIZE * p:PAGE_SIZE * (p + 1)]
          for p in range(npg)]
    m_old = m_ref[...]
    m_new = m_old
    for s in ss:
        m_new = jnp.maximum(m_new, jnp.max(s, axis=-1, keepdims=True))
    alpha = jnp.exp(m_old - m_new)
    l_new = l_ref[...] * alpha
    acc_new = acc_ref[...] * alpha
    for p in range(npg):
        pe = jnp.exp(ss[p] - m_new)
        l_new = l_new + jnp.sum(pe, axis=-1, keepdims=True)
        acc_new = acc_new + _mm_nt(pe, pages[p][0, 128:256, :])
    m_ref[...] = m_new
    l_ref[...] = l_new
    acc_ref[...] = acc_new

    @pl.when(j == nsteps - 1)
    def _():
        kvs = kvs_ref[0]
        krow = kvs[:, :128].astype(BF16).astype(F32)
        s = jnp.sum(qp.astype(F32) * krow, axis=-1, keepdims=True) + bself_ref[0][:, 0:1]
        m_old = m_ref[...]
        m_new = jnp.maximum(m_old, s)
        alpha = jnp.exp(m_old - m_new)
        p = jnp.exp(s - m_new)
        l = l_ref[...] * alpha + p
        vrow = kvs[:, 128:].astype(BF16).astype(F32)
        acc = acc_ref[...] * alpha + p.astype(BF16).astype(F32) * vrow
        out = acc / l
        rowi = lax.broadcasted_iota(I32, out.shape, 0)
        out = jnp.where(rowi < 4, out, pltpu.roll(out, HD_B, 1))
        o_ref[0] = out.astype(BF16)


def _attn_decode(page_table, qpad, bias, kv_self, bself, cache_kv, *, npg):
    r, n_pages = page_table.shape
    nsteps = n_pages // npg
    page_spec = lambda p: pl.BlockSpec((1, 256, PAGE_SIZE), lambda i, j, pt: (pt[i, j * npg + p], 0, 0))
    grid_spec = pltpu.PrefetchScalarGridSpec(
        num_scalar_prefetch=1,
        grid=(r, nsteps),
        in_specs=[pl.BlockSpec((1, H_B, 128), lambda i, j, pt: (i, 0, 0)),
                  pl.BlockSpec((1, 1, PAGE_SIZE * npg), lambda i, j, pt: (i, 0, j)),
                  pl.BlockSpec((1, 1, 256), lambda i, j, pt: (i, 0, 0)),
                  pl.BlockSpec((1, 1, 128), lambda i, j, pt: (i, 0, 0))]
                 + [page_spec(p) for p in range(npg)],
        out_specs=pl.BlockSpec((1, H_B, 128), lambda i, j, pt: (i, 0, 0)),
        scratch_shapes=[pltpu.VMEM((H_B, 1), F32), pltpu.VMEM((H_B, 1), F32), pltpu.VMEM((H_B, 128), F32)],
    )
    return pl.pallas_call(
        functools.partial(_attn_decode_body, npg=npg, nsteps=nsteps),
        out_shape=jax.ShapeDtypeStruct((r, H_B, 128), BF16),
        grid_spec=grid_spec,
        compiler_params=_cparams(2),
        name="attn_decode",
    )(page_table, qpad, bias.reshape(r, 1, -1), kv_self.reshape(r, 1, 256), bself.reshape(r, 1, 128),
      *([cache_kv] * npg))


def _prep_weights(norm1_g, w_in, conv_w, a_log, dt_bias, gdn_norm_g, q_norm_g, k_norm_g,
                  w_out_a, w_out_b, w_o, norm2_g, w_up, w_down):
    o = _OFF
    wb = w_in.astype(BF16)
    d = w_in.shape[0]
    wsm = jnp.concatenate([wb[:, o["ik"]:o["iw"]], wb[:, o["ab"]:o["bq"]], wb[:, o["iw"]:o["ga"]],
                           jnp.zeros((d, 128 - 80), BF16)], axis=1)

    def lanes(vals, start):
        return jnp.zeros((1, 128), F32).at[0, start:start + vals.shape[0]].set(vals.astype(F32))

    seg = np.zeros((512, 128), np.float32)
    for h in range(H_IDX):
        seg[64 * h:64 * (h + 1), _SM_WI + h] = 1.0
    ones_bd = np.kron(np.eye(8, dtype=np.float32), np.ones((64, 64), np.float32))
    wob = w_out_b.astype(BF16).reshape(H_B, HD_B, d)
    wob_pad = jnp.concatenate([wob, jnp.zeros_like(wob)], axis=1).reshape(H_B * 128, d)
    col8 = lambda v: jnp.concatenate([jnp.zeros((4,), F32), v.astype(F32)]).reshape(8, 1)
    return dict(
        g1=norm1_g.reshape(1, d).astype(F32),
        wqkv=wb[:, o["aq"]:o["az"]], wz=wb[:, o["az"]:o["ab"]], wbq=wb[:, o["bq"]:o["bk"]],
        wkv=wb[:, o["bk"]:o["iq"]], wiq=wb[:, o["iq"]:o["ik"]], wsm=wsm,
        wbqt=wb[:, o["bq"]:o["bk"]].T, wiqt=wb[:, o["iq"]:o["ik"]].T, wvt=wb[:, o["bv"]:o["iq"]].T,
        wsmt=wsm.T, wg=wb[:, o["ga"]:o["end"]],
        convw=conv_w.astype(F32),
        qng=jnp.tile(q_norm_g.astype(F32), H_B).reshape(1, 512),
        kng=jnp.tile(k_norm_g.astype(F32), N_KV_B).reshape(1, 128),
        qngt=jnp.tile(q_norm_g.astype(F32).reshape(HD_B, 1), (1, 128)),
        alog_row=lanes(a_log, _SM_G), dtb_row=lanes(dt_bias, _SM_G),
        alog_col=col8(a_log), dtb_col=col8(dt_bias),
        ones_bd=jnp.asarray(ones_bd, BF16), ones_bd128=jnp.asarray(ones_bd[:128, :128], BF16),
        seg_sum=jnp.asarray(seg),
        gg=gdn_norm_g.reshape(1, DV_A).astype(F32),
        woa=w_out_a.astype(BF16), wob=w_out_b.astype(BF16), wob_pad=wob_pad, wo=w_o.astype(BF16),
        g2=norm2_g.reshape(1, d).astype(F32), wup=w_up.astype(BF16), wdn=w_down.astype(BF16),
    )


def _rope_tables(pos):
    half = HD_B // 2
    inv = ROPE_THETA ** (-jnp.arange(half, dtype=F32) / half)
    ang = pos.astype(F32)[:, None] * inv[None, :]
    cos, sin = jnp.cos(ang), jnp.sin(ang)
    return dict(cos=jnp.tile(cos, (1, 4)), sin=jnp.tile(jnp.concatenate([-sin, sin], axis=1), (1, 2)),
                cos_t=cos.T, sin_t=sin.T)


def _layer_prompt(x, w, *, tm=512, tq=256, nb=4):
    b, t, d = x.shape
    tabs = _rope_tables(jnp.arange(t))
    (q, k, v, z, kv, kbf, kibf, sm, kidx, qtp, qit, vt, gt, wit, conv_new) = _proj_prompt(x, w, tabs, tm=tm, kc=tq)
    oa, ssm = _gdn_prompt(q, k, v, z, sm, gt, w["gg"], b=b, t=t, nb=nb)
    ob = _dsa_prompt(qit, wit, qtp, kibf, kbf, vt, b=b, t=t, tq=tq, k_top=min(TOPK_MAX, t // 4))
    h = _merge(x.reshape(b * t, d), oa.reshape(b * t, 512), ob, w["wob"], w, tm=tm)
    y = _mlp(h, w, tm=tm)
    return (y.reshape(b, t, d), kv.reshape(b, t, 2, N_KV_B, HD_B), kidx.reshape(b, t, D_IDX), ssm, conv_new)


def _layer_decode(x, w, state_ssm, state_conv, cache_kv, cache_kidx, page_table, *, npg=16):
    r, d = x.shape
    n_pages = page_table.shape[1]
    past = n_pages * PAGE_SIZE
    tabs = _rope_tables(jnp.full((1,), past))
    st = jnp.swapaxes(state_conv, 0, 1)
    (u, q, k, v, z, qb, kv, qi, sm, sself) = _proj_decode(x, st, w, tabs)
    oa, ssm = _gdn_decode(q, k, v, z, sm, w["gg"], state_ssm)
    wi_col = sm[:, _SM_WI:_SM_WI + H_IDX].reshape(r, H_IDX, 1)
    kidx_pages = jnp.swapaxes(cache_kidx, 1, 2)
    kv_pages = jnp.transpose(cache_kv, (0, 2, 3, 4, 1)).reshape(-1, 2 * N_KV_B * HD_B, PAGE_SIZE)
    scores = _idx_decode(page_table, qi, wi_col, kidx_pages, npg=npg).reshape(r, past)
    bias, bself = _select_decode(scores, sself, k_top=min(TOPK_MAX, (past + 1) // 4))
    qh = qb.reshape(r, H_B, HD_B)
    zq = jnp.zeros_like(qh[:, :4])
    qpad = jnp.concatenate([jnp.concatenate([qh[:, :4], zq], axis=-1),
                            jnp.concatenate([zq, qh[:, 4:]], axis=-1)], axis=1)
    ob = _attn_decode(page_table, qpad, bias, kv, bself, kv_pages, npg=npg)
    h = _merge(x, oa, ob.reshape(r, H_B * 128), w["wob_pad"], w, tm=r)
    y = _mlp(h, w, tm=r)
    conv_new = jnp.concatenate([state_conv[:, 1:], u[:, None, :]], axis=1)
    return (y, kv.reshape(r, 1, 2, N_KV_B, HD_B), sm[:, :D_IDX].reshape(r, 1, D_IDX), ssm, conv_new)


def kernel(x_prompt, x_sample, cache_kv, cache_kidx, page_table, state_ssm, state_conv, norm1_g, w_in, conv_w,
           a_log, dt_bias, gdn_norm_g, q_norm_g, k_norm_g, w_out_a, w_out_b, w_o, norm2_g, w_up, w_down):
    assert w_in.shape[0] == 1, "single-layer trunk"
    w = _prep_weights(norm1_g[0], w_in[0], conv_w[0], a_log[0], dt_bias[0], gdn_norm_g[0], q_norm_g[0],
                      k_norm_g[0], w_out_a[0], w_out_b[0], w_o[0], norm2_g[0], w_up[0], w_down[0])
    yp, kv_p, ki_p, ssm_p, cv_p = _layer_prompt(x_prompt, w)
    ys, kv_s, ki_s, ssm_s, cv_s = _layer_decode(x_sample[:, 0], w, state_ssm[0], state_conv[0],
                                                cache_kv[0], cache_kidx[0], page_table)
    return (yp, ys[:, None, :], kv_p[None], ki_p[None], ssm_p[None], cv_p[None],
            kv_s[None], ki_s[None], ssm_s[None], cv_s[None])
```

```python
import functools
import math

import jax
import jax.numpy as jnp
import numpy as np
from jax import lax
from jax.experimental import pallas as pl
from jax.experimental.pallas import tpu as pltpu

F32 = jnp.float32
BF16 = jnp.bfloat16
I32 = jnp.int32

D_MODEL = 1024
H_A, DK_A, DV_A = 4, 128, 128
CONV_W = 4
CONV_CH = H_A * (2 * DK_A + DV_A)
CHUNK = 64
HD_B, H_B, N_KV_B = 64, 8, 2
H_IDX, D_IDX = 8, 64
TOPK_MAX = 256
PAGE_SIZE = 128
ROPE_THETA = 10000.0
D_FF = 4 * D_MODEL
EPS = 1e-6

_OFF = dict(aq=0, ak=512, av=1024, az=1536, ab=2048, aa=2052, bq=2056, bk=2568, bv=2696,
            iq=2824, ik=3336, iw=3400, ga=3408, gb=4432, end=5456)
_SM_BETA, _SM_G, _SM_WI = 64, 68, 72

LOG2_E = math.log2(math.e)
NEG_BIG = -1e30
INT_MIN = -(2 ** 31)
NEG_INF_KEY = (0xFF800000 ^ 0x7FFFFFFF) - 2 ** 32

VMEM_LIMIT = 56 * 1024 * 1024
HIGHEST = lax.Precision.HIGHEST


def _cparams(n_axes):
    return pltpu.CompilerParams(dimension_semantics=("arbitrary",) * n_axes,
                                vmem_limit_bytes=VMEM_LIMIT)


def _const_spec(shape):
    nd = len(shape)
    return pl.BlockSpec(shape, lambda *a: (0,) * nd, pipeline_mode=pl.Buffered(1))


def _mm(a, b):
    return jnp.dot(a.astype(BF16), b.astype(BF16), preferred_element_type=F32)


def _mm_nt(a, b):
    return lax.dot_general(a.astype(BF16), b.astype(BF16), (((1,), (1,)), ((), ())),
                           preferred_element_type=F32)


def _mm_tn(a, b):
    return lax.dot_general(a.astype(BF16), b.astype(BF16), (((0,), (0,)), ((), ())),
                           preferred_element_type=F32)


def _mm_f32(a, b):
    return jnp.dot(a, b, preferred_element_type=F32, precision=HIGHEST)


def _rms_rows(x, g):
    return x * lax.rsqrt(jnp.mean(x * x, axis=-1, keepdims=True) + EPS) * g


def _silu(x):
    return x * jax.nn.sigmoid(x)


def _softplus(x):
    return jnp.maximum(x, 0.0) + jnp.log1p(jnp.exp(-jnp.abs(x)))


def _rope_rows(x, cos, sin_signed):
    n = x.shape[-1]
    reps = n // 128
    if reps > 1:
        cos = jnp.concatenate([cos] * reps, axis=-1)
        sin_signed = jnp.concatenate([sin_signed] * reps, axis=-1)
    lane = lax.broadcasted_iota(I32, x.shape, x.ndim - 1)
    first_half = (lane % HD_B) < (HD_B // 2)
    partner = jnp.where(first_half, pltpu.roll(x, n - HD_B // 2, x.ndim - 1),
                        pltpu.roll(x, HD_B // 2, x.ndim - 1))
    return x * cos + partner * sin_signed


def _rope_cols(x, cos_t, sin_t):
    x1, x2 = x[:32], x[32:]
    return jnp.concatenate([x1 * cos_t - x2 * sin_t, x2 * cos_t + x1 * sin_t], axis=0)


def _seg_rms(x, ones_bd, g):
    sq = x * x
    hi = sq.astype(BF16)
    lo = (sq - hi.astype(F32)).astype(BF16)
    ss = (jnp.dot(hi, ones_bd, preferred_element_type=F32)
          + jnp.dot(lo, ones_bd, preferred_element_type=F32))
    return x * lax.rsqrt(ss * (1.0 / HD_B) + EPS) * g


def _small_block(sm, cos, sin_signed, alog_row, dtb_row):
    lane = lax.broadcasted_iota(I32, sm.shape, 1)
    roped = _rope_rows(sm, cos, sin_signed)
    beta = jax.nn.sigmoid(sm)
    g = -jnp.exp(alog_row) * _softplus(sm + dtb_row)
    wi = sm * (H_IDX ** -0.5)
    out = jnp.where(lane < _SM_BETA, roped,
                    jnp.where(lane < _SM_G, beta, jnp.where(lane < _SM_WI, g, wi)))
    return out


def _proj_prompt_body(x_ref, g1_ref, wqkv_ref, wz_ref, wkv_ref, wsm_ref,
                      wbqt_ref, wiqt_ref, wvt_ref, wsmt_ref, convw_ref, kng_ref, qngt_ref, alog_ref, dtb_ref,
                      alogt_ref, dtbt_ref, ones_ref, cos_ref, sin_ref, cost_ref, sint_ref,
                      q_out, k_out, v_out, z_out, kv_out, kbf_out, kibf_out, sm_out, kidx_out,
                      qtp_out, qit_out, vt_out, gt_out, wit_out, conv_out, ubuf, *, tm, kc):
    t = pl.program_id(1)
    x = x_ref[0]
    xn = _rms_rows(x, g1_ref[...]).astype(BF16)

    u = jnp.dot(xn, wqkv_ref[...], preferred_element_type=F32)

    @pl.when(t == 0)
    def _():
        ubuf[0:8, :] = jnp.zeros((8, CONV_CH), F32)

    ubuf[8:tm + 8, :] = u
    cw = convw_ref[...]
    acc = (ubuf[5:5 + tm, :] * cw[0:1] + ubuf[6:6 + tm, :] * cw[1:2]
           + ubuf[7:7 + tm, :] * cw[2:3] + u * cw[3:4])
    conv_out[0] = ubuf[tm + 5:tm + 8, :]
    ubuf[0:8, :] = ubuf[tm:tm + 8, :]
    c = _silu(acc)
    for h in range(H_A):
        qh = c[:, h * DK_A:(h + 1) * DK_A]
        kh = c[:, (H_A + h) * DK_A:(H_A + h + 1) * DK_A]
        q_out[:, h * DK_A:(h + 1) * DK_A] = qh * (lax.rsqrt(jnp.sum(qh * qh, -1, keepdims=True) + EPS)
                                                 * (DK_A ** -0.5))
        k_out[:, h * DK_A:(h + 1) * DK_A] = kh * lax.rsqrt(jnp.sum(kh * kh, -1, keepdims=True) + EPS)
    v_out[...] = c[:, 2 * H_A * DK_A:]
    z_out[...] = jnp.dot(xn, wz_ref[...], preferred_element_type=F32)

    cos, sin = cos_ref[...], sin_ref[...]
    ones_bd = ones_ref[...]

    kv = jnp.dot(xn, wkv_ref[...], preferred_element_type=F32)
    kb = _rope_rows(_seg_rms(kv[:, :128], ones_bd, kng_ref[...]), cos, sin)
    kv_out[:, 0:128] = kb
    kv_out[:, 128:256] = kv[:, 128:]
    kbf_out[...] = kb.astype(BF16)

    sm = jnp.dot(xn, wsm_ref[...], preferred_element_type=F32)
    smo = _small_block(sm, cos, sin, alog_ref[...], dtb_ref[...])
    sm_out[...] = smo
    kidx_out[...] = smo[:, :D_IDX]
    kibf_out[...] = smo[:, :D_IDX].astype(BF16)

    nt_dims = (((1,), (1,)), ((), ()))
    cos_t, sin_t = cost_ref[...], sint_ref[...]
    qt = lax.dot_general(wbqt_ref[...], xn, nt_dims, preferred_element_type=F32)
    qit = lax.dot_general(wiqt_ref[...], xn, nt_dims, preferred_element_type=F32)
    vt = lax.dot_general(wvt_ref[...], xn, nt_dims, preferred_element_type=F32)
    smt = lax.dot_general(wsmt_ref[...], xn, nt_dims, preferred_element_type=F32)
    qg = jnp.concatenate([qngt_ref[...]] * (tm // 128), axis=1)
    zero64 = jnp.zeros((HD_B, tm), BF16)
    for h in range(H_B):
        qh = qt[HD_B * h:HD_B * (h + 1)]
        qh = qh * lax.rsqrt(jnp.mean(qh * qh, axis=0, keepdims=True) + EPS) * qg
        qh = (_rope_cols(qh, cos_t, sin_t) * (HD_B ** -0.5 * LOG2_E)).astype(BF16)
        n = h // (H_B // N_KV_B)
        qtp_out[0, h, HD_B * n:HD_B * (n + 1), :] = qh
        qtp_out[0, h, HD_B * (1 - n):HD_B * (2 - n), :] = zero64
        qit_out[0, h] = (_rope_cols(qit[D_IDX * h:D_IDX * (h + 1)], cos_t, sin_t) * (D_IDX ** -0.5)).astype(BF16)
    vtb = vt.astype(BF16)
    for j in range(tm // kc):
        vt_out[0, j] = vtb[:, j * kc:(j + 1) * kc]
    sm8 = smt[64:72]
    row = lax.broadcasted_iota(I32, sm8.shape, 0)
    g8 = -jnp.exp(alogt_ref[...]) * _softplus(sm8 + dtbt_ref[...])
    gt_out[0] = jnp.where(row < 4, jax.nn.sigmoid(sm8), g8)
    wit_out[0] = smt[72:80] * (H_IDX ** -0.5)


def _proj_prompt(x, w, tabs, *, tm, kc):
    b, t, d = x.shape
    nt = t // tm
    m = b * t
    row = lambda n: pl.BlockSpec((tm, n), lambda i, j: (i * nt + j, 0))
    ins = [
        pl.BlockSpec((1, tm, d), lambda i, j: (i, j, 0)),
        _const_spec((1, d)),
        _const_spec((d, CONV_CH)), _const_spec((d, 512)), _const_spec((d, 256)), _const_spec((d, 128)),
        _const_spec((512, d)), _const_spec((512, d)), _const_spec((128, d)), _const_spec((128, d)),
        _const_spec((CONV_W, CONV_CH)), _const_spec((1, 128)), _const_spec((HD_B, 128)),
        _const_spec((1, 128)), _const_spec((1, 128)), _const_spec((8, 1)), _const_spec((8, 1)),
        _const_spec((128, 128)),
        pl.BlockSpec((tm, 128), lambda i, j: (j, 0)), pl.BlockSpec((tm, 128), lambda i, j: (j, 0)),
        pl.BlockSpec((32, tm), lambda i, j: (0, j)), pl.BlockSpec((32, tm), lambda i, j: (0, j)),
    ]
    nkc = t // kc
    outs = [
        (jax.ShapeDtypeStruct((m, 512), F32), row(512)),
        (jax.ShapeDtypeStruct((m, 512), F32), row(512)),
        (jax.ShapeDtypeStruct((m, 512), F32), row(512)),
        (jax.ShapeDtypeStruct((m, 512), F32), row(512)),
        (jax.ShapeDtypeStruct((m, 256), F32), row(256)),
        (jax.ShapeDtypeStruct((m, 128), BF16), row(128)),
        (jax.ShapeDtypeStruct((m, D_IDX), BF16), row(D_IDX)),
        (jax.ShapeDtypeStruct((m, 128), F32), row(128)),
        (jax.ShapeDtypeStruct((m, D_IDX), F32), row(D_IDX)),
        (jax.ShapeDtypeStruct((b, H_B, 128, t), BF16),
         pl.BlockSpec((1, H_B, 128, tm), lambda i, j: (i, 0, 0, j))),
        (jax.ShapeDtypeStruct((b, H_IDX, D_IDX, t), BF16),
         pl.BlockSpec((1, H_IDX, D_IDX, tm), lambda i, j: (i, 0, 0, j))),
        (jax.ShapeDtypeStruct((b, nkc, 128, kc), BF16),
         pl.BlockSpec((1, tm // kc, 128, kc), lambda i, j: (i, j, 0, 0))),
        (jax.ShapeDtypeStruct((b, 8, t), F32), pl.BlockSpec((1, 8, tm), lambda i, j: (i, 0, j))),
        (jax.ShapeDtypeStruct((b, H_IDX, t), F32), pl.BlockSpec((1, H_IDX, tm), lambda i, j: (i, 0, j))),
        (jax.ShapeDtypeStruct((b, CONV_W - 1, CONV_CH), F32),
         pl.BlockSpec((1, CONV_W - 1, CONV_CH), lambda i, j: (i, 0, 0))),
    ]
    return pl.pallas_call(
        functools.partial(_proj_prompt_body, tm=tm, kc=kc),
        out_shape=[o[0] for o in outs],
        grid=(b, nt),
        in_specs=ins,
        out_specs=[o[1] for o in outs],
        scratch_shapes=[pltpu.VMEM((tm + 8, CONV_CH), F32)],
        compiler_params=_cparams(2),
        name="proj_prompt",
    )(x, w["g1"], w["wqkv"], w["wz"], w["wkv"], w["wsm"], w["wbqt"], w["wiqt"], w["wvt"], w["wsmt"],
      w["convw"], w["kng"], w["qngt"], w["alog_row"], w["dtb_row"], w["alog_col"], w["dtb_col"],
      w["ones_bd128"], tabs["cos"], tabs["sin"], tabs["cos_t"], tabs["sin_t"])


def _split3(x):
    hi = x.astype(BF16)
    r1 = x - hi.astype(F32)
    mid = r1.astype(BF16)
    lo = (r1 - mid.astype(F32)).astype(BF16)
    return hi, mid, lo


def _gdn_body(q_ref, k_ref, v_ref, z_ref, sm_ref, gt_ref, gg_ref, bdk_ref, bdp_ref, o_ref, s_ref, *, nb):
    c = pl.program_id(1)

    @pl.when(c == 0)
    def _():
        s_ref[...] = jnp.zeros_like(s_ref)

    n, nh = CHUNK, H_A
    w4 = nh * n
    ri4 = lax.broadcasted_iota(I32, (n, w4), 0)
    ci4 = lax.broadcasted_iota(I32, (n, w4), 1) % n
    strict4, incl4 = ri4 > ci4, ri4 >= ci4
    eye4 = jnp.where(ri4 == ci4, 1.0, 0.0)
    triu4 = jnp.where(ri4 <= ci4, 1.0, 0.0).astype(BF16)
    ri = lax.broadcasted_iota(I32, (n, n), 0)
    tril = jnp.where(ri >= lax.broadcasted_iota(I32, (n, n), 1), 1.0, 0.0).astype(BF16)
    blk_row = lax.broadcasted_iota(I32, (1, w4), 1) // n
    low_half = lax.broadcasted_iota(I32, (n, 128), 1) < n
    zeros_pad = jnp.zeros((n, 2 * DV_A), BF16)
    gg = gg_ref[...]
    dot = functools.partial(jnp.dot, preferred_element_type=F32)

    def pad_rows(x, e):
        z = zeros_pad[:, :x.shape[1]]
        return jnp.concatenate([x, z] if e == 0 else [z, x], axis=0)

    seqs = range(nb)
    st = [dict() for _ in seqs]

    for bi in seqs:
        hi, mid, lo = _split3(sm_ref[bi])
        st[bi]["gc_all"] = dot(tril, hi) + dot(tril, mid) + dot(tril, lo)
        hi, mid, lo = _split3(gt_ref[bi, 0])
        st[bi]["grows"] = dot(hi, triu4) + dot(mid, triu4) + dot(lo, triu4)

    for bi in seqs:
        d = st[bi]
        sm, k, grows = sm_ref[bi], k_ref[bi], d["grows"]
        grow = jnp.where(blk_row == 0, grows[4:5], jnp.where(blk_row == 1, grows[5:6],
                         jnp.where(blk_row == 2, grows[6:7], grows[7:8])))
        gcols = [jnp.broadcast_to(d["gc_all"][:, _SM_G + h:_SM_G + h + 1], (n, 128)) for h in range(nh)]
        betas = [jnp.broadcast_to(sm[:, _SM_BETA + h:_SM_BETA + h + 1], (n, 128)) for h in range(nh)]
        d["gc512"] = jnp.concatenate(gcols, axis=1)
        d["beta512"] = jnp.concatenate(betas, axis=1)
        gc256 = jnp.concatenate([jnp.where(low_half, gcols[0], gcols[1]),
                                 jnp.where(low_half, gcols[2], gcols[3])], axis=1)
        d["dec"] = jnp.exp(jnp.minimum(gc256 - grow, 0.0))
        d["kb"] = k * d["beta512"]
        kbd = jnp.concatenate([k.astype(BF16)] * nh, axis=0) * bdk_ref[...]
        d["aq"] = lax.dot_general(jnp.concatenate([d["kb"], q_ref[bi]], axis=0).astype(BF16), kbd,
                                  (((1,), (1,)), ((), ())), preferred_element_type=F32)

    for bi in seqs:
        d = st[bi]
        x = jnp.where(strict4, -d["aq"][:n] * d["dec"], 0.0)
        d["qkm"] = jnp.where(incl4, d["aq"][n:] * d["dec"], 0.0).astype(BF16)
        d["inv"], d["p"] = eye4 + x, x
    for lvl in range(6):
        for bi in seqs:
            d = st[bi]
            p, inv = d["p"], d["inv"]
            wbd = jnp.concatenate([p.astype(BF16)] * nh, axis=0) * bdp_ref[...]
            if lvl == 0:
                d["p"] = dot(p.astype(BF16), wbd)
            elif lvl < 5:
                r = dot(jnp.concatenate([p, inv], axis=0).astype(BF16), wbd)
                d["p"], d["inv"] = r[:n], inv + r[n:]
            else:
                d["inv"] = inv + dot(inv.astype(BF16), wbd)

    heads = [(bi, h) for bi in seqs for h in range(nh)]
    hsl = lambda h: slice(h * DK_A, (h + 1) * DK_A)
    psl = lambda h: slice(128 * (h // 2), 128 * (h // 2 + 1))
    for bi in seqs:
        d = st[bi]
        k, gc512 = k_ref[bi], d["gc512"]
        d["tinv"] = d["inv"].astype(BF16)
        egc = jnp.exp(gc512)
        d["rhs_v"] = (v_ref[bi] * d["beta512"]).astype(BF16)
        d["rhs_k"] = (d["kb"] * egc).astype(BF16)
        g_last = gc512[n - 1:n, :]
        d["q_dec"] = q_ref[bi] * egc
        d["k_dec"] = (k * jnp.exp(g_last - gc512)).astype(BF16)
        d["c_dec"] = jnp.exp(g_last)
    sol, ws, vn = {}, {}, {}
    for bi, h in heads:
        d = st[bi]
        rhs = pad_rows(jnp.concatenate([d["rhs_v"][:, hsl(h)], d["rhs_k"][:, hsl(h)]], axis=1), h % 2)
        sol[bi, h] = dot(d["tinv"][:, psl(h)], rhs)
    for bi, h in heads:
        lhs = jnp.concatenate([sol[bi, h][:, DV_A:], st[bi]["q_dec"][:, hsl(h)]], axis=0)
        ws[bi, h] = dot(lhs.astype(BF16), s_ref[bi, h].astype(BF16))
    for bi, h in heads:
        d = st[bi]
        vn[bi, h] = (sol[bi, h][:, :DV_A] - ws[bi, h][:n]).astype(BF16)
        s_ref[bi, h] = d["c_dec"][:, hsl(h)] * s_ref[bi, h] + lax.dot_general(
            d["k_dec"][:, hsl(h)], vn[bi, h], (((0,), (0,)), ((), ())), preferred_element_type=F32)
    for bi, h in heads:
        o = ws[bi, h][n:] + dot(st[bi]["qkm"][:, psl(h)], pad_rows(vn[bi, h], h % 2))
        o_ref[bi, :, hsl(h)] = (_rms_rows(o, gg) * _silu(z_ref[bi, :, hsl(h)])).astype(BF16)


def _gdn_prompt(q, k, v, z, sm, gt, gg, *, b, t, nb):
    nc = t // CHUNK
    r3 = lambda a, n: a.reshape(b, t, n)
    blk = lambda n: pl.BlockSpec((nb, CHUNK, n), lambda i, j: (i, j, 0))
    w4 = H_A * CHUNK
    bdk = np.kron(np.eye(H_A, dtype=np.float32), np.ones((CHUNK, DK_A), np.float32))
    bdp = np.kron(np.eye(H_A, dtype=np.float32), np.ones((CHUNK, CHUNK), np.float32))
    return pl.pallas_call(
        functools.partial(_gdn_body, nb=nb),
        out_shape=[jax.ShapeDtypeStruct((b, t, 512), BF16),
                   jax.ShapeDtypeStruct((b, H_A, DK_A, DV_A), F32)],
        grid=(b // nb, nc),
        in_specs=[blk(512), blk(512), blk(512), blk(512), blk(128),
                  pl.BlockSpec((nb, 1, 8, CHUNK), lambda i, j: (i, j, 0, 0)),
                  _const_spec((1, DV_A)), _const_spec((w4, H_A * DK_A)), _const_spec((w4, w4))],
        out_specs=[blk(512), pl.BlockSpec((nb, H_A, DK_A, DV_A), lambda i, j: (i, 0, 0, 0))],
        compiler_params=_cparams(2),
        name="gdn_prompt",
    )(r3(q, 512), r3(k, 512), r3(v, 512), r3(z, 512), r3(sm, 128),
      jnp.swapaxes(gt.reshape(b, 8, nc, CHUNK), 1, 2), gg, jnp.asarray(bdk, BF16), jnp.asarray(bdp, BF16))


def _ordinal_to_float(k):
    k = jnp.maximum(k, NEG_INF_KEY)
    return pltpu.bitcast(k ^ ((k >> 31) & 0x7FFFFFFF), F32)


def _topk_threshold(count_ge, count_gt_eq, count_eq_below, shape, k_top, idx_bits):
    def bit_step(i, o):
        cand = o + jnp.left_shift(jnp.int32(1), 31 - i)
        return jnp.where(count_ge(_ordinal_to_float(cand)) >= k_top, cand, o)

    o = lax.fori_loop(0, 32, bit_step, jnp.full(shape, INT_MIN, I32))
    t = _ordinal_to_float(o)
    c_gt, c_eq = count_gt_eq(t)
    need = k_top - c_gt
    untied = jnp.logical_or(c_eq == need, o <= NEG_INF_KEY)
    all_untied = jnp.min(jnp.where(untied, 1.0, 0.0)) > 0.5

    def tie_search(_):
        def idx_step(i, x):
            cand = x + jnp.left_shift(jnp.int32(1), idx_bits - 1 - i)
            return jnp.where(count_eq_below(t, cand) < need, cand, x)
        return lax.fori_loop(0, idx_bits, idx_step, jnp.zeros(shape, I32))

    x = lax.cond(all_untied, lambda _: jnp.full(shape, 2 ** 30, I32), tie_search, 0)
    return t, x


_ACC_ROWS = HD_B + 16


def _dsa_body(qit_ref, wit_ref, qtp_ref, ki_ref, k_ref, vt_ref, o_ref,
              sc_ref, m_ref, acc_ref, *, tq, kc, k_top):
    i = pl.program_id(1)
    nkc = i + 1
    rowi = lax.broadcasted_iota(I32, (kc, tq), 0)
    coli = lax.broadcasted_iota(I32, (kc, tq), 1)
    ng = kc // 32

    wit = wit_ref[0]

    def score_chunk(c, carry):
        kic = ki_ref[0, c]
        score = jnp.zeros((kc, tq), F32)
        for h in range(H_IDX):
            s = jnp.dot(kic, qit_ref[0, h], preferred_element_type=F32)
            score = score + jnp.maximum(s, 0.0) * wit[h:h + 1, :]
        causal = rowi + (c - i) * kc <= coli
        sc_ref[c] = jnp.where(causal, score, -jnp.inf)
        return carry

    lax.fori_loop(0, nkc, score_chunk, 0)

    def key_sum(f):
        def body(c, acc):
            ind = f(sc_ref[c], c * kc)
            return acc + jnp.sum(ind.reshape(ng, 32, tq), axis=0)
        acc = lax.fori_loop(0, nkc, body, jnp.zeros((32, tq), F32))
        return jnp.sum(acc, axis=0, keepdims=True).astype(I32)

    def count_ge(cand):
        return key_sum(lambda s, off: jnp.where(s >= cand, 1.0, 0.0))

    def count_gt_eq(t):
        return (key_sum(lambda s, off: jnp.where(s > t, 1.0, 0.0)),
                key_sum(lambda s, off: jnp.where(s == t, 1.0, 0.0)))

    def count_eq_below(t, j):
        return key_sum(lambda s, off: jnp.where(s == t, jnp.where(rowi + off < j, 1.0, 0.0), 0.0))

    t, x = _topk_threshold(count_ge, count_gt_eq, count_eq_below, (1, tq), k_top, 12)

    m_ref[...] = jnp.full(m_ref.shape, NEG_BIG, F32)
    acc_ref[...] = jnp.zeros(acc_ref.shape, F32)
    ones_rows = jnp.ones((_ACC_ROWS - HD_B, kc), BF16)

    def attn_chunk(c, carry):
        kk = k_ref[0, c]
        sc = sc_ref[c]
        tie = jnp.where(sc == t, jnp.where(rowi + c * kc <= x, 0.0, NEG_BIG), NEG_BIG)
        bias = jnp.where(sc > -jnp.inf, jnp.where(sc > t, 0.0, tie), NEG_BIG)
        vt = vt_ref[0, c]
        vaug = [jnp.concatenate([vt[HD_B * n:HD_B * (n + 1)], ones_rows], axis=0) for n in range(N_KV_B)]
        ahead = 3
        logits = {h: jnp.dot(kk, qtp_ref[0, h], preferred_element_type=F32) for h in range(ahead)}
        for h in range(H_B):
            if h + ahead < H_B:
                logits[h + ahead] = jnp.dot(kk, qtp_ref[0, h + ahead], preferred_element_type=F32)
            n = h // (H_B // N_KV_B)
            s = bias + logits.pop(h)
            m_old = m_ref[h:h + 1, :]
            m_new = jnp.maximum(m_old, jnp.max(s, axis=0, keepdims=True))
            alpha = jnp.exp2(m_old - m_new)
            pexp = jnp.exp2(s - m_new).astype(BF16)
            acc_ref[h] = acc_ref[h] * alpha + jnp.dot(vaug[n], pexp, preferred_element_type=F32)
            m_ref[h:h + 1, :] = m_new
        return carry

    lax.fori_loop(0, nkc, attn_chunk, 0)
    for p in range(H_B // 2):
        halves = []
        for h in (2 * p, 2 * p + 1):
            acc = acc_ref[h]
            halves.append(acc[:HD_B] / acc[HD_B:HD_B + 1])
        o_ref[:, 128 * p:128 * (p + 1)] = jnp.concatenate(halves, axis=0).T.astype(BF16)


def _dsa_prompt(qit, wit, qtp, kibf, kbf, vt, *, b, t, tq, k_top):
    kc = tq
    nq = t // tq
    nkc = t // kc
    whole = lambda *s: pl.BlockSpec((1,) + s, lambda i, j: (i,) + (0,) * len(s))
    return pl.pallas_call(
        functools.partial(_dsa_body, tq=tq, kc=kc, k_top=k_top),
        out_shape=jax.ShapeDtypeStruct((b * t, H_B * HD_B), BF16),
        grid=(b, nq),
        in_specs=[pl.BlockSpec((1, H_IDX, D_IDX, tq), lambda i, j: (i, 0, 0, j)),
                  pl.BlockSpec((1, H_IDX, tq), lambda i, j: (i, 0, j)),
                  pl.BlockSpec((1, H_B, 128, tq), lambda i, j: (i, 0, 0, j)),
                  whole(nkc, kc, D_IDX), whole(nkc, kc, 128), whole(nkc, 128, kc)],
        out_specs=pl.BlockSpec((tq, H_B * HD_B), lambda i, j: (i * nq + j, 0)),
        scratch_shapes=[pltpu.VMEM((nkc, kc, tq), F32),
                        pltpu.VMEM((H_B, tq), F32), pltpu.VMEM((H_B, _ACC_ROWS, tq), F32)],
        compiler_params=_cparams(2),
        name="dsa_prompt",
    )(qit, wit, qtp, kibf.reshape(b, nkc, kc, D_IDX), kbf.reshape(b, nkc, kc, 128), vt)


def _merge_body(x_ref, oa_ref, ob_ref, g1_ref, wg_ref, woa_ref, wob_ref, wo_ref, h_ref):
    x = x_ref[...]
    xn = _rms_rows(x, g1_ref[...]).astype(BF16)
    gates = jax.nn.sigmoid(jnp.dot(xn, wg_ref[...], preferred_element_type=F32))
    a = jnp.dot(oa_ref[...], woa_ref[...], preferred_element_type=F32)
    bb = jnp.dot(ob_ref[...], wob_ref[...], preferred_element_type=F32)
    mix = gates[:, :D_MODEL] * a + gates[:, D_MODEL:] * bb
    h_ref[...] = x + jnp.dot(mix.astype(BF16), wo_ref[...], preferred_element_type=F32)


def _merge(x, oa, ob, wob, w, *, tm):
    m, d = x.shape
    nb = ob.shape[1]
    row = lambda n: pl.BlockSpec((tm, n), lambda i: (i, 0))
    return pl.pallas_call(
        _merge_body,
        out_shape=jax.ShapeDtypeStruct((m, d), F32),
        grid=(m // tm,),
        in_specs=[row(d), row(512), row(nb), _const_spec((1, d)), _const_spec((d, 2 * d)),
                  _const_spec((512, d)), _const_spec((nb, d)), _const_spec((d, d))],
        out_specs=row(d),
        compiler_params=_cparams(1),
        name="merge",
    )(x, oa, ob, w["g1"], w["wg"], w["woa"], wob, w["wo"])


def _mlp_body(h_ref, g2_ref, wup_ref, wdn_ref, y_ref):
    h = h_ref[...]
    hn = _rms_rows(h, g2_ref[...]).astype(BF16)
    up = jnp.dot(hn, wup_ref[...], preferred_element_type=F32)
    act = jnp.square(jnp.maximum(up, 0.0)).astype(BF16)
    y_ref[...] = h + jnp.dot(act, wdn_ref[...], preferred_element_type=F32)


def _mlp(h, w, *, tm):
    m, d = h.shape
    row = pl.BlockSpec((tm, d), lambda i: (i, 0))
    return pl.pallas_call(
        _mlp_body,
        out_shape=jax.ShapeDtypeStruct((m, d), F32),
        grid=(m // tm,),
        in_specs=[row, _const_spec((1, d)), _const_spec((d, D_FF)), _const_spec((D_FF, d))],
        out_specs=row,
        compiler_params=_cparams(1),
        name="mlp",
    )(h, w["g2"], w["wup"], w["wdn"])


def _proj_decode_body(x_ref, st_ref, g1_ref, wqkv_ref, wz_ref, wbq_ref, wkv_ref, wiq_ref, wsm_ref,
                      convw_ref, qng_ref, kng_ref, alog_ref, dtb_ref, ones_ref, seg_ref, cos_ref, sin_ref,
                      u_out, q_out, k_out, v_out, z_out, qb_out, kv_out, qi_out, sm_out, ss_out):
    x = x_ref[...]
    xn = _rms_rows(x, g1_ref[...]).astype(BF16)
    u = jnp.dot(xn, wqkv_ref[...], preferred_element_type=F32)
    u_out[...] = u
    cw = convw_ref[...]
    acc = st_ref[0] * cw[0:1] + st_ref[1] * cw[1:2] + st_ref[2] * cw[2:3] + u * cw[3:4]
    c = _silu(acc)
    for h in range(H_A):
        qh = c[:, h * DK_A:(h + 1) * DK_A]
        kh = c[:, (H_A + h) * DK_A:(H_A + h + 1) * DK_A]
        q_out[:, h * DK_A:(h + 1) * DK_A] = qh * (lax.rsqrt(jnp.sum(qh * qh, -1, keepdims=True) + EPS)
                                                 * (DK_A ** -0.5))
        k_out[:, h * DK_A:(h + 1) * DK_A] = kh * lax.rsqrt(jnp.sum(kh * kh, -1, keepdims=True) + EPS)
    v_out[...] = c[:, 2 * H_A * DK_A:]
    z_out[...] = jnp.dot(xn, wz_ref[...], preferred_element_type=F32)
    cos, sin = cos_ref[...], sin_ref[...]
    ones_bd = ones_ref[...]
    bq = jnp.dot(xn, wbq_ref[...], preferred_element_type=F32)
    qb = _rope_rows(_seg_rms(bq, ones_bd, qng_ref[...]), cos, sin)
    qb_out[...] = (qb * (HD_B ** -0.5)).astype(BF16)
    kv = jnp.dot(xn, wkv_ref[...], preferred_element_type=F32)
    kb = _rope_rows(_seg_rms(kv[:, :128], ones_bd[:128, :128], kng_ref[...]), cos, sin)
    kv_out[:, 0:128] = kb
    kv_out[:, 128:256] = kv[:, 128:]
    iq = jnp.dot(xn, wiq_ref[...], preferred_element_type=F32)
    qi = (_rope_rows(iq, cos, sin) * (D_IDX ** -0.5)).astype(BF16)
    qi_out[...] = qi
    sm = jnp.dot(xn, wsm_ref[...], preferred_element_type=F32)
    smo = _small_block(sm, cos, sin, alog_ref[...], dtb_ref[...])
    sm_out[...] = smo
    lane = lax.broadcasted_iota(I32, smo.shape, 1)
    ki = smo.astype(BF16).astype(F32)
    ki2 = jnp.where(lane < D_IDX, ki, pltpu.roll(ki, D_IDX, 1))
    prod = qi.astype(F32) * jnp.concatenate([ki2] * 4, axis=-1)
    sh = _mm_f32(prod, seg_ref[...])
    contrib = jnp.where(jnp.logical_and(lane >= _SM_WI, lane < _SM_WI + H_IDX),
                        jnp.maximum(sh, 0.0) * smo, 0.0)
    ss_out[...] = jnp.broadcast_to(jnp.sum(contrib, axis=-1, keepdims=True), ss_out.shape)


def _proj_decode(x, st, w, tabs):
    r, d = x.shape
    outs = [(r, CONV_CH, F32), (r, 512, F32), (r, 512, F32), (r, 512, F32), (r, 512, F32), (r, 512, BF16),
            (r, 256, F32), (r, 512, BF16), (r, 128, F32), (r, 128, F32)]
    args = (x, st, w["g1"], w["wqkv"], w["wz"], w["wbq"], w["wkv"], w["wiq"], w["wsm"], w["convw"],
            w["qng"], w["kng"], w["alog_row"], w["dtb_row"], w["ones_bd"], w["seg_sum"], tabs["cos"], tabs["sin"])
    return pl.pallas_call(
        _proj_decode_body,
        out_shape=[jax.ShapeDtypeStruct((a, n), dt) for a, n, dt in outs],
        compiler_params=pltpu.CompilerParams(vmem_limit_bytes=VMEM_LIMIT),
        name="proj_decode",
    )(*args)


def _gdn_decode_body(qc_ref, kc_ref, v_ref, z_ref, sm_ref, gg_ref, s_ref, o_ref, so_ref):
    sm = sm_ref[0]
    gg = gg_ref[...]
    for h in range(H_A):
        g = sm[:, _SM_G + h:_SM_G + h + 1]
        beta = sm[:, _SM_BETA + h:_SM_BETA + h + 1]
        kcol, qcol = kc_ref[0, h], qc_ref[0, h]
        v = v_ref[0, h]
        s = s_ref[0, h] * jnp.exp(g)
        vn = beta * (v - jnp.sum(s * kcol, axis=0, keepdims=True))
        s = s + kcol * vn
        so_ref[0, h] = s
        o = jnp.sum(s * qcol, axis=0, keepdims=True)
        o_ref[0, h] = (_rms_rows(o, gg) * _silu(z_ref[0, h])).astype(BF16)


def _gdn_decode(q, k, v, z, sm, gg, state):
    r = q.shape[0]
    col = lambda a: a.reshape(r, H_A, DK_A, 1)
    rowv = lambda a: a.reshape(r, H_A, 1, DV_A)
    cspec = pl.BlockSpec((1, H_A, DK_A, 1), lambda i: (i, 0, 0, 0))
    rspec = pl.BlockSpec((1, H_A, 1, DV_A), lambda i: (i, 0, 0, 0))
    sspec = pl.BlockSpec((1, H_A, DK_A, DV_A), lambda i: (i, 0, 0, 0))
    o, s_new = pl.pallas_call(
        _gdn_decode_body,
        out_shape=[jax.ShapeDtypeStruct((r, H_A, 1, DV_A), BF16),
                   jax.ShapeDtypeStruct((r, H_A, DK_A, DV_A), F32)],
        grid=(r,),
        in_specs=[cspec, cspec, rspec, rspec, pl.BlockSpec((1, 1, 128), lambda i: (i, 0, 0)),
                  _const_spec((1, DV_A)), sspec],
        out_specs=[rspec, sspec],
        compiler_params=_cparams(1),
        name="gdn_decode",
    )(col(q), col(k), rowv(v), rowv(z), sm.reshape(r, 1, 128), gg, state)
    return o.reshape(r, H_A * DV_A), s_new


def _idx_decode_body(pt_ref, qi_ref, w_ref, *refs, npg):
    pages, out_ref = refs[:npg], refs[npg]
    qi = qi_ref[0]
    w = w_ref[0]
    for p in range(npg):
        s = _mm(qi, pages[p][0])
        out_ref[0, :, PAGE_SIZE * p:PAGE_SIZE * (p + 1)] = jnp.sum(jnp.maximum(s, 0.0) * w, axis=0,
                                                                   keepdims=True)


def _idx_decode(page_table, qi, wi_col, cache_kidx, *, npg):
    r, n_pages = page_table.shape
    page_spec = lambda p: pl.BlockSpec((1, D_IDX, PAGE_SIZE), lambda i, j, pt: (pt[i, j * npg + p], 0, 0))
    grid_spec = pltpu.PrefetchScalarGridSpec(
        num_scalar_prefetch=1,
        grid=(r, n_pages // npg),
        in_specs=[pl.BlockSpec((1, H_IDX, D_IDX), lambda i, j, pt: (i, 0, 0)),
                  pl.BlockSpec((1, H_IDX, 1), lambda i, j, pt: (i, 0, 0))]
                 + [page_spec(p) for p in range(npg)],
        out_specs=pl.BlockSpec((1, 1, PAGE_SIZE * npg), lambda i, j, pt: (i, 0, j)),
    )
    return pl.pallas_call(
        functools.partial(_idx_decode_body, npg=npg),
        out_shape=jax.ShapeDtypeStruct((r, 1, n_pages * PAGE_SIZE), F32),
        grid_spec=grid_spec,
        compiler_params=_cparams(2),
        name="idx_decode",
    )(page_table, qi.reshape(r, H_IDX, D_IDX), wi_col, *([cache_kidx] * npg))


def _select_decode_body(sc_ref, ss_ref, bias_ref, bself_ref, *, k_top, cw):
    r, n = sc_ref.shape
    nch = n // cw
    key_ref = sc_ref
    skey = ss_ref[:, 0:1]

    def lane_sum(f):
        def body(c, acc):
            off = pl.multiple_of(c * cw, cw)
            kk = key_ref[:, pl.ds(off, cw)]
            for j in range(cw // 128):
                acc = acc + f(kk[:, 128 * j:128 * (j + 1)], off + 128 * j)
            return acc
        acc = lax.fori_loop(0, nch, body, jnp.zeros((r, 128), F32))
        return jnp.sum(acc, axis=-1, keepdims=True).astype(I32)

    def count_ge(cand):
        cb = jnp.broadcast_to(cand, (r, 128))
        return lane_sum(lambda kk, off: jnp.where(kk >= cb, 1.0, 0.0)) + (skey >= cand).astype(I32)

    def count_gt_eq(t):
        tb = jnp.broadcast_to(t, (r, 128))
        return (lane_sum(lambda kk, off: jnp.where(kk > tb, 1.0, 0.0)) + (skey > t).astype(I32),
                lane_sum(lambda kk, off: jnp.where(kk == tb, 1.0, 0.0)) + (skey == t).astype(I32))

    def count_eq_below(t, j):
        tb = jnp.broadcast_to(t, (r, 128))
        jb = jnp.broadcast_to(j, (r, 128))
        lane = lax.broadcasted_iota(I32, (r, 128), 1)
        return (lane_sum(lambda kk, off: jnp.where(kk == tb, jnp.where(lane + off < jb, 1.0, 0.0), 0.0))
                + jnp.logical_and(skey == t, j > n).astype(I32))

    t, x = _topk_threshold(count_ge, count_gt_eq, count_eq_below, (r, 1), k_top, 15)
    tb = jnp.broadcast_to(t, (r, cw))
    xb = jnp.broadcast_to(x, (r, cw))

    def bias_chunk(c, carry):
        off = pl.multiple_of(c * cw, cw)
        kk = key_ref[:, pl.ds(off, cw)]
        idx = lax.broadcasted_iota(I32, (r, cw), 1) + off
        tie = jnp.where(kk == tb, jnp.where(idx <= xb, 0.0, NEG_BIG), NEG_BIG)
        bias_ref[:, pl.ds(off, cw)] = jnp.where(kk > tb, 0.0, tie)
        return carry

    lax.fori_loop(0, nch, bias_chunk, 0)
    tie_self = jnp.where(skey == t, jnp.where(x >= n, 0.0, NEG_BIG), NEG_BIG)
    bself_ref[...] = jnp.broadcast_to(jnp.where(skey > t, 0.0, tie_self), bself_ref.shape)


def _select_decode(scores, sself, *, k_top):
    r, n = scores.shape
    return pl.pallas_call(
        functools.partial(_select_decode_body, k_top=k_top, cw=512),
        out_shape=[jax.ShapeDtypeStruct((r, n), F32), jax.ShapeDtypeStruct((r, 128), F32)],
        compiler_params=pltpu.CompilerParams(vmem_limit_bytes=VMEM_LIMIT),
        name="select_decode",
    )(scores, sself)


def _attn_decode_body(pt_ref, qp_ref, bias_ref, kvs_ref, bself_ref, *refs, npg, nsteps):
    pages, o_ref, m_ref, l_ref, acc_ref = refs[:npg], refs[npg], refs[npg + 1], refs[npg + 2], refs[npg + 3]
    j = pl.program_id(1)

    @pl.when(j == 0)
    def _():
        m_ref[...] = jnp.full(m_ref.shape, NEG_BIG, F32)
        l_ref[...] = jnp.zeros(l_ref.shape, F32)
        acc_ref[...] = jnp.zeros(acc_ref.shape, F32)

    qp = qp_ref[0]

    ss = [_mm(qp, pages[p][0, 0:128, :]) + bias_ref[0, :, PAGE_SIZE * p:PAGE_SIZE * (p + 1)]
          for p in range(npg)]
    m_old = m_ref[...]
    m_new = m_old
    for s in ss:
        m_new = jnp.maximum(m_new, jnp.max(s, axis=-1, keepdims=True))
    alpha = jnp.exp(m_old - m_new)
    l_new = l_ref[...] * alpha
    acc_new = acc_ref[...] * alpha
    for p in range(npg):
        pe = jnp.exp(ss[p] - m_new)
        l_new = l_new + jnp.sum(pe, axis=-1, keepdims=True)
        acc_new = acc_new + _mm_nt(pe, pages[p][0, 128:256, :])
    m_ref[...] = m_new
    l_ref[...] = l_new
    acc_ref[...] = acc_new

    @pl.when(j == nsteps - 1)
    def _():
        kvs = kvs_ref[0]
        krow = kvs[:, :128].astype(BF16).astype(F32)
        s = jnp.sum(qp.astype(F32) * krow, axis=-1, keepdims=True) + bself_ref[0][:, 0:1]
        m_old = m_ref[...]
        m_new = jnp.maximum(m_old, s)
        alpha = jnp.exp(m_old - m_new)
        p = jnp.exp(s - m_new)
        l = l_ref[...] * alpha + p
        vrow = kvs[:, 128:].astype(BF16).astype(F32)
        acc = acc_ref[...] * alpha + p.astype(BF16).astype(F32) * vrow
        out = acc / l
        rowi = lax.broadcasted_iota(I32, out.shape, 0)
        out = jnp.where(rowi < 4, out, pltpu.roll(out, HD_B, 1))
        o_ref[0] = out.astype(BF16)


def _attn_decode(page_table, qpad, bias, kv_self, bself, cache_kv, *, npg):
    r, n_pages = page_table.shape
    nsteps = n_pages // npg
    page_spec = lambda p: pl.BlockSpec((1, 256, PAGE_SIZE), lambda i, j, pt: (pt[i, j * npg + p], 0, 0))
    grid_spec = pltpu.PrefetchScalarGridSpec(
        num_scalar_prefetch=1,
        grid=(r, nsteps),
        in_specs=[pl.BlockSpec((1, H_B, 128), lambda i, j, pt: (i, 0, 0)),
                  pl.BlockSpec((1, 1, PAGE_SIZE * npg), lambda i, j, pt: (i, 0, j)),
                  pl.BlockSpec((1, 1, 256), lambda i, j, pt: (i, 0, 0)),
                  pl.BlockSpec((1, 1, 128), lambda i, j, pt: (i, 0, 0))]
                 + [page_spec(p) for p in range(npg)],
        out_specs=pl.BlockSpec((1, H_B, 128), lambda i, j, pt: (i, 0, 0)),
        scratch_shapes=[pltpu.VMEM((H_B, 1), F32), pltpu.VMEM((H_B, 1), F32), pltpu.VMEM((H_B, 128), F32)],
    )
    return pl.pallas_call(
        functools.partial(_attn_decode_body, npg=npg, nsteps=nsteps),
        out_shape=jax.ShapeDtypeStruct((r, H_B, 128), BF16),
        grid_spec=grid_spec,
        compiler_params=_cparams(2),
        name="attn_decode",
    )(page_table, qpad, bias.reshape(r, 1, -1), kv_self.reshape(r, 1, 256), bself.reshape(r, 1, 128),
      *([cache_kv] * npg))


def _prep_weights(norm1_g, w_in, conv_w, a_log, dt_bias, gdn_norm_g, q_norm_g, k_norm_g,
                  w_out_a, w_out_b, w_o, norm2_g, w_up, w_down):
    o = _OFF
    wb = w_in.astype(BF16)
    d = w_in.shape[0]
    wsm = jnp.concatenate([wb[:, o["ik"]:o["iw"]], wb[:, o["ab"]:o["bq"]], wb[:, o["iw"]:o["ga"]],
                           jnp.zeros((d, 128 - 80), BF16)], axis=1)

    def lanes(vals, start):
        return jnp.zeros((1, 128), F32).at[0, start:start + vals.shape[0]].set(vals.astype(F32))

    seg = np.zeros((512, 128), np.float32)
    for h in range(H_IDX):
        seg[64 * h:64 * (h + 1), _SM_WI + h] = 1.0
    ones_bd = np.kron(np.eye(8, dtype=np.float32), np.ones((64, 64), np.float32))
    wob = w_out_b.astype(BF16).reshape(H_B, HD_B, d)
    wob_pad = jnp.concatenate([wob, jnp.zeros_like(wob)], axis=1).reshape(H_B * 128, d)
    col8 = lambda v: jnp.concatenate([jnp.zeros((4,), F32), v.astype(F32)]).reshape(8, 1)
    return dict(
        g1=norm1_g.reshape(1, d).astype(F32),
        wqkv=wb[:, o["aq"]:o["az"]], wz=wb[:, o["az"]:o["ab"]], wbq=wb[:, o["bq"]:o["bk"]],
        wkv=wb[:, o["bk"]:o["iq"]], wiq=wb[:, o["iq"]:o["ik"]], wsm=wsm,
        wbqt=wb[:, o["bq"]:o["bk"]].T, wiqt=wb[:, o["iq"]:o["ik"]].T, wvt=wb[:, o["bv"]:o["iq"]].T,
        wsmt=wsm.T, wg=wb[:, o["ga"]:o["end"]],
        convw=conv_w.astype(F32),
        qng=jnp.tile(q_norm_g.astype(F32), H_B).reshape(1, 512),
        kng=jnp.tile(k_norm_g.astype(F32), N_KV_B).reshape(1, 128),
        qngt=jnp.tile(q_norm_g.astype(F32).reshape(HD_B, 1), (1, 128)),
        alog_row=lanes(a_log, _SM_G), dtb_row=lanes(dt_bias, _SM_G),
        alog_col=col8(a_log), dtb_col=col8(dt_bias),
        ones_bd=jnp.asarray(ones_bd, BF16), ones_bd128=jnp.asarray(ones_bd[:128, :128], BF16),
        seg_sum=jnp.asarray(seg),
        gg=gdn_norm_g.reshape(1, DV_A).astype(F32),
        woa=w_out_a.astype(BF16), wob=w_out_b.astype(BF16), wob_pad=wob_pad, wo=w_o.astype(BF16),
        g2=norm2_g.reshape(1, d).astype(F32), wup=w_up.astype(BF16), wdn=w_down.astype(BF16),
    )


def _rope_tables(pos):
    half = HD_B // 2
    inv = ROPE_THETA ** (-jnp.arange(half, dtype=F32) / half)
    ang = pos.astype(F32)[:, None] * inv[None, :]
    cos, sin = jnp.cos(ang), jnp.sin(ang)
    return dict(cos=jnp.tile(cos, (1, 4)), sin=jnp.tile(jnp.concatenate([-sin, sin], axis=1), (1, 2)),
                cos_t=cos.T, sin_t=sin.T)


def _layer_prompt(x, w, *, tm=512, tq=256, nb=4):
    b, t, d = x.shape
    tabs = _rope_tables(jnp.arange(t))
    (q, k, v, z, kv, kbf, kibf, sm, kidx, qtp, qit, vt, gt, wit, conv_new) = _proj_prompt(x, w, tabs, tm=tm, kc=tq)
    oa, ssm = _gdn_prompt(q, k, v, z, sm, gt, w["gg"], b=b, t=t, nb=nb)
    ob = _dsa_prompt(qit, wit, qtp, kibf, kbf, vt, b=b, t=t, tq=tq, k_top=min(TOPK_MAX, t // 4))
    h = _merge(x.reshape(b * t, d), oa.reshape(b * t, 512), ob, w["wob"], w, tm=tm)
    y = _mlp(h, w, tm=tm)
    return (y.reshape(b, t, d), kv.reshape(b, t, 2, N_KV_B, HD_B), kidx.reshape(b, t, D_IDX), ssm, conv_new)


def _layer_decode(x, w, state_ssm, state_conv, cache_kv, cache_kidx, page_table, *, npg=32):
    r, d = x.shape
    n_pages = page_table.shape[1]
    past = n_pages * PAGE_SIZE
    tabs = _rope_tables(jnp.full((1,), past))
    st = jnp.swapaxes(state_conv, 0, 1)
    (u, q, k, v, z, qb, kv, qi, sm, sself) = _proj_decode(x, st, w, tabs)
    oa, ssm = _gdn_decode(q, k, v, z, sm, w["gg"], state_ssm)
    wi_col = sm[:, _SM_WI:_SM_WI + H_IDX].reshape(r, H_IDX, 1)
    kidx_pages = jnp.swapaxes(cache_kidx, 1, 2)
    kv_pages = jnp.transpose(cache_kv, (0, 2, 3, 4, 1)).reshape(-1, 2 * N_KV_B * HD_B, PAGE_SIZE)
    scores = _idx_decode(page_table, qi, wi_col, kidx_pages, npg=npg).reshape(r, past)
    bias, bself = _select_decode(scores, sself, k_top=min(TOPK_MAX, (past + 1) // 4))
    qh = qb.reshape(r, H_B, HD_B)
    zq = jnp.zeros_like(qh[:, :4])
    qpad = jnp.concatenate([jnp.concatenate([qh[:, :4], zq], axis=-1),
                            jnp.concatenate([zq, qh[:, 4:]], axis=-1)], axis=1)
    ob = _attn_decode(page_table, qpad, bias, kv, bself, kv_pages, npg=npg)
    h = _merge(x, oa, ob.reshape(r, H_B * 128), w["wob_pad"], w, tm=r)
    y = _mlp(h, w, tm=r)
    conv_new = jnp.concatenate([state_conv[:, 1:], u[:, None, :]], axis=1)
    return (y, kv.reshape(r, 1, 2, N_KV_B, HD_B), sm[:, :D_IDX].reshape(r, 1, D_IDX), ssm, conv_new)


def kernel(x_prompt, x_sample, cache_kv, cache_kidx, page_table, state_ssm, state_conv, norm1_g, w_in, conv_w,
           a_log, dt_bias, gdn_norm_g, q_norm_g, k_norm_g, w_out_a, w_out_b, w_o, norm2_g, w_up, w_down):
    assert w_in.shape[0] == 1, "single-layer trunk"
    w = _prep_weights(norm1_g[0], w_in[0], conv_w[0], a_log[0], dt_bias[0], gdn_norm_g[0], q_norm_g[0],
                      k_norm_g[0], w_out_a[0], w_out_b[0], w_o[0], norm2_g[0], w_up[0], w_down[0])
    yp, kv_p, ki_p, ssm_p, cv_p = _layer_prompt(x_prompt, w)
    ys, kv_s, ki_s, ssm_s, cv_s = _layer_decode(x_sample[:, 0], w, state_ssm[0], state_conv[0],
                                                cache_kv[0], cache_kidx[0], page_table)
    return (yp, ys[:, None, :], kv_p[None], ki_p[None], ssm_p[None], cv_p[None],
            kv_s[None], ki_s[None], ssm_s[None], cv_s[None])
```

```python
import functools
import math

import jax
import jax.numpy as jnp
import numpy as np
from jax import lax
from jax.experimental import pallas as pl
from jax.experimental.pallas import tpu as pltpu

F32 = jnp.float32
BF16 = jnp.bfloat16
I32 = jnp.int32

D_MODEL = 1024
H_A, DK_A, DV_A = 4, 128, 128
CONV_W = 4
CONV_CH = H_A * (2 * DK_A + DV_A)
CHUNK = 64
HD_B, H_B, N_KV_B = 64, 8, 2
H_IDX, D_IDX = 8, 64
TOPK_MAX = 256
PAGE_SIZE = 128
ROPE_THETA = 10000.0
D_FF = 4 * D_MODEL
EPS = 1e-6

_OFF = dict(aq=0, ak=512, av=1024, az=1536, ab=2048, aa=2052, bq=2056, bk=2568, bv=2696,
            iq=2824, ik=3336, iw=3400, ga=3408, gb=4432, end=5456)
_SM_BETA, _SM_G, _SM_WI = 64, 68, 72

LOG2_E = math.log2(math.e)
NEG_BIG = -1e30
INT_MIN = -(2 ** 31)
NEG_INF_KEY = (0xFF800000 ^ 0x7FFFFFFF) - 2 ** 32

VMEM_LIMIT = 56 * 1024 * 1024
HIGHEST = lax.Precision.HIGHEST


def _cparams(n_axes):
    return pltpu.CompilerParams(dimension_semantics=("arbitrary",) * n_axes,
                                vmem_limit_bytes=VMEM_LIMIT)


def _const_spec(shape):
    nd = len(shape)
    return pl.BlockSpec(shape, lambda *a: (0,) * nd, pipeline_mode=pl.Buffered(1))


def _mm(a, b):
    return jnp.dot(a.astype(BF16), b.astype(BF16), preferred_element_type=F32)


def _mm_nt(a, b):
    return lax.dot_general(a.astype(BF16), b.astype(BF16), (((1,), (1,)), ((), ())),
                           preferred_element_type=F32)


def _mm_tn(a, b):
    return lax.dot_general(a.astype(BF16), b.astype(BF16), (((0,), (0,)), ((), ())),
                           preferred_element_type=F32)


def _mm_f32(a, b):
    return jnp.dot(a, b, preferred_element_type=F32, precision=HIGHEST)


def _rms_rows(x, g):
    return x * lax.rsqrt(jnp.mean(x * x, axis=-1, keepdims=True) + EPS) * g


def _silu(x):
    return x * jax.nn.sigmoid(x)


def _softplus(x):
    return jnp.maximum(x, 0.0) + jnp.log1p(jnp.exp(-jnp.abs(x)))


def _rope_rows(x, cos, sin_signed):
    n = x.shape[-1]
    reps = n // 128
    if reps > 1:
        cos = jnp.concatenate([cos] * reps, axis=-1)
        sin_signed = jnp.concatenate([sin_signed] * reps, axis=-1)
    lane = lax.broadcasted_iota(I32, x.shape, x.ndim - 1)
    first_half = (lane % HD_B) < (HD_B // 2)
    partner = jnp.where(first_half, pltpu.roll(x, n - HD_B // 2, x.ndim - 1),
                        pltpu.roll(x, HD_B // 2, x.ndim - 1))
    return x * cos + partner * sin_signed


def _rope_cols(x, cos_t, sin_t):
    x1, x2 = x[:32], x[32:]
    return jnp.concatenate([x1 * cos_t - x2 * sin_t, x2 * cos_t + x1 * sin_t], axis=0)


def _seg_rms(x, ones_bd, g):
    sq = x * x
    hi = sq.astype(BF16)
    lo = (sq - hi.astype(F32)).astype(BF16)
    ss = (jnp.dot(hi, ones_bd, preferred_element_type=F32)
          + jnp.dot(lo, ones_bd, preferred_element_type=F32))
    return x * lax.rsqrt(ss * (1.0 / HD_B) + EPS) * g


def _small_block(sm, cos, sin_signed, alog_row, dtb_row):
    lane = lax.broadcasted_iota(I32, sm.shape, 1)
    roped = _rope_rows(sm, cos, sin_signed)
    beta = jax.nn.sigmoid(sm)
    g = -jnp.exp(alog_row) * _softplus(sm + dtb_row)
    wi = sm * (H_IDX ** -0.5)
    out = jnp.where(lane < _SM_BETA, roped,
                    jnp.where(lane < _SM_G, beta, jnp.where(lane < _SM_WI, g, wi)))
    return out


def _proj_prompt_body(x_ref, g1_ref, wqkv_ref, wz_ref, wkv_ref, wsm_ref,
                      wbqt_ref, wiqt_ref, wvt_ref, wsmt_ref, convw_ref, kng_ref, qngt_ref, alog_ref, dtb_ref,
                      alogt_ref, dtbt_ref, ones_ref, cos_ref, sin_ref, cost_ref, sint_ref,
                      q_out, k_out, v_out, z_out, kv_out, kbf_out, kibf_out, sm_out, kidx_out,
                      qtp_out, qit_out, vt_out, gt_out, wit_out, conv_out, ubuf, *, tm, kc):
    t = pl.program_id(1)
    x = x_ref[0]
    xn = _rms_rows(x, g1_ref[...]).astype(BF16)

    u = jnp.dot(xn, wqkv_ref[...], preferred_element_type=F32)

    @pl.when(t == 0)
    def _():
        ubuf[0:8, :] = jnp.zeros((8, CONV_CH), F32)

    ubuf[8:tm + 8, :] = u
    cw = convw_ref[...]
    acc = (ubuf[5:5 + tm, :] * cw[0:1] + ubuf[6:6 + tm, :] * cw[1:2]
           + ubuf[7:7 + tm, :] * cw[2:3] + u * cw[3:4])
    conv_out[0] = ubuf[tm + 5:tm + 8, :]
    ubuf[0:8, :] = ubuf[tm:tm + 8, :]
    c = _silu(acc)
    for h in range(H_A):
        qh = c[:, h * DK_A:(h + 1) * DK_A]
        kh = c[:, (H_A + h) * DK_A:(H_A + h + 1) * DK_A]
        q_out[:, h * DK_A:(h + 1) * DK_A] = qh * (lax.rsqrt(jnp.sum(qh * qh, -1, keepdims=True) + EPS)
                                                 * (DK_A ** -0.5))
        k_out[:, h * DK_A:(h + 1) * DK_A] = kh * lax.rsqrt(jnp.sum(kh * kh, -1, keepdims=True) + EPS)
    v_out[...] = c[:, 2 * H_A * DK_A:]
    z_out[...] = jnp.dot(xn, wz_ref[...], preferred_element_type=F32)

    cos, sin = cos_ref[...], sin_ref[...]
    ones_bd = ones_ref[...]

    kv = jnp.dot(xn, wkv_ref[...], preferred_element_type=F32)
    kb = _rope_rows(_seg_rms(kv[:, :128], ones_bd, kng_ref[...]), cos, sin)
    kv_out[:, 0:128] = kb
    kv_out[:, 128:256] = kv[:, 128:]
    kbf_out[...] = kb.astype(BF16)

    sm = jnp.dot(xn, wsm_ref[...], preferred_element_type=F32)
    smo = _small_block(sm, cos, sin, alog_ref[...], dtb_ref[...])
    sm_out[...] = smo
    kidx_out[...] = smo[:, :D_IDX]
    kibf_out[...] = smo[:, :D_IDX].astype(BF16)

    nt_dims = (((1,), (1,)), ((), ()))
    cos_t, sin_t = cost_ref[...], sint_ref[...]
    qt = lax.dot_general(wbqt_ref[...], xn, nt_dims, preferred_element_type=F32)
    qit = lax.dot_general(wiqt_ref[...], xn, nt_dims, preferred_element_type=F32)
    vt = lax.dot_general(wvt_ref[...], xn, nt_dims, preferred_element_type=F32)
    smt = lax.dot_general(wsmt_ref[...], xn, nt_dims, preferred_element_type=F32)
    qg = jnp.concatenate([qngt_ref[...]] * (tm // 128), axis=1)
    zero64 = jnp.zeros((HD_B, tm), BF16)
    for h in range(H_B):
        qh = qt[HD_B * h:HD_B * (h + 1)]
        qh = qh * lax.rsqrt(jnp.mean(qh * qh, axis=0, keepdims=True) + EPS) * qg
        qh = (_rope_cols(qh, cos_t, sin_t) * (HD_B ** -0.5 * LOG2_E)).astype(BF16)
        n = h // (H_B // N_KV_B)
        qtp_out[0, h, HD_B * n:HD_B * (n + 1), :] = qh
        qtp_out[0, h, HD_B * (1 - n):HD_B * (2 - n), :] = zero64
        qit_out[0, h] = (_rope_cols(qit[D_IDX * h:D_IDX * (h + 1)], cos_t, sin_t) * (D_IDX ** -0.5)).astype(BF16)
    vtb = vt.astype(BF16)
    for j in range(tm // kc):
        vt_out[0, j] = vtb[:, j * kc:(j + 1) * kc]
    sm8 = smt[64:72]
    row = lax.broadcasted_iota(I32, sm8.shape, 0)
    g8 = -jnp.exp(alogt_ref[...]) * _softplus(sm8 + dtbt_ref[...])
    gt_out[0] = jnp.where(row < 4, jax.nn.sigmoid(sm8), g8)
    wit_out[0] = smt[72:80] * (H_IDX ** -0.5)


def _proj_prompt(x, w, tabs, *, tm, kc):
    b, t, d = x.shape
    nt = t // tm
    m = b * t
    row = lambda n: pl.BlockSpec((tm, n), lambda i, j: (i * nt + j, 0))
    ins = [
        pl.BlockSpec((1, tm, d), lambda i, j: (i, j, 0)),
        _const_spec((1, d)),
        _const_spec((d, CONV_CH)), _const_spec((d, 512)), _const_spec((d, 256)), _const_spec((d, 128)),
        _const_spec((512, d)), _const_spec((512, d)), _const_spec((128, d)), _const_spec((128, d)),
        _const_spec((CONV_W, CONV_CH)), _const_spec((1, 128)), _const_spec((HD_B, 128)),
        _const_spec((1, 128)), _const_spec((1, 128)), _const_spec((8, 1)), _const_spec((8, 1)),
        _const_spec((128, 128)),
        pl.BlockSpec((tm, 128), lambda i, j: (j, 0)), pl.BlockSpec((tm, 128), lambda i, j: (j, 0)),
        pl.BlockSpec((32, tm), lambda i, j: (0, j)), pl.BlockSpec((32, tm), lambda i, j: (0, j)),
    ]
    nkc = t // kc
    outs = [
        (jax.ShapeDtypeStruct((m, 512), F32), row(512)),
        (jax.ShapeDtypeStruct((m, 512), F32), row(512)),
        (jax.ShapeDtypeStruct((m, 512), F32), row(512)),
        (jax.ShapeDtypeStruct((m, 512), F32), row(512)),
        (jax.ShapeDtypeStruct((m, 256), F32), row(256)),
        (jax.ShapeDtypeStruct((m, 128), BF16), row(128)),
        (jax.ShapeDtypeStruct((m, D_IDX), BF16), row(D_IDX)),
        (jax.ShapeDtypeStruct((m, 128), F32), row(128)),
        (jax.ShapeDtypeStruct((m, D_IDX), F32), row(D_IDX)),
        (jax.ShapeDtypeStruct((b, H_B, 128, t), BF16),
         pl.BlockSpec((1, H_B, 128, tm), lambda i, j: (i, 0, 0, j))),
        (jax.ShapeDtypeStruct((b, H_IDX, D_IDX, t), BF16),
         pl.BlockSpec((1, H_IDX, D_IDX, tm), lambda i, j: (i, 0, 0, j))),
        (jax.ShapeDtypeStruct((b, nkc, 128, kc), BF16),
         pl.BlockSpec((1, tm // kc, 128, kc), lambda i, j: (i, j, 0, 0))),
        (jax.ShapeDtypeStruct((b, 8, t), F32), pl.BlockSpec((1, 8, tm), lambda i, j: (i, 0, j))),
        (jax.ShapeDtypeStruct((b, H_IDX, t), F32), pl.BlockSpec((1, H_IDX, tm), lambda i, j: (i, 0, j))),
        (jax.ShapeDtypeStruct((b, CONV_W - 1, CONV_CH), F32),
         pl.BlockSpec((1, CONV_W - 1, CONV_CH), lambda i, j: (i, 0, 0))),
    ]
    return pl.pallas_call(
        functools.partial(_proj_prompt_body, tm=tm, kc=kc),
        out_shape=[o[0] for o in outs],
        grid=(b, nt),
        in_specs=ins,
        out_specs=[o[1] for o in outs],
        scratch_shapes=[pltpu.VMEM((tm + 8, CONV_CH), F32)],
        compiler_params=_cparams(2),
        name="proj_prompt",
    )(x, w["g1"], w["wqkv"], w["wz"], w["wkv"], w["wsm"], w["wbqt"], w["wiqt"], w["wvt"], w["wsmt"],
      w["convw"], w["kng"], w["qngt"], w["alog_row"], w["dtb_row"], w["alog_col"], w["dtb_col"],
      w["ones_bd128"], tabs["cos"], tabs["sin"], tabs["cos_t"], tabs["sin_t"])


def _split3(x):
    hi = x.astype(BF16)
    r1 = x - hi.astype(F32)
    mid = r1.astype(BF16)
    lo = (r1 - mid.astype(F32)).astype(BF16)
    return hi, mid, lo


def _gdn_body(q_ref, k_ref, v_ref, z_ref, sm_ref, gt_ref, gg_ref, bdk_ref, bdp_ref, o_ref, s_ref, *, nb):
    c = pl.program_id(1)

    @pl.when(c == 0)
    def _():
        s_ref[...] = jnp.zeros_like(s_ref)

    n, nh = CHUNK, H_A
    w4 = nh * n
    ri4 = lax.broadcasted_iota(I32, (n, w4), 0)
    ci4 = lax.broadcasted_iota(I32, (n, w4), 1) % n
    strict4, incl4 = ri4 > ci4, ri4 >= ci4
    eye4 = jnp.where(ri4 == ci4, 1.0, 0.0)
    triu4 = jnp.where(ri4 <= ci4, 1.0, 0.0).astype(BF16)
    ri = lax.broadcasted_iota(I32, (n, n), 0)
    tril = jnp.where(ri >= lax.broadcasted_iota(I32, (n, n), 1), 1.0, 0.0).astype(BF16)
    blk_row = lax.broadcasted_iota(I32, (1, w4), 1) // n
    low_half = lax.broadcasted_iota(I32, (n, 128), 1) < n
    zeros_pad = jnp.zeros((n, 2 * DV_A), BF16)
    gg = gg_ref[...]
    dot = functools.partial(jnp.dot, preferred_element_type=F32)

    def pad_rows(x, e):
        z = zeros_pad[:, :x.shape[1]]
        return jnp.concatenate([x, z] if e == 0 else [z, x], axis=0)

    seqs = range(nb)
    st = [dict() for _ in seqs]

    for bi in seqs:
        hi, mid, lo = _split3(sm_ref[bi])
        st[bi]["gc_all"] = dot(tril, hi) + dot(tril, mid) + dot(tril, lo)
        hi, mid, lo = _split3(gt_ref[bi, 0])
        st[bi]["grows"] = dot(hi, triu4) + dot(mid, triu4) + dot(lo, triu4)

    for bi in seqs:
        d = st[bi]
        sm, k, grows = sm_ref[bi], k_ref[bi], d["grows"]
        grow = jnp.where(blk_row == 0, grows[4:5], jnp.where(blk_row == 1, grows[5:6],
                         jnp.where(blk_row == 2, grows[6:7], grows[7:8])))
        gcols = [jnp.broadcast_to(d["gc_all"][:, _SM_G + h:_SM_G + h + 1], (n, 128)) for h in range(nh)]
        betas = [jnp.broadcast_to(sm[:, _SM_BETA + h:_SM_BETA + h + 1], (n, 128)) for h in range(nh)]
        d["gc512"] = jnp.concatenate(gcols, axis=1)
        d["beta512"] = jnp.concatenate(betas, axis=1)
        gc256 = jnp.concatenate([jnp.where(low_half, gcols[0], gcols[1]),
                                 jnp.where(low_half, gcols[2], gcols[3])], axis=1)
        d["dec"] = jnp.exp(jnp.minimum(gc256 - grow, 0.0))
        d["kb"] = k * d["beta512"]
        kbd = jnp.concatenate([k.astype(BF16)] * nh, axis=0) * bdk_ref[...]
        d["aq"] = lax.dot_general(jnp.concatenate([d["kb"], q_ref[bi]], axis=0).astype(BF16), kbd,
                                  (((1,), (1,)), ((), ())), preferred_element_type=F32)

    for bi in seqs:
        d = st[bi]
        x = jnp.where(strict4, -d["aq"][:n] * d["dec"], 0.0)
        d["qkm"] = jnp.where(incl4, d["aq"][n:] * d["dec"], 0.0).astype(BF16)
        d["inv"], d["p"] = eye4 + x, x
    for lvl in range(6):
        for bi in seqs:
            d = st[bi]
            p, inv = d["p"], d["inv"]
            wbd = jnp.concatenate([p.astype(BF16)] * nh, axis=0) * bdp_ref[...]
            if lvl == 0:
                d["p"] = dot(p.astype(BF16), wbd)
            elif lvl < 5:
                r = dot(jnp.concatenate([p, inv], axis=0).astype(BF16), wbd)
                d["p"], d["inv"] = r[:n], inv + r[n:]
            else:
                d["inv"] = inv + dot(inv.astype(BF16), wbd)

    heads = [(bi, h) for bi in seqs for h in range(nh)]
    hsl = lambda h: slice(h * DK_A, (h + 1) * DK_A)
    psl = lambda h: slice(128 * (h // 2), 128 * (h // 2 + 1))
    for bi in seqs:
        d = st[bi]
        k, gc512 = k_ref[bi], d["gc512"]
        d["tinv"] = d["inv"].astype(BF16)
        egc = jnp.exp(gc512)
        d["rhs_v"] = (v_ref[bi] * d["beta512"]).astype(BF16)
        d["rhs_k"] = (d["kb"] * egc).astype(BF16)
        g_last = gc512[n - 1:n, :]
        d["q_dec"] = q_ref[bi] * egc
        d["k_dec"] = (k * jnp.exp(g_last - gc512)).astype(BF16)
        d["c_dec"] = jnp.exp(g_last)
    sol, ws, vn = {}, {}, {}
    for bi, h in heads:
        d = st[bi]
        rhs = pad_rows(jnp.concatenate([d["rhs_v"][:, hsl(h)], d["rhs_k"][:, hsl(h)]], axis=1), h % 2)
        sol[bi, h] = dot(d["tinv"][:, psl(h)], rhs)
    for bi, h in heads:
        lhs = jnp.concatenate([sol[bi, h][:, DV_A:], st[bi]["q_dec"][:, hsl(h)]], axis=0)
        ws[bi, h] = dot(lhs.astype(BF16), s_ref[bi, h].astype(BF16))
    for bi, h in heads:
        d = st[bi]
        vn[bi, h] = (sol[bi, h][:, :DV_A] - ws[bi, h][:n]).astype(BF16)
        s_ref[bi, h] = d["c_dec"][:, hsl(h)] * s_ref[bi, h] + lax.dot_general(
            d["k_dec"][:, hsl(h)], vn[bi, h], (((0,), (0,)), ((), ())), preferred_element_type=F32)
    for bi, h in heads:
        o = ws[bi, h][n:] + dot(st[bi]["qkm"][:, psl(h)], pad_rows(vn[bi, h], h % 2))
        o_ref[bi, :, hsl(h)] = (_rms_rows(o, gg) * _silu(z_ref[bi, :, hsl(h)])).astype(BF16)


def _gdn_prompt(q, k, v, z, sm, gt, gg, *, b, t, nb):
    nc = t // CHUNK
    r3 = lambda a, n: a.reshape(b, t, n)
    blk = lambda n: pl.BlockSpec((nb, CHUNK, n), lambda i, j: (i, j, 0))
    w4 = H_A * CHUNK
    bdk = np.kron(np.eye(H_A, dtype=np.float32), np.ones((CHUNK, DK_A), np.float32))
    bdp = np.kron(np.eye(H_A, dtype=np.float32), np.ones((CHUNK, CHUNK), np.float32))
    return pl.pallas_call(
        functools.partial(_gdn_body, nb=nb),
        out_shape=[jax.ShapeDtypeStruct((b, t, 512), BF16),
                   jax.ShapeDtypeStruct((b, H_A, DK_A, DV_A), F32)],
        grid=(b // nb, nc),
        in_specs=[blk(512), blk(512), blk(512), blk(512), blk(128),
                  pl.BlockSpec((nb, 1, 8, CHUNK), lambda i, j: (i, j, 0, 0)),
                  _const_spec((1, DV_A)), _const_spec((w4, H_A * DK_A)), _const_spec((w4, w4))],
        out_specs=[blk(512), pl.BlockSpec((nb, H_A, DK_A, DV_A), lambda i, j: (i, 0, 0, 0))],
        compiler_params=_cparams(2),
        name="gdn_prompt",
    )(r3(q, 512), r3(k, 512), r3(v, 512), r3(z, 512), r3(sm, 128),
      jnp.swapaxes(gt.reshape(b, 8, nc, CHUNK), 1, 2), gg, jnp.asarray(bdk, BF16), jnp.asarray(bdp, BF16))


def _ordinal_to_float(k):
    k = jnp.maximum(k, NEG_INF_KEY)
    return pltpu.bitcast(k ^ ((k >> 31) & 0x7FFFFFFF), F32)


def _kth_largest(count_ge, shape, k_top):
    def bit_step(i, o):
        cand = o + jnp.left_shift(jnp.int32(1), 31 - i)
        return jnp.where(count_ge(_ordinal_to_float(cand)) >= k_top, cand, o)

    o = lax.fori_loop(0, 32, bit_step, jnp.full(shape, INT_MIN, I32))
    return _ordinal_to_float(o), o


def _topk_threshold(count_ge, count_gt_eq, count_eq_below, shape, k_top, idx_bits):
    t, o = _kth_largest(count_ge, shape, k_top)
    c_gt, c_eq = count_gt_eq(t)
    need = k_top - c_gt
    untied = jnp.logical_or(c_eq == need, o <= NEG_INF_KEY)
    all_untied = jnp.min(jnp.where(untied, 1.0, 0.0)) > 0.5

    def tie_search(_):
        def idx_step(i, x):
            cand = x + jnp.left_shift(jnp.int32(1), idx_bits - 1 - i)
            return jnp.where(count_eq_below(t, cand) < need, cand, x)
        return lax.fori_loop(0, idx_bits, idx_step, jnp.zeros(shape, I32))

    x = lax.cond(all_untied, lambda _: jnp.full(shape, 2 ** 30, I32), tie_search, 0)
    return t, x


_ACC_ROWS = HD_B + 16


def _dsa_body(qit_ref, wit_ref, qtp_ref, ki_ref, k_ref, vt_ref, o_ref,
              sc_ref, m_ref, acc_ref, *, tq, kc, k_top):
    i = pl.program_id(1)
    nkc = i + 1
    rowi = lax.broadcasted_iota(I32, (kc, tq), 0)
    coli = lax.broadcasted_iota(I32, (kc, tq), 1)
    ng = kc // 32

    wit = wit_ref[0]

    def score_chunk(c, carry):
        kic = ki_ref[0, c]
        score = jnp.zeros((kc, tq), F32)
        for h in range(H_IDX):
            s = jnp.dot(kic, qit_ref[0, h], preferred_element_type=F32)
            score = score + jnp.maximum(s, 0.0) * wit[h:h + 1, :]
        causal = rowi + (c - i) * kc <= coli
        sc_ref[c] = jnp.where(causal, score, -jnp.inf)
        return carry

    lax.fori_loop(0, nkc, score_chunk, 0)

    def key_sum(f):
        def body(c, acc):
            ind = f(sc_ref[c], c * kc)
            return acc + jnp.sum(ind.reshape(ng, 32, tq), axis=0)
        acc = lax.fori_loop(0, nkc, body, jnp.zeros((32, tq), F32))
        return jnp.sum(acc, axis=0, keepdims=True).astype(I32)

    t, _ = _kth_largest(lambda cand: key_sum(lambda s, off: jnp.where(s >= cand, 1.0, 0.0)), (1, tq), k_top)
    ties_wanted = (k_top - key_sum(lambda s, off: jnp.where(s > t, 1.0, 0.0))).astype(F32)

    m_ref[...] = jnp.full(m_ref.shape, NEG_BIG, F32)
    acc_ref[...] = jnp.zeros(acc_ref.shape, F32)
    ones_rows = jnp.ones((_ACC_ROWS - HD_B, kc), BF16)
    prefix = jnp.where(lax.broadcasted_iota(I32, (kc, kc), 0) >= lax.broadcasted_iota(I32, (kc, kc), 1),
                       1.0, 0.0).astype(BF16)

    def attn_chunk(c, ties_before):
        kk = k_ref[0, c]
        sc = sc_ref[c]
        tied = jnp.where(sc == t, 1.0, 0.0).astype(BF16)
        rank = ties_before + jnp.dot(prefix, tied, preferred_element_type=F32)
        tie = jnp.where(sc == t, jnp.where(rank <= ties_wanted, 0.0, NEG_BIG), NEG_BIG)
        bias = jnp.where(sc > -jnp.inf, jnp.where(sc > t, 0.0, tie), NEG_BIG)
        ties_after = rank[kc - 1:kc, :]
        vt = vt_ref[0, c]
        vaug = [jnp.concatenate([vt[HD_B * n:HD_B * (n + 1)], ones_rows], axis=0) for n in range(N_KV_B)]
        ahead = 3
        logits = {h: jnp.dot(kk, qtp_ref[0, h], preferred_element_type=F32) for h in range(ahead)}
        for h in range(H_B):
            if h + ahead < H_B:
                logits[h + ahead] = jnp.dot(kk, qtp_ref[0, h + ahead], preferred_element_type=F32)
            n = h // (H_B // N_KV_B)
            s = bias + logits.pop(h)
            m_old = m_ref[h:h + 1, :]
            m_new = jnp.maximum(m_old, jnp.max(s, axis=0, keepdims=True))
            alpha = jnp.exp2(m_old - m_new)
            pexp = jnp.exp2(s - m_new).astype(BF16)
            acc_ref[h] = acc_ref[h] * alpha + jnp.dot(vaug[n], pexp, preferred_element_type=F32)
            m_ref[h:h + 1, :] = m_new
        return ties_after

    lax.fori_loop(0, nkc, attn_chunk, jnp.zeros((1, tq), F32))
    for p in range(H_B // 2):
        halves = []
        for h in (2 * p, 2 * p + 1):
            acc = acc_ref[h]
            halves.append(acc[:HD_B] / acc[HD_B:HD_B + 1])
        o_ref[:, 128 * p:128 * (p + 1)] = jnp.concatenate(halves, axis=0).T.astype(BF16)


def _dsa_prompt(qit, wit, qtp, kibf, kbf, vt, *, b, t, tq, k_top):
    kc = tq
    nq = t // tq
    nkc = t // kc
    whole = lambda *s: pl.BlockSpec((1,) + s, lambda i, j: (i,) + (0,) * len(s))
    return pl.pallas_call(
        functools.partial(_dsa_body, tq=tq, kc=kc, k_top=k_top),
        out_shape=jax.ShapeDtypeStruct((b * t, H_B * HD_B), BF16),
        grid=(b, nq),
        in_specs=[pl.BlockSpec((1, H_IDX, D_IDX, tq), lambda i, j: (i, 0, 0, j)),
                  pl.BlockSpec((1, H_IDX, tq), lambda i, j: (i, 0, j)),
                  pl.BlockSpec((1, H_B, 128, tq), lambda i, j: (i, 0, 0, j)),
                  whole(nkc, kc, D_IDX), whole(nkc, kc, 128), whole(nkc, 128, kc)],
        out_specs=pl.BlockSpec((tq, H_B * HD_B), lambda i, j: (i * nq + j, 0)),
        scratch_shapes=[pltpu.VMEM((nkc, kc, tq), F32),
                        pltpu.VMEM((H_B, tq), F32), pltpu.VMEM((H_B, _ACC_ROWS, tq), F32)],
        compiler_params=_cparams(2),
        name="dsa_prompt",
    )(qit, wit, qtp, kibf.reshape(b, nkc, kc, D_IDX), kbf.reshape(b, nkc, kc, 128), vt)


def _merge_body(x_ref, oa_ref, ob_ref, g1_ref, wg_ref, woa_ref, wob_ref, wo_ref, h_ref):
    x = x_ref[...]
    xn = _rms_rows(x, g1_ref[...]).astype(BF16)
    gates = jax.nn.sigmoid(jnp.dot(xn, wg_ref[...], preferred_element_type=F32))
    a = jnp.dot(oa_ref[...], woa_ref[...], preferred_element_type=F32)
    bb = jnp.dot(ob_ref[...], wob_ref[...], preferred_element_type=F32)
    mix = gates[:, :D_MODEL] * a + gates[:, D_MODEL:] * bb
    h_ref[...] = x + jnp.dot(mix.astype(BF16), wo_ref[...], preferred_element_type=F32)


def _merge(x, oa, ob, wob, w, *, tm):
    m, d = x.shape
    nb = ob.shape[1]
    row = lambda n: pl.BlockSpec((tm, n), lambda i: (i, 0))
    return pl.pallas_call(
        _merge_body,
        out_shape=jax.ShapeDtypeStruct((m, d), F32),
        grid=(m // tm,),
        in_specs=[row(d), row(512), row(nb), _const_spec((1, d)), _const_spec((d, 2 * d)),
                  _const_spec((512, d)), _const_spec((nb, d)), _const_spec((d, d))],
        out_specs=row(d),
        compiler_params=_cparams(1),
        name="merge",
    )(x, oa, ob, w["g1"], w["wg"], w["woa"], wob, w["wo"])


def _mlp_body(h_ref, g2_ref, wup_ref, wdn_ref, y_ref):
    h = h_ref[...]
    hn = _rms_rows(h, g2_ref[...]).astype(BF16)
    up = jnp.dot(hn, wup_ref[...], preferred_element_type=F32)
    act = jnp.square(jnp.maximum(up, 0.0)).astype(BF16)
    y_ref[...] = h + jnp.dot(act, wdn_ref[...], preferred_element_type=F32)


def _mlp(h, w, *, tm):
    m, d = h.shape
    row = pl.BlockSpec((tm, d), lambda i: (i, 0))
    return pl.pallas_call(
        _mlp_body,
        out_shape=jax.ShapeDtypeStruct((m, d), F32),
        grid=(m // tm,),
        in_specs=[row, _const_spec((1, d)), _const_spec((d, D_FF)), _const_spec((D_FF, d))],
        out_specs=row,
        compiler_params=_cparams(1),
        name="mlp",
    )(h, w["g2"], w["wup"], w["wdn"])


def _proj_decode_body(x_ref, st_ref, g1_ref, wqkv_ref, wz_ref, wbq_ref, wkv_ref, wiq_ref, wsm_ref,
                      convw_ref, qng_ref, kng_ref, alog_ref, dtb_ref, ones_ref, seg_ref, cos_ref, sin_ref,
                      u_out, q_out, k_out, v_out, z_out, qb_out, kv_out, qi_out, sm_out, ss_out):
    x = x_ref[...]
    xn = _rms_rows(x, g1_ref[...]).astype(BF16)
    u = jnp.dot(xn, wqkv_ref[...], preferred_element_type=F32)
    u_out[...] = u
    cw = convw_ref[...]
    acc = st_ref[0] * cw[0:1] + st_ref[1] * cw[1:2] + st_ref[2] * cw[2:3] + u * cw[3:4]
    c = _silu(acc)
    for h in range(H_A):
        qh = c[:, h * DK_A:(h + 1) * DK_A]
        kh = c[:, (H_A + h) * DK_A:(H_A + h + 1) * DK_A]
        q_out[:, h * DK_A:(h + 1) * DK_A] = qh * (lax.rsqrt(jnp.sum(qh * qh, -1, keepdims=True) + EPS)
                                                 * (DK_A ** -0.5))
        k_out[:, h * DK_A:(h + 1) * DK_A] = kh * lax.rsqrt(jnp.sum(kh * kh, -1, keepdims=True) + EPS)
    v_out[...] = c[:, 2 * H_A * DK_A:]
    z_out[...] = jnp.dot(xn, wz_ref[...], preferred_element_type=F32)
    cos, sin = cos_ref[...], sin_ref[...]
    ones_bd = ones_ref[...]
    bq = jnp.dot(xn, wbq_ref[...], preferred_element_type=F32)
    qb = _rope_rows(_seg_rms(bq, ones_bd, qng_ref[...]), cos, sin)
    qb_out[...] = (qb * (HD_B ** -0.5)).astype(BF16)
    kv = jnp.dot(xn, wkv_ref[...], preferred_element_type=F32)
    kb = _rope_rows(_seg_rms(kv[:, :128], ones_bd[:128, :128], kng_ref[...]), cos, sin)
    kv_out[:, 0:128] = kb
    kv_out[:, 128:256] = kv[:, 128:]
    iq = jnp.dot(xn, wiq_ref[...], preferred_element_type=F32)
    qi = (_rope_rows(iq, cos, sin) * (D_IDX ** -0.5)).astype(BF16)
    qi_out[...] = qi
    sm = jnp.dot(xn, wsm_ref[...], preferred_element_type=F32)
    smo = _small_block(sm, cos, sin, alog_ref[...], dtb_ref[...])
    sm_out[...] = smo
    lane = lax.broadcasted_iota(I32, smo.shape, 1)
    ki = smo.astype(BF16).astype(F32)
    ki2 = jnp.where(lane < D_IDX, ki, pltpu.roll(ki, D_IDX, 1))
    prod = qi.astype(F32) * jnp.concatenate([ki2] * 4, axis=-1)
    sh = _mm_f32(prod, seg_ref[...])
    contrib = jnp.where(jnp.logical_and(lane >= _SM_WI, lane < _SM_WI + H_IDX),
                        jnp.maximum(sh, 0.0) * smo, 0.0)
    ss_out[...] = jnp.broadcast_to(jnp.sum(contrib, axis=-1, keepdims=True), ss_out.shape)


def _proj_decode(x, st, w, tabs):
    r, d = x.shape
    outs = [(r, CONV_CH, F32), (r, 512, F32), (r, 512, F32), (r, 512, F32), (r, 512, F32), (r, 512, BF16),
            (r, 256, F32), (r, 512, BF16), (r, 128, F32), (r, 128, F32)]
    args = (x, st, w["g1"], w["wqkv"], w["wz"], w["wbq"], w["wkv"], w["wiq"], w["wsm"], w["convw"],
            w["qng"], w["kng"], w["alog_row"], w["dtb_row"], w["ones_bd"], w["seg_sum"], tabs["cos"], tabs["sin"])
    return pl.pallas_call(
        _proj_decode_body,
        out_shape=[jax.ShapeDtypeStruct((a, n), dt) for a, n, dt in outs],
        compiler_params=pltpu.CompilerParams(vmem_limit_bytes=VMEM_LIMIT),
        name="proj_decode",
    )(*args)


def _gdn_decode_body(qc_ref, kc_ref, v_ref, z_ref, sm_ref, gg_ref, s_ref, o_ref, so_ref):
    sm = sm_ref[0]
    gg = gg_ref[...]
    for h in range(H_A):
        g = sm[:, _SM_G + h:_SM_G + h + 1]
        beta = sm[:, _SM_BETA + h:_SM_BETA + h + 1]
        kcol, qcol = kc_ref[0, h], qc_ref[0, h]
        v = v_ref[0, h]
        s = s_ref[0, h] * jnp.exp(g)
        vn = beta * (v - jnp.sum(s * kcol, axis=0, keepdims=True))
        s = s + kcol * vn
        so_ref[0, h] = s
        o = jnp.sum(s * qcol, axis=0, keepdims=True)
        o_ref[0, h] = (_rms_rows(o, gg) * _silu(z_ref[0, h])).astype(BF16)


def _gdn_decode(q, k, v, z, sm, gg, state):
    r = q.shape[0]
    col = lambda a: a.reshape(r, H_A, DK_A, 1)
    rowv = lambda a: a.reshape(r, H_A, 1, DV_A)
    cspec = pl.BlockSpec((1, H_A, DK_A, 1), lambda i: (i, 0, 0, 0))
    rspec = pl.BlockSpec((1, H_A, 1, DV_A), lambda i: (i, 0, 0, 0))
    sspec = pl.BlockSpec((1, H_A, DK_A, DV_A), lambda i: (i, 0, 0, 0))
    o, s_new = pl.pallas_call(
        _gdn_decode_body,
        out_shape=[jax.ShapeDtypeStruct((r, H_A, 1, DV_A), BF16),
                   jax.ShapeDtypeStruct((r, H_A, DK_A, DV_A), F32)],
        grid=(r,),
        in_specs=[cspec, cspec, rspec, rspec, pl.BlockSpec((1, 1, 128), lambda i: (i, 0, 0)),
                  _const_spec((1, DV_A)), sspec],
        out_specs=[rspec, sspec],
        compiler_params=_cparams(1),
        name="gdn_decode",
    )(col(q), col(k), rowv(v), rowv(z), sm.reshape(r, 1, 128), gg, state)
    return o.reshape(r, H_A * DV_A), s_new


def _idx_decode_body(pt_ref, qi_ref, w_ref, *refs, npg):
    pages, out_ref = refs[:npg], refs[npg]
    qi = qi_ref[0]
    w = w_ref[0]
    for p in range(npg):
        s = _mm(qi, pages[p][0])
        out_ref[0, :, PAGE_SIZE * p:PAGE_SIZE * (p + 1)] = jnp.sum(jnp.maximum(s, 0.0) * w, axis=0,
                                                                   keepdims=True)


def _idx_decode(page_table, qi, wi_col, cache_kidx, *, npg):
    r, n_pages = page_table.shape
    page_spec = lambda p: pl.BlockSpec((1, D_IDX, PAGE_SIZE), lambda i, j, pt: (pt[i, j * npg + p], 0, 0))
    grid_spec = pltpu.PrefetchScalarGridSpec(
        num_scalar_prefetch=1,
        grid=(r, n_pages // npg),
        in_specs=[pl.BlockSpec((1, H_IDX, D_IDX), lambda i, j, pt: (i, 0, 0)),
                  pl.BlockSpec((1, H_IDX, 1), lambda i, j, pt: (i, 0, 0))]
                 + [page_spec(p) for p in range(npg)],
        out_specs=pl.BlockSpec((1, 1, PAGE_SIZE * npg), lambda i, j, pt: (i, 0, j)),
    )
    return pl.pallas_call(
        functools.partial(_idx_decode_body, npg=npg),
        out_shape=jax.ShapeDtypeStruct((r, 1, n_pages * PAGE_SIZE), F32),
        grid_spec=grid_spec,
        compiler_params=_cparams(2),
        name="idx_decode",
    )(page_table, qi.reshape(r, H_IDX, D_IDX), wi_col, *([cache_kidx] * npg))


def _select_decode_body(sc_ref, ss_ref, bias_ref, bself_ref, *, k_top, cw):
    r, n = sc_ref.shape
    nch = n // cw
    key_ref = sc_ref
    skey = ss_ref[:, 0:1]

    def lane_sum(f):
        def body(c, acc):
            off = pl.multiple_of(c * cw, cw)
            kk = key_ref[:, pl.ds(off, cw)]
            for j in range(cw // 128):
                acc = acc + f(kk[:, 128 * j:128 * (j + 1)], off + 128 * j)
            return acc
        acc = lax.fori_loop(0, nch, body, jnp.zeros((r, 128), F32))
        return jnp.sum(acc, axis=-1, keepdims=True).astype(I32)

    def count_ge(cand):
        cb = jnp.broadcast_to(cand, (r, 128))
        return lane_sum(lambda kk, off: jnp.where(kk >= cb, 1.0, 0.0)) + (skey >= cand).astype(I32)

    def count_gt_eq(t):
        tb = jnp.broadcast_to(t, (r, 128))
        return (lane_sum(lambda kk, off: jnp.where(kk > tb, 1.0, 0.0)) + (skey > t).astype(I32),
                lane_sum(lambda kk, off: jnp.where(kk == tb, 1.0, 0.0)) + (skey == t).astype(I32))

    def count_eq_below(t, j):
        tb = jnp.broadcast_to(t, (r, 128))
        jb = jnp.broadcast_to(j, (r, 128))
        lane = lax.broadcasted_iota(I32, (r, 128), 1)
        return (lane_sum(lambda kk, off: jnp.where(kk == tb, jnp.where(lane + off < jb, 1.0, 0.0), 0.0))
                + jnp.logical_and(skey == t, j > n).astype(I32))

    t, x = _topk_threshold(count_ge, count_gt_eq, count_eq_below, (r, 1), k_top, 15)
    tb = jnp.broadcast_to(t, (r, cw))
    xb = jnp.broadcast_to(x, (r, cw))

    def bias_chunk(c, carry):
        off = pl.multiple_of(c * cw, cw)
        kk = key_ref[:, pl.ds(off, cw)]
        idx = lax.broadcasted_iota(I32, (r, cw), 1) + off
        tie = jnp.where(kk == tb, jnp.where(idx <= xb, 0.0, NEG_BIG), NEG_BIG)
        bias_ref[:, pl.ds(off, cw)] = jnp.where(kk > tb, 0.0, tie)
        return carry

    lax.fori_loop(0, nch, bias_chunk, 0)
    tie_self = jnp.where(skey == t, jnp.where(x >= n, 0.0, NEG_BIG), NEG_BIG)
    bself_ref[...] = jnp.broadcast_to(jnp.where(skey > t, 0.0, tie_self), bself_ref.shape)


def _select_decode(scores, sself, *, k_top):
    r, n = scores.shape
    return pl.pallas_call(
        functools.partial(_select_decode_body, k_top=k_top, cw=512),
        out_shape=[jax.ShapeDtypeStruct((r, n), F32), jax.ShapeDtypeStruct((r, 128), F32)],
        compiler_params=pltpu.CompilerParams(vmem_limit_bytes=VMEM_LIMIT),
        name="select_decode",
    )(scores, sself)


def _attn_decode_body(pt_ref, qp_ref, bias_ref, kvs_ref, bself_ref, *refs, npg, nsteps):
    pages, o_ref, m_ref, l_ref, acc_ref = refs[:npg], refs[npg], refs[npg + 1], refs[npg + 2], refs[npg + 3]
    j = pl.program_id(1)

    @pl.when(j == 0)
    def _():
        m_ref[...] = jnp.full(m_ref.shape, NEG_BIG, F32)
        l_ref[...] = jnp.zeros(l_ref.shape, F32)
        acc_ref[...] = jnp.zeros(acc_ref.shape, F32)

    qp = qp_ref[0]

    ss = [_mm(qp, pages[p][0, 0:128, :]) + bias_ref[0, :, PAGE_SIZE * p:PAGE_SIZE * (p + 1)]
          for p in range(npg)]
    m_old = m_ref[...]
    m_new = m_old
    for s in ss:
        m_new = jnp.maximum(m_new, jnp.max(s, axis=-1, keepdims=True))
    alpha = jnp.exp(m_old - m_new)
    l_new = l_ref[...] * alpha
    acc_new = acc_ref[...] * alpha
    for p in range(npg):
        pe = jnp.exp(ss[p] - m_new)
        l_new = l_new + jnp.sum(pe, axis=-1, keepdims=True)
        acc_new = acc_new + _mm_nt(pe, pages[p][0, 128:256, :])
    m_ref[...] = m_new
    l_ref[...] = l_new
    acc_ref[...] = acc_new

    @pl.when(j == nsteps - 1)
    def _():
        kvs = kvs_ref[0]
        krow = kvs[:, :128].astype(BF16).astype(F32)
        s = jnp.sum(qp.astype(F32) * krow, axis=-1, keepdims=True) + bself_ref[0][:, 0:1]
        m_old = m_ref[...]
        m_new = jnp.maximum(m_old, s)
        alpha = jnp.exp(m_old - m_new)
        p = jnp.exp(s - m_new)
        l = l_ref[...] * alpha + p
        vrow = kvs[:, 128:].astype(BF16).astype(F32)
        acc = acc_ref[...] * alpha + p.astype(BF16).astype(F32) * vrow
        out = acc / l
        rowi = lax.broadcasted_iota(I32, out.shape, 0)
        out = jnp.where(rowi < 4, out, pltpu.roll(out, HD_B, 1))
        o_ref[0] = out.astype(BF16)


def _attn_decode(page_table, qpad, bias, kv_self, bself, cache_kv, *, npg):
    r, n_pages = page_table.shape
    nsteps = n_pages // npg
    page_spec = lambda p: pl.BlockSpec((1, 256, PAGE_SIZE), lambda i, j, pt: (pt[i, j * npg + p], 0, 0))
    grid_spec = pltpu.PrefetchScalarGridSpec(
        num_scalar_prefetch=1,
        grid=(r, nsteps),
        in_specs=[pl.BlockSpec((1, H_B, 128), lambda i, j, pt: (i, 0, 0)),
                  pl.BlockSpec((1, 1, PAGE_SIZE * npg), lambda i, j, pt: (i, 0, j)),
                  pl.BlockSpec((1, 1, 256), lambda i, j, pt: (i, 0, 0)),
                  pl.BlockSpec((1, 1, 128), lambda i, j, pt: (i, 0, 0))]
                 + [page_spec(p) for p in range(npg)],
        out_specs=pl.BlockSpec((1, H_B, 128), lambda i, j, pt: (i, 0, 0)),
        scratch_shapes=[pltpu.VMEM((H_B, 1), F32), pltpu.VMEM((H_B, 1), F32), pltpu.VMEM((H_B, 128), F32)],
    )
    return pl.pallas_call(
        functools.partial(_attn_decode_body, npg=npg, nsteps=nsteps),
        out_shape=jax.ShapeDtypeStruct((r, H_B, 128), BF16),
        grid_spec=grid_spec,
        compiler_params=_cparams(2),
        name="attn_decode",
    )(page_table, qpad, bias.reshape(r, 1, -1), kv_self.reshape(r, 1, 256), bself.reshape(r, 1, 128),
      *([cache_kv] * npg))


def _prep_weights(norm1_g, w_in, conv_w, a_log, dt_bias, gdn_norm_g, q_norm_g, k_norm_g,
                  w_out_a, w_out_b, w_o, norm2_g, w_up, w_down):
    o = _OFF
    wb = w_in.astype(BF16)
    d = w_in.shape[0]
    wsm = jnp.concatenate([wb[:, o["ik"]:o["iw"]], wb[:, o["ab"]:o["bq"]], wb[:, o["iw"]:o["ga"]],
                           jnp.zeros((d, 128 - 80), BF16)], axis=1)

    def lanes(vals, start):
        return jnp.zeros((1, 128), F32).at[0, start:start + vals.shape[0]].set(vals.astype(F32))

    seg = np.zeros((512, 128), np.float32)
    for h in range(H_IDX):
        seg[64 * h:64 * (h + 1), _SM_WI + h] = 1.0
    ones_bd = np.kron(np.eye(8, dtype=np.float32), np.ones((64, 64), np.float32))
    wob = w_out_b.astype(BF16).reshape(H_B, HD_B, d)
    wob_pad = jnp.concatenate([wob, jnp.zeros_like(wob)], axis=1).reshape(H_B * 128, d)
    col8 = lambda v: jnp.concatenate([jnp.zeros((4,), F32), v.astype(F32)]).reshape(8, 1)
    return dict(
        g1=norm1_g.reshape(1, d).astype(F32),
        wqkv=wb[:, o["aq"]:o["az"]], wz=wb[:, o["az"]:o["ab"]], wbq=wb[:, o["bq"]:o["bk"]],
        wkv=wb[:, o["bk"]:o["iq"]], wiq=wb[:, o["iq"]:o["ik"]], wsm=wsm,
        wbqt=wb[:, o["bq"]:o["bk"]].T, wiqt=wb[:, o["iq"]:o["ik"]].T, wvt=wb[:, o["bv"]:o["iq"]].T,
        wsmt=wsm.T, wg=wb[:, o["ga"]:o["end"]],
        convw=conv_w.astype(F32),
        qng=jnp.tile(q_norm_g.astype(F32), H_B).reshape(1, 512),
        kng=jnp.tile(k_norm_g.astype(F32), N_KV_B).reshape(1, 128),
        qngt=jnp.tile(q_norm_g.astype(F32).reshape(HD_B, 1), (1, 128)),
        alog_row=lanes(a_log, _SM_G), dtb_row=lanes(dt_bias, _SM_G),
        alog_col=col8(a_log), dtb_col=col8(dt_bias),
        ones_bd=jnp.asarray(ones_bd, BF16), ones_bd128=jnp.asarray(ones_bd[:128, :128], BF16),
        seg_sum=jnp.asarray(seg),
        gg=gdn_norm_g.reshape(1, DV_A).astype(F32),
        woa=w_out_a.astype(BF16), wob=w_out_b.astype(BF16), wob_pad=wob_pad, wo=w_o.astype(BF16),
        g2=norm2_g.reshape(1, d).astype(F32), wup=w_up.astype(BF16), wdn=w_down.astype(BF16),
    )


def _rope_tables(pos):
    half = HD_B // 2
    inv = ROPE_THETA ** (-jnp.arange(half, dtype=F32) / half)
    ang = pos.astype(F32)[:, None] * inv[None, :]
    cos, sin = jnp.cos(ang), jnp.sin(ang)
    return dict(cos=jnp.tile(cos, (1, 4)), sin=jnp.tile(jnp.concatenate([-sin, sin], axis=1), (1, 2)),
                cos_t=cos.T, sin_t=sin.T)


def _layer_prompt(x, w, *, tm=512, tq=256, nb=4):
    b, t, d = x.shape
    tabs = _rope_tables(jnp.arange(t))
    (q, k, v, z, kv, kbf, kibf, sm, kidx, qtp, qit, vt, gt, wit, conv_new) = _proj_prompt(x, w, tabs, tm=tm, kc=tq)
    oa, ssm = _gdn_prompt(q, k, v, z, sm, gt, w["gg"], b=b, t=t, nb=nb)
    ob = _dsa_prompt(qit, wit, qtp, kibf, kbf, vt, b=b, t=t, tq=tq, k_top=min(TOPK_MAX, t // 4))
    h = _merge(x.reshape(b * t, d), oa.reshape(b * t, 512), ob, w["wob"], w, tm=tm)
    y = _mlp(h, w, tm=tm)
    return (y.reshape(b, t, d), kv.reshape(b, t, 2, N_KV_B, HD_B), kidx.reshape(b, t, D_IDX), ssm, conv_new)


def _layer_decode(x, w, state_ssm, state_conv, cache_kv, cache_kidx, page_table, *, npg=32):
    r, d = x.shape
    n_pages = page_table.shape[1]
    past = n_pages * PAGE_SIZE
    tabs = _rope_tables(jnp.full((1,), past))
    st = jnp.swapaxes(state_conv, 0, 1)
    (u, q, k, v, z, qb, kv, qi, sm, sself) = _proj_decode(x, st, w, tabs)
    oa, ssm = _gdn_decode(q, k, v, z, sm, w["gg"], state_ssm)
    wi_col = sm[:, _SM_WI:_SM_WI + H_IDX].reshape(r, H_IDX, 1)
    kidx_pages = jnp.swapaxes(cache_kidx, 1, 2)
    kv_pages = jnp.transpose(cache_kv, (0, 2, 3, 4, 1)).reshape(-1, 2 * N_KV_B * HD_B, PAGE_SIZE)
    scores = _idx_decode(page_table, qi, wi_col, kidx_pages, npg=npg).reshape(r, past)
    bias, bself = _select_decode(scores, sself, k_top=min(TOPK_MAX, (past + 1) // 4))
    qh = qb.reshape(r, H_B, HD_B)
    zq = jnp.zeros_like(qh[:, :4])
    qpad = jnp.concatenate([jnp.concatenate([qh[:, :4], zq], axis=-1),
                            jnp.concatenate([zq, qh[:, 4:]], axis=-1)], axis=1)
    ob = _attn_decode(page_table, qpad, bias, kv, bself, kv_pages, npg=npg)
    h = _merge(x, oa, ob.reshape(r, H_B * 128), w["wob_pad"], w, tm=r)
    y = _mlp(h, w, tm=r)
    conv_new = jnp.concatenate([state_conv[:, 1:], u[:, None, :]], axis=1)
    return (y, kv.reshape(r, 1, 2, N_KV_B, HD_B), sm[:, :D_IDX].reshape(r, 1, D_IDX), ssm, conv_new)


def kernel(x_prompt, x_sample, cache_kv, cache_kidx, page_table, state_ssm, state_conv, norm1_g, w_in, conv_w,
           a_log, dt_bias, gdn_norm_g, q_norm_g, k_norm_g, w_out_a, w_out_b, w_o, norm2_g, w_up, w_down):
    assert w_in.shape[0] == 1, "single-layer trunk"
    w = _prep_weights(norm1_g[0], w_in[0], conv_w[0], a_log[0], dt_bias[0], gdn_norm_g[0], q_norm_g[0],
                      k_norm_g[0], w_out_a[0], w_out_b[0], w_o[0], norm2_g[0], w_up[0], w_down[0])
    yp, kv_p, ki_p, ssm_p, cv_p = _layer_prompt(x_prompt, w)
    ys, kv_s, ki_s, ssm_s, cv_s = _layer_decode(x_sample[:, 0], w, state_ssm[0], state_conv[0],
                                                cache_kv[0], cache_kidx[0], page_table)
    return (yp, ys[:, None, :], kv_p[None], ki_p[None], ssm_p[None], cv_p[None],
            kv_s[None], ki_s[None], ssm_s[None], cv_s[None])
```

```python
import functools
import math

import jax
import jax.numpy as jnp
import numpy as np
from jax import lax
from jax.experimental import pallas as pl
from jax.experimental.pallas import tpu as pltpu

F32 = jnp.float32
BF16 = jnp.bfloat16
I32 = jnp.int32

D_MODEL = 1024
H_A, DK_A, DV_A = 4, 128, 128
CONV_W = 4
CONV_CH = H_A * (2 * DK_A + DV_A)
CHUNK = 64
HD_B, H_B, N_KV_B = 64, 8, 2
H_IDX, D_IDX = 8, 64
TOPK_MAX = 256
PAGE_SIZE = 128
ROPE_THETA = 10000.0
D_FF = 4 * D_MODEL
EPS = 1e-6

_OFF = dict(aq=0, ak=512, av=1024, az=1536, ab=2048, aa=2052, bq=2056, bk=2568, bv=2696,
            iq=2824, ik=3336, iw=3400, ga=3408, gb=4432, end=5456)
_SM_BETA, _SM_G, _SM_WI = 64, 68, 72

LOG2_E = math.log2(math.e)
NEG_BIG = -1e30
INT_MIN = -(2 ** 31)
NEG_INF_KEY = (0xFF800000 ^ 0x7FFFFFFF) - 2 ** 32

VMEM_LIMIT = 56 * 1024 * 1024
HIGHEST = lax.Precision.HIGHEST


def _cparams(n_axes):
    return pltpu.CompilerParams(dimension_semantics=("arbitrary",) * n_axes,
                                vmem_limit_bytes=VMEM_LIMIT)


def _const_spec(shape):
    nd = len(shape)
    return pl.BlockSpec(shape, lambda *a: (0,) * nd, pipeline_mode=pl.Buffered(1))


def _mm(a, b):
    return jnp.dot(a.astype(BF16), b.astype(BF16), preferred_element_type=F32)


def _mm_nt(a, b):
    return lax.dot_general(a.astype(BF16), b.astype(BF16), (((1,), (1,)), ((), ())),
                           preferred_element_type=F32)


def _mm_tn(a, b):
    return lax.dot_general(a.astype(BF16), b.astype(BF16), (((0,), (0,)), ((), ())),
                           preferred_element_type=F32)


def _mm_f32(a, b):
    return jnp.dot(a, b, preferred_element_type=F32, precision=HIGHEST)


def _rms_rows(x, g):
    return x * lax.rsqrt(jnp.mean(x * x, axis=-1, keepdims=True) + EPS) * g


def _silu(x):
    return x * jax.nn.sigmoid(x)


def _softplus(x):
    return jnp.maximum(x, 0.0) + jnp.log1p(jnp.exp(-jnp.abs(x)))


def _rope_rows(x, cos, sin_signed):
    n = x.shape[-1]
    reps = n // 128
    if reps > 1:
        cos = jnp.concatenate([cos] * reps, axis=-1)
        sin_signed = jnp.concatenate([sin_signed] * reps, axis=-1)
    lane = lax.broadcasted_iota(I32, x.shape, x.ndim - 1)
    first_half = (lane % HD_B) < (HD_B // 2)
    partner = jnp.where(first_half, pltpu.roll(x, n - HD_B // 2, x.ndim - 1),
                        pltpu.roll(x, HD_B // 2, x.ndim - 1))
    return x * cos + partner * sin_signed


def _rope_cols(x, cos_t, sin_t):
    x1, x2 = x[:32], x[32:]
    return jnp.concatenate([x1 * cos_t - x2 * sin_t, x2 * cos_t + x1 * sin_t], axis=0)


def _seg_rms(x, ones_bd, g):
    sq = x * x
    hi = sq.astype(BF16)
    lo = (sq - hi.astype(F32)).astype(BF16)
    ss = (jnp.dot(hi, ones_bd, preferred_element_type=F32)
          + jnp.dot(lo, ones_bd, preferred_element_type=F32))
    return x * lax.rsqrt(ss * (1.0 / HD_B) + EPS) * g


def _small_block(sm, cos, sin_signed, alog_row, dtb_row):
    lane = lax.broadcasted_iota(I32, sm.shape, 1)
    roped = _rope_rows(sm, cos, sin_signed)
    beta = jax.nn.sigmoid(sm)
    g = -jnp.exp(alog_row) * _softplus(sm + dtb_row)
    wi = sm * (H_IDX ** -0.5)
    out = jnp.where(lane < _SM_BETA, roped,
                    jnp.where(lane < _SM_G, beta, jnp.where(lane < _SM_WI, g, wi)))
    return out


def _proj_prompt_body(x_ref, g1_ref, wqkv_ref, wz_ref, wkv_ref, wsm_ref,
                      wbqt_ref, wiqt_ref, wvt_ref, wsmt_ref, convw_ref, kng_ref, qngt_ref, alog_ref, dtb_ref,
                      alogt_ref, dtbt_ref, ones_ref, cos_ref, sin_ref, cost_ref, sint_ref,
                      q_out, k_out, v_out, z_out, kv_out, kbf_out, kibf_out, sm_out, kidx_out,
                      qtp_out, qit_out, vt_out, gt_out, wit_out, conv_out, ubuf, *, tm, kc):
    t = pl.program_id(1)
    x = x_ref[0]
    xn = _rms_rows(x, g1_ref[...]).astype(BF16)

    u = jnp.dot(xn, wqkv_ref[...], preferred_element_type=F32)

    @pl.when(t == 0)
    def _():
        ubuf[0:8, :] = jnp.zeros((8, CONV_CH), F32)

    ubuf[8:tm + 8, :] = u
    cw = convw_ref[...]
    acc = (ubuf[5:5 + tm, :] * cw[0:1] + ubuf[6:6 + tm, :] * cw[1:2]
           + ubuf[7:7 + tm, :] * cw[2:3] + u * cw[3:4])
    conv_out[0] = ubuf[tm + 5:tm + 8, :]
    ubuf[0:8, :] = ubuf[tm:tm + 8, :]
    c = _silu(acc)
    for h in range(H_A):
        qh = c[:, h * DK_A:(h + 1) * DK_A]
        kh = c[:, (H_A + h) * DK_A:(H_A + h + 1) * DK_A]
        q_out[:, h * DK_A:(h + 1) * DK_A] = qh * (lax.rsqrt(jnp.sum(qh * qh, -1, keepdims=True) + EPS)
                                                 * (DK_A ** -0.5))
        k_out[:, h * DK_A:(h + 1) * DK_A] = kh * lax.rsqrt(jnp.sum(kh * kh, -1, keepdims=True) + EPS)
    v_out[...] = c[:, 2 * H_A * DK_A:]
    z_out[...] = jnp.dot(xn, wz_ref[...], preferred_element_type=F32)

    cos, sin = cos_ref[...], sin_ref[...]
    ones_bd = ones_ref[...]

    kv = jnp.dot(xn, wkv_ref[...], preferred_element_type=F32)
    kb = _rope_rows(_seg_rms(kv[:, :128], ones_bd, kng_ref[...]), cos, sin)
    kv_out[:, 0:128] = kb
    kv_out[:, 128:256] = kv[:, 128:]
    kbf_out[...] = kb.astype(BF16)

    sm = jnp.dot(xn, wsm_ref[...], preferred_element_type=F32)
    smo = _small_block(sm, cos, sin, alog_ref[...], dtb_ref[...])
    sm_out[...] = smo
    kidx_out[...] = smo[:, :D_IDX]
    kibf_out[...] = smo[:, :D_IDX].astype(BF16)

    nt_dims = (((1,), (1,)), ((), ()))
    cos_t, sin_t = cost_ref[...], sint_ref[...]
    qt = lax.dot_general(wbqt_ref[...], xn, nt_dims, preferred_element_type=F32)
    qit = lax.dot_general(wiqt_ref[...], xn, nt_dims, preferred_element_type=F32)
    vt = lax.dot_general(wvt_ref[...], xn, nt_dims, preferred_element_type=F32)
    smt = lax.dot_general(wsmt_ref[...], xn, nt_dims, preferred_element_type=F32)
    qg = jnp.concatenate([qngt_ref[...]] * (tm // 128), axis=1)
    zero64 = jnp.zeros((HD_B, tm), BF16)
    for h in range(H_B):
        qh = qt[HD_B * h:HD_B * (h + 1)]
        qh = qh * lax.rsqrt(jnp.mean(qh * qh, axis=0, keepdims=True) + EPS) * qg
        qh = (_rope_cols(qh, cos_t, sin_t) * (HD_B ** -0.5 * LOG2_E)).astype(BF16)
        n = h // (H_B // N_KV_B)
        qtp_out[0, h, HD_B * n:HD_B * (n + 1), :] = qh
        qtp_out[0, h, HD_B * (1 - n):HD_B * (2 - n), :] = zero64
        qit_out[0, h] = (_rope_cols(qit[D_IDX * h:D_IDX * (h + 1)], cos_t, sin_t) * (D_IDX ** -0.5)).astype(BF16)
    vtb = vt.astype(BF16)
    for j in range(tm // kc):
        vt_out[0, j] = vtb[:, j * kc:(j + 1) * kc]
    sm8 = smt[64:72]
    row = lax.broadcasted_iota(I32, sm8.shape, 0)
    g8 = -jnp.exp(alogt_ref[...]) * _softplus(sm8 + dtbt_ref[...])
    gt_out[0] = jnp.where(row < 4, jax.nn.sigmoid(sm8), g8)
    wit_out[0] = smt[72:80] * (H_IDX ** -0.5)


def _proj_prompt(x, w, tabs, *, tm, kc):
    b, t, d = x.shape
    nt = t // tm
    m = b * t
    row = lambda n: pl.BlockSpec((tm, n), lambda i, j: (i * nt + j, 0))
    ins = [
        pl.BlockSpec((1, tm, d), lambda i, j: (i, j, 0)),
        _const_spec((1, d)),
        _const_spec((d, CONV_CH)), _const_spec((d, 512)), _const_spec((d, 256)), _const_spec((d, 128)),
        _const_spec((512, d)), _const_spec((512, d)), _const_spec((128, d)), _const_spec((128, d)),
        _const_spec((CONV_W, CONV_CH)), _const_spec((1, 128)), _const_spec((HD_B, 128)),
        _const_spec((1, 128)), _const_spec((1, 128)), _const_spec((8, 1)), _const_spec((8, 1)),
        _const_spec((128, 128)),
        pl.BlockSpec((tm, 128), lambda i, j: (j, 0)), pl.BlockSpec((tm, 128), lambda i, j: (j, 0)),
        pl.BlockSpec((32, tm), lambda i, j: (0, j)), pl.BlockSpec((32, tm), lambda i, j: (0, j)),
    ]
    nkc = t // kc
    outs = [
        (jax.ShapeDtypeStruct((m, 512), F32), row(512)),
        (jax.ShapeDtypeStruct((m, 512), F32), row(512)),
        (jax.ShapeDtypeStruct((m, 512), F32), row(512)),
        (jax.ShapeDtypeStruct((m, 512), F32), row(512)),
        (jax.ShapeDtypeStruct((m, 256), F32), row(256)),
        (jax.ShapeDtypeStruct((m, 128), BF16), row(128)),
        (jax.ShapeDtypeStruct((m, D_IDX), BF16), row(D_IDX)),
        (jax.ShapeDtypeStruct((m, 128), F32), row(128)),
        (jax.ShapeDtypeStruct((m, D_IDX), F32), row(D_IDX)),
        (jax.ShapeDtypeStruct((b, H_B, 128, t), BF16),
         pl.BlockSpec((1, H_B, 128, tm), lambda i, j: (i, 0, 0, j))),
        (jax.ShapeDtypeStruct((b, H_IDX, D_IDX, t), BF16),
         pl.BlockSpec((1, H_IDX, D_IDX, tm), lambda i, j: (i, 0, 0, j))),
        (jax.ShapeDtypeStruct((b, nkc, 128, kc), BF16),
         pl.BlockSpec((1, tm // kc, 128, kc), lambda i, j: (i, j, 0, 0))),
        (jax.ShapeDtypeStruct((b, 8, t), F32), pl.BlockSpec((1, 8, tm), lambda i, j: (i, 0, j))),
        (jax.ShapeDtypeStruct((b, H_IDX, t), F32), pl.BlockSpec((1, H_IDX, tm), lambda i, j: (i, 0, j))),
        (jax.ShapeDtypeStruct((b, CONV_W - 1, CONV_CH), F32),
         pl.BlockSpec((1, CONV_W - 1, CONV_CH), lambda i, j: (i, 0, 0))),
    ]
    return pl.pallas_call(
        functools.partial(_proj_prompt_body, tm=tm, kc=kc),
        out_shape=[o[0] for o in outs],
        grid=(b, nt),
        in_specs=ins,
        out_specs=[o[1] for o in outs],
        scratch_shapes=[pltpu.VMEM((tm + 8, CONV_CH), F32)],
        compiler_params=_cparams(2),
        name="proj_prompt",
    )(x, w["g1"], w["wqkv"], w["wz"], w["wkv"], w["wsm"], w["wbqt"], w["wiqt"], w["wvt"], w["wsmt"],
      w["convw"], w["kng"], w["qngt"], w["alog_row"], w["dtb_row"], w["alog_col"], w["dtb_col"],
      w["ones_bd128"], tabs["cos"], tabs["sin"], tabs["cos_t"], tabs["sin_t"])


def _split3(x):
    hi = x.astype(BF16)
    r1 = x - hi.astype(F32)
    mid = r1.astype(BF16)
    lo = (r1 - mid.astype(F32)).astype(BF16)
    return hi, mid, lo


def _gdn_body(q_ref, k_ref, v_ref, z_ref, sm_ref, gt_ref, gg_ref, bdk_ref, bdp_ref, o_ref, s_ref, *, nb):
    c = pl.program_id(1)

    @pl.when(c == 0)
    def _():
        s_ref[...] = jnp.zeros_like(s_ref)

    n, nh = CHUNK, H_A
    w4 = nh * n
    ri4 = lax.broadcasted_iota(I32, (n, w4), 0)
    ci4 = lax.broadcasted_iota(I32, (n, w4), 1) % n
    strict4, incl4 = ri4 > ci4, ri4 >= ci4
    eye4 = jnp.where(ri4 == ci4, 1.0, 0.0)
    triu4 = jnp.where(ri4 <= ci4, 1.0, 0.0).astype(BF16)
    ri = lax.broadcasted_iota(I32, (n, n), 0)
    tril = jnp.where(ri >= lax.broadcasted_iota(I32, (n, n), 1), 1.0, 0.0).astype(BF16)
    blk_row = lax.broadcasted_iota(I32, (1, w4), 1) // n
    low_half = lax.broadcasted_iota(I32, (n, 128), 1) < n
    zeros_pad = jnp.zeros((n, 2 * DV_A), BF16)
    gg = gg_ref[...]
    dot = functools.partial(jnp.dot, preferred_element_type=F32)

    def pad_rows(x, e):
        z = zeros_pad[:, :x.shape[1]]
        return jnp.concatenate([x, z] if e == 0 else [z, x], axis=0)

    seqs = range(nb)
    st = [dict() for _ in seqs]

    for bi in seqs:
        hi, mid, lo = _split3(sm_ref[bi])
        st[bi]["gc_all"] = dot(tril, hi) + dot(tril, mid) + dot(tril, lo)
        hi, mid, lo = _split3(gt_ref[bi, 0])
        st[bi]["grows"] = dot(hi, triu4) + dot(mid, triu4) + dot(lo, triu4)

    for bi in seqs:
        d = st[bi]
        sm, k, grows = sm_ref[bi], k_ref[bi], d["grows"]
        grow = jnp.where(blk_row == 0, grows[4:5], jnp.where(blk_row == 1, grows[5:6],
                         jnp.where(blk_row == 2, grows[6:7], grows[7:8])))
        gcols = [jnp.broadcast_to(d["gc_all"][:, _SM_G + h:_SM_G + h + 1], (n, 128)) for h in range(nh)]
        betas = [jnp.broadcast_to(sm[:, _SM_BETA + h:_SM_BETA + h + 1], (n, 128)) for h in range(nh)]
        d["gc512"] = jnp.concatenate(gcols, axis=1)
        d["beta512"] = jnp.concatenate(betas, axis=1)
        gc256 = jnp.concatenate([jnp.where(low_half, gcols[0], gcols[1]),
                                 jnp.where(low_half, gcols[2], gcols[3])], axis=1)
        d["dec"] = jnp.exp(jnp.minimum(gc256 - grow, 0.0))
        d["kb"] = k * d["beta512"]
        kbd = jnp.concatenate([k.astype(BF16)] * nh, axis=0) * bdk_ref[...]
        d["aq"] = lax.dot_general(jnp.concatenate([d["kb"], q_ref[bi]], axis=0).astype(BF16), kbd,
                                  (((1,), (1,)), ((), ())), preferred_element_type=F32)

    for bi in seqs:
        d = st[bi]
        x = jnp.where(strict4, -d["aq"][:n] * d["dec"], 0.0)
        d["qkm"] = jnp.where(incl4, d["aq"][n:] * d["dec"], 0.0).astype(BF16)
        d["inv"], d["p"] = eye4 + x, x
    for lvl in range(6):
        for bi in seqs:
            d = st[bi]
            p, inv = d["p"], d["inv"]
            wbd = jnp.concatenate([p.astype(BF16)] * nh, axis=0) * bdp_ref[...]
            if lvl == 0:
                d["p"] = dot(p.astype(BF16), wbd)
            elif lvl < 5:
                r = dot(jnp.concatenate([p, inv], axis=0).astype(BF16), wbd)
                d["p"], d["inv"] = r[:n], inv + r[n:]
            else:
                d["inv"] = inv + dot(inv.astype(BF16), wbd)

    heads = [(bi, h) for bi in seqs for h in range(nh)]
    hsl = lambda h: slice(h * DK_A, (h + 1) * DK_A)
    psl = lambda h: slice(128 * (h // 2), 128 * (h // 2 + 1))
    for bi in seqs:
        d = st[bi]
        k, gc512 = k_ref[bi], d["gc512"]
        d["tinv"] = d["inv"].astype(BF16)
        egc = jnp.exp(gc512)
        d["rhs_v"] = (v_ref[bi] * d["beta512"]).astype(BF16)
        d["rhs_k"] = (d["kb"] * egc).astype(BF16)
        g_last = gc512[n - 1:n, :]
        d["q_dec"] = q_ref[bi] * egc
        d["k_dec"] = (k * jnp.exp(g_last - gc512)).astype(BF16)
        d["c_dec"] = jnp.exp(g_last)
    sol, ws, vn = {}, {}, {}
    for bi, h in heads:
        d = st[bi]
        rhs = pad_rows(jnp.concatenate([d["rhs_v"][:, hsl(h)], d["rhs_k"][:, hsl(h)]], axis=1), h % 2)
        sol[bi, h] = dot(d["tinv"][:, psl(h)], rhs)
    for bi, h in heads:
        lhs = jnp.concatenate([sol[bi, h][:, DV_A:], st[bi]["q_dec"][:, hsl(h)]], axis=0)
        ws[bi, h] = dot(lhs.astype(BF16), s_ref[bi, h].astype(BF16))
    for bi, h in heads:
        d = st[bi]
        vn[bi, h] = (sol[bi, h][:, :DV_A] - ws[bi, h][:n]).astype(BF16)
        s_ref[bi, h] = d["c_dec"][:, hsl(h)] * s_ref[bi, h] + lax.dot_general(
            d["k_dec"][:, hsl(h)], vn[bi, h], (((0,), (0,)), ((), ())), preferred_element_type=F32)
    for bi, h in heads:
        o = ws[bi, h][n:] + dot(st[bi]["qkm"][:, psl(h)], pad_rows(vn[bi, h], h % 2))
        o_ref[bi, :, hsl(h)] = (_rms_rows(o, gg) * _silu(z_ref[bi, :, hsl(h)])).astype(BF16)


def _gdn_prompt(q, k, v, z, sm, gt, gg, *, b, t, nb):
    nc = t // CHUNK
    r3 = lambda a, n: a.reshape(b, t, n)
    blk = lambda n: pl.BlockSpec((nb, CHUNK, n), lambda i, j: (i, j, 0))
    w4 = H_A * CHUNK
    bdk = np.kron(np.eye(H_A, dtype=np.float32), np.ones((CHUNK, DK_A), np.float32))
    bdp = np.kron(np.eye(H_A, dtype=np.float32), np.ones((CHUNK, CHUNK), np.float32))
    return pl.pallas_call(
        functools.partial(_gdn_body, nb=nb),
        out_shape=[jax.ShapeDtypeStruct((b, t, 512), BF16),
                   jax.ShapeDtypeStruct((b, H_A, DK_A, DV_A), F32)],
        grid=(b // nb, nc),
        in_specs=[blk(512), blk(512), blk(512), blk(512), blk(128),
                  pl.BlockSpec((nb, 1, 8, CHUNK), lambda i, j: (i, j, 0, 0)),
                  _const_spec((1, DV_A)), _const_spec((w4, H_A * DK_A)), _const_spec((w4, w4))],
        out_specs=[blk(512), pl.BlockSpec((nb, H_A, DK_A, DV_A), lambda i, j: (i, 0, 0, 0))],
        compiler_params=_cparams(2),
        name="gdn_prompt",
    )(r3(q, 512), r3(k, 512), r3(v, 512), r3(z, 512), r3(sm, 128),
      jnp.swapaxes(gt.reshape(b, 8, nc, CHUNK), 1, 2), gg, jnp.asarray(bdk, BF16), jnp.asarray(bdp, BF16))


def _ordinal_to_float(k):
    k = jnp.maximum(k, NEG_INF_KEY)
    return pltpu.bitcast(k ^ ((k >> 31) & 0x7FFFFFFF), F32)


def _kth_largest(count_ge, shape, k_top):
    def bit_step(i, o):
        cand = o + jnp.left_shift(jnp.int32(1), 31 - i)
        return jnp.where(count_ge(_ordinal_to_float(cand)) >= k_top, cand, o)

    o = lax.fori_loop(0, 32, bit_step, jnp.full(shape, INT_MIN, I32))
    return _ordinal_to_float(o), o


def _kth_largest_two_stage(count_ge, count_ge_rounded, shape, k_top):
    def bf16_ordinal_as_f32_ordinal(h):
        return jnp.where(h >= 0, h << 16, (h << 16) | 0xFFFF)

    def coarse_step(i, h):
        cand = h + jnp.left_shift(jnp.int32(1), 15 - i)
        c = _ordinal_to_float(bf16_ordinal_as_f32_ordinal(cand)).astype(BF16)
        return jnp.where(count_ge_rounded(c) >= k_top, cand, h)

    h = lax.fori_loop(0, 16, coarse_step, jnp.full(shape, -(2 ** 15), I32))
    o = bf16_ordinal_as_f32_ordinal(h)
    o = jnp.where(count_ge(_ordinal_to_float(o)) >= k_top, o, bf16_ordinal_as_f32_ordinal(h - 1))

    def fine_step(i, o):
        cand = o + jnp.left_shift(jnp.int32(1), 15 - i)
        return jnp.where(count_ge(_ordinal_to_float(cand)) >= k_top, cand, o)

    o = lax.fori_loop(0, 16, fine_step, o)
    return _ordinal_to_float(o), o


def _topk_threshold(count_ge, count_gt_eq, count_eq_below, shape, k_top, idx_bits):
    t, o = _kth_largest(count_ge, shape, k_top)
    c_gt, c_eq = count_gt_eq(t)
    need = k_top - c_gt
    untied = jnp.logical_or(c_eq == need, o <= NEG_INF_KEY)
    all_untied = jnp.min(jnp.where(untied, 1.0, 0.0)) > 0.5

    def tie_search(_):
        def idx_step(i, x):
            cand = x + jnp.left_shift(jnp.int32(1), idx_bits - 1 - i)
            return jnp.where(count_eq_below(t, cand) < need, cand, x)
        return lax.fori_loop(0, idx_bits, idx_step, jnp.zeros(shape, I32))

    x = lax.cond(all_untied, lambda _: jnp.full(shape, 2 ** 30, I32), tie_search, 0)
    return t, x


_ACC_ROWS = HD_B + 16
_LOGITS_AHEAD = 3


def _dsa_body(qit_ref, wit_ref, qtp_ref, ki_ref, k_ref, vt_ref, o_ref,
              sc_ref, sb_ref, m_ref, acc_ref, *, tq, kc, k_top):
    i = pl.program_id(1)
    nkc = i + 1
    rowi = lax.broadcasted_iota(I32, (kc, tq), 0)
    coli = lax.broadcasted_iota(I32, (kc, tq), 1)
    ng = kc // 32

    wit = wit_ref[0]

    def score_chunk(c, carry):
        kic = ki_ref[0, c]
        score = jnp.zeros((kc, tq), F32)
        for h in range(H_IDX):
            s = jnp.dot(kic, qit_ref[0, h], preferred_element_type=F32)
            score = score + jnp.maximum(s, 0.0) * wit[h:h + 1, :]
        causal = rowi + (c - i) * kc <= coli
        score = jnp.where(causal, score, -jnp.inf)
        sc_ref[c] = score
        sb_ref[c] = score.astype(BF16)
        return carry

    lax.fori_loop(0, nkc, score_chunk, 0)

    def key_sum(f):
        def body(c, acc):
            ind = f(sc_ref[c], c * kc)
            return acc + jnp.sum(ind.reshape(ng, 32, tq), axis=0)
        acc = lax.fori_loop(0, nkc, body, jnp.zeros((32, tq), F32))
        return jnp.sum(acc, axis=0, keepdims=True).astype(I32)

    def count_ge_rounded(cand):
        one, zero = jnp.ones((), BF16), jnp.zeros((), BF16)

        def body(c, acc):
            ind = jnp.where(sb_ref[c] >= cand, one, zero)
            for g in range(ng):
                acc = acc + ind[32 * g:32 * (g + 1)]
            return acc
        acc = lax.fori_loop(0, nkc, body, jnp.zeros((32, tq), BF16))
        return jnp.sum(acc.astype(F32), axis=0, keepdims=True).astype(I32)

    t, _ = _kth_largest_two_stage(lambda cand: key_sum(lambda s, off: jnp.where(s >= cand, 1.0, 0.0)),
                                  count_ge_rounded, (1, tq), k_top)
    ties_wanted = (k_top - key_sum(lambda s, off: jnp.where(s > t, 1.0, 0.0))).astype(F32)

    m_ref[...] = jnp.full(m_ref.shape, NEG_BIG, F32)
    acc_ref[...] = jnp.zeros(acc_ref.shape, F32)
    ones_rows = jnp.ones((_ACC_ROWS - HD_B, kc), BF16)
    prefix = jnp.where(lax.broadcasted_iota(I32, (kc, kc), 0) >= lax.broadcasted_iota(I32, (kc, kc), 1),
                       1.0, 0.0).astype(BF16)

    def attn_chunk(c, ties_before):
        kk = k_ref[0, c]
        sc = sc_ref[c]
        tied = jnp.where(sc == t, 1.0, 0.0).astype(BF16)
        rank = ties_before + jnp.dot(prefix, tied, preferred_element_type=F32)
        tie = jnp.where(sc == t, jnp.where(rank <= ties_wanted, 0.0, NEG_BIG), NEG_BIG)
        bias = jnp.where(sc > -jnp.inf, jnp.where(sc > t, 0.0, tie), NEG_BIG)
        ties_after = rank[kc - 1:kc, :]
        vt = vt_ref[0, c]
        vaug = [jnp.concatenate([vt[HD_B * n:HD_B * (n + 1)], ones_rows], axis=0) for n in range(N_KV_B)]
        ahead = _LOGITS_AHEAD
        logits = {h: jnp.dot(kk, qtp_ref[0, h], preferred_element_type=F32) for h in range(ahead)}
        for h in range(H_B):
            if h + ahead < H_B:
                logits[h + ahead] = jnp.dot(kk, qtp_ref[0, h + ahead], preferred_element_type=F32)
            n = h // (H_B // N_KV_B)
            s = bias + logits.pop(h)
            m_old = m_ref[h:h + 1, :]
            m_new = jnp.maximum(m_old, jnp.max(s, axis=0, keepdims=True))
            alpha = jnp.exp2(m_old - m_new)
            pexp = jnp.exp2(s - m_new).astype(BF16)
            acc_ref[h] = acc_ref[h] * alpha + jnp.dot(vaug[n], pexp, preferred_element_type=F32)
            m_ref[h:h + 1, :] = m_new
        return ties_after

    lax.fori_loop(0, nkc, attn_chunk, jnp.zeros((1, tq), F32))
    for p in range(H_B // 2):
        halves = []
        for h in (2 * p, 2 * p + 1):
            acc = acc_ref[h]
            halves.append(acc[:HD_B] / acc[HD_B:HD_B + 1])
        o_ref[:, 128 * p:128 * (p + 1)] = jnp.concatenate(halves, axis=0).T.astype(BF16)


def _dsa_prompt(qit, wit, qtp, kibf, kbf, vt, *, b, t, tq, k_top):
    kc = tq
    nq = t // tq
    nkc = t // kc
    whole = lambda *s: pl.BlockSpec((1,) + s, lambda i, j: (i,) + (0,) * len(s))
    return pl.pallas_call(
        functools.partial(_dsa_body, tq=tq, kc=kc, k_top=k_top),
        out_shape=jax.ShapeDtypeStruct((b * t, H_B * HD_B), BF16),
        grid=(b, nq),
        in_specs=[pl.BlockSpec((1, H_IDX, D_IDX, tq), lambda i, j: (i, 0, 0, j)),
                  pl.BlockSpec((1, H_IDX, tq), lambda i, j: (i, 0, j)),
                  pl.BlockSpec((1, H_B, 128, tq), lambda i, j: (i, 0, 0, j)),
                  whole(nkc, kc, D_IDX), whole(nkc, kc, 128), whole(nkc, 128, kc)],
        out_specs=pl.BlockSpec((tq, H_B * HD_B), lambda i, j: (i * nq + j, 0)),
        scratch_shapes=[pltpu.VMEM((nkc, kc, tq), F32), pltpu.VMEM((nkc, kc, tq), BF16),
                        pltpu.VMEM((H_B, tq), F32), pltpu.VMEM((H_B, _ACC_ROWS, tq), F32)],
        compiler_params=_cparams(2),
        name="dsa_prompt",
    )(qit, wit, qtp, kibf.reshape(b, nkc, kc, D_IDX), kbf.reshape(b, nkc, kc, 128), vt)


def _merge_body(x_ref, oa_ref, ob_ref, g1_ref, wg_ref, woa_ref, wob_ref, wo_ref, h_ref):
    x = x_ref[...]
    xn = _rms_rows(x, g1_ref[...]).astype(BF16)
    gates = jax.nn.sigmoid(jnp.dot(xn, wg_ref[...], preferred_element_type=F32))
    a = jnp.dot(oa_ref[...], woa_ref[...], preferred_element_type=F32)
    bb = jnp.dot(ob_ref[...], wob_ref[...], preferred_element_type=F32)
    mix = gates[:, :D_MODEL] * a + gates[:, D_MODEL:] * bb
    h_ref[...] = x + jnp.dot(mix.astype(BF16), wo_ref[...], preferred_element_type=F32)


def _merge(x, oa, ob, wob, w, *, tm):
    m, d = x.shape
    nb = ob.shape[1]
    row = lambda n: pl.BlockSpec((tm, n), lambda i: (i, 0))
    return pl.pallas_call(
        _merge_body,
        out_shape=jax.ShapeDtypeStruct((m, d), F32),
        grid=(m // tm,),
        in_specs=[row(d), row(512), row(nb), _const_spec((1, d)), _const_spec((d, 2 * d)),
                  _const_spec((512, d)), _const_spec((nb, d)), _const_spec((d, d))],
        out_specs=row(d),
        compiler_params=_cparams(1),
        name="merge",
    )(x, oa, ob, w["g1"], w["wg"], w["woa"], wob, w["wo"])


def _mlp_body(h_ref, g2_ref, wup_ref, wdn_ref, y_ref):
    h = h_ref[...]
    hn = _rms_rows(h, g2_ref[...]).astype(BF16)
    up = jnp.dot(hn, wup_ref[...], preferred_element_type=F32)
    act = jnp.square(jnp.maximum(up, 0.0)).astype(BF16)
    y_ref[...] = h + jnp.dot(act, wdn_ref[...], preferred_element_type=F32)


def _mlp(h, w, *, tm):
    m, d = h.shape
    row = pl.BlockSpec((tm, d), lambda i: (i, 0))
    return pl.pallas_call(
        _mlp_body,
        out_shape=jax.ShapeDtypeStruct((m, d), F32),
        grid=(m // tm,),
        in_specs=[row, _const_spec((1, d)), _const_spec((d, D_FF)), _const_spec((D_FF, d))],
        out_specs=row,
        compiler_params=_cparams(1),
        name="mlp",
    )(h, w["g2"], w["wup"], w["wdn"])


def _proj_decode_body(x_ref, st_ref, g1_ref, wqkv_ref, wz_ref, wbq_ref, wkv_ref, wiq_ref, wsm_ref,
                      convw_ref, qng_ref, kng_ref, alog_ref, dtb_ref, ones_ref, seg_ref, cos_ref, sin_ref,
                      u_out, q_out, k_out, v_out, z_out, qb_out, kv_out, qi_out, sm_out, ss_out):
    x = x_ref[...]
    xn = _rms_rows(x, g1_ref[...]).astype(BF16)
    u = jnp.dot(xn, wqkv_ref[...], preferred_element_type=F32)
    u_out[...] = u
    cw = convw_ref[...]
    acc = st_ref[0] * cw[0:1] + st_ref[1] * cw[1:2] + st_ref[2] * cw[2:3] + u * cw[3:4]
    c = _silu(acc)
    for h in range(H_A):
        qh = c[:, h * DK_A:(h + 1) * DK_A]
        kh = c[:, (H_A + h) * DK_A:(H_A + h + 1) * DK_A]
        q_out[:, h * DK_A:(h + 1) * DK_A] = qh * (lax.rsqrt(jnp.sum(qh * qh, -1, keepdims=True) + EPS)
                                                 * (DK_A ** -0.5))
        k_out[:, h * DK_A:(h + 1) * DK_A] = kh * lax.rsqrt(jnp.sum(kh * kh, -1, keepdims=True) + EPS)
    v_out[...] = c[:, 2 * H_A * DK_A:]
    z_out[...] = jnp.dot(xn, wz_ref[...], preferred_element_type=F32)
    cos, sin = cos_ref[...], sin_ref[...]
    ones_bd = ones_ref[...]
    bq = jnp.dot(xn, wbq_ref[...], preferred_element_type=F32)
    qb = _rope_rows(_seg_rms(bq, ones_bd, qng_ref[...]), cos, sin)
    qb_out[...] = (qb * (HD_B ** -0.5)).astype(BF16)
    kv = jnp.dot(xn, wkv_ref[...], preferred_element_type=F32)
    kb = _rope_rows(_seg_rms(kv[:, :128], ones_bd[:128, :128], kng_ref[...]), cos, sin)
    kv_out[:, 0:128] = kb
    kv_out[:, 128:256] = kv[:, 128:]
    iq = jnp.dot(xn, wiq_ref[...], preferred_element_type=F32)
    qi = (_rope_rows(iq, cos, sin) * (D_IDX ** -0.5)).astype(BF16)
    qi_out[...] = qi
    sm = jnp.dot(xn, wsm_ref[...], preferred_element_type=F32)
    smo = _small_block(sm, cos, sin, alog_ref[...], dtb_ref[...])
    sm_out[...] = smo
    lane = lax.broadcasted_iota(I32, smo.shape, 1)
    ki = smo.astype(BF16).astype(F32)
    ki2 = jnp.where(lane < D_IDX, ki, pltpu.roll(ki, D_IDX, 1))
    prod = qi.astype(F32) * jnp.concatenate([ki2] * 4, axis=-1)
    sh = _mm_f32(prod, seg_ref[...])
    contrib = jnp.where(jnp.logical_and(lane >= _SM_WI, lane < _SM_WI + H_IDX),
                        jnp.maximum(sh, 0.0) * smo, 0.0)
    ss_out[...] = jnp.broadcast_to(jnp.sum(contrib, axis=-1, keepdims=True), ss_out.shape)


def _proj_decode(x, st, w, tabs):
    r, d = x.shape
    outs = [(r, CONV_CH, F32), (r, 512, F32), (r, 512, F32), (r, 512, F32), (r, 512, F32), (r, 512, BF16),
            (r, 256, F32), (r, 512, BF16), (r, 128, F32), (r, 128, F32)]
    args = (x, st, w["g1"], w["wqkv"], w["wz"], w["wbq"], w["wkv"], w["wiq"], w["wsm"], w["convw"],
            w["qng"], w["kng"], w["alog_row"], w["dtb_row"], w["ones_bd"], w["seg_sum"], tabs["cos"], tabs["sin"])
    return pl.pallas_call(
        _proj_decode_body,
        out_shape=[jax.ShapeDtypeStruct((a, n), dt) for a, n, dt in outs],
        compiler_params=pltpu.CompilerParams(vmem_limit_bytes=VMEM_LIMIT),
        name="proj_decode",
    )(*args)


def _gdn_decode_body(qc_ref, kc_ref, v_ref, z_ref, sm_ref, gg_ref, s_ref, o_ref, so_ref):
    sm = sm_ref[0]
    gg = gg_ref[...]
    for h in range(H_A):
        g = sm[:, _SM_G + h:_SM_G + h + 1]
        beta = sm[:, _SM_BETA + h:_SM_BETA + h + 1]
        kcol, qcol = kc_ref[0, h], qc_ref[0, h]
        v = v_ref[0, h]
        s = s_ref[0, h] * jnp.exp(g)
        vn = beta * (v - jnp.sum(s * kcol, axis=0, keepdims=True))
        s = s + kcol * vn
        so_ref[0, h] = s
        o = jnp.sum(s * qcol, axis=0, keepdims=True)
        o_ref[0, h] = (_rms_rows(o, gg) * _silu(z_ref[0, h])).astype(BF16)


def _gdn_decode(q, k, v, z, sm, gg, state):
    r = q.shape[0]
    col = lambda a: a.reshape(r, H_A, DK_A, 1)
    rowv = lambda a: a.reshape(r, H_A, 1, DV_A)
    cspec = pl.BlockSpec((1, H_A, DK_A, 1), lambda i: (i, 0, 0, 0))
    rspec = pl.BlockSpec((1, H_A, 1, DV_A), lambda i: (i, 0, 0, 0))
    sspec = pl.BlockSpec((1, H_A, DK_A, DV_A), lambda i: (i, 0, 0, 0))
    o, s_new = pl.pallas_call(
        _gdn_decode_body,
        out_shape=[jax.ShapeDtypeStruct((r, H_A, 1, DV_A), BF16),
                   jax.ShapeDtypeStruct((r, H_A, DK_A, DV_A), F32)],
        grid=(r,),
        in_specs=[cspec, cspec, rspec, rspec, pl.BlockSpec((1, 1, 128), lambda i: (i, 0, 0)),
                  _const_spec((1, DV_A)), sspec],
        out_specs=[rspec, sspec],
        compiler_params=_cparams(1),
        name="gdn_decode",
    )(col(q), col(k), rowv(v), rowv(z), sm.reshape(r, 1, 128), gg, state)
    return o.reshape(r, H_A * DV_A), s_new


def _idx_decode_body(pt_ref, qi_ref, w_ref, *refs, npg):
    pages, out_ref = refs[:npg], refs[npg]
    qi = qi_ref[0]
    w = w_ref[0]
    for p in range(npg):
        s = _mm(qi, pages[p][0])
        out_ref[0, :, PAGE_SIZE * p:PAGE_SIZE * (p + 1)] = jnp.sum(jnp.maximum(s, 0.0) * w, axis=0,
                                                                   keepdims=True)


def _idx_decode(page_table, qi, wi_col, cache_kidx, *, npg):
    r, n_pages = page_table.shape
    page_spec = lambda p: pl.BlockSpec((1, D_IDX, PAGE_SIZE), lambda i, j, pt: (pt[i, j * npg + p], 0, 0))
    grid_spec = pltpu.PrefetchScalarGridSpec(
        num_scalar_prefetch=1,
        grid=(r, n_pages // npg),
        in_specs=[pl.BlockSpec((1, H_IDX, D_IDX), lambda i, j, pt: (i, 0, 0)),
                  pl.BlockSpec((1, H_IDX, 1), lambda i, j, pt: (i, 0, 0))]
                 + [page_spec(p) for p in range(npg)],
        out_specs=pl.BlockSpec((1, 1, PAGE_SIZE * npg), lambda i, j, pt: (i, 0, j)),
    )
    return pl.pallas_call(
        functools.partial(_idx_decode_body, npg=npg),
        out_shape=jax.ShapeDtypeStruct((r, 1, n_pages * PAGE_SIZE), F32),
        grid_spec=grid_spec,
        compiler_params=_cparams(2),
        name="idx_decode",
    )(page_table, qi.reshape(r, H_IDX, D_IDX), wi_col, *([cache_kidx] * npg))


def _select_decode_body(sc_ref, ss_ref, bias_ref, bself_ref, *, k_top, cw):
    r, n = sc_ref.shape
    nch = n // cw
    key_ref = sc_ref
    skey = ss_ref[:, 0:1]

    def lane_sum(f):
        def body(c, acc):
            off = pl.multiple_of(c * cw, cw)
            kk = key_ref[:, pl.ds(off, cw)]
            for j in range(cw // 128):
                acc = acc + f(kk[:, 128 * j:128 * (j + 1)], off + 128 * j)
            return acc
        acc = lax.fori_loop(0, nch, body, jnp.zeros((r, 128), F32))
        return jnp.sum(acc, axis=-1, keepdims=True).astype(I32)

    def count_ge(cand):
        cb = jnp.broadcast_to(cand, (r, 128))
        return lane_sum(lambda kk, off: jnp.where(kk >= cb, 1.0, 0.0)) + (skey >= cand).astype(I32)

    def count_gt_eq(t):
        tb = jnp.broadcast_to(t, (r, 128))
        return (lane_sum(lambda kk, off: jnp.where(kk > tb, 1.0, 0.0)) + (skey > t).astype(I32),
                lane_sum(lambda kk, off: jnp.where(kk == tb, 1.0, 0.0)) + (skey == t).astype(I32))

    def count_eq_below(t, j):
        tb = jnp.broadcast_to(t, (r, 128))
        jb = jnp.broadcast_to(j, (r, 128))
        lane = lax.broadcasted_iota(I32, (r, 128), 1)
        return (lane_sum(lambda kk, off: jnp.where(kk == tb, jnp.where(lane + off < jb, 1.0, 0.0), 0.0))
                + jnp.logical_and(skey == t, j > n).astype(I32))

    t, x = _topk_threshold(count_ge, count_gt_eq, count_eq_below, (r, 1), k_top, 15)
    tb = jnp.broadcast_to(t, (r, cw))
    xb = jnp.broadcast_to(x, (r, cw))

    def bias_chunk(c, carry):
        off = pl.multiple_of(c * cw, cw)
        kk = key_ref[:, pl.ds(off, cw)]
        idx = lax.broadcasted_iota(I32, (r, cw), 1) + off
        tie = jnp.where(kk == tb, jnp.where(idx <= xb, 0.0, NEG_BIG), NEG_BIG)
        bias_ref[:, pl.ds(off, cw)] = jnp.where(kk > tb, 0.0, tie)
        return carry

    lax.fori_loop(0, nch, bias_chunk, 0)
    tie_self = jnp.where(skey == t, jnp.where(x >= n, 0.0, NEG_BIG), NEG_BIG)
    bself_ref[...] = jnp.broadcast_to(jnp.where(skey > t, 0.0, tie_self), bself_ref.shape)


def _select_decode(scores, sself, *, k_top):
    r, n = scores.shape
    return pl.pallas_call(
        functools.partial(_select_decode_body, k_top=k_top, cw=512),
        out_shape=[jax.ShapeDtypeStruct((r, n), F32), jax.ShapeDtypeStruct((r, 128), F32)],
        compiler_params=pltpu.CompilerParams(vmem_limit_bytes=VMEM_LIMIT),
        name="select_decode",
    )(scores, sself)


def _attn_decode_body(pt_ref, qp_ref, bias_ref, kvs_ref, bself_ref, *refs, npg, nsteps):
    pages, o_ref, m_ref, l_ref, acc_ref = refs[:npg], refs[npg], refs[npg + 1], refs[npg + 2], refs[npg + 3]
    j = pl.program_id(1)

    @pl.when(j == 0)
    def _():
        m_ref[...] = jnp.full(m_ref.shape, NEG_BIG, F32)
        l_ref[...] = jnp.zeros(l_ref.shape, F32)
        acc_ref[...] = jnp.zeros(acc_ref.shape, F32)

    qp = qp_ref[0]

    ss = [_mm(qp, pages[p][0, 0:128, :]) + bias_ref[0, :, PAGE_SIZE * p:PAGE_SIZE * (p + 1)]
          for p in range(npg)]
    m_old = m_ref[...]
    m_new = m_old
    for s in ss:
        m_new = jnp.maximum(m_new, jnp.max(s, axis=-1, keepdims=True))
    alpha = jnp.exp(m_old - m_new)
    l_new = l_ref[...] * alpha
    acc_new = acc_ref[...] * alpha
    for p in range(npg):
        pe = jnp.exp(ss[p] - m_new)
        l_new = l_new + jnp.sum(pe, axis=-1, keepdims=True)
        acc_new = acc_new + _mm_nt(pe, pages[p][0, 128:256, :])
    m_ref[...] = m_new
    l_ref[...] = l_new
    acc_ref[...] = acc_new

    @pl.when(j == nsteps - 1)
    def _():
        kvs = kvs_ref[0]
        krow = kvs[:, :128].astype(BF16).astype(F32)
        s = jnp.sum(qp.astype(F32) * krow, axis=-1, keepdims=True) + bself_ref[0][:, 0:1]
        m_old = m_ref[...]
        m_new = jnp.maximum(m_old, s)
        alpha = jnp.exp(m_old - m_new)
        p = jnp.exp(s - m_new)
        l = l_ref[...] * alpha + p
        vrow = kvs[:, 128:].astype(BF16).astype(F32)
        acc = acc_ref[...] * alpha + p.astype(BF16).astype(F32) * vrow
        out = acc / l
        rowi = lax.broadcasted_iota(I32, out.shape, 0)
        out = jnp.where(rowi < 4, out, pltpu.roll(out, HD_B, 1))
        o_ref[0] = out.astype(BF16)


def _attn_decode(page_table, qpad, bias, kv_self, bself, cache_kv, *, npg):
    r, n_pages = page_table.shape
    nsteps = n_pages // npg
    page_spec = lambda p: pl.BlockSpec((1, 256, PAGE_SIZE), lambda i, j, pt: (pt[i, j * npg + p], 0, 0))
    grid_spec = pltpu.PrefetchScalarGridSpec(
        num_scalar_prefetch=1,
        grid=(r, nsteps),
        in_specs=[pl.BlockSpec((1, H_B, 128), lambda i, j, pt: (i, 0, 0)),
                  pl.BlockSpec((1, 1, PAGE_SIZE * npg), lambda i, j, pt: (i, 0, j)),
                  pl.BlockSpec((1, 1, 256), lambda i, j, pt: (i, 0, 0)),
                  pl.BlockSpec((1, 1, 128), lambda i, j, pt: (i, 0, 0))]
                 + [page_spec(p) for p in range(npg)],
        out_specs=pl.BlockSpec((1, H_B, 128), lambda i, j, pt: (i, 0, 0)),
        scratch_shapes=[pltpu.VMEM((H_B, 1), F32), pltpu.VMEM((H_B, 1), F32), pltpu.VMEM((H_B, 128), F32)],
    )
    return pl.pallas_call(
        functools.partial(_attn_decode_body, npg=npg, nsteps=nsteps),
        out_shape=jax.ShapeDtypeStruct((r, H_B, 128), BF16),
        grid_spec=grid_spec,
        compiler_params=_cparams(2),
        name="attn_decode",
    )(page_table, qpad, bias.reshape(r, 1, -1), kv_self.reshape(r, 1, 256), bself.reshape(r, 1, 128),
      *([cache_kv] * npg))


def _prep_weights(norm1_g, w_in, conv_w, a_log, dt_bias, gdn_norm_g, q_norm_g, k_norm_g,
                  w_out_a, w_out_b, w_o, norm2_g, w_up, w_down):
    o = _OFF
    wb = w_in.astype(BF16)
    d = w_in.shape[0]
    wsm = jnp.concatenate([wb[:, o["ik"]:o["iw"]], wb[:, o["ab"]:o["bq"]], wb[:, o["iw"]:o["ga"]],
                           jnp.zeros((d, 128 - 80), BF16)], axis=1)

    def lanes(vals, start):
        return jnp.zeros((1, 128), F32).at[0, start:start + vals.shape[0]].set(vals.astype(F32))

    seg = np.zeros((512, 128), np.float32)
    for h in range(H_IDX):
        seg[64 * h:64 * (h + 1), _SM_WI + h] = 1.0
    ones_bd = np.kron(np.eye(8, dtype=np.float32), np.ones((64, 64), np.float32))
    wob = w_out_b.astype(BF16).reshape(H_B, HD_B, d)
    wob_pad = jnp.concatenate([wob, jnp.zeros_like(wob)], axis=1).reshape(H_B * 128, d)
    col8 = lambda v: jnp.concatenate([jnp.zeros((4,), F32), v.astype(F32)]).reshape(8, 1)
    return dict(
        g1=norm1_g.reshape(1, d).astype(F32),
        wqkv=wb[:, o["aq"]:o["az"]], wz=wb[:, o["az"]:o["ab"]], wbq=wb[:, o["bq"]:o["bk"]],
        wkv=wb[:, o["bk"]:o["iq"]], wiq=wb[:, o["iq"]:o["ik"]], wsm=wsm,
        wbqt=wb[:, o["bq"]:o["bk"]].T, wiqt=wb[:, o["iq"]:o["ik"]].T, wvt=wb[:, o["bv"]:o["iq"]].T,
        wsmt=wsm.T, wg=wb[:, o["ga"]:o["end"]],
        convw=conv_w.astype(F32),
        qng=jnp.tile(q_norm_g.astype(F32), H_B).reshape(1, 512),
        kng=jnp.tile(k_norm_g.astype(F32), N_KV_B).reshape(1, 128),
        qngt=jnp.tile(q_norm_g.astype(F32).reshape(HD_B, 1), (1, 128)),
        alog_row=lanes(a_log, _SM_G), dtb_row=lanes(dt_bias, _SM_G),
        alog_col=col8(a_log), dtb_col=col8(dt_bias),
        ones_bd=jnp.asarray(ones_bd, BF16), ones_bd128=jnp.asarray(ones_bd[:128, :128], BF16),
        seg_sum=jnp.asarray(seg),
        gg=gdn_norm_g.reshape(1, DV_A).astype(F32),
        woa=w_out_a.astype(BF16), wob=w_out_b.astype(BF16), wob_pad=wob_pad, wo=w_o.astype(BF16),
        g2=norm2_g.reshape(1, d).astype(F32), wup=w_up.astype(BF16), wdn=w_down.astype(BF16),
    )


def _rope_tables(pos):
    half = HD_B // 2
    inv = ROPE_THETA ** (-jnp.arange(half, dtype=F32) / half)
    ang = pos.astype(F32)[:, None] * inv[None, :]
    cos, sin = jnp.cos(ang), jnp.sin(ang)
    return dict(cos=jnp.tile(cos, (1, 4)), sin=jnp.tile(jnp.concatenate([-sin, sin], axis=1), (1, 2)),
                cos_t=cos.T, sin_t=sin.T)


def _layer_prompt(x, w, *, tm=512, tq=256, nb=8):
    b, t, d = x.shape
    tabs = _rope_tables(jnp.arange(t))
    (q, k, v, z, kv, kbf, kibf, sm, kidx, qtp, qit, vt, gt, wit, conv_new) = _proj_prompt(x, w, tabs, tm=tm, kc=tq)
    oa, ssm = _gdn_prompt(q, k, v, z, sm, gt, w["gg"], b=b, t=t, nb=nb)
    ob = _dsa_prompt(qit, wit, qtp, kibf, kbf, vt, b=b, t=t, tq=tq, k_top=min(TOPK_MAX, t // 4))
    h = _merge(x.reshape(b * t, d), oa.reshape(b * t, 512), ob, w["wob"], w, tm=tm)
    y = _mlp(h, w, tm=tm)
    return (y.reshape(b, t, d), kv.reshape(b, t, 2, N_KV_B, HD_B), kidx.reshape(b, t, D_IDX), ssm, conv_new)


def _layer_decode(x, w, state_ssm, state_conv, cache_kv, cache_kidx, page_table, *, npg=32):
    r, d = x.shape
    n_pages = page_table.shape[1]
    past = n_pages * PAGE_SIZE
    tabs = _rope_tables(jnp.full((1,), past))
    st = jnp.swapaxes(state_conv, 0, 1)
    (u, q, k, v, z, qb, kv, qi, sm, sself) = _proj_decode(x, st, w, tabs)
    oa, ssm = _gdn_decode(q, k, v, z, sm, w["gg"], state_ssm)
    wi_col = sm[:, _SM_WI:_SM_WI + H_IDX].reshape(r, H_IDX, 1)
    kidx_pages = jnp.swapaxes(cache_kidx, 1, 2)
    kv_pages = jnp.transpose(cache_kv, (0, 2, 3, 4, 1)).reshape(-1, 2 * N_KV_B * HD_B, PAGE_SIZE)
    scores = _idx_decode(page_table, qi, wi_col, kidx_pages, npg=npg).reshape(r, past)
    bias, bself = _select_decode(scores, sself, k_top=min(TOPK_MAX, (past + 1) // 4))
    qh = qb.reshape(r, H_B, HD_B)
    zq = jnp.zeros_like(qh[:, :4])
    qpad = jnp.concatenate([jnp.concatenate([qh[:, :4], zq], axis=-1),
                            jnp.concatenate([zq, qh[:, 4:]], axis=-1)], axis=1)
    ob = _attn_decode(page_table, qpad, bias, kv, bself, kv_pages, npg=npg)
    h = _merge(x, oa, ob.reshape(r, H_B * 128), w["wob_pad"], w, tm=r)
    y = _mlp(h, w, tm=r)
    conv_new = jnp.concatenate([state_conv[:, 1:], u[:, None, :]], axis=1)
    return (y, kv.reshape(r, 1, 2, N_KV_B, HD_B), sm[:, :D_IDX].reshape(r, 1, D_IDX), ssm, conv_new)


def kernel(x_prompt, x_sample, cache_kv, cache_kidx, page_table, state_ssm, state_conv, norm1_g, w_in, conv_w,
           a_log, dt_bias, gdn_norm_g, q_norm_g, k_norm_g, w_out_a, w_out_b, w_o, norm2_g, w_up, w_down):
    assert w_in.shape[0] == 1, "single-layer trunk"
    w = _prep_weights(norm1_g[0], w_in[0], conv_w[0], a_log[0], dt_bias[0], gdn_norm_g[0], q_norm_g[0],
                      k_norm_g[0], w_out_a[0], w_out_b[0], w_o[0], norm2_g[0], w_up[0], w_down[0])
    yp, kv_p, ki_p, ssm_p, cv_p = _layer_prompt(x_prompt, w)
    ys, kv_s, ki_s, ssm_s, cv_s = _layer_decode(x_sample[:, 0], w, state_ssm[0], state_conv[0],
                                                cache_kv[0], cache_kidx[0], page_table)
    return (yp, ys[:, None, :], kv_p[None], ki_p[None], ssm_p[None], cv_p[None],
            kv_s[None], ki_s[None], ssm_s[None], cv_s[None])
```

```python
import functools
import math

import jax
import jax.numpy as jnp
import numpy as np
from jax import lax
from jax.experimental import pallas as pl
from jax.experimental.pallas import tpu as pltpu

F32 = jnp.float32
BF16 = jnp.bfloat16
I32 = jnp.int32

D_MODEL = 1024
H_A, DK_A, DV_A = 4, 128, 128
CONV_W = 4
CONV_CH = H_A * (2 * DK_A + DV_A)
CHUNK = 64
HD_B, H_B, N_KV_B = 64, 8, 2
H_IDX, D_IDX = 8, 64
TOPK_MAX = 256
PAGE_SIZE = 128
ROPE_THETA = 10000.0
D_FF = 4 * D_MODEL
EPS = 1e-6

_OFF = dict(aq=0, ak=512, av=1024, az=1536, ab=2048, aa=2052, bq=2056, bk=2568, bv=2696,
            iq=2824, ik=3336, iw=3400, ga=3408, gb=4432, end=5456)
_SM_BETA, _SM_G, _SM_WI = 64, 68, 72

LOG2_E = math.log2(math.e)
NEG_BIG = -1e30
INT_MIN = -(2 ** 31)
NEG_INF_KEY = (0xFF800000 ^ 0x7FFFFFFF) - 2 ** 32

VMEM_LIMIT = 56 * 1024 * 1024
HIGHEST = lax.Precision.HIGHEST


def _cparams(n_axes):
    return pltpu.CompilerParams(dimension_semantics=("arbitrary",) * n_axes,
                                vmem_limit_bytes=VMEM_LIMIT)


def _const_spec(shape):
    nd = len(shape)
    return pl.BlockSpec(shape, lambda *a: (0,) * nd, pipeline_mode=pl.Buffered(1))


def _mm(a, b):
    return jnp.dot(a.astype(BF16), b.astype(BF16), preferred_element_type=F32)


def _mm_nt(a, b):
    return lax.dot_general(a.astype(BF16), b.astype(BF16), (((1,), (1,)), ((), ())),
                           preferred_element_type=F32)


def _mm_tn(a, b):
    return lax.dot_general(a.astype(BF16), b.astype(BF16), (((0,), (0,)), ((), ())),
                           preferred_element_type=F32)


def _mm_f32(a, b):
    return jnp.dot(a, b, preferred_element_type=F32, precision=HIGHEST)


def _rms_rows(x, g):
    return x * lax.rsqrt(jnp.mean(x * x, axis=-1, keepdims=True) + EPS) * g


def _silu(x):
    return x * jax.nn.sigmoid(x)


def _softplus(x):
    return jnp.maximum(x, 0.0) + jnp.log1p(jnp.exp(-jnp.abs(x)))


def _rope_rows(x, cos, sin_signed):
    n = x.shape[-1]
    reps = n // 128
    if reps > 1:
        cos = jnp.concatenate([cos] * reps, axis=-1)
        sin_signed = jnp.concatenate([sin_signed] * reps, axis=-1)
    lane = lax.broadcasted_iota(I32, x.shape, x.ndim - 1)
    first_half = (lane % HD_B) < (HD_B // 2)
    partner = jnp.where(first_half, pltpu.roll(x, n - HD_B // 2, x.ndim - 1),
                        pltpu.roll(x, HD_B // 2, x.ndim - 1))
    return x * cos + partner * sin_signed


def _rope_cols(x, cos_t, sin_t):
    x1, x2 = x[:32], x[32:]
    return jnp.concatenate([x1 * cos_t - x2 * sin_t, x2 * cos_t + x1 * sin_t], axis=0)


def _seg_rms(x, ones_bd, g):
    sq = x * x
    hi = sq.astype(BF16)
    lo = (sq - hi.astype(F32)).astype(BF16)
    ss = (jnp.dot(hi, ones_bd, preferred_element_type=F32)
          + jnp.dot(lo, ones_bd, preferred_element_type=F32))
    return x * lax.rsqrt(ss * (1.0 / HD_B) + EPS) * g


def _small_block(sm, cos, sin_signed, alog_row, dtb_row):
    lane = lax.broadcasted_iota(I32, sm.shape, 1)
    roped = _rope_rows(sm, cos, sin_signed)
    beta = jax.nn.sigmoid(sm)
    g = -jnp.exp(alog_row) * _softplus(sm + dtb_row)
    wi = sm * (H_IDX ** -0.5)
    out = jnp.where(lane < _SM_BETA, roped,
                    jnp.where(lane < _SM_G, beta, jnp.where(lane < _SM_WI, g, wi)))
    return out


def _proj_prompt_body(x_ref, g1_ref, wqkv_ref, wz_ref, wkv_ref, wsm_ref,
                      wbqt_ref, wiqt_ref, wvt_ref, wsmt_ref, convw_ref, kng_ref, qngt_ref, alog_ref, dtb_ref,
                      alogt_ref, dtbt_ref, ones_ref, cos_ref, sin_ref, cost_ref, sint_ref,
                      q_out, k_out, v_out, z_out, kv_out, kbf_out, kibf_out, sm_out, kidx_out,
                      qtp_out, qit_out, vt_out, gt_out, wit_out, conv_out, ubuf, *, tm, kc):
    t = pl.program_id(1)
    x = x_ref[0]
    xn = _rms_rows(x, g1_ref[...]).astype(BF16)

    u = jnp.dot(xn, wqkv_ref[...], preferred_element_type=F32)

    @pl.when(t == 0)
    def _():
        ubuf[0:8, :] = jnp.zeros((8, CONV_CH), F32)

    ubuf[8:tm + 8, :] = u
    cw = convw_ref[...]
    acc = (ubuf[5:5 + tm, :] * cw[0:1] + ubuf[6:6 + tm, :] * cw[1:2]
           + ubuf[7:7 + tm, :] * cw[2:3] + u * cw[3:4])
    conv_out[0] = ubuf[tm + 5:tm + 8, :]
    ubuf[0:8, :] = ubuf[tm:tm + 8, :]
    c = _silu(acc)
    for h in range(H_A):
        qh = c[:, h * DK_A:(h + 1) * DK_A]
        kh = c[:, (H_A + h) * DK_A:(H_A + h + 1) * DK_A]
        q_out[:, h * DK_A:(h + 1) * DK_A] = qh * (lax.rsqrt(jnp.sum(qh * qh, -1, keepdims=True) + EPS)
                                                 * (DK_A ** -0.5))
        k_out[:, h * DK_A:(h + 1) * DK_A] = kh * lax.rsqrt(jnp.sum(kh * kh, -1, keepdims=True) + EPS)
    v_out[...] = c[:, 2 * H_A * DK_A:]
    z_out[...] = jnp.dot(xn, wz_ref[...], preferred_element_type=F32)

    cos, sin = cos_ref[...], sin_ref[...]
    ones_bd = ones_ref[...]

    kv = jnp.dot(xn, wkv_ref[...], preferred_element_type=F32)
    kb = _rope_rows(_seg_rms(kv[:, :128], ones_bd, kng_ref[...]), cos, sin)
    kv_out[:, 0:128] = kb
    kv_out[:, 128:256] = kv[:, 128:]
    kbf_out[...] = kb.astype(BF16)

    sm = jnp.dot(xn, wsm_ref[...], preferred_element_type=F32)
    smo = _small_block(sm, cos, sin, alog_ref[...], dtb_ref[...])
    sm_out[...] = smo
    kidx_out[...] = smo[:, :D_IDX]
    kibf_out[...] = smo[:, :D_IDX].astype(BF16)

    nt_dims = (((1,), (1,)), ((), ()))
    cos_t, sin_t = cost_ref[...], sint_ref[...]
    qt = lax.dot_general(wbqt_ref[...], xn, nt_dims, preferred_element_type=F32)
    qit = lax.dot_general(wiqt_ref[...], xn, nt_dims, preferred_element_type=F32)
    vt = lax.dot_general(wvt_ref[...], xn, nt_dims, preferred_element_type=F32)
    smt = lax.dot_general(wsmt_ref[...], xn, nt_dims, preferred_element_type=F32)
    qg = jnp.concatenate([qngt_ref[...]] * (tm // 128), axis=1)
    zero64 = jnp.zeros((HD_B, tm), BF16)
    for h in range(H_B):
        qh = qt[HD_B * h:HD_B * (h + 1)]
        qh = qh * lax.rsqrt(jnp.mean(qh * qh, axis=0, keepdims=True) + EPS) * qg
        qh = (_rope_cols(qh, cos_t, sin_t) * (HD_B ** -0.5 * LOG2_E)).astype(BF16)
        n = h // (H_B // N_KV_B)
        qtp_out[0, h, HD_B * n:HD_B * (n + 1), :] = qh
        qtp_out[0, h, HD_B * (1 - n):HD_B * (2 - n), :] = zero64
        qit_out[0, h] = (_rope_cols(qit[D_IDX * h:D_IDX * (h + 1)], cos_t, sin_t) * (D_IDX ** -0.5)).astype(BF16)
    vtb = vt.astype(BF16)
    for j in range(tm // kc):
        vt_out[0, j] = vtb[:, j * kc:(j + 1) * kc]
    sm8 = smt[64:72]
    row = lax.broadcasted_iota(I32, sm8.shape, 0)
    g8 = -jnp.exp(alogt_ref[...]) * _softplus(sm8 + dtbt_ref[...])
    gt_out[0] = jnp.where(row < 4, jax.nn.sigmoid(sm8), g8)
    wit_out[0] = smt[72:80] * (H_IDX ** -0.5)


def _proj_prompt(x, w, tabs, *, tm, kc):
    b, t, d = x.shape
    nt = t // tm
    m = b * t
    row = lambda n: pl.BlockSpec((tm, n), lambda i, j: (i * nt + j, 0))
    ins = [
        pl.BlockSpec((1, tm, d), lambda i, j: (i, j, 0)),
        _const_spec((1, d)),
        _const_spec((d, CONV_CH)), _const_spec((d, 512)), _const_spec((d, 256)), _const_spec((d, 128)),
        _const_spec((512, d)), _const_spec((512, d)), _const_spec((128, d)), _const_spec((128, d)),
        _const_spec((CONV_W, CONV_CH)), _const_spec((1, 128)), _const_spec((HD_B, 128)),
        _const_spec((1, 128)), _const_spec((1, 128)), _const_spec((8, 1)), _const_spec((8, 1)),
        _const_spec((128, 128)),
        pl.BlockSpec((tm, 128), lambda i, j: (j, 0)), pl.BlockSpec((tm, 128), lambda i, j: (j, 0)),
        pl.BlockSpec((32, tm), lambda i, j: (0, j)), pl.BlockSpec((32, tm), lambda i, j: (0, j)),
    ]
    nkc = t // kc
    outs = [
        (jax.ShapeDtypeStruct((m, 512), F32), row(512)),
        (jax.ShapeDtypeStruct((m, 512), F32), row(512)),
        (jax.ShapeDtypeStruct((m, 512), F32), row(512)),
        (jax.ShapeDtypeStruct((m, 512), F32), row(512)),
        (jax.ShapeDtypeStruct((m, 256), F32), row(256)),
        (jax.ShapeDtypeStruct((m, 128), BF16), row(128)),
        (jax.ShapeDtypeStruct((m, D_IDX), BF16), row(D_IDX)),
        (jax.ShapeDtypeStruct((m, 128), F32), row(128)),
        (jax.ShapeDtypeStruct((m, D_IDX), F32), row(D_IDX)),
        (jax.ShapeDtypeStruct((b, H_B, 128, t), BF16),
         pl.BlockSpec((1, H_B, 128, tm), lambda i, j: (i, 0, 0, j))),
        (jax.ShapeDtypeStruct((b, H_IDX, D_IDX, t), BF16),
         pl.BlockSpec((1, H_IDX, D_IDX, tm), lambda i, j: (i, 0, 0, j))),
        (jax.ShapeDtypeStruct((b, nkc, 128, kc), BF16),
         pl.BlockSpec((1, tm // kc, 128, kc), lambda i, j: (i, j, 0, 0))),
        (jax.ShapeDtypeStruct((b, 8, t), F32), pl.BlockSpec((1, 8, tm), lambda i, j: (i, 0, j))),
        (jax.ShapeDtypeStruct((b, H_IDX, t), F32), pl.BlockSpec((1, H_IDX, tm), lambda i, j: (i, 0, j))),
        (jax.ShapeDtypeStruct((b, CONV_W - 1, CONV_CH), F32),
         pl.BlockSpec((1, CONV_W - 1, CONV_CH), lambda i, j: (i, 0, 0))),
    ]
    return pl.pallas_call(
        functools.partial(_proj_prompt_body, tm=tm, kc=kc),
        out_shape=[o[0] for o in outs],
        grid=(b, nt),
        in_specs=ins,
        out_specs=[o[1] for o in outs],
        scratch_shapes=[pltpu.VMEM((tm + 8, CONV_CH), F32)],
        compiler_params=_cparams(2),
        name="proj_prompt",
    )(x, w["g1"], w["wqkv"], w["wz"], w["wkv"], w["wsm"], w["wbqt"], w["wiqt"], w["wvt"], w["wsmt"],
      w["convw"], w["kng"], w["qngt"], w["alog_row"], w["dtb_row"], w["alog_col"], w["dtb_col"],
      w["ones_bd128"], tabs["cos"], tabs["sin"], tabs["cos_t"], tabs["sin_t"])


def _split3(x):
    hi = x.astype(BF16)
    r1 = x - hi.astype(F32)
    mid = r1.astype(BF16)
    lo = (r1 - mid.astype(F32)).astype(BF16)
    return hi, mid, lo


def _gdn_body(q_ref, k_ref, v_ref, z_ref, sm_ref, gt_ref, gg_ref, bdk_ref, bdp_ref, o_ref, s_ref, *, nb):
    c = pl.program_id(1)

    @pl.when(c == 0)
    def _():
        s_ref[...] = jnp.zeros_like(s_ref)

    n, nh = CHUNK, H_A
    w4 = nh * n
    ri4 = lax.broadcasted_iota(I32, (n, w4), 0)
    ci4 = lax.broadcasted_iota(I32, (n, w4), 1) % n
    strict4, incl4 = ri4 > ci4, ri4 >= ci4
    eye4 = jnp.where(ri4 == ci4, 1.0, 0.0)
    triu4 = jnp.where(ri4 <= ci4, 1.0, 0.0).astype(BF16)
    ri = lax.broadcasted_iota(I32, (n, n), 0)
    tril = jnp.where(ri >= lax.broadcasted_iota(I32, (n, n), 1), 1.0, 0.0).astype(BF16)
    blk_row = lax.broadcasted_iota(I32, (1, w4), 1) // n
    low_half = lax.broadcasted_iota(I32, (n, 128), 1) < n
    zeros_pad = jnp.zeros((n, 2 * DV_A), BF16)
    gg = gg_ref[...]
    dot = functools.partial(jnp.dot, preferred_element_type=F32)

    def pad_rows(x, e):
        z = zeros_pad[:, :x.shape[1]]
        return jnp.concatenate([x, z] if e == 0 else [z, x], axis=0)

    seqs = range(nb)
    st = [dict() for _ in seqs]

    for bi in seqs:
        hi, mid, lo = _split3(sm_ref[bi])
        st[bi]["gc_all"] = dot(tril, hi) + dot(tril, mid) + dot(tril, lo)
        hi, mid, lo = _split3(gt_ref[bi, 0])
        st[bi]["grows"] = dot(hi, triu4) + dot(mid, triu4) + dot(lo, triu4)

    for bi in seqs:
        d = st[bi]
        sm, k, grows = sm_ref[bi], k_ref[bi], d["grows"]
        grow = jnp.where(blk_row == 0, grows[4:5], jnp.where(blk_row == 1, grows[5:6],
                         jnp.where(blk_row == 2, grows[6:7], grows[7:8])))
        gcols = [jnp.broadcast_to(d["gc_all"][:, _SM_G + h:_SM_G + h + 1], (n, 128)) for h in range(nh)]
        betas = [jnp.broadcast_to(sm[:, _SM_BETA + h:_SM_BETA + h + 1], (n, 128)) for h in range(nh)]
        d["gc512"] = jnp.concatenate(gcols, axis=1)
        d["beta512"] = jnp.concatenate(betas, axis=1)
        gc256 = jnp.concatenate([jnp.where(low_half, gcols[0], gcols[1]),
                                 jnp.where(low_half, gcols[2], gcols[3])], axis=1)
        d["dec"] = jnp.exp(jnp.minimum(gc256 - grow, 0.0))
        d["kb"] = k * d["beta512"]
        kbd = jnp.concatenate([k.astype(BF16)] * nh, axis=0) * bdk_ref[...]
        d["aq"] = lax.dot_general(jnp.concatenate([d["kb"], q_ref[bi]], axis=0).astype(BF16), kbd,
                                  (((1,), (1,)), ((), ())), preferred_element_type=F32)

    for bi in seqs:
        d = st[bi]
        x = jnp.where(strict4, -d["aq"][:n] * d["dec"], 0.0)
        d["qkm"] = jnp.where(incl4, d["aq"][n:] * d["dec"], 0.0).astype(BF16)
        d["inv"], d["p"] = eye4 + x, x
    for lvl in range(6):
        for bi in seqs:
            d = st[bi]
            p, inv = d["p"], d["inv"]
            wbd = jnp.concatenate([p.astype(BF16)] * nh, axis=0) * bdp_ref[...]
            if lvl == 0:
                d["p"] = dot(p.astype(BF16), wbd)
            elif lvl < 5:
                r = dot(jnp.concatenate([p, inv], axis=0).astype(BF16), wbd)
                d["p"], d["inv"] = r[:n], inv + r[n:]
            else:
                d["inv"] = inv + dot(inv.astype(BF16), wbd)

    heads = [(bi, h) for bi in seqs for h in range(nh)]
    hsl = lambda h: slice(h * DK_A, (h + 1) * DK_A)
    psl = lambda h: slice(128 * (h // 2), 128 * (h // 2 + 1))
    for bi in seqs:
        d = st[bi]
        k, gc512 = k_ref[bi], d["gc512"]
        d["tinv"] = d["inv"].astype(BF16)
        egc = jnp.exp(gc512)
        d["rhs_v"] = (v_ref[bi] * d["beta512"]).astype(BF16)
        d["rhs_k"] = (d["kb"] * egc).astype(BF16)
        g_last = gc512[n - 1:n, :]
        d["q_dec"] = q_ref[bi] * egc
        d["k_dec"] = (k * jnp.exp(g_last - gc512)).astype(BF16)
        d["c_dec"] = jnp.exp(g_last)
    sol, ws, vn = {}, {}, {}
    for bi, h in heads:
        d = st[bi]
        rhs = pad_rows(jnp.concatenate([d["rhs_v"][:, hsl(h)], d["rhs_k"][:, hsl(h)]], axis=1), h % 2)
        sol[bi, h] = dot(d["tinv"][:, psl(h)], rhs)
    for bi, h in heads:
        lhs = jnp.concatenate([sol[bi, h][:, DV_A:], st[bi]["q_dec"][:, hsl(h)]], axis=0)
        ws[bi, h] = dot(lhs.astype(BF16), s_ref[bi, h].astype(BF16))
    for bi, h in heads:
        d = st[bi]
        vn[bi, h] = (sol[bi, h][:, :DV_A] - ws[bi, h][:n]).astype(BF16)
        s_ref[bi, h] = d["c_dec"][:, hsl(h)] * s_ref[bi, h] + lax.dot_general(
            d["k_dec"][:, hsl(h)], vn[bi, h], (((0,), (0,)), ((), ())), preferred_element_type=F32)
    for bi, h in heads:
        o = ws[bi, h][n:] + dot(st[bi]["qkm"][:, psl(h)], pad_rows(vn[bi, h], h % 2))
        o_ref[bi, :, hsl(h)] = (_rms_rows(o, gg) * _silu(z_ref[bi, :, hsl(h)])).astype(BF16)


def _gdn_prompt(q, k, v, z, sm, gt, gg, *, b, t, nb):
    nc = t // CHUNK
    r3 = lambda a, n: a.reshape(b, t, n)
    blk = lambda n: pl.BlockSpec((nb, CHUNK, n), lambda i, j: (i, j, 0))
    w4 = H_A * CHUNK
    bdk = np.kron(np.eye(H_A, dtype=np.float32), np.ones((CHUNK, DK_A), np.float32))
    bdp = np.kron(np.eye(H_A, dtype=np.float32), np.ones((CHUNK, CHUNK), np.float32))
    return pl.pallas_call(
        functools.partial(_gdn_body, nb=nb),
        out_shape=[jax.ShapeDtypeStruct((b, t, 512), BF16),
                   jax.ShapeDtypeStruct((b, H_A, DK_A, DV_A), F32)],
        grid=(b // nb, nc),
        in_specs=[blk(512), blk(512), blk(512), blk(512), blk(128),
                  pl.BlockSpec((nb, 1, 8, CHUNK), lambda i, j: (i, j, 0, 0)),
                  _const_spec((1, DV_A)), _const_spec((w4, H_A * DK_A)), _const_spec((w4, w4))],
        out_specs=[blk(512), pl.BlockSpec((nb, H_A, DK_A, DV_A), lambda i, j: (i, 0, 0, 0))],
        compiler_params=_cparams(2),
        name="gdn_prompt",
    )(r3(q, 512), r3(k, 512), r3(v, 512), r3(z, 512), r3(sm, 128),
      jnp.swapaxes(gt.reshape(b, 8, nc, CHUNK), 1, 2), gg, jnp.asarray(bdk, BF16), jnp.asarray(bdp, BF16))


def _ordinal_to_float(k):
    k = jnp.maximum(k, NEG_INF_KEY)
    return pltpu.bitcast(k ^ ((k >> 31) & 0x7FFFFFFF), F32)


def _kth_largest(count_ge, shape, k_top):
    def bit_step(i, o):
        cand = o + jnp.left_shift(jnp.int32(1), 31 - i)
        return jnp.where(count_ge(_ordinal_to_float(cand)) >= k_top, cand, o)

    o = lax.fori_loop(0, 32, bit_step, jnp.full(shape, INT_MIN, I32))
    return _ordinal_to_float(o), o


def _kth_largest_two_stage(count_ge, count_ge_rounded, shape, k_top):
    def bf16_ordinal_as_f32_ordinal(h):
        return jnp.where(h >= 0, h << 16, (h << 16) | 0xFFFF)

    def coarse_step(i, h):
        cand = h + jnp.left_shift(jnp.int32(1), 15 - i)
        c = _ordinal_to_float(bf16_ordinal_as_f32_ordinal(cand)).astype(BF16)
        return jnp.where(count_ge_rounded(c) >= k_top, cand, h)

    h = lax.fori_loop(0, 16, coarse_step, jnp.full(shape, -(2 ** 15), I32))
    o = bf16_ordinal_as_f32_ordinal(h)
    o = jnp.where(count_ge(_ordinal_to_float(o)) >= k_top, o, bf16_ordinal_as_f32_ordinal(h - 1))

    def fine_step(i, o):
        cand = o + jnp.left_shift(jnp.int32(1), 15 - i)
        return jnp.where(count_ge(_ordinal_to_float(cand)) >= k_top, cand, o)

    o = lax.fori_loop(0, 16, fine_step, o)
    return _ordinal_to_float(o), o


def _topk_threshold(count_ge, count_gt_eq, count_eq_below, shape, k_top, idx_bits):
    t, o = _kth_largest(count_ge, shape, k_top)
    c_gt, c_eq = count_gt_eq(t)
    need = k_top - c_gt
    untied = jnp.logical_or(c_eq == need, o <= NEG_INF_KEY)
    all_untied = jnp.min(jnp.where(untied, 1.0, 0.0)) > 0.5

    def tie_search(_):
        def idx_step(i, x):
            cand = x + jnp.left_shift(jnp.int32(1), idx_bits - 1 - i)
            return jnp.where(count_eq_below(t, cand) < need, cand, x)
        return lax.fori_loop(0, idx_bits, idx_step, jnp.zeros(shape, I32))

    x = lax.cond(all_untied, lambda _: jnp.full(shape, 2 ** 30, I32), tie_search, 0)
    return t, x


_ACC_ROWS = HD_B + 16
_LOGITS_AHEAD = 5


def _dsa_body(qit_ref, wit_ref, qtp_ref, ki_ref, k_ref, vt_ref, o_ref,
              sc_ref, sb_ref, m_ref, acc_ref, *, tq, kc, k_top):
    i = pl.program_id(1)
    nkc = i + 1
    rowi = lax.broadcasted_iota(I32, (kc, tq), 0)
    coli = lax.broadcasted_iota(I32, (kc, tq), 1)
    ng = kc // 32

    wit = wit_ref[0]

    def score_chunks(chunks):
        for c in chunks:
            kic = ki_ref[0, c]
            score = jnp.zeros((kc, tq), F32)
            for h in range(H_IDX):
                s = jnp.dot(kic, qit_ref[0, h], preferred_element_type=F32)
                score = score + jnp.maximum(s, 0.0) * wit[h:h + 1, :]
            causal = rowi + (c - i) * kc <= coli
            score = jnp.where(causal, score, -jnp.inf)
            sc_ref[c] = score
            sb_ref[c] = score.astype(BF16)

    def score_pair(p, carry):
        score_chunks([2 * p, 2 * p + 1])
        return carry

    lax.fori_loop(0, nkc // 2, score_pair, 0)

    @pl.when(nkc % 2 == 1)
    def _():
        score_chunks([nkc - 1])

    def key_sum(f):
        def body(c, acc):
            ind = f(sc_ref[c], c * kc)
            return acc + jnp.sum(ind.reshape(ng, 32, tq), axis=0)
        acc = lax.fori_loop(0, nkc, body, jnp.zeros((32, tq), F32))
        return jnp.sum(acc, axis=0, keepdims=True).astype(I32)

    def count_ge_rounded(cand):
        one, zero = jnp.ones((), BF16), jnp.zeros((), BF16)

        def body(c, acc):
            ind = jnp.where(sb_ref[c] >= cand, one, zero)
            for g in range(ng):
                acc = acc + ind[32 * g:32 * (g + 1)]
            return acc
        acc = lax.fori_loop(0, nkc, body, jnp.zeros((32, tq), BF16))
        return jnp.sum(acc.astype(F32), axis=0, keepdims=True).astype(I32)

    t, _ = _kth_largest_two_stage(lambda cand: key_sum(lambda s, off: jnp.where(s >= cand, 1.0, 0.0)),
                                  count_ge_rounded, (1, tq), k_top)
    ties_wanted = (k_top - key_sum(lambda s, off: jnp.where(s > t, 1.0, 0.0))).astype(F32)

    m_ref[...] = jnp.full(m_ref.shape, NEG_BIG, F32)
    acc_ref[...] = jnp.zeros(acc_ref.shape, F32)
    ones_rows = jnp.ones((_ACC_ROWS - HD_B, kc), BF16)
    prefix = jnp.where(lax.broadcasted_iota(I32, (kc, kc), 0) >= lax.broadcasted_iota(I32, (kc, kc), 1),
                       1.0, 0.0).astype(BF16)

    def attn_chunks(chunks, ties):
        kks, biases, vaugs = [], [], []
        for c in chunks:
            sc = sc_ref[c]
            tied = jnp.where(sc == t, 1.0, 0.0).astype(BF16)
            rank = ties + jnp.dot(prefix, tied, preferred_element_type=F32)
            tie = jnp.where(sc == t, jnp.where(rank <= ties_wanted, 0.0, NEG_BIG), NEG_BIG)
            biases.append(jnp.where(sc > -jnp.inf, jnp.where(sc > t, 0.0, tie), NEG_BIG))
            ties = rank[kc - 1:kc, :]
            kks.append(k_ref[0, c])
            vt = vt_ref[0, c]
            vaugs.append([jnp.concatenate([vt[HD_B * n:HD_B * (n + 1)], ones_rows], axis=0)
                          for n in range(N_KV_B)])
        units = [(j, h) for j in range(len(chunks)) for h in range(H_B)]
        logits = {}

        def issue(u):
            j, h = units[u]
            logits[u] = jnp.dot(kks[j], qtp_ref[0, h], preferred_element_type=F32)

        for u in range(_LOGITS_AHEAD):
            issue(u)
        for u, (j, h) in enumerate(units):
            if u + _LOGITS_AHEAD < len(units):
                issue(u + _LOGITS_AHEAD)
            n = h // (H_B // N_KV_B)
            s = biases[j] + logits.pop(u)
            m_old = m_ref[h:h + 1, :]
            m_new = jnp.maximum(m_old, jnp.max(s, axis=0, keepdims=True))
            alpha = jnp.exp2(m_old - m_new)
            pexp = jnp.exp2(s - m_new).astype(BF16)
            acc_ref[h] = acc_ref[h] * alpha + jnp.dot(vaugs[j][n], pexp, preferred_element_type=F32)
            m_ref[h:h + 1, :] = m_new
        return ties

    ties = lax.fori_loop(0, nkc // 2, lambda p, tb: attn_chunks([2 * p, 2 * p + 1], tb),
                         jnp.zeros((1, tq), F32))

    @pl.when(nkc % 2 == 1)
    def _():
        attn_chunks([nkc - 1], ties)

    for p in range(H_B // 2):
        halves = []
        for h in (2 * p, 2 * p + 1):
            acc = acc_ref[h]
            halves.append(acc[:HD_B] / acc[HD_B:HD_B + 1])
        o_ref[:, 128 * p:128 * (p + 1)] = jnp.concatenate(halves, axis=0).T.astype(BF16)


def _dsa_prompt(qit, wit, qtp, kibf, kbf, vt, *, b, t, tq, k_top):
    kc = tq
    nq = t // tq
    nkc = t // kc
    whole = lambda *s: pl.BlockSpec((1,) + s, lambda i, j: (i,) + (0,) * len(s))
    return pl.pallas_call(
        functools.partial(_dsa_body, tq=tq, kc=kc, k_top=k_top),
        out_shape=jax.ShapeDtypeStruct((b * t, H_B * HD_B), BF16),
        grid=(b, nq),
        in_specs=[pl.BlockSpec((1, H_IDX, D_IDX, tq), lambda i, j: (i, 0, 0, j)),
                  pl.BlockSpec((1, H_IDX, tq), lambda i, j: (i, 0, j)),
                  pl.BlockSpec((1, H_B, 128, tq), lambda i, j: (i, 0, 0, j)),
                  whole(nkc, kc, D_IDX), whole(nkc, kc, 128), whole(nkc, 128, kc)],
        out_specs=pl.BlockSpec((tq, H_B * HD_B), lambda i, j: (i * nq + j, 0)),
        scratch_shapes=[pltpu.VMEM((nkc, kc, tq), F32), pltpu.VMEM((nkc, kc, tq), BF16),
                        pltpu.VMEM((H_B, tq), F32), pltpu.VMEM((H_B, _ACC_ROWS, tq), F32)],
        compiler_params=_cparams(2),
        name="dsa_prompt",
    )(qit, wit, qtp, kibf.reshape(b, nkc, kc, D_IDX), kbf.reshape(b, nkc, kc, 128), vt)


def _merge_body(x_ref, oa_ref, ob_ref, g1_ref, wg_ref, woa_ref, wob_ref, wo_ref, h_ref):
    x = x_ref[...]
    xn = _rms_rows(x, g1_ref[...]).astype(BF16)
    gates = jax.nn.sigmoid(jnp.dot(xn, wg_ref[...], preferred_element_type=F32))
    a = jnp.dot(oa_ref[...], woa_ref[...], preferred_element_type=F32)
    bb = jnp.dot(ob_ref[...], wob_ref[...], preferred_element_type=F32)
    mix = gates[:, :D_MODEL] * a + gates[:, D_MODEL:] * bb
    h_ref[...] = x + jnp.dot(mix.astype(BF16), wo_ref[...], preferred_element_type=F32)


def _merge(x, oa, ob, wob, w, *, tm):
    m, d = x.shape
    nb = ob.shape[1]
    row = lambda n: pl.BlockSpec((tm, n), lambda i: (i, 0))
    return pl.pallas_call(
        _merge_body,
        out_shape=jax.ShapeDtypeStruct((m, d), F32),
        grid=(m // tm,),
        in_specs=[row(d), row(512), row(nb), _const_spec((1, d)), _const_spec((d, 2 * d)),
                  _const_spec((512, d)), _const_spec((nb, d)), _const_spec((d, d))],
        out_specs=row(d),
        compiler_params=_cparams(1),
        name="merge",
    )(x, oa, ob, w["g1"], w["wg"], w["woa"], wob, w["wo"])


def _mlp_body(h_ref, g2_ref, wup_ref, wdn_ref, y_ref):
    h = h_ref[...]
    hn = _rms_rows(h, g2_ref[...]).astype(BF16)
    up = jnp.dot(hn, wup_ref[...], preferred_element_type=F32)
    act = jnp.square(jnp.maximum(up, 0.0)).astype(BF16)
    y_ref[...] = h + jnp.dot(act, wdn_ref[...], preferred_element_type=F32)


def _mlp(h, w, *, tm):
    m, d = h.shape
    row = pl.BlockSpec((tm, d), lambda i: (i, 0))
    return pl.pallas_call(
        _mlp_body,
        out_shape=jax.ShapeDtypeStruct((m, d), F32),
        grid=(m // tm,),
        in_specs=[row, _const_spec((1, d)), _const_spec((d, D_FF)), _const_spec((D_FF, d))],
        out_specs=row,
        compiler_params=_cparams(1),
        name="mlp",
    )(h, w["g2"], w["wup"], w["wdn"])


def _proj_decode_body(x_ref, st_ref, g1_ref, wqkv_ref, wz_ref, wbq_ref, wkv_ref, wiq_ref, wsm_ref,
                      convw_ref, qng_ref, kng_ref, alog_ref, dtb_ref, ones_ref, seg_ref, cos_ref, sin_ref,
                      u_out, q_out, k_out, v_out, z_out, qb_out, kv_out, qi_out, sm_out, ss_out):
    x = x_ref[...]
    xn = _rms_rows(x, g1_ref[...]).astype(BF16)
    u = jnp.dot(xn, wqkv_ref[...], preferred_element_type=F32)
    u_out[...] = u
    cw = convw_ref[...]
    acc = st_ref[0] * cw[0:1] + st_ref[1] * cw[1:2] + st_ref[2] * cw[2:3] + u * cw[3:4]
    c = _silu(acc)
    for h in range(H_A):
        qh = c[:, h * DK_A:(h + 1) * DK_A]
        kh = c[:, (H_A + h) * DK_A:(H_A + h + 1) * DK_A]
        q_out[:, h * DK_A:(h + 1) * DK_A] = qh * (lax.rsqrt(jnp.sum(qh * qh, -1, keepdims=True) + EPS)
                                                 * (DK_A ** -0.5))
        k_out[:, h * DK_A:(h + 1) * DK_A] = kh * lax.rsqrt(jnp.sum(kh * kh, -1, keepdims=True) + EPS)
    v_out[...] = c[:, 2 * H_A * DK_A:]
    z_out[...] = jnp.dot(xn, wz_ref[...], preferred_element_type=F32)
    cos, sin = cos_ref[...], sin_ref[...]
    ones_bd = ones_ref[...]
    bq = jnp.dot(xn, wbq_ref[...], preferred_element_type=F32)
    qb = _rope_rows(_seg_rms(bq, ones_bd, qng_ref[...]), cos, sin)
    qb_out[...] = (qb * (HD_B ** -0.5)).astype(BF16)
    kv = jnp.dot(xn, wkv_ref[...], preferred_element_type=F32)
    kb = _rope_rows(_seg_rms(kv[:, :128], ones_bd[:128, :128], kng_ref[...]), cos, sin)
    kv_out[:, 0:128] = kb
    kv_out[:, 128:256] = kv[:, 128:]
    iq = jnp.dot(xn, wiq_ref[...], preferred_element_type=F32)
    qi = (_rope_rows(iq, cos, sin) * (D_IDX ** -0.5)).astype(BF16)
    qi_out[...] = qi
    sm = jnp.dot(xn, wsm_ref[...], preferred_element_type=F32)
    smo = _small_block(sm, cos, sin, alog_ref[...], dtb_ref[...])
    sm_out[...] = smo
    lane = lax.broadcasted_iota(I32, smo.shape, 1)
    ki = smo.astype(BF16).astype(F32)
    ki2 = jnp.where(lane < D_IDX, ki, pltpu.roll(ki, D_IDX, 1))
    prod = qi.astype(F32) * jnp.concatenate([ki2] * 4, axis=-1)
    sh = _mm_f32(prod, seg_ref[...])
    contrib = jnp.where(jnp.logical_and(lane >= _SM_WI, lane < _SM_WI + H_IDX),
                        jnp.maximum(sh, 0.0) * smo, 0.0)
    ss_out[...] = jnp.broadcast_to(jnp.sum(contrib, axis=-1, keepdims=True), ss_out.shape)


def _proj_decode(x, st, w, tabs):
    r, d = x.shape
    outs = [(r, CONV_CH, F32), (r, 512, F32), (r, 512, F32), (r, 512, F32), (r, 512, F32), (r, 512, BF16),
            (r, 256, F32), (r, 512, BF16), (r, 128, F32), (r, 128, F32)]
    args = (x, st, w["g1"], w["wqkv"], w["wz"], w["wbq"], w["wkv"], w["wiq"], w["wsm"], w["convw"],
            w["qng"], w["kng"], w["alog_row"], w["dtb_row"], w["ones_bd"], w["seg_sum"], tabs["cos"], tabs["sin"])
    return pl.pallas_call(
        _proj_decode_body,
        out_shape=[jax.ShapeDtypeStruct((a, n), dt) for a, n, dt in outs],
        compiler_params=pltpu.CompilerParams(vmem_limit_bytes=VMEM_LIMIT),
        name="proj_decode",
    )(*args)


def _gdn_decode_body(qc_ref, kc_ref, v_ref, z_ref, sm_ref, gg_ref, s_ref, o_ref, so_ref):
    sm = sm_ref[0]
    gg = gg_ref[...]
    for h in range(H_A):
        g = sm[:, _SM_G + h:_SM_G + h + 1]
        beta = sm[:, _SM_BETA + h:_SM_BETA + h + 1]
        kcol, qcol = kc_ref[0, h], qc_ref[0, h]
        v = v_ref[0, h]
        s = s_ref[0, h] * jnp.exp(g)
        vn = beta * (v - jnp.sum(s * kcol, axis=0, keepdims=True))
        s = s + kcol * vn
        so_ref[0, h] = s
        o = jnp.sum(s * qcol, axis=0, keepdims=True)
        o_ref[0, h] = (_rms_rows(o, gg) * _silu(z_ref[0, h])).astype(BF16)


def _gdn_decode(q, k, v, z, sm, gg, state):
    r = q.shape[0]
    col = lambda a: a.reshape(r, H_A, DK_A, 1)
    rowv = lambda a: a.reshape(r, H_A, 1, DV_A)
    cspec = pl.BlockSpec((1, H_A, DK_A, 1), lambda i: (i, 0, 0, 0))
    rspec = pl.BlockSpec((1, H_A, 1, DV_A), lambda i: (i, 0, 0, 0))
    sspec = pl.BlockSpec((1, H_A, DK_A, DV_A), lambda i: (i, 0, 0, 0))
    o, s_new = pl.pallas_call(
        _gdn_decode_body,
        out_shape=[jax.ShapeDtypeStruct((r, H_A, 1, DV_A), BF16),
                   jax.ShapeDtypeStruct((r, H_A, DK_A, DV_A), F32)],
        grid=(r,),
        in_specs=[cspec, cspec, rspec, rspec, pl.BlockSpec((1, 1, 128), lambda i: (i, 0, 0)),
                  _const_spec((1, DV_A)), sspec],
        out_specs=[rspec, sspec],
        compiler_params=_cparams(1),
        name="gdn_decode",
    )(col(q), col(k), rowv(v), rowv(z), sm.reshape(r, 1, 128), gg, state)
    return o.reshape(r, H_A * DV_A), s_new


def _idx_decode_body(pt_ref, qi_ref, w_ref, *refs, npg):
    pages, out_ref = refs[:npg], refs[npg]
    qi = qi_ref[0]
    w = w_ref[0]
    for p in range(npg):
        s = _mm(qi, pages[p][0])
        out_ref[0, :, PAGE_SIZE * p:PAGE_SIZE * (p + 1)] = jnp.sum(jnp.maximum(s, 0.0) * w, axis=0,
                                                                   keepdims=True)


def _idx_decode(page_table, qi, wi_col, cache_kidx, *, npg):
    r, n_pages = page_table.shape
    page_spec = lambda p: pl.BlockSpec((1, D_IDX, PAGE_SIZE), lambda i, j, pt: (pt[i, j * npg + p], 0, 0))
    grid_spec = pltpu.PrefetchScalarGridSpec(
        num_scalar_prefetch=1,
        grid=(r, n_pages // npg),
        in_specs=[pl.BlockSpec((1, H_IDX, D_IDX), lambda i, j, pt: (i, 0, 0)),
                  pl.BlockSpec((1, H_IDX, 1), lambda i, j, pt: (i, 0, 0))]
                 + [page_spec(p) for p in range(npg)],
        out_specs=pl.BlockSpec((1, 1, PAGE_SIZE * npg), lambda i, j, pt: (i, 0, j)),
    )
    return pl.pallas_call(
        functools.partial(_idx_decode_body, npg=npg),
        out_shape=jax.ShapeDtypeStruct((r, 1, n_pages * PAGE_SIZE), F32),
        grid_spec=grid_spec,
        compiler_params=_cparams(2),
        name="idx_decode",
    )(page_table, qi.reshape(r, H_IDX, D_IDX), wi_col, *([cache_kidx] * npg))


def _select_decode_body(sc_ref, ss_ref, bias_ref, bself_ref, *, k_top, cw):
    r, n = sc_ref.shape
    nch = n // cw
    key_ref = sc_ref
    skey = ss_ref[:, 0:1]

    def lane_sum(f):
        def body(c, acc):
            off = pl.multiple_of(c * cw, cw)
            kk = key_ref[:, pl.ds(off, cw)]
            for j in range(cw // 128):
                acc = acc + f(kk[:, 128 * j:128 * (j + 1)], off + 128 * j)
            return acc
        acc = lax.fori_loop(0, nch, body, jnp.zeros((r, 128), F32))
        return jnp.sum(acc, axis=-1, keepdims=True).astype(I32)

    def count_ge(cand):
        cb = jnp.broadcast_to(cand, (r, 128))
        return lane_sum(lambda kk, off: jnp.where(kk >= cb, 1.0, 0.0)) + (skey >= cand).astype(I32)

    def count_gt_eq(t):
        tb = jnp.broadcast_to(t, (r, 128))
        return (lane_sum(lambda kk, off: jnp.where(kk > tb, 1.0, 0.0)) + (skey > t).astype(I32),
                lane_sum(lambda kk, off: jnp.where(kk == tb, 1.0, 0.0)) + (skey == t).astype(I32))

    def count_eq_below(t, j):
        tb = jnp.broadcast_to(t, (r, 128))
        jb = jnp.broadcast_to(j, (r, 128))
        lane = lax.broadcasted_iota(I32, (r, 128), 1)
        return (lane_sum(lambda kk, off: jnp.where(kk == tb, jnp.where(lane + off < jb, 1.0, 0.0), 0.0))
                + jnp.logical_and(skey == t, j > n).astype(I32))

    t, x = _topk_threshold(count_ge, count_gt_eq, count_eq_below, (r, 1), k_top, 15)
    tb = jnp.broadcast_to(t, (r, cw))
    xb = jnp.broadcast_to(x, (r, cw))

    def bias_chunk(c, carry):
        off = pl.multiple_of(c * cw, cw)
        kk = key_ref[:, pl.ds(off, cw)]
        idx = lax.broadcasted_iota(I32, (r, cw), 1) + off
        tie = jnp.where(kk == tb, jnp.where(idx <= xb, 0.0, NEG_BIG), NEG_BIG)
        bias_ref[:, pl.ds(off, cw)] = jnp.where(kk > tb, 0.0, tie)
        return carry

    lax.fori_loop(0, nch, bias_chunk, 0)
    tie_self = jnp.where(skey == t, jnp.where(x >= n, 0.0, NEG_BIG), NEG_BIG)
    bself_ref[...] = jnp.broadcast_to(jnp.where(skey > t, 0.0, tie_self), bself_ref.shape)


def _select_decode(scores, sself, *, k_top):
    r, n = scores.shape
    return pl.pallas_call(
        functools.partial(_select_decode_body, k_top=k_top, cw=512),
        out_shape=[jax.ShapeDtypeStruct((r, n), F32), jax.ShapeDtypeStruct((r, 128), F32)],
        compiler_params=pltpu.CompilerParams(vmem_limit_bytes=VMEM_LIMIT),
        name="select_decode",
    )(scores, sself)


def _attn_decode_body(pt_ref, qp_ref, bias_ref, kvs_ref, bself_ref, *refs, npg, nsteps):
    pages, o_ref, m_ref, l_ref, acc_ref = refs[:npg], refs[npg], refs[npg + 1], refs[npg + 2], refs[npg + 3]
    j = pl.program_id(1)

    @pl.when(j == 0)
    def _():
        m_ref[...] = jnp.full(m_ref.shape, NEG_BIG, F32)
        l_ref[...] = jnp.zeros(l_ref.shape, F32)
        acc_ref[...] = jnp.zeros(acc_ref.shape, F32)

    qp = qp_ref[0]

    ss = [_mm(qp, pages[p][0, 0:128, :]) + bias_ref[0, :, PAGE_SIZE * p:PAGE_SIZE * (p + 1)]
          for p in range(npg)]
    m_old = m_ref[...]
    m_new = m_old
    for s in ss:
        m_new = jnp.maximum(m_new, jnp.max(s, axis=-1, keepdims=True))
    alpha = jnp.exp(m_old - m_new)
    l_new = l_ref[...] * alpha
    acc_new = acc_ref[...] * alpha
    for p in range(npg):
        pe = jnp.exp(ss[p] - m_new)
        l_new = l_new + jnp.sum(pe, axis=-1, keepdims=True)
        acc_new = acc_new + _mm_nt(pe, pages[p][0, 128:256, :])
    m_ref[...] = m_new
    l_ref[...] = l_new
    acc_ref[...] = acc_new

    @pl.when(j == nsteps - 1)
    def _():
        kvs = kvs_ref[0]
        krow = kvs[:, :128].astype(BF16).astype(F32)
        s = jnp.sum(qp.astype(F32) * krow, axis=-1, keepdims=True) + bself_ref[0][:, 0:1]
        m_old = m_ref[...]
        m_new = jnp.maximum(m_old, s)
        alpha = jnp.exp(m_old - m_new)
        p = jnp.exp(s - m_new)
        l = l_ref[...] * alpha + p
        vrow = kvs[:, 128:].astype(BF16).astype(F32)
        acc = acc_ref[...] * alpha + p.astype(BF16).astype(F32) * vrow
        out = acc / l
        rowi = lax.broadcasted_iota(I32, out.shape, 0)
        out = jnp.where(rowi < 4, out, pltpu.roll(out, HD_B, 1))
        o_ref[0] = out.astype(BF16)


def _attn_decode(page_table, qpad, bias, kv_self, bself, cache_kv, *, npg):
    r, n_pages = page_table.shape
    nsteps = n_pages // npg
    page_spec = lambda p: pl.BlockSpec((1, 256, PAGE_SIZE), lambda i, j, pt: (pt[i, j * npg + p], 0, 0))
    grid_spec = pltpu.PrefetchScalarGridSpec(
        num_scalar_prefetch=1,
        grid=(r, nsteps),
        in_specs=[pl.BlockSpec((1, H_B, 128), lambda i, j, pt: (i, 0, 0)),
                  pl.BlockSpec((1, 1, PAGE_SIZE * npg), lambda i, j, pt: (i, 0, j)),
                  pl.BlockSpec((1, 1, 256), lambda i, j, pt: (i, 0, 0)),
                  pl.BlockSpec((1, 1, 128), lambda i, j, pt: (i, 0, 0))]
                 + [page_spec(p) for p in range(npg)],
        out_specs=pl.BlockSpec((1, H_B, 128), lambda i, j, pt: (i, 0, 0)),
        scratch_shapes=[pltpu.VMEM((H_B, 1), F32), pltpu.VMEM((H_B, 1), F32), pltpu.VMEM((H_B, 128), F32)],
    )
    return pl.pallas_call(
        functools.partial(_attn_decode_body, npg=npg, nsteps=nsteps),
        out_shape=jax.ShapeDtypeStruct((r, H_B, 128), BF16),
        grid_spec=grid_spec,
        compiler_params=_cparams(2),
        name="attn_decode",
    )(page_table, qpad, bias.reshape(r, 1, -1), kv_self.reshape(r, 1, 256), bself.reshape(r, 1, 128),
      *([cache_kv] * npg))


def _prep_weights(norm1_g, w_in, conv_w, a_log, dt_bias, gdn_norm_g, q_norm_g, k_norm_g,
                  w_out_a, w_out_b, w_o, norm2_g, w_up, w_down):
    o = _OFF
    wb = w_in.astype(BF16)
    d = w_in.shape[0]
    wsm = jnp.concatenate([wb[:, o["ik"]:o["iw"]], wb[:, o["ab"]:o["bq"]], wb[:, o["iw"]:o["ga"]],
                           jnp.zeros((d, 128 - 80), BF16)], axis=1)

    def lanes(vals, start):
        return jnp.zeros((1, 128), F32).at[0, start:start + vals.shape[0]].set(vals.astype(F32))

    seg = np.zeros((512, 128), np.float32)
    for h in range(H_IDX):
        seg[64 * h:64 * (h + 1), _SM_WI + h] = 1.0
    ones_bd = np.kron(np.eye(8, dtype=np.float32), np.ones((64, 64), np.float32))
    wob = w_out_b.astype(BF16).reshape(H_B, HD_B, d)
    wob_pad = jnp.concatenate([wob, jnp.zeros_like(wob)], axis=1).reshape(H_B * 128, d)
    col8 = lambda v: jnp.concatenate([jnp.zeros((4,), F32), v.astype(F32)]).reshape(8, 1)
    return dict(
        g1=norm1_g.reshape(1, d).astype(F32),
        wqkv=wb[:, o["aq"]:o["az"]], wz=wb[:, o["az"]:o["ab"]], wbq=wb[:, o["bq"]:o["bk"]],
        wkv=wb[:, o["bk"]:o["iq"]], wiq=wb[:, o["iq"]:o["ik"]], wsm=wsm,
        wbqt=wb[:, o["bq"]:o["bk"]].T, wiqt=wb[:, o["iq"]:o["ik"]].T, wvt=wb[:, o["bv"]:o["iq"]].T,
        wsmt=wsm.T, wg=wb[:, o["ga"]:o["end"]],
        convw=conv_w.astype(F32),
        qng=jnp.tile(q_norm_g.astype(F32), H_B).reshape(1, 512),
        kng=jnp.tile(k_norm_g.astype(F32), N_KV_B).reshape(1, 128),
        qngt=jnp.tile(q_norm_g.astype(F32).reshape(HD_B, 1), (1, 128)),
        alog_row=lanes(a_log, _SM_G), dtb_row=lanes(dt_bias, _SM_G),
        alog_col=col8(a_log), dtb_col=col8(dt_bias),
        ones_bd=jnp.asarray(ones_bd, BF16), ones_bd128=jnp.asarray(ones_bd[:128, :128], BF16),
        seg_sum=jnp.asarray(seg),
        gg=gdn_norm_g.reshape(1, DV_A).astype(F32),
        woa=w_out_a.astype(BF16), wob=w_out_b.astype(BF16), wob_pad=wob_pad, wo=w_o.astype(BF16),
        g2=norm2_g.reshape(1, d).astype(F32), wup=w_up.astype(BF16), wdn=w_down.astype(BF16),
    )


def _rope_tables(pos):
    half = HD_B // 2
    inv = ROPE_THETA ** (-jnp.arange(half, dtype=F32) / half)
    ang = pos.astype(F32)[:, None] * inv[None, :]
    cos, sin = jnp.cos(ang), jnp.sin(ang)
    return dict(cos=jnp.tile(cos, (1, 4)), sin=jnp.tile(jnp.concatenate([-sin, sin], axis=1), (1, 2)),
                cos_t=cos.T, sin_t=sin.T)


def _layer_prompt(x, w, *, tm=512, tq=256, nb=8):
    b, t, d = x.shape
    tabs = _rope_tables(jnp.arange(t))
    (q, k, v, z, kv, kbf, kibf, sm, kidx, qtp, qit, vt, gt, wit, conv_new) = _proj_prompt(x, w, tabs, tm=tm, kc=tq)
    oa, ssm = _gdn_prompt(q, k, v, z, sm, gt, w["gg"], b=b, t=t, nb=nb)
    ob = _dsa_prompt(qit, wit, qtp, kibf, kbf, vt, b=b, t=t, tq=tq, k_top=min(TOPK_MAX, t // 4))
    h = _merge(x.reshape(b * t, d), oa.reshape(b * t, 512), ob, w["wob"], w, tm=tm)
    y = _mlp(h, w, tm=tm)
    return (y.reshape(b, t, d), kv.reshape(b, t, 2, N_KV_B, HD_B), kidx.reshape(b, t, D_IDX), ssm, conv_new)


def _layer_decode(x, w, state_ssm, state_conv, cache_kv, cache_kidx, page_table, *, npg=32):
    r, d = x.shape
    n_pages = page_table.shape[1]
    past = n_pages * PAGE_SIZE
    tabs = _rope_tables(jnp.full((1,), past))
    st = jnp.swapaxes(state_conv, 0, 1)
    (u, q, k, v, z, qb, kv, qi, sm, sself) = _proj_decode(x, st, w, tabs)
    oa, ssm = _gdn_decode(q, k, v, z, sm, w["gg"], state_ssm)
    wi_col = sm[:, _SM_WI:_SM_WI + H_IDX].reshape(r, H_IDX, 1)
    kidx_pages = jnp.swapaxes(cache_kidx, 1, 2)
    kv_pages = jnp.transpose(cache_kv, (0, 2, 3, 4, 1)).reshape(-1, 2 * N_KV_B * HD_B, PAGE_SIZE)
    scores = _idx_decode(page_table, qi, wi_col, kidx_pages, npg=npg).reshape(r, past)
    bias, bself = _select_decode(scores, sself, k_top=min(TOPK_MAX, (past + 1) // 4))
    qh = qb.reshape(r, H_B, HD_B)
    zq = jnp.zeros_like(qh[:, :4])
    qpad = jnp.concatenate([jnp.concatenate([qh[:, :4], zq], axis=-1),
                            jnp.concatenate([zq, qh[:, 4:]], axis=-1)], axis=1)
    ob = _attn_decode(page_table, qpad, bias, kv, bself, kv_pages, npg=npg)
    h = _merge(x, oa, ob.reshape(r, H_B * 128), w["wob_pad"], w, tm=r)
    y = _mlp(h, w, tm=r)
    conv_new = jnp.concatenate([state_conv[:, 1:], u[:, None, :]], axis=1)
    return (y, kv.reshape(r, 1, 2, N_KV_B, HD_B), sm[:, :D_IDX].reshape(r, 1, D_IDX), ssm, conv_new)


def kernel(x_prompt, x_sample, cache_kv, cache_kidx, page_table, state_ssm, state_conv, norm1_g, w_in, conv_w,
           a_log, dt_bias, gdn_norm_g, q_norm_g, k_norm_g, w_out_a, w_out_b, w_o, norm2_g, w_up, w_down):
    assert w_in.shape[0] == 1, "single-layer trunk"
    w = _prep_weights(norm1_g[0], w_in[0], conv_w[0], a_log[0], dt_bias[0], gdn_norm_g[0], q_norm_g[0],
                      k_norm_g[0], w_out_a[0], w_out_b[0], w_o[0], norm2_g[0], w_up[0], w_down[0])
    yp, kv_p, ki_p, ssm_p, cv_p = _layer_prompt(x_prompt, w)
    ys, kv_s, ki_s, ssm_s, cv_s = _layer_decode(x_sample[:, 0], w, state_ssm[0], state_conv[0],
                                                cache_kv[0], cache_kidx[0], page_table)
    return (yp, ys[:, None, :], kv_p[None], ki_p[None], ssm_p[None], cv_p[None],
            kv_s[None], ki_s[None], ssm_s[None], cv_s[None])
```

```python
import functools
import math

import jax
import jax.numpy as jnp
import numpy as np
from jax import lax
from jax.experimental import pallas as pl
from jax.experimental.pallas import tpu as pltpu

F32 = jnp.float32
BF16 = jnp.bfloat16
I32 = jnp.int32

D_MODEL = 1024
H_A, DK_A, DV_A = 4, 128, 128
CONV_W = 4
CONV_CH = H_A * (2 * DK_A + DV_A)
CHUNK = 64
HD_B, H_B, N_KV_B = 64, 8, 2
H_IDX, D_IDX = 8, 64
TOPK_MAX = 256
PAGE_SIZE = 128
ROPE_THETA = 10000.0
D_FF = 4 * D_MODEL
EPS = 1e-6

_OFF = dict(aq=0, ak=512, av=1024, az=1536, ab=2048, aa=2052, bq=2056, bk=2568, bv=2696,
            iq=2824, ik=3336, iw=3400, ga=3408, gb=4432, end=5456)
_SM_BETA, _SM_G, _SM_WI = 64, 68, 72

LOG2_E = math.log2(math.e)
NEG_BIG = -1e30
INT_MIN = -(2 ** 31)
NEG_INF_KEY = (0xFF800000 ^ 0x7FFFFFFF) - 2 ** 32

VMEM_LIMIT = 56 * 1024 * 1024
HIGHEST = lax.Precision.HIGHEST


def _cparams(n_axes):
    return pltpu.CompilerParams(dimension_semantics=("arbitrary",) * n_axes,
                                vmem_limit_bytes=VMEM_LIMIT)


def _const_spec(shape):
    nd = len(shape)
    return pl.BlockSpec(shape, lambda *a: (0,) * nd, pipeline_mode=pl.Buffered(1))


def _mm(a, b):
    return jnp.dot(a.astype(BF16), b.astype(BF16), preferred_element_type=F32)


def _mm_nt(a, b):
    return lax.dot_general(a.astype(BF16), b.astype(BF16), (((1,), (1,)), ((), ())),
                           preferred_element_type=F32)


def _mm_tn(a, b):
    return lax.dot_general(a.astype(BF16), b.astype(BF16), (((0,), (0,)), ((), ())),
                           preferred_element_type=F32)


def _mm_f32(a, b):
    return jnp.dot(a, b, preferred_element_type=F32, precision=HIGHEST)


def _rms_rows(x, g):
    return x * lax.rsqrt(jnp.mean(x * x, axis=-1, keepdims=True) + EPS) * g


def _silu(x):
    return x * jax.nn.sigmoid(x)


def _softplus(x):
    return jnp.maximum(x, 0.0) + jnp.log1p(jnp.exp(-jnp.abs(x)))


def _rope_rows(x, cos, sin_signed):
    n = x.shape[-1]
    reps = n // 128
    if reps > 1:
        cos = jnp.concatenate([cos] * reps, axis=-1)
        sin_signed = jnp.concatenate([sin_signed] * reps, axis=-1)
    lane = lax.broadcasted_iota(I32, x.shape, x.ndim - 1)
    first_half = (lane % HD_B) < (HD_B // 2)
    partner = jnp.where(first_half, pltpu.roll(x, n - HD_B // 2, x.ndim - 1),
                        pltpu.roll(x, HD_B // 2, x.ndim - 1))
    return x * cos + partner * sin_signed


def _rope_cols(x, cos_t, sin_t):
    x1, x2 = x[:32], x[32:]
    return jnp.concatenate([x1 * cos_t - x2 * sin_t, x2 * cos_t + x1 * sin_t], axis=0)


def _seg_rms(x, ones_bd, g):
    sq = x * x
    hi = sq.astype(BF16)
    lo = (sq - hi.astype(F32)).astype(BF16)
    ss = (jnp.dot(hi, ones_bd, preferred_element_type=F32)
          + jnp.dot(lo, ones_bd, preferred_element_type=F32))
    return x * lax.rsqrt(ss * (1.0 / HD_B) + EPS) * g


def _small_block(sm, cos, sin_signed, alog_row, dtb_row):
    lane = lax.broadcasted_iota(I32, sm.shape, 1)
    roped = _rope_rows(sm, cos, sin_signed)
    beta = jax.nn.sigmoid(sm)
    g = -jnp.exp(alog_row) * _softplus(sm + dtb_row)
    wi = sm * (H_IDX ** -0.5)
    out = jnp.where(lane < _SM_BETA, roped,
                    jnp.where(lane < _SM_G, beta, jnp.where(lane < _SM_WI, g, wi)))
    return out


def _proj_prompt_body(x_ref, g1_ref, wqkv_ref, wz_ref, wkv_ref, wsm_ref,
                      wbqt_ref, wiqt_ref, wvt_ref, wsmt_ref, convw_ref, kng_ref, qngt_ref, alog_ref, dtb_ref,
                      alogt_ref, dtbt_ref, ones_ref, cos_ref, sin_ref, cost_ref, sint_ref,
                      q_out, k_out, v_out, z_out, kv_out, kbf_out, kibf_out, sm_out, kidx_out,
                      qtp_out, qit_out, vt_out, gt_out, wit_out, conv_out, ubuf, *, tm, kc):
    @pl.when(pl.program_id(1) == 0)
    def _():
        ubuf[0:8, :] = jnp.zeros((8, CONV_CH), F32)

    x = x_ref[0]
    xn = _rms_rows(x, g1_ref[...]).astype(BF16)

    nt_dims = (((1,), (1,)), ((), ()))
    u = jnp.dot(xn, wqkv_ref[...], preferred_element_type=F32)
    z = jnp.dot(xn, wz_ref[...], preferred_element_type=F32)
    kv = jnp.dot(xn, wkv_ref[...], preferred_element_type=F32)
    sm = jnp.dot(xn, wsm_ref[...], preferred_element_type=F32)
    qt = lax.dot_general(wbqt_ref[...], xn, nt_dims, preferred_element_type=F32)
    qit = lax.dot_general(wiqt_ref[...], xn, nt_dims, preferred_element_type=F32)
    vt = lax.dot_general(wvt_ref[...], xn, nt_dims, preferred_element_type=F32)
    smt = lax.dot_general(wsmt_ref[...], xn, nt_dims, preferred_element_type=F32)

    ubuf[8:tm + 8, :] = u
    cw = convw_ref[...]
    acc = (ubuf[5:5 + tm, :] * cw[0:1] + ubuf[6:6 + tm, :] * cw[1:2]
           + ubuf[7:7 + tm, :] * cw[2:3] + u * cw[3:4])
    conv_out[0] = ubuf[tm + 5:tm + 8, :]
    ubuf[0:8, :] = ubuf[tm:tm + 8, :]
    c = _silu(acc)
    for h in range(H_A):
        qh = c[:, h * DK_A:(h + 1) * DK_A]
        kh = c[:, (H_A + h) * DK_A:(H_A + h + 1) * DK_A]
        q_out[:, h * DK_A:(h + 1) * DK_A] = qh * (lax.rsqrt(jnp.sum(qh * qh, -1, keepdims=True) + EPS)
                                                 * (DK_A ** -0.5))
        k_out[:, h * DK_A:(h + 1) * DK_A] = kh * lax.rsqrt(jnp.sum(kh * kh, -1, keepdims=True) + EPS)
    v_out[...] = c[:, 2 * H_A * DK_A:]
    z_out[...] = z

    cos, sin = cos_ref[...], sin_ref[...]
    ones_bd = ones_ref[...]

    kb =_rope_rows(_seg_rms(kv[:, :128], ones_bd, kng_ref[...]), cos, sin)
    kv_out[:, 0:128] = kb
    kv_out[:, 128:256] = kv[:, 128:]
    kbf_out[...] = kb.astype(BF16)

    smo = _small_block(sm, cos, sin, alog_ref[...], dtb_ref[...])
    sm_out[...] = smo
    kidx_out[...] = smo[:, :D_IDX]
    kibf_out[...] = smo[:, :D_IDX].astype(BF16)

    cos_t, sin_t = cost_ref[...], sint_ref[...]
    qg =jnp.concatenate([qngt_ref[...]] * (tm // 128), axis=1)
    zero64 = jnp.zeros((HD_B, tm), BF16)
    for h in range(H_B):
        qh = qt[HD_B * h:HD_B * (h + 1)]
        qh = qh * lax.rsqrt(jnp.mean(qh * qh, axis=0, keepdims=True) + EPS) * qg
        qh = (_rope_cols(qh, cos_t, sin_t) * (HD_B ** -0.5 * LOG2_E)).astype(BF16)
        n = h // (H_B // N_KV_B)
        qtp_out[0, h, HD_B * n:HD_B * (n + 1), :] = qh
        qtp_out[0, h, HD_B * (1 - n):HD_B * (2 - n), :] = zero64
        qit_out[0, h] = (_rope_cols(qit[D_IDX * h:D_IDX * (h + 1)], cos_t, sin_t) * (D_IDX ** -0.5)).astype(BF16)
    vtb = vt.astype(BF16)
    for j in range(tm // kc):
        vt_out[0, j] = vtb[:, j * kc:(j + 1) * kc]
    sm8 = smt[64:72]
    row = lax.broadcasted_iota(I32, sm8.shape, 0)
    g8 = -jnp.exp(alogt_ref[...]) * _softplus(sm8 + dtbt_ref[...])
    gt_out[0] = jnp.where(row < 4, jax.nn.sigmoid(sm8), g8)
    wit_out[0] = smt[72:80] * (H_IDX ** -0.5)


def _proj_prompt(x, w, tabs, *, tm, kc):
    b, t, d = x.shape
    nt = t // tm
    m = b * t
    row = lambda n: pl.BlockSpec((tm, n), lambda i, j: (i * nt + j, 0))
    ins = [
        pl.BlockSpec((1, tm, d), lambda i, j: (i, j, 0)),
        _const_spec((1, d)),
        _const_spec((d, CONV_CH)), _const_spec((d, 512)), _const_spec((d, 256)), _const_spec((d, 128)),
        _const_spec((512, d)), _const_spec((512, d)), _const_spec((128, d)), _const_spec((128, d)),
        _const_spec((CONV_W, CONV_CH)), _const_spec((1, 128)), _const_spec((HD_B, 128)),
        _const_spec((1, 128)), _const_spec((1, 128)), _const_spec((8, 1)), _const_spec((8, 1)),
        _const_spec((128, 128)),
        pl.BlockSpec((tm, 128), lambda i, j: (j, 0)), pl.BlockSpec((tm, 128), lambda i, j: (j, 0)),
        pl.BlockSpec((32, tm), lambda i, j: (0, j)), pl.BlockSpec((32, tm), lambda i, j: (0, j)),
    ]
    nkc = t // kc
    outs = [
        (jax.ShapeDtypeStruct((m, 512), F32), row(512)),
        (jax.ShapeDtypeStruct((m, 512), F32), row(512)),
        (jax.ShapeDtypeStruct((m, 512), F32), row(512)),
        (jax.ShapeDtypeStruct((m, 512), F32), row(512)),
        (jax.ShapeDtypeStruct((m, 256), F32), row(256)),
        (jax.ShapeDtypeStruct((m, 128), BF16), row(128)),
        (jax.ShapeDtypeStruct((m, D_IDX), BF16), row(D_IDX)),
        (jax.ShapeDtypeStruct((m, 128), F32), row(128)),
        (jax.ShapeDtypeStruct((m, D_IDX), F32), row(D_IDX)),
        (jax.ShapeDtypeStruct((b, H_B, 128, t), BF16),
         pl.BlockSpec((1, H_B, 128, tm), lambda i, j: (i, 0, 0, j))),
        (jax.ShapeDtypeStruct((b, H_IDX, D_IDX, t), BF16),
         pl.BlockSpec((1, H_IDX, D_IDX, tm), lambda i, j: (i, 0, 0, j))),
        (jax.ShapeDtypeStruct((b, nkc, 128, kc), BF16),
         pl.BlockSpec((1, tm // kc, 128, kc), lambda i, j: (i, j, 0, 0))),
        (jax.ShapeDtypeStruct((b, 8, t), F32), pl.BlockSpec((1, 8, tm), lambda i, j: (i, 0, j))),
        (jax.ShapeDtypeStruct((b, H_IDX, t), F32), pl.BlockSpec((1, H_IDX, tm), lambda i, j: (i, 0, j))),
        (jax.ShapeDtypeStruct((b, CONV_W - 1, CONV_CH), F32),
         pl.BlockSpec((1, CONV_W - 1, CONV_CH), lambda i, j: (i, 0, 0))),
    ]
    return pl.pallas_call(
        functools.partial(_proj_prompt_body, tm=tm, kc=kc),
        out_shape=[o[0] for o in outs],
        grid=(b, nt),
        in_specs=ins,
        out_specs=[o[1] for o in outs],
        scratch_shapes=[pltpu.VMEM((tm + 8, CONV_CH), F32)],
        compiler_params=_cparams(2),
        name="proj_prompt",
    )(x, w["g1"], w["wqkv"], w["wz"], w["wkv"], w["wsm"], w["wbqt"], w["wiqt"], w["wvt"], w["wsmt"],
      w["convw"], w["kng"], w["qngt"], w["alog_row"], w["dtb_row"], w["alog_col"], w["dtb_col"],
      w["ones_bd128"], tabs["cos"], tabs["sin"], tabs["cos_t"], tabs["sin_t"])


def _split3(x):
    hi = x.astype(BF16)
    r1 = x - hi.astype(F32)
    mid = r1.astype(BF16)
    lo = (r1 - mid.astype(F32)).astype(BF16)
    return hi, mid, lo


def _gdn_body(q_ref, k_ref, v_ref, z_ref, sm_ref, gt_ref, gg_ref, bdk_ref, bdp_ref, o_ref, s_ref, *, nb):
    c = pl.program_id(1)

    @pl.when(c == 0)
    def _():
        s_ref[...] = jnp.zeros_like(s_ref)

    n, nh = CHUNK, H_A
    w4 = nh * n
    ri4 = lax.broadcasted_iota(I32, (n, w4), 0)
    ci4 = lax.broadcasted_iota(I32, (n, w4), 1) % n
    strict4, incl4 = ri4 > ci4, ri4 >= ci4
    eye4 = jnp.where(ri4 == ci4, 1.0, 0.0)
    triu4 = jnp.where(ri4 <= ci4, 1.0, 0.0).astype(BF16)
    ri = lax.broadcasted_iota(I32, (n, n), 0)
    tril = jnp.where(ri >= lax.broadcasted_iota(I32, (n, n), 1), 1.0, 0.0).astype(BF16)
    blk_row = lax.broadcasted_iota(I32, (1, w4), 1) // n
    low_half = lax.broadcasted_iota(I32, (n, 128), 1) < n
    zeros_pad = jnp.zeros((n, 2 * DV_A), BF16)
    gg = gg_ref[...]
    dot = functools.partial(jnp.dot, preferred_element_type=F32)

    def pad_rows(x, e):
        z = zeros_pad[:, :x.shape[1]]
        return jnp.concatenate([x, z] if e == 0 else [z, x], axis=0)

    seqs = range(nb)
    st = [dict() for _ in seqs]

    for bi in seqs:
        hi, mid, lo = _split3(sm_ref[bi])
        st[bi]["gc_all"] = dot(tril, hi) + dot(tril, mid) + dot(tril, lo)
        hi, mid, lo = _split3(gt_ref[bi, 0])
        st[bi]["grows"] = dot(hi, triu4) + dot(mid, triu4) + dot(lo, triu4)

    for bi in seqs:
        d = st[bi]
        sm, k, grows = sm_ref[bi], k_ref[bi], d["grows"]
        grow = jnp.where(blk_row == 0, grows[4:5], jnp.where(blk_row == 1, grows[5:6],
                         jnp.where(blk_row == 2, grows[6:7], grows[7:8])))
        gcols = [jnp.broadcast_to(d["gc_all"][:, _SM_G + h:_SM_G + h + 1], (n, 128)) for h in range(nh)]
        betas = [jnp.broadcast_to(sm[:, _SM_BETA + h:_SM_BETA + h + 1], (n, 128)) for h in range(nh)]
        d["gc512"] = jnp.concatenate(gcols, axis=1)
        d["beta512"] = jnp.concatenate(betas, axis=1)
        gc256 = jnp.concatenate([jnp.where(low_half, gcols[0], gcols[1]),
                                 jnp.where(low_half, gcols[2], gcols[3])], axis=1)
        d["dec"] = jnp.exp(jnp.minimum(gc256 - grow, 0.0))
        d["kb"] = k * d["beta512"]
        kbd = jnp.concatenate([k.astype(BF16)] * nh, axis=0) * bdk_ref[...]
        d["aq"] = lax.dot_general(jnp.concatenate([d["kb"], q_ref[bi]], axis=0).astype(BF16), kbd,
                                  (((1,), (1,)), ((), ())), preferred_element_type=F32)

    for bi in seqs:
        d = st[bi]
        x = jnp.where(strict4, -d["aq"][:n] * d["dec"], 0.0)
        d["qkm"] = jnp.where(incl4, d["aq"][n:] * d["dec"], 0.0).astype(BF16)
        d["inv"], d["p"] = eye4 + x, x
    for lvl in range(6):
        for bi in seqs:
            d = st[bi]
            p, inv = d["p"], d["inv"]
            wbd = jnp.concatenate([p.astype(BF16)] * nh, axis=0) * bdp_ref[...]
            if lvl == 0:
                d["p"] = dot(p.astype(BF16), wbd)
            elif lvl < 5:
                r = dot(jnp.concatenate([p, inv], axis=0).astype(BF16), wbd)
                d["p"], d["inv"] = r[:n], inv + r[n:]
            else:
                d["inv"] = inv + dot(inv.astype(BF16), wbd)

    heads = [(bi, h) for bi in seqs for h in range(nh)]
    hsl = lambda h: slice(h * DK_A, (h + 1) * DK_A)
    psl = lambda h: slice(128 * (h // 2), 128 * (h // 2 + 1))
    for bi in seqs:
        d = st[bi]
        k, gc512 = k_ref[bi], d["gc512"]
        d["tinv"] = d["inv"].astype(BF16)
        egc = jnp.exp(gc512)
        d["rhs_v"] = (v_ref[bi] * d["beta512"]).astype(BF16)
        d["rhs_k"] = (d["kb"] * egc).astype(BF16)
        g_last = gc512[n - 1:n, :]
        d["q_dec"] = q_ref[bi] * egc
        d["k_dec"] = (k * jnp.exp(g_last - gc512)).astype(BF16)
        d["c_dec"] = jnp.exp(g_last)
    sol, ws, vn = {}, {}, {}
    for bi, h in heads:
        d = st[bi]
        rhs = pad_rows(jnp.concatenate([d["rhs_v"][:, hsl(h)], d["rhs_k"][:, hsl(h)]], axis=1), h % 2)
        sol[bi, h] = dot(d["tinv"][:, psl(h)], rhs)
    for bi, h in heads:
        lhs = jnp.concatenate([sol[bi, h][:, DV_A:], st[bi]["q_dec"][:, hsl(h)]], axis=0)
        ws[bi, h] = dot(lhs.astype(BF16), s_ref[bi, h].astype(BF16))
    for bi, h in heads:
        d = st[bi]
        vn[bi, h] = (sol[bi, h][:, :DV_A] - ws[bi, h][:n]).astype(BF16)
        s_ref[bi, h] = d["c_dec"][:, hsl(h)] * s_ref[bi, h] + lax.dot_general(
            d["k_dec"][:, hsl(h)], vn[bi, h], (((0,), (0,)), ((), ())), preferred_element_type=F32)
    for bi, h in heads:
        o = ws[bi, h][n:] + dot(st[bi]["qkm"][:, psl(h)], pad_rows(vn[bi, h], h % 2))
        o_ref[bi, :, hsl(h)] = (_rms_rows(o, gg) * _silu(z_ref[bi, :, hsl(h)])).astype(BF16)


def _gdn_prompt(q, k, v, z, sm, gt, gg, *, b, t, nb):
    nc = t // CHUNK
    r3 = lambda a, n: a.reshape(b, t, n)
    blk = lambda n: pl.BlockSpec((nb, CHUNK, n), lambda i, j: (i, j, 0))
    w4 = H_A * CHUNK
    bdk = np.kron(np.eye(H_A, dtype=np.float32), np.ones((CHUNK, DK_A), np.float32))
    bdp = np.kron(np.eye(H_A, dtype=np.float32), np.ones((CHUNK, CHUNK), np.float32))
    return pl.pallas_call(
        functools.partial(_gdn_body, nb=nb),
        out_shape=[jax.ShapeDtypeStruct((b, t, 512), BF16),
                   jax.ShapeDtypeStruct((b, H_A, DK_A, DV_A), F32)],
        grid=(b // nb, nc),
        in_specs=[blk(512), blk(512), blk(512), blk(512), blk(128),
                  pl.BlockSpec((nb, 1, 8, CHUNK), lambda i, j: (i, j, 0, 0)),
                  _const_spec((1, DV_A)), _const_spec((w4, H_A * DK_A)), _const_spec((w4, w4))],
        out_specs=[blk(512), pl.BlockSpec((nb, H_A, DK_A, DV_A), lambda i, j: (i, 0, 0, 0))],
        compiler_params=_cparams(2),
        name="gdn_prompt",
    )(r3(q, 512), r3(k, 512), r3(v, 512), r3(z, 512), r3(sm, 128),
      jnp.swapaxes(gt.reshape(b, 8, nc, CHUNK), 1, 2), gg, jnp.asarray(bdk, BF16), jnp.asarray(bdp, BF16))


def _ordinal_to_float(k):
    k = jnp.maximum(k, NEG_INF_KEY)
    return pltpu.bitcast(k ^ ((k >> 31) & 0x7FFFFFFF), F32)


def _kth_largest(count_ge, shape, k_top):
    def bit_step(i, o):
        cand = o + jnp.left_shift(jnp.int32(1), 31 - i)
        return jnp.where(count_ge(_ordinal_to_float(cand)) >= k_top, cand, o)

    o = lax.fori_loop(0, 32, bit_step, jnp.full(shape, INT_MIN, I32))
    return _ordinal_to_float(o), o


def _kth_largest_two_stage(count_ge, count_ge_rounded, shape, k_top):
    def bf16_ordinal_as_f32_ordinal(h):
        return jnp.where(h >= 0, h << 16, (h << 16) | 0xFFFF)

    def coarse_step(i, h):
        cand = h + jnp.left_shift(jnp.int32(1), 15 - i)
        c = _ordinal_to_float(bf16_ordinal_as_f32_ordinal(cand)).astype(BF16)
        return jnp.where(count_ge_rounded(c) >= k_top, cand, h)

    h = lax.fori_loop(0, 16, coarse_step, jnp.full(shape, -(2 ** 15), I32))
    o = bf16_ordinal_as_f32_ordinal(h)
    o = jnp.where(count_ge(_ordinal_to_float(o)) >= k_top, o, bf16_ordinal_as_f32_ordinal(h - 1))

    def fine_step(i, o):
        cand = o + jnp.left_shift(jnp.int32(1), 15 - i)
        return jnp.where(count_ge(_ordinal_to_float(cand)) >= k_top, cand, o)

    o = lax.fori_loop(0, 16, fine_step, o)
    return _ordinal_to_float(o), o


def _topk_threshold(count_ge, count_gt_eq, count_eq_below, shape, k_top, idx_bits):
    t, o = _kth_largest(count_ge, shape, k_top)
    c_gt, c_eq = count_gt_eq(t)
    need = k_top - c_gt
    untied = jnp.logical_or(c_eq == need, o <= NEG_INF_KEY)
    all_untied = jnp.min(jnp.where(untied, 1.0, 0.0)) > 0.5

    def tie_search(_):
        def idx_step(i, x):
            cand = x + jnp.left_shift(jnp.int32(1), idx_bits - 1 - i)
            return jnp.where(count_eq_below(t, cand) < need, cand, x)
        return lax.fori_loop(0, idx_bits, idx_step, jnp.zeros(shape, I32))

    x = lax.cond(all_untied, lambda _: jnp.full(shape, 2 ** 30, I32), tie_search, 0)
    return t, x


_ACC_ROWS = HD_B + 16
_LOGITS_AHEAD = 5


def _dsa_body(qit_ref, wit_ref, qtp_ref, ki_ref, k_ref, vt_ref, o_ref,
              sc_ref, sb_ref, m_ref, acc_ref, *, tq, kc, k_top):
    i = pl.program_id(1)
    nkc = i + 1
    rowi = lax.broadcasted_iota(I32, (kc, tq), 0)
    coli = lax.broadcasted_iota(I32, (kc, tq), 1)
    ng = kc // 32

    wit = wit_ref[0]

    def score_chunks(chunks):
        for c in chunks:
            kic = ki_ref[0, c]
            score = jnp.zeros((kc, tq), F32)
            for h in range(H_IDX):
                s = jnp.dot(kic, qit_ref[0, h], preferred_element_type=F32)
                score = score + jnp.maximum(s, 0.0) * wit[h:h + 1, :]
            causal = rowi + (c - i) * kc <= coli
            score = jnp.where(causal, score, -jnp.inf)
            sc_ref[c] = score
            sb_ref[c] = score.astype(BF16)

    def score_pair(p, carry):
        score_chunks([2 * p, 2 * p + 1])
        return carry

    lax.fori_loop(0, nkc // 2, score_pair, 0)

    @pl.when(nkc % 2 == 1)
    def _():
        score_chunks([nkc - 1])

    def key_sum(f):
        def body(c, acc):
            ind = f(sc_ref[c], c * kc)
            return acc + jnp.sum(ind.reshape(ng, 32, tq), axis=0)
        acc = lax.fori_loop(0, nkc, body, jnp.zeros((32, tq), F32))
        return jnp.sum(acc, axis=0, keepdims=True).astype(I32)

    def count_ge_rounded(cand):
        one, zero = jnp.ones((), BF16), jnp.zeros((), BF16)

        def body(c, acc):
            ind = jnp.where(sb_ref[c] >= cand, one, zero)
            for g in range(ng):
                acc = acc + ind[32 * g:32 * (g + 1)]
            return acc
        acc = lax.fori_loop(0, nkc, body, jnp.zeros((32, tq), BF16))
        return jnp.sum(acc.astype(F32), axis=0, keepdims=True).astype(I32)

    t, _ = _kth_largest_two_stage(lambda cand: key_sum(lambda s, off: jnp.where(s >= cand, 1.0, 0.0)),
                                  count_ge_rounded, (1, tq), k_top)
    ties_wanted = (k_top - key_sum(lambda s, off: jnp.where(s > t, 1.0, 0.0))).astype(F32)

    m_ref[...] = jnp.full(m_ref.shape, NEG_BIG, F32)
    acc_ref[...] = jnp.zeros(acc_ref.shape, F32)
    ones_rows = jnp.ones((_ACC_ROWS - HD_B, kc), BF16)
    prefix = jnp.where(lax.broadcasted_iota(I32, (kc, kc), 0) >= lax.broadcasted_iota(I32, (kc, kc), 1),
                       1.0, 0.0).astype(BF16)

    def attn_chunks(chunks, ties):
        kks, biases, vaugs = [], [], []
        for c in chunks:
            sc = sc_ref[c]
            tied = jnp.where(sc == t, 1.0, 0.0).astype(BF16)
            rank = ties + jnp.dot(prefix, tied, preferred_element_type=F32)
            tie = jnp.where(sc == t, jnp.where(rank <= ties_wanted, 0.0, NEG_BIG), NEG_BIG)
            biases.append(jnp.where(sc > -jnp.inf, jnp.where(sc > t, 0.0, tie), NEG_BIG))
            ties = rank[kc - 1:kc, :]
            kks.append(k_ref[0, c])
            vt = vt_ref[0, c]
            vaugs.append([jnp.concatenate([vt[HD_B * n:HD_B * (n + 1)], ones_rows], axis=0)
                          for n in range(N_KV_B)])
        units = [(j, h) for j in range(len(chunks)) for h in range(H_B)]
        logits = {}

        def issue(u):
            j, h = units[u]
            logits[u] = jnp.dot(kks[j], qtp_ref[0, h], preferred_element_type=F32)

        for u in range(_LOGITS_AHEAD):
            issue(u)
        for u, (j, h) in enumerate(units):
            if u + _LOGITS_AHEAD < len(units):
                issue(u + _LOGITS_AHEAD)
            n = h // (H_B // N_KV_B)
            s = biases[j] + logits.pop(u)
            m_old = m_ref[h:h + 1, :]
            m_new = jnp.maximum(m_old, jnp.max(s, axis=0, keepdims=True))
            alpha = jnp.exp2(m_old - m_new)
            pexp = jnp.exp2(s - m_new).astype(BF16)
            acc_ref[h] = acc_ref[h] * alpha + jnp.dot(vaugs[j][n], pexp, preferred_element_type=F32)
            m_ref[h:h + 1, :] = m_new
        return ties

    ties = lax.fori_loop(0, nkc // 2, lambda p, tb: attn_chunks([2 * p, 2 * p + 1], tb),
                         jnp.zeros((1, tq), F32))

    @pl.when(nkc % 2 == 1)
    def _():
        attn_chunks([nkc - 1], ties)

    for p in range(H_B // 2):
        halves = []
        for h in (2 * p, 2 * p + 1):
            acc = acc_ref[h]
            halves.append(acc[:HD_B] / acc[HD_B:HD_B + 1])
        o_ref[:, 128 * p:128 * (p + 1)] = jnp.concatenate(halves, axis=0).T.astype(BF16)


def _dsa_prompt(qit, wit, qtp, kibf, kbf, vt, *, b, t, tq, k_top):
    kc = tq
    nq = t // tq
    nkc = t // kc
    whole = lambda *s: pl.BlockSpec((1,) + s, lambda i, j: (i,) + (0,) * len(s))
    return pl.pallas_call(
        functools.partial(_dsa_body, tq=tq, kc=kc, k_top=k_top),
        out_shape=jax.ShapeDtypeStruct((b * t, H_B * HD_B), BF16),
        grid=(b, nq),
        in_specs=[pl.BlockSpec((1, H_IDX, D_IDX, tq), lambda i, j: (i, 0, 0, j)),
                  pl.BlockSpec((1, H_IDX, tq), lambda i, j: (i, 0, j)),
                  pl.BlockSpec((1, H_B, 128, tq), lambda i, j: (i, 0, 0, j)),
                  whole(nkc, kc, D_IDX), whole(nkc, kc, 128), whole(nkc, 128, kc)],
        out_specs=pl.BlockSpec((tq, H_B * HD_B), lambda i, j: (i * nq + j, 0)),
        scratch_shapes=[pltpu.VMEM((nkc, kc, tq), F32), pltpu.VMEM((nkc, kc, tq), BF16),
                        pltpu.VMEM((H_B, tq), F32), pltpu.VMEM((H_B, _ACC_ROWS, tq), F32)],
        compiler_params=_cparams(2),
        name="dsa_prompt",
    )(qit, wit, qtp, kibf.reshape(b, nkc, kc, D_IDX), kbf.reshape(b, nkc, kc, 128), vt)


def _merge_body(x_ref, oa_ref, ob_ref, g1_ref, wg_ref, woa_ref, wob_ref, wo_ref, h_ref):
    x = x_ref[...]
    xn = _rms_rows(x, g1_ref[...]).astype(BF16)
    gates = jax.nn.sigmoid(jnp.dot(xn, wg_ref[...], preferred_element_type=F32))
    a = jnp.dot(oa_ref[...], woa_ref[...], preferred_element_type=F32)
    bb = jnp.dot(ob_ref[...], wob_ref[...], preferred_element_type=F32)
    mix = gates[:, :D_MODEL] * a + gates[:, D_MODEL:] * bb
    h_ref[...] = x + jnp.dot(mix.astype(BF16), wo_ref[...], preferred_element_type=F32)


def _merge(x, oa, ob, wob, w, *, tm):
    m, d = x.shape
    nb = ob.shape[1]
    row = lambda n: pl.BlockSpec((tm, n), lambda i: (i, 0))
    return pl.pallas_call(
        _merge_body,
        out_shape=jax.ShapeDtypeStruct((m, d), F32),
        grid=(m // tm,),
        in_specs=[row(d), row(512), row(nb), _const_spec((1, d)), _const_spec((d, 2 * d)),
                  _const_spec((512, d)), _const_spec((nb, d)), _const_spec((d, d))],
        out_specs=row(d),
        compiler_params=_cparams(1),
        name="merge",
    )(x, oa, ob, w["g1"], w["wg"], w["woa"], wob, w["wo"])


def _mlp_body(h_ref, g2_ref, wup_ref, wdn_ref, y_ref):
    h = h_ref[...]
    hn = _rms_rows(h, g2_ref[...]).astype(BF16)
    up = jnp.dot(hn, wup_ref[...], preferred_element_type=F32)
    act = jnp.square(jnp.maximum(up, 0.0)).astype(BF16)
    y_ref[...] = h + jnp.dot(act, wdn_ref[...], preferred_element_type=F32)


def _mlp(h, w, *, tm):
    m, d = h.shape
    row = pl.BlockSpec((tm, d), lambda i: (i, 0))
    return pl.pallas_call(
        _mlp_body,
        out_shape=jax.ShapeDtypeStruct((m, d), F32),
        grid=(m // tm,),
        in_specs=[row, _const_spec((1, d)), _const_spec((d, D_FF)), _const_spec((D_FF, d))],
        out_specs=row,
        compiler_params=_cparams(1),
        name="mlp",
    )(h, w["g2"], w["wup"], w["wdn"])


def _proj_decode_body(x_ref, st_ref, g1_ref, wqkv_ref, wz_ref, wbq_ref, wkv_ref, wiq_ref, wsm_ref,
                      convw_ref, qng_ref, kng_ref, alog_ref, dtb_ref, ones_ref, seg_ref, cos_ref, sin_ref,
                      u_out, q_out, k_out, v_out, z_out, qb_out, kv_out, qi_out, sm_out, ss_out):
    x = x_ref[...]
    xn = _rms_rows(x, g1_ref[...]).astype(BF16)
    u = jnp.dot(xn, wqkv_ref[...], preferred_element_type=F32)
    u_out[...] = u
    cw = convw_ref[...]
    acc = st_ref[0] * cw[0:1] + st_ref[1] * cw[1:2] + st_ref[2] * cw[2:3] + u * cw[3:4]
    c = _silu(acc)
    for h in range(H_A):
        qh = c[:, h * DK_A:(h + 1) * DK_A]
        kh = c[:, (H_A + h) * DK_A:(H_A + h + 1) * DK_A]
        q_out[:, h * DK_A:(h + 1) * DK_A] = qh * (lax.rsqrt(jnp.sum(qh * qh, -1, keepdims=True) + EPS)
                                                 * (DK_A ** -0.5))
        k_out[:, h * DK_A:(h + 1) * DK_A] = kh * lax.rsqrt(jnp.sum(kh * kh, -1, keepdims=True) + EPS)
    v_out[...] = c[:, 2 * H_A * DK_A:]
    z_out[...] = jnp.dot(xn, wz_ref[...], preferred_element_type=F32)
    cos, sin = cos_ref[...], sin_ref[...]
    ones_bd = ones_ref[...]
    bq = jnp.dot(xn, wbq_ref[...], preferred_element_type=F32)
    qb = _rope_rows(_seg_rms(bq, ones_bd, qng_ref[...]), cos, sin)
    qb_out[...] = (qb * (HD_B ** -0.5)).astype(BF16)
    kv = jnp.dot(xn, wkv_ref[...], preferred_element_type=F32)
    kb = _rope_rows(_seg_rms(kv[:, :128], ones_bd[:128, :128], kng_ref[...]), cos, sin)
    kv_out[:, 0:128] = kb
    kv_out[:, 128:256] = kv[:, 128:]
    iq = jnp.dot(xn, wiq_ref[...], preferred_element_type=F32)
    qi = (_rope_rows(iq, cos, sin) * (D_IDX ** -0.5)).astype(BF16)
    qi_out[...] = qi
    sm = jnp.dot(xn, wsm_ref[...], preferred_element_type=F32)
    smo = _small_block(sm, cos, sin, alog_ref[...], dtb_ref[...])
    sm_out[...] = smo
    lane = lax.broadcasted_iota(I32, smo.shape, 1)
    ki = smo.astype(BF16).astype(F32)
    ki2 = jnp.where(lane < D_IDX, ki, pltpu.roll(ki, D_IDX, 1))
    prod = qi.astype(F32) * jnp.concatenate([ki2] * 4, axis=-1)
    sh = _mm_f32(prod, seg_ref[...])
    contrib = jnp.where(jnp.logical_and(lane >= _SM_WI, lane < _SM_WI + H_IDX),
                        jnp.maximum(sh, 0.0) * smo, 0.0)
    ss_out[...] = jnp.broadcast_to(jnp.sum(contrib, axis=-1, keepdims=True), ss_out.shape)


def _proj_decode(x, st, w, tabs):
    r, d = x.shape
    outs = [(r, CONV_CH, F32), (r, 512, F32), (r, 512, F32), (r, 512, F32), (r, 512, F32), (r, 512, BF16),
            (r, 256, F32), (r, 512, BF16), (r, 128, F32), (r, 128, F32)]
    args = (x, st, w["g1"], w["wqkv"], w["wz"], w["wbq"], w["wkv"], w["wiq"], w["wsm"], w["convw"],
            w["qng"], w["kng"], w["alog_row"], w["dtb_row"], w["ones_bd"], w["seg_sum"], tabs["cos"], tabs["sin"])
    return pl.pallas_call(
        _proj_decode_body,
        out_shape=[jax.ShapeDtypeStruct((a, n), dt) for a, n, dt in outs],
        compiler_params=pltpu.CompilerParams(vmem_limit_bytes=VMEM_LIMIT),
        name="proj_decode",
    )(*args)


def _gdn_decode_body(qc_ref, kc_ref, v_ref, z_ref, sm_ref, gg_ref, s_ref, o_ref, so_ref):
    sm = sm_ref[0]
    gg = gg_ref[...]
    for h in range(H_A):
        g = sm[:, _SM_G + h:_SM_G + h + 1]
        beta = sm[:, _SM_BETA + h:_SM_BETA + h + 1]
        kcol, qcol = kc_ref[0, h], qc_ref[0, h]
        v = v_ref[0, h]
        s = s_ref[0, h] * jnp.exp(g)
        vn = beta * (v - jnp.sum(s * kcol, axis=0, keepdims=True))
        s = s + kcol * vn
        so_ref[0, h] = s
        o = jnp.sum(s * qcol, axis=0, keepdims=True)
        o_ref[0, h] = (_rms_rows(o, gg) * _silu(z_ref[0, h])).astype(BF16)


def _gdn_decode(q, k, v, z, sm, gg, state):
    r = q.shape[0]
    col = lambda a: a.reshape(r, H_A, DK_A, 1)
    rowv = lambda a: a.reshape(r, H_A, 1, DV_A)
    cspec = pl.BlockSpec((1, H_A, DK_A, 1), lambda i: (i, 0, 0, 0))
    rspec = pl.BlockSpec((1, H_A, 1, DV_A), lambda i: (i, 0, 0, 0))
    sspec = pl.BlockSpec((1, H_A, DK_A, DV_A), lambda i: (i, 0, 0, 0))
    o, s_new = pl.pallas_call(
        _gdn_decode_body,
        out_shape=[jax.ShapeDtypeStruct((r, H_A, 1, DV_A), BF16),
                   jax.ShapeDtypeStruct((r, H_A, DK_A, DV_A), F32)],
        grid=(r,),
        in_specs=[cspec, cspec, rspec, rspec, pl.BlockSpec((1, 1, 128), lambda i: (i, 0, 0)),
                  _const_spec((1, DV_A)), sspec],
        out_specs=[rspec, sspec],
        compiler_params=_cparams(1),
        name="gdn_decode",
    )(col(q), col(k), rowv(v), rowv(z), sm.reshape(r, 1, 128), gg, state)
    return o.reshape(r, H_A * DV_A), s_new


def _idx_decode_body(pt_ref, qi_ref, w_ref, *refs, npg):
    pages, out_ref = refs[:npg], refs[npg]
    qi = qi_ref[0]
    w = w_ref[0]
    for p in range(npg):
        s = _mm(qi, pages[p][0])
        out_ref[0, :, PAGE_SIZE * p:PAGE_SIZE * (p + 1)] = jnp.sum(jnp.maximum(s, 0.0) * w, axis=0,
                                                                   keepdims=True)


def _idx_decode(page_table, qi, wi_col, cache_kidx, *, npg):
    r, n_pages = page_table.shape
    page_spec = lambda p: pl.BlockSpec((1, D_IDX, PAGE_SIZE), lambda i, j, pt: (pt[i, j * npg + p], 0, 0))
    grid_spec = pltpu.PrefetchScalarGridSpec(
        num_scalar_prefetch=1,
        grid=(r, n_pages // npg),
        in_specs=[pl.BlockSpec((1, H_IDX, D_IDX), lambda i, j, pt: (i, 0, 0)),
                  pl.BlockSpec((1, H_IDX, 1), lambda i, j, pt: (i, 0, 0))]
                 + [page_spec(p) for p in range(npg)],
        out_specs=pl.BlockSpec((1, 1, PAGE_SIZE * npg), lambda i, j, pt: (i, 0, j)),
    )
    return pl.pallas_call(
        functools.partial(_idx_decode_body, npg=npg),
        out_shape=jax.ShapeDtypeStruct((r, 1, n_pages * PAGE_SIZE), F32),
        grid_spec=grid_spec,
        compiler_params=_cparams(2),
        name="idx_decode",
    )(page_table, qi.reshape(r, H_IDX, D_IDX), wi_col, *([cache_kidx] * npg))


def _select_decode_body(sc_ref, ss_ref, bias_ref, bself_ref, *, k_top, cw):
    r, n = sc_ref.shape
    nch = n // cw
    key_ref = sc_ref
    skey = ss_ref[:, 0:1]

    def lane_sum(f):
        def body(c, acc):
            off = pl.multiple_of(c * cw, cw)
            kk = key_ref[:, pl.ds(off, cw)]
            for j in range(cw // 128):
                acc = acc + f(kk[:, 128 * j:128 * (j + 1)], off + 128 * j)
            return acc
        acc = lax.fori_loop(0, nch, body, jnp.zeros((r, 128), F32))
        return jnp.sum(acc, axis=-1, keepdims=True).astype(I32)

    def count_ge(cand):
        cb = jnp.broadcast_to(cand, (r, 128))
        return lane_sum(lambda kk, off: jnp.where(kk >= cb, 1.0, 0.0)) + (skey >= cand).astype(I32)

    def count_gt_eq(t):
        tb = jnp.broadcast_to(t, (r, 128))
        return (lane_sum(lambda kk, off: jnp.where(kk > tb, 1.0, 0.0)) + (skey > t).astype(I32),
                lane_sum(lambda kk, off: jnp.where(kk == tb, 1.0, 0.0)) + (skey == t).astype(I32))

    def count_eq_below(t, j):
        tb = jnp.broadcast_to(t, (r, 128))
        jb = jnp.broadcast_to(j, (r, 128))
        lane = lax.broadcasted_iota(I32, (r, 128), 1)
        return (lane_sum(lambda kk, off: jnp.where(kk == tb, jnp.where(lane + off < jb, 1.0, 0.0), 0.0))
                + jnp.logical_and(skey == t, j > n).astype(I32))

    t, x = _topk_threshold(count_ge, count_gt_eq, count_eq_below, (r, 1), k_top, 15)
    tb = jnp.broadcast_to(t, (r, cw))
    xb = jnp.broadcast_to(x, (r, cw))

    def bias_chunk(c, carry):
        off = pl.multiple_of(c * cw, cw)
        kk = key_ref[:, pl.ds(off, cw)]
        idx = lax.broadcasted_iota(I32, (r, cw), 1) + off
        tie = jnp.where(kk == tb, jnp.where(idx <= xb, 0.0, NEG_BIG), NEG_BIG)
        bias_ref[:, pl.ds(off, cw)] = jnp.where(kk > tb, 0.0, tie)
        return carry

    lax.fori_loop(0, nch, bias_chunk, 0)
    tie_self = jnp.where(skey == t, jnp.where(x >= n, 0.0, NEG_BIG), NEG_BIG)
    bself_ref[...] = jnp.broadcast_to(jnp.where(skey > t, 0.0, tie_self), bself_ref.shape)


def _select_decode(scores, sself, *, k_top):
    r, n = scores.shape
    return pl.pallas_call(
        functools.partial(_select_decode_body, k_top=k_top, cw=512),
        out_shape=[jax.ShapeDtypeStruct((r, n), F32), jax.ShapeDtypeStruct((r, 128), F32)],
        compiler_params=pltpu.CompilerParams(vmem_limit_bytes=VMEM_LIMIT),
        name="select_decode",
    )(scores, sself)


def _attn_decode_body(pt_ref, qp_ref, bias_ref, kvs_ref, bself_ref, *refs, npg, nsteps):
    pages, o_ref, m_ref, l_ref, acc_ref = refs[:npg], refs[npg], refs[npg + 1], refs[npg + 2], refs[npg + 3]
    j = pl.program_id(1)

    @pl.when(j == 0)
    def _():
        m_ref[...] = jnp.full(m_ref.shape, NEG_BIG, F32)
        l_ref[...] = jnp.zeros(l_ref.shape, F32)
        acc_ref[...] = jnp.zeros(acc_ref.shape, F32)

    qp = qp_ref[0]

    ss = [_mm(qp, pages[p][0, 0:128, :]) + bias_ref[0, :, PAGE_SIZE * p:PAGE_SIZE * (p + 1)]
          for p in range(npg)]
    m_old = m_ref[...]
    m_new = m_old
    for s in ss:
        m_new = jnp.maximum(m_new, jnp.max(s, axis=-1, keepdims=True))
    alpha = jnp.exp(m_old - m_new)
    l_new = l_ref[...] * alpha
    acc_new = acc_ref[...] * alpha
    for p in range(npg):
        pe = jnp.exp(ss[p] - m_new)
        l_new = l_new + jnp.sum(pe, axis=-1, keepdims=True)
        acc_new = acc_new + _mm_nt(pe, pages[p][0, 128:256, :])
    m_ref[...] = m_new
    l_ref[...] = l_new
    acc_ref[...] = acc_new

    @pl.when(j == nsteps - 1)
    def _():
        kvs = kvs_ref[0]
        krow = kvs[:, :128].astype(BF16).astype(F32)
        s = jnp.sum(qp.astype(F32) * krow, axis=-1, keepdims=True) + bself_ref[0][:, 0:1]
        m_old = m_ref[...]
        m_new = jnp.maximum(m_old, s)
        alpha = jnp.exp(m_old - m_new)
        p = jnp.exp(s - m_new)
        l = l_ref[...] * alpha + p
        vrow = kvs[:, 128:].astype(BF16).astype(F32)
        acc = acc_ref[...] * alpha + p.astype(BF16).astype(F32) * vrow
        out = acc / l
        rowi = lax.broadcasted_iota(I32, out.shape, 0)
        out = jnp.where(rowi < 4, out, pltpu.roll(out, HD_B, 1))
        o_ref[0] = out.astype(BF16)


def _attn_decode(page_table, qpad, bias, kv_self, bself, cache_kv, *, npg):
    r, n_pages = page_table.shape
    nsteps = n_pages // npg
    page_spec = lambda p: pl.BlockSpec((1, 256, PAGE_SIZE), lambda i, j, pt: (pt[i, j * npg + p], 0, 0))
    grid_spec = pltpu.PrefetchScalarGridSpec(
        num_scalar_prefetch=1,
        grid=(r, nsteps),
        in_specs=[pl.BlockSpec((1, H_B, 128), lambda i, j, pt: (i, 0, 0)),
                  pl.BlockSpec((1, 1, PAGE_SIZE * npg), lambda i, j, pt: (i, 0, j)),
                  pl.BlockSpec((1, 1, 256), lambda i, j, pt: (i, 0, 0)),
                  pl.BlockSpec((1, 1, 128), lambda i, j, pt: (i, 0, 0))]
                 + [page_spec(p) for p in range(npg)],
        out_specs=pl.BlockSpec((1, H_B, 128), lambda i, j, pt: (i, 0, 0)),
        scratch_shapes=[pltpu.VMEM((H_B, 1), F32), pltpu.VMEM((H_B, 1), F32), pltpu.VMEM((H_B, 128), F32)],
    )
    return pl.pallas_call(
        functools.partial(_attn_decode_body, npg=npg, nsteps=nsteps),
        out_shape=jax.ShapeDtypeStruct((r, H_B, 128), BF16),
        grid_spec=grid_spec,
        compiler_params=_cparams(2),
        name="attn_decode",
    )(page_table, qpad, bias.reshape(r, 1, -1), kv_self.reshape(r, 1, 256), bself.reshape(r, 1, 128),
      *([cache_kv] * npg))


def _prep_weights(norm1_g, w_in, conv_w, a_log, dt_bias, gdn_norm_g, q_norm_g, k_norm_g,
                  w_out_a, w_out_b, w_o, norm2_g, w_up, w_down):
    o = _OFF
    wb = w_in.astype(BF16)
    d = w_in.shape[0]
    wsm = jnp.concatenate([wb[:, o["ik"]:o["iw"]], wb[:, o["ab"]:o["bq"]], wb[:, o["iw"]:o["ga"]],
                           jnp.zeros((d, 128 - 80), BF16)], axis=1)

    def lanes(vals, start):
        return jnp.zeros((1, 128), F32).at[0, start:start + vals.shape[0]].set(vals.astype(F32))

    seg = np.zeros((512, 128), np.float32)
    for h in range(H_IDX):
        seg[64 * h:64 * (h + 1), _SM_WI + h] = 1.0
    ones_bd = np.kron(np.eye(8, dtype=np.float32), np.ones((64, 64), np.float32))
    wob = w_out_b.astype(BF16).reshape(H_B, HD_B, d)
    wob_pad = jnp.concatenate([wob, jnp.zeros_like(wob)], axis=1).reshape(H_B * 128, d)
    col8 = lambda v: jnp.concatenate([jnp.zeros((4,), F32), v.astype(F32)]).reshape(8, 1)
    return dict(
        g1=norm1_g.reshape(1, d).astype(F32),
        wqkv=wb[:, o["aq"]:o["az"]], wz=wb[:, o["az"]:o["ab"]], wbq=wb[:, o["bq"]:o["bk"]],
        wkv=wb[:, o["bk"]:o["iq"]], wiq=wb[:, o["iq"]:o["ik"]], wsm=wsm,
        wbqt=wb[:, o["bq"]:o["bk"]].T, wiqt=wb[:, o["iq"]:o["ik"]].T, wvt=wb[:, o["bv"]:o["iq"]].T,
        wsmt=wsm.T, wg=wb[:, o["ga"]:o["end"]],
        convw=conv_w.astype(F32),
        qng=jnp.tile(q_norm_g.astype(F32), H_B).reshape(1, 512),
        kng=jnp.tile(k_norm_g.astype(F32), N_KV_B).reshape(1, 128),
        qngt=jnp.tile(q_norm_g.astype(F32).reshape(HD_B, 1), (1, 128)),
        alog_row=lanes(a_log, _SM_G), dtb_row=lanes(dt_bias, _SM_G),
        alog_col=col8(a_log), dtb_col=col8(dt_bias),
        ones_bd=jnp.asarray(ones_bd, BF16), ones_bd128=jnp.asarray(ones_bd[:128, :128], BF16),
        seg_sum=jnp.asarray(seg),
        gg=gdn_norm_g.reshape(1, DV_A).astype(F32),
        woa=w_out_a.astype(BF16), wob=w_out_b.astype(BF16), wob_pad=wob_pad, wo=w_o.astype(BF16),
        g2=norm2_g.reshape(1, d).astype(F32), wup=w_up.astype(BF16), wdn=w_down.astype(BF16),
    )


def _rope_tables(pos):
    half = HD_B // 2
    inv = ROPE_THETA ** (-jnp.arange(half, dtype=F32) / half)
    ang = pos.astype(F32)[:, None] * inv[None, :]
    cos, sin = jnp.cos(ang), jnp.sin(ang)
    return dict(cos=jnp.tile(cos, (1, 4)), sin=jnp.tile(jnp.concatenate([-sin, sin], axis=1), (1, 2)),
                cos_t=cos.T, sin_t=sin.T)


def _layer_prompt(x, w, *, tm=512, tq=256, nb=8):
    b, t, d = x.shape
    tabs = _rope_tables(jnp.arange(t))
    (q, k, v, z, kv, kbf, kibf, sm, kidx, qtp, qit, vt, gt, wit, conv_new) = _proj_prompt(x, w, tabs, tm=tm, kc=tq)
    oa, ssm = _gdn_prompt(q, k, v, z, sm, gt, w["gg"], b=b, t=t, nb=nb)
    ob = _dsa_prompt(qit, wit, qtp, kibf, kbf, vt, b=b, t=t, tq=tq, k_top=min(TOPK_MAX, t // 4))
    h = _merge(x.reshape(b * t, d), oa.reshape(b * t, 512), ob, w["wob"], w, tm=tm)
    y = _mlp(h, w, tm=tm)
    return (y.reshape(b, t, d), kv.reshape(b, t, 2, N_KV_B, HD_B), kidx.reshape(b, t, D_IDX), ssm, conv_new)


def _layer_decode(x, w, state_ssm, state_conv, cache_kv, cache_kidx, page_table, *, npg=32):
    r, d = x.shape
    n_pages = page_table.shape[1]
    past = n_pages * PAGE_SIZE
    tabs = _rope_tables(jnp.full((1,), past))
    st = jnp.swapaxes(state_conv, 0, 1)
    (u, q, k, v, z, qb, kv, qi, sm, sself) = _proj_decode(x, st, w, tabs)
    oa, ssm = _gdn_decode(q, k, v, z, sm, w["gg"], state_ssm)
    wi_col = sm[:, _SM_WI:_SM_WI + H_IDX].reshape(r, H_IDX, 1)
    kidx_pages = jnp.swapaxes(cache_kidx, 1, 2)
    kv_pages = jnp.transpose(cache_kv, (0, 2, 3, 4, 1)).reshape(-1, 2 * N_KV_B * HD_B, PAGE_SIZE)
    scores = _idx_decode(page_table, qi, wi_col, kidx_pages, npg=npg).reshape(r, past)
    bias, bself = _select_decode(scores, sself, k_top=min(TOPK_MAX, (past + 1) // 4))
    qh = qb.reshape(r, H_B, HD_B)
    zq = jnp.zeros_like(qh[:, :4])
    qpad = jnp.concatenate([jnp.concatenate([qh[:, :4], zq], axis=-1),
                            jnp.concatenate([zq, qh[:, 4:]], axis=-1)], axis=1)
    ob = _attn_decode(page_table, qpad, bias, kv, bself, kv_pages, npg=npg)
    h = _merge(x, oa, ob.reshape(r, H_B * 128), w["wob_pad"], w, tm=r)
    y = _mlp(h, w, tm=r)
    conv_new = jnp.concatenate([state_conv[:, 1:], u[:, None, :]], axis=1)
    return (y, kv.reshape(r, 1, 2, N_KV_B, HD_B), sm[:, :D_IDX].reshape(r, 1, D_IDX), ssm, conv_new)


def kernel(x_prompt, x_sample, cache_kv, cache_kidx, page_table, state_ssm, state_conv, norm1_g, w_in, conv_w,
           a_log, dt_bias, gdn_norm_g, q_norm_g, k_norm_g, w_out_a, w_out_b, w_o, norm2_g, w_up, w_down):
    assert w_in.shape[0] == 1, "single-layer trunk"
    w = _prep_weights(norm1_g[0], w_in[0], conv_w[0], a_log[0], dt_bias[0], gdn_norm_g[0], q_norm_g[0],
                      k_norm_g[0], w_out_a[0], w_out_b[0], w_o[0], norm2_g[0], w_up[0], w_down[0])
    yp, kv_p, ki_p, ssm_p, cv_p = _layer_prompt(x_prompt, w)
    ys, kv_s, ki_s, ssm_s, cv_s = _layer_decode(x_sample[:, 0], w, state_ssm[0], state_conv[0],
                                                cache_kv[0], cache_kidx[0], page_table)
    return (yp, ys[:, None, :], kv_p[None], ki_p[None], ssm_p[None], cv_p[None],
            kv_s[None], ki_s[None], ssm_s[None], cv_s[None])
```

```python
import functools
import math

import jax
import jax.numpy as jnp
import numpy as np
from jax import lax
from jax.experimental import pallas as pl
from jax.experimental.pallas import tpu as pltpu

F32 = jnp.float32
BF16 = jnp.bfloat16
I32 = jnp.int32

D_MODEL = 1024
H_A, DK_A, DV_A = 4, 128, 128
CONV_W = 4
CONV_CH = H_A * (2 * DK_A + DV_A)
CHUNK = 64
HD_B, H_B, N_KV_B = 64, 8, 2
H_IDX, D_IDX = 8, 64
TOPK_MAX = 256
PAGE_SIZE = 128
ROPE_THETA = 10000.0
D_FF = 4 * D_MODEL
EPS = 1e-6

_OFF = dict(aq=0, ak=512, av=1024, az=1536, ab=2048, aa=2052, bq=2056, bk=2568, bv=2696,
            iq=2824, ik=3336, iw=3400, ga=3408, gb=4432, end=5456)
_SM_BETA, _SM_G, _SM_WI = 64, 68, 72

LOG2_E = math.log2(math.e)
NEG_BIG = -1e30
INT_MIN = -(2 ** 31)
NEG_INF_KEY = (0xFF800000 ^ 0x7FFFFFFF) - 2 ** 32

VMEM_LIMIT = 56 * 1024 * 1024
HIGHEST = lax.Precision.HIGHEST


def _cparams(n_axes):
    return pltpu.CompilerParams(dimension_semantics=("arbitrary",) * n_axes,
                                vmem_limit_bytes=VMEM_LIMIT)


def _const_spec(shape):
    nd = len(shape)
    return pl.BlockSpec(shape, lambda *a: (0,) * nd, pipeline_mode=pl.Buffered(1))


def _mm(a, b):
    return jnp.dot(a.astype(BF16), b.astype(BF16), preferred_element_type=F32)


def _mm_nt(a, b):
    return lax.dot_general(a.astype(BF16), b.astype(BF16), (((1,), (1,)), ((), ())),
                           preferred_element_type=F32)


def _mm_tn(a, b):
    return lax.dot_general(a.astype(BF16), b.astype(BF16), (((0,), (0,)), ((), ())),
                           preferred_element_type=F32)


def _mm_f32(a, b):
    return jnp.dot(a, b, preferred_element_type=F32, precision=HIGHEST)


def _rms_rows(x, g):
    return x * lax.rsqrt(jnp.mean(x * x, axis=-1, keepdims=True) + EPS) * g


def _silu(x):
    return x * jax.nn.sigmoid(x)


def _softplus(x):
    return jnp.maximum(x, 0.0) + jnp.log1p(jnp.exp(-jnp.abs(x)))


def _rope_rows(x, cos, sin_signed):
    n = x.shape[-1]
    reps = n // 128
    if reps > 1:
        cos = jnp.concatenate([cos] * reps, axis=-1)
        sin_signed = jnp.concatenate([sin_signed] * reps, axis=-1)
    lane = lax.broadcasted_iota(I32, x.shape, x.ndim - 1)
    first_half = (lane % HD_B) < (HD_B // 2)
    partner = jnp.where(first_half, pltpu.roll(x, n - HD_B // 2, x.ndim - 1),
                        pltpu.roll(x, HD_B // 2, x.ndim - 1))
    return x * cos + partner * sin_signed


def _rope_cols(x, cos_t, sin_t):
    x1, x2 = x[:32], x[32:]
    return jnp.concatenate([x1 * cos_t - x2 * sin_t, x2 * cos_t + x1 * sin_t], axis=0)


def _seg_rms(x, ones_bd, g):
    sq = x * x
    hi = sq.astype(BF16)
    lo = (sq - hi.astype(F32)).astype(BF16)
    ss = (jnp.dot(hi, ones_bd, preferred_element_type=F32)
          + jnp.dot(lo, ones_bd, preferred_element_type=F32))
    return x * lax.rsqrt(ss * (1.0 / HD_B) + EPS) * g


def _small_block(sm, cos, sin_signed, alog_row, dtb_row):
    lane = lax.broadcasted_iota(I32, sm.shape, 1)
    roped = _rope_rows(sm, cos, sin_signed)
    beta = jax.nn.sigmoid(sm)
    g = -jnp.exp(alog_row) * _softplus(sm + dtb_row)
    wi = sm * (H_IDX ** -0.5)
    out = jnp.where(lane < _SM_BETA, roped,
                    jnp.where(lane < _SM_G, beta, jnp.where(lane < _SM_WI, g, wi)))
    return out


def _proj_prompt_body(x_ref, g1_ref, wqkv_ref, wz_ref, wkv_ref, wsm_ref,
                      wbqt_ref, wiqt_ref, wvt_ref, wsmt_ref, convw_ref, kng_ref, qngt_ref, alog_ref, dtb_ref,
                      alogt_ref, dtbt_ref, ones_ref, cos_ref, sin_ref, cost_ref, sint_ref,
                      q_out, k_out, v_out, z_out, kv_out, kbf_out, kibf_out, sm_out, kidx_out,
                      qtp_out, qit_out, vt_out, gt_out, wit_out, conv_out, ubuf, *, tm, kc):
    @pl.when(pl.program_id(1) == 0)
    def _():
        ubuf[0:8, :] = jnp.zeros((8, CONV_CH), F32)

    x = x_ref[0]
    xn = _rms_rows(x, g1_ref[...]).astype(BF16)

    nt_dims = (((1,), (1,)), ((), ()))
    u = jnp.dot(xn, wqkv_ref[...], preferred_element_type=F32)
    z = jnp.dot(xn, wz_ref[...], preferred_element_type=F32)
    kv = jnp.dot(xn, wkv_ref[...], preferred_element_type=F32)
    sm = jnp.dot(xn, wsm_ref[...], preferred_element_type=F32)
    qt = lax.dot_general(wbqt_ref[...], xn, nt_dims, preferred_element_type=F32)
    qit = lax.dot_general(wiqt_ref[...], xn, nt_dims, preferred_element_type=F32)
    vt = lax.dot_general(wvt_ref[...], xn, nt_dims, preferred_element_type=F32)
    smt = lax.dot_general(wsmt_ref[...], xn, nt_dims, preferred_element_type=F32)

    ubuf[8:tm + 8, :] = u
    cw = convw_ref[...]
    acc = (ubuf[5:5 + tm, :] * cw[0:1] + ubuf[6:6 + tm, :] * cw[1:2]
           + ubuf[7:7 + tm, :] * cw[2:3] + u * cw[3:4])
    conv_out[0] = ubuf[tm + 5:tm + 8, :]
    ubuf[0:8, :] = ubuf[tm:tm + 8, :]
    c = _silu(acc)
    for h in range(H_A):
        qh = c[:, h * DK_A:(h + 1) * DK_A]
        kh = c[:, (H_A + h) * DK_A:(H_A + h + 1) * DK_A]
        q_out[:, h * DK_A:(h + 1) * DK_A] = qh * (lax.rsqrt(jnp.sum(qh * qh, -1, keepdims=True) + EPS)
                                                 * (DK_A ** -0.5))
        k_out[:, h * DK_A:(h + 1) * DK_A] = kh * lax.rsqrt(jnp.sum(kh * kh, -1, keepdims=True) + EPS)
    v_out[...] = c[:, 2 * H_A * DK_A:]
    z_out[...] = z

    cos, sin = cos_ref[...], sin_ref[...]
    ones_bd = ones_ref[...]

    kb =_rope_rows(_seg_rms(kv[:, :128], ones_bd, kng_ref[...]), cos, sin)
    kv_out[:, 0:128] = kb
    kv_out[:, 128:256] = kv[:, 128:]
    kbf_out[...] = kb.astype(BF16)

    smo = _small_block(sm, cos, sin, alog_ref[...], dtb_ref[...])
    sm_out[...] = smo
    kidx_out[...] = smo[:, :D_IDX]
    kibf_out[...] = smo[:, :D_IDX].astype(BF16)

    cos_t, sin_t = cost_ref[...], sint_ref[...]
    qg =jnp.concatenate([qngt_ref[...]] * (tm // 128), axis=1)
    zero64 = jnp.zeros((HD_B, tm), BF16)
    for h in range(H_B):
        qh = qt[HD_B * h:HD_B * (h + 1)]
        qh = qh * lax.rsqrt(jnp.mean(qh * qh, axis=0, keepdims=True) + EPS) * qg
        qh = (_rope_cols(qh, cos_t, sin_t) * (HD_B ** -0.5 * LOG2_E)).astype(BF16)
        n = h // (H_B // N_KV_B)
        qtp_out[0, h, HD_B * n:HD_B * (n + 1), :] = qh
        qtp_out[0, h, HD_B * (1 - n):HD_B * (2 - n), :] = zero64
        qit_out[0, h] = (_rope_cols(qit[D_IDX * h:D_IDX * (h + 1)], cos_t, sin_t) * (D_IDX ** -0.5)).astype(BF16)
    vtb = vt.astype(BF16)
    for j in range(tm // kc):
        vt_out[0, j] = vtb[:, j * kc:(j + 1) * kc]
    sm8 = smt[64:72]
    row = lax.broadcasted_iota(I32, sm8.shape, 0)
    g8 = -jnp.exp(alogt_ref[...]) * _softplus(sm8 + dtbt_ref[...])
    gt_out[0] = jnp.where(row < 4, jax.nn.sigmoid(sm8), g8)
    wit_out[0] = smt[72:80] * (H_IDX ** -0.5)


def _proj_prompt(x, w, tabs, *, tm, kc):
    b, t, d = x.shape
    nt = t // tm
    m = b * t
    row = lambda n: pl.BlockSpec((tm, n), lambda i, j: (i * nt + j, 0))
    ins = [
        pl.BlockSpec((1, tm, d), lambda i, j: (i, j, 0)),
        _const_spec((1, d)),
        _const_spec((d, CONV_CH)), _const_spec((d, 512)), _const_spec((d, 256)), _const_spec((d, 128)),
        _const_spec((512, d)), _const_spec((512, d)), _const_spec((128, d)), _const_spec((128, d)),
        _const_spec((CONV_W, CONV_CH)), _const_spec((1, 128)), _const_spec((HD_B, 128)),
        _const_spec((1, 128)), _const_spec((1, 128)), _const_spec((8, 1)), _const_spec((8, 1)),
        _const_spec((128, 128)),
        pl.BlockSpec((tm, 128), lambda i, j: (j, 0)), pl.BlockSpec((tm, 128), lambda i, j: (j, 0)),
        pl.BlockSpec((32, tm), lambda i, j: (0, j)), pl.BlockSpec((32, tm), lambda i, j: (0, j)),
    ]
    nkc = t // kc
    outs = [
        (jax.ShapeDtypeStruct((m, 512), F32), row(512)),
        (jax.ShapeDtypeStruct((m, 512), F32), row(512)),
        (jax.ShapeDtypeStruct((m, 512), F32), row(512)),
        (jax.ShapeDtypeStruct((m, 512), F32), row(512)),
        (jax.ShapeDtypeStruct((m, 256), F32), row(256)),
        (jax.ShapeDtypeStruct((m, 128), BF16), row(128)),
        (jax.ShapeDtypeStruct((m, D_IDX), BF16), row(D_IDX)),
        (jax.ShapeDtypeStruct((m, 128), F32), row(128)),
        (jax.ShapeDtypeStruct((m, D_IDX), F32), row(D_IDX)),
        (jax.ShapeDtypeStruct((b, H_B, 128, t), BF16),
         pl.BlockSpec((1, H_B, 128, tm), lambda i, j: (i, 0, 0, j))),
        (jax.ShapeDtypeStruct((b, H_IDX, D_IDX, t), BF16),
         pl.BlockSpec((1, H_IDX, D_IDX, tm), lambda i, j: (i, 0, 0, j))),
        (jax.ShapeDtypeStruct((b, nkc, 128, kc), BF16),
         pl.BlockSpec((1, tm // kc, 128, kc), lambda i, j: (i, j, 0, 0))),
        (jax.ShapeDtypeStruct((b, 8, t), F32), pl.BlockSpec((1, 8, tm), lambda i, j: (i, 0, j))),
        (jax.ShapeDtypeStruct((b, H_IDX, t), F32), pl.BlockSpec((1, H_IDX, tm), lambda i, j: (i, 0, j))),
        (jax.ShapeDtypeStruct((b, CONV_W - 1, CONV_CH), F32),
         pl.BlockSpec((1, CONV_W - 1, CONV_CH), lambda i, j: (i, 0, 0))),
    ]
    return pl.pallas_call(
        functools.partial(_proj_prompt_body, tm=tm, kc=kc),
        out_shape=[o[0] for o in outs],
        grid=(b, nt),
        in_specs=ins,
        out_specs=[o[1] for o in outs],
        scratch_shapes=[pltpu.VMEM((tm + 8, CONV_CH), F32)],
        compiler_params=_cparams(2),
        name="proj_prompt",
    )(x, w["g1"], w["wqkv"], w["wz"], w["wkv"], w["wsm"], w["wbqt"], w["wiqt"], w["wvt"], w["wsmt"],
      w["convw"], w["kng"], w["qngt"], w["alog_row"], w["dtb_row"], w["alog_col"], w["dtb_col"],
      w["ones_bd128"], tabs["cos"], tabs["sin"], tabs["cos_t"], tabs["sin_t"])


def _split3(x):
    hi = x.astype(BF16)
    r1 = x - hi.astype(F32)
    mid = r1.astype(BF16)
    lo = (r1 - mid.astype(F32)).astype(BF16)
    return hi, mid, lo


def _gdn_body(q_ref, k_ref, v_ref, z_ref, sm_ref, gt_ref, gg_ref, bdk_ref, bdp_ref, o_ref, s_ref, *, nb):
    c = pl.program_id(1)

    @pl.when(c == 0)
    def _():
        s_ref[...] = jnp.zeros_like(s_ref)

    n, nh = CHUNK, H_A
    w4 = nh * n
    ri4 = lax.broadcasted_iota(I32, (n, w4), 0)
    ci4 = lax.broadcasted_iota(I32, (n, w4), 1) % n
    strict4, incl4 = ri4 > ci4, ri4 >= ci4
    eye4 = jnp.where(ri4 == ci4, 1.0, 0.0)
    triu4 = jnp.where(ri4 <= ci4, 1.0, 0.0).astype(BF16)
    ri = lax.broadcasted_iota(I32, (n, n), 0)
    tril = jnp.where(ri >= lax.broadcasted_iota(I32, (n, n), 1), 1.0, 0.0).astype(BF16)
    blk_row = lax.broadcasted_iota(I32, (1, w4), 1) // n
    low_half = lax.broadcasted_iota(I32, (n, 128), 1) < n
    zeros_pad = jnp.zeros((n, 2 * DV_A), BF16)
    gg = gg_ref[...]
    dot = functools.partial(jnp.dot, preferred_element_type=F32)

    def pad_rows(x, e):
        z = zeros_pad[:, :x.shape[1]]
        return jnp.concatenate([x, z] if e == 0 else [z, x], axis=0)

    seqs = range(nb)
    st = [dict() for _ in seqs]

    for bi in seqs:
        hi, mid, lo = _split3(sm_ref[bi])
        st[bi]["gc_all"] = dot(tril, hi) + dot(tril, mid) + dot(tril, lo)
        hi, mid, lo = _split3(gt_ref[bi, 0])
        st[bi]["grows"] = dot(hi, triu4) + dot(mid, triu4) + dot(lo, triu4)

    for bi in seqs:
        d = st[bi]
        sm, k, grows = sm_ref[bi], k_ref[bi], d["grows"]
        grow = jnp.where(blk_row == 0, grows[4:5], jnp.where(blk_row == 1, grows[5:6],
                         jnp.where(blk_row == 2, grows[6:7], grows[7:8])))
        gcols = [jnp.broadcast_to(d["gc_all"][:, _SM_G + h:_SM_G + h + 1], (n, 128)) for h in range(nh)]
        betas = [jnp.broadcast_to(sm[:, _SM_BETA + h:_SM_BETA + h + 1], (n, 128)) for h in range(nh)]
        d["gc512"] = jnp.concatenate(gcols, axis=1)
        d["beta512"] = jnp.concatenate(betas, axis=1)
        gc256 = jnp.concatenate([jnp.where(low_half, gcols[0], gcols[1]),
                                 jnp.where(low_half, gcols[2], gcols[3])], axis=1)
        d["dec"] = jnp.exp(jnp.minimum(gc256 - grow, 0.0))
        d["kb"] = k * d["beta512"]
        kbd = jnp.concatenate([k.astype(BF16)] * nh, axis=0) * bdk_ref[...]
        d["aq"] = lax.dot_general(jnp.concatenate([d["kb"], q_ref[bi]], axis=0).astype(BF16), kbd,
                                  (((1,), (1,)), ((), ())), preferred_element_type=F32)

    for bi in seqs:
        d = st[bi]
        x = jnp.where(strict4, -d["aq"][:n] * d["dec"], 0.0)
        d["qkm"] = jnp.where(incl4, d["aq"][n:] * d["dec"], 0.0).astype(BF16)
        d["inv"], d["p"] = eye4 + x, x
    for lvl in range(6):
        for bi in seqs:
            d = st[bi]
            p, inv = d["p"], d["inv"]
            wbd = jnp.concatenate([p.astype(BF16)] * nh, axis=0) * bdp_ref[...]
            if lvl == 0:
                d["p"] = dot(p.astype(BF16), wbd)
            elif lvl < 5:
                r = dot(jnp.concatenate([p, inv], axis=0).astype(BF16), wbd)
                d["p"], d["inv"] = r[:n], inv + r[n:]
            else:
                d["inv"] = inv + dot(inv.astype(BF16), wbd)

    heads = [(bi, h) for bi in seqs for h in range(nh)]
    hsl = lambda h: slice(h * DK_A, (h + 1) * DK_A)
    psl = lambda h: slice(128 * (h // 2), 128 * (h // 2 + 1))
    for bi in seqs:
        d = st[bi]
        k, gc512 = k_ref[bi], d["gc512"]
        d["tinv"] = d["inv"].astype(BF16)
        egc = jnp.exp(gc512)
        d["rhs_v"] = (v_ref[bi] * d["beta512"]).astype(BF16)
        d["rhs_k"] = (d["kb"] * egc).astype(BF16)
        g_last = gc512[n - 1:n, :]
        d["q_dec"] = q_ref[bi] * egc
        d["k_dec"] = (k * jnp.exp(g_last - gc512)).astype(BF16)
        d["c_dec"] = jnp.exp(g_last)
    sol, ws, vn = {}, {}, {}
    for bi, h in heads:
        d = st[bi]
        rhs = pad_rows(jnp.concatenate([d["rhs_v"][:, hsl(h)], d["rhs_k"][:, hsl(h)]], axis=1), h % 2)
        sol[bi, h] = dot(d["tinv"][:, psl(h)], rhs)
    for bi, h in heads:
        lhs = jnp.concatenate([sol[bi, h][:, DV_A:], st[bi]["q_dec"][:, hsl(h)]], axis=0)
        ws[bi, h] = dot(lhs.astype(BF16), s_ref[bi, h].astype(BF16))
    for bi, h in heads:
        d = st[bi]
        vn[bi, h] = (sol[bi, h][:, :DV_A] - ws[bi, h][:n]).astype(BF16)
        s_ref[bi, h] = d["c_dec"][:, hsl(h)] * s_ref[bi, h] + lax.dot_general(
            d["k_dec"][:, hsl(h)], vn[bi, h], (((0,), (0,)), ((), ())), preferred_element_type=F32)
    for bi, h in heads:
        o = ws[bi, h][n:] + dot(st[bi]["qkm"][:, psl(h)], pad_rows(vn[bi, h], h % 2))
        o_ref[bi, :, hsl(h)] = (_rms_rows(o, gg) * _silu(z_ref[bi, :, hsl(h)])).astype(BF16)


def _gdn_prompt(q, k, v, z, sm, gt, gg, *, b, t, nb):
    nc = t // CHUNK
    r3 = lambda a, n: a.reshape(b, t, n)
    blk = lambda n: pl.BlockSpec((nb, CHUNK, n), lambda i, j: (i, j, 0))
    w4 = H_A * CHUNK
    bdk = np.kron(np.eye(H_A, dtype=np.float32), np.ones((CHUNK, DK_A), np.float32))
    bdp = np.kron(np.eye(H_A, dtype=np.float32), np.ones((CHUNK, CHUNK), np.float32))
    return pl.pallas_call(
        functools.partial(_gdn_body, nb=nb),
        out_shape=[jax.ShapeDtypeStruct((b, t, 512), BF16),
                   jax.ShapeDtypeStruct((b, H_A, DK_A, DV_A), F32)],
        grid=(b // nb, nc),
        in_specs=[blk(512), blk(512), blk(512), blk(512), blk(128),
                  pl.BlockSpec((nb, 1, 8, CHUNK), lambda i, j: (i, j, 0, 0)),
                  _const_spec((1, DV_A)), _const_spec((w4, H_A * DK_A)), _const_spec((w4, w4))],
        out_specs=[blk(512), pl.BlockSpec((nb, H_A, DK_A, DV_A), lambda i, j: (i, 0, 0, 0))],
        compiler_params=_cparams(2),
        name="gdn_prompt",
    )(r3(q, 512), r3(k, 512), r3(v, 512), r3(z, 512), r3(sm, 128),
      jnp.swapaxes(gt.reshape(b, 8, nc, CHUNK), 1, 2), gg, jnp.asarray(bdk, BF16), jnp.asarray(bdp, BF16))


def _ordinal_to_float(k):
    k = jnp.maximum(k, NEG_INF_KEY)
    return pltpu.bitcast(k ^ ((k >> 31) & 0x7FFFFFFF), F32)


def _kth_largest(count_ge, shape, k_top):
    def bit_step(i, o):
        cand = o + jnp.left_shift(jnp.int32(1), 31 - i)
        return jnp.where(count_ge(_ordinal_to_float(cand)) >= k_top, cand, o)

    o = lax.fori_loop(0, 32, bit_step, jnp.full(shape, INT_MIN, I32))
    return _ordinal_to_float(o), o


def _kth_largest_two_stage(count_ge, count_ge_rounded, shape, k_top):
    def bf16_ordinal_as_f32_ordinal(h):
        return jnp.where(h >= 0, h << 16, (h << 16) | 0xFFFF)

    def coarse_step(i, h):
        cand = h + jnp.left_shift(jnp.int32(1), 15 - i)
        c = _ordinal_to_float(bf16_ordinal_as_f32_ordinal(cand)).astype(BF16)
        return jnp.where(count_ge_rounded(c) >= k_top, cand, h)

    h = lax.fori_loop(0, 16, coarse_step, jnp.full(shape, -(2 ** 15), I32))
    o = bf16_ordinal_as_f32_ordinal(h)
    o = jnp.where(count_ge(_ordinal_to_float(o)) >= k_top, o, bf16_ordinal_as_f32_ordinal(h - 1))

    def fine_step(i, o):
        cand = o + jnp.left_shift(jnp.int32(1), 15 - i)
        return jnp.where(count_ge(_ordinal_to_float(cand)) >= k_top, cand, o)

    o = lax.fori_loop(0, 16, fine_step, o)
    return _ordinal_to_float(o), o


def _topk_threshold(count_ge, count_gt_eq, count_eq_below, shape, k_top, idx_bits):
    t, o = _kth_largest(count_ge, shape, k_top)
    c_gt, c_eq = count_gt_eq(t)
    need = k_top - c_gt
    untied = jnp.logical_or(c_eq == need, o <= NEG_INF_KEY)
    all_untied = jnp.min(jnp.where(untied, 1.0, 0.0)) > 0.5

    def tie_search(_):
        def idx_step(i, x):
            cand = x + jnp.left_shift(jnp.int32(1), idx_bits - 1 - i)
            return jnp.where(count_eq_below(t, cand) < need, cand, x)
        return lax.fori_loop(0, idx_bits, idx_step, jnp.zeros(shape, I32))

    x = lax.cond(all_untied, lambda _: jnp.full(shape, 2 ** 30, I32), tie_search, 0)
    return t, x


_ACC_ROWS = HD_B + 16
_LOGITS_AHEAD = 5


def _dsa_body(qit_ref, wit_ref, qtp_ref, ki_ref, k_ref, vt_ref, o_ref,
              sc_ref, sb_ref, m_ref, acc_ref, *, tq, kc, k_top):
    i = pl.program_id(1)
    nkc = i + 1
    rowi = lax.broadcasted_iota(I32, (kc, tq), 0)
    coli = lax.broadcasted_iota(I32, (kc, tq), 1)
    ng = kc // 32

    wit = wit_ref[0]

    def score_chunks(chunks):
        for c in chunks:
            kic = ki_ref[0, c]
            score = jnp.zeros((kc, tq), F32)
            for h in range(H_IDX):
                s = jnp.dot(kic, qit_ref[0, h], preferred_element_type=F32)
                score = score + jnp.maximum(s, 0.0) * wit[h:h + 1, :]
            causal = rowi + (c - i) * kc <= coli
            score = jnp.where(causal, score, -jnp.inf)
            sc_ref[c] = score
            sb_ref[c] = score.astype(BF16)

    def score_pair(p, carry):
        score_chunks([2 * p, 2 * p + 1])
        return carry

    lax.fori_loop(0, nkc // 2, score_pair, 0)

    @pl.when(nkc % 2 == 1)
    def _():
        score_chunks([nkc - 1])

    def key_sum(f):
        def body(c, acc):
            ind = f(sc_ref[c], c * kc)
            return acc + jnp.sum(ind.reshape(ng, 32, tq), axis=0)
        acc = lax.fori_loop(0, nkc, body, jnp.zeros((32, tq), F32))
        return jnp.sum(acc, axis=0, keepdims=True).astype(I32)

    def count_ge_rounded(cand):
        one, zero = jnp.ones((), BF16), jnp.zeros((), BF16)

        def body(c, acc):
            ind = jnp.where(sb_ref[c] >= cand, one, zero)
            for g in range(ng):
                acc = acc + ind[32 * g:32 * (g + 1)]
            return acc
        acc = lax.fori_loop(0, nkc, body, jnp.zeros((32, tq), BF16))
        return jnp.sum(acc.astype(F32), axis=0, keepdims=True).astype(I32)

    t, _ = _kth_largest_two_stage(lambda cand: key_sum(lambda s, off: jnp.where(s >= cand, 1.0, 0.0)),
                                  count_ge_rounded, (1, tq), k_top)
    ties_wanted = (k_top - key_sum(lambda s, off: jnp.where(s > t, 1.0, 0.0))).astype(F32)

    m_ref[...] = jnp.full(m_ref.shape, NEG_BIG, F32)
    acc_ref[...] = jnp.zeros(acc_ref.shape, F32)
    ones_rows = jnp.ones((_ACC_ROWS - HD_B, kc), BF16)
    prefix = jnp.where(lax.broadcasted_iota(I32, (kc, kc), 0) >= lax.broadcasted_iota(I32, (kc, kc), 1),
                       1.0, 0.0).astype(BF16)

    def attn_chunks(chunks, ties):
        kks, biases, vaugs = [], [], []
        for c in chunks:
            sc = sc_ref[c]
            tied = jnp.where(sc == t, 1.0, 0.0).astype(BF16)
            rank = ties + jnp.dot(prefix, tied, preferred_element_type=F32)
            tie = jnp.where(sc == t, jnp.where(rank <= ties_wanted, 0.0, NEG_BIG), NEG_BIG)
            biases.append(jnp.where(sc > -jnp.inf, jnp.where(sc > t, 0.0, tie), NEG_BIG))
            ties = rank[kc - 1:kc, :]
            kks.append(k_ref[0, c])
            vt = vt_ref[0, c]
            vaugs.append([jnp.concatenate([vt[HD_B * n:HD_B * (n + 1)], ones_rows], axis=0)
                          for n in range(N_KV_B)])
        units = [(j, h) for j in range(len(chunks)) for h in range(H_B)]
        logits = {}

        def issue(u):
            j, h = units[u]
            logits[u] = jnp.dot(kks[j], qtp_ref[0, h], preferred_element_type=F32)

        for u in range(_LOGITS_AHEAD):
            issue(u)
        for u, (j, h) in enumerate(units):
            if u + _LOGITS_AHEAD < len(units):
                issue(u + _LOGITS_AHEAD)
            n = h // (H_B // N_KV_B)
            s = biases[j] + logits.pop(u)
            m_old = m_ref[h:h + 1, :]
            m_new = jnp.maximum(m_old, jnp.max(s, axis=0, keepdims=True))
            alpha = jnp.exp2(m_old - m_new)
            pexp = jnp.exp2(s - m_new).astype(BF16)
            acc_ref[h] = acc_ref[h] * alpha + jnp.dot(vaugs[j][n], pexp, preferred_element_type=F32)
            m_ref[h:h + 1, :] = m_new
        return ties

    ties = lax.fori_loop(0, nkc // 2, lambda p, tb: attn_chunks([2 * p, 2 * p + 1], tb),
                         jnp.zeros((1, tq), F32))

    @pl.when(nkc % 2 == 1)
    def _():
        attn_chunks([nkc - 1], ties)

    for p in range(H_B // 2):
        halves = []
        for h in (2 * p, 2 * p + 1):
            acc = acc_ref[h]
            halves.append(acc[:HD_B] / acc[HD_B:HD_B + 1])
        o_ref[:, 128 * p:128 * (p + 1)] = jnp.concatenate(halves, axis=0).T.astype(BF16)


def _dsa_prompt(qit, wit, qtp, kibf, kbf, vt, *, b, t, tq, k_top):
    kc = tq
    nq = t // tq
    nkc = t // kc
    whole = lambda *s: pl.BlockSpec((1,) + s, lambda i, j: (i,) + (0,) * len(s))
    return pl.pallas_call(
        functools.partial(_dsa_body, tq=tq, kc=kc, k_top=k_top),
        out_shape=jax.ShapeDtypeStruct((b * t, H_B * HD_B), BF16),
        grid=(b, nq),
        in_specs=[pl.BlockSpec((1, H_IDX, D_IDX, tq), lambda i, j: (i, 0, 0, j)),
                  pl.BlockSpec((1, H_IDX, tq), lambda i, j: (i, 0, j)),
                  pl.BlockSpec((1, H_B, 128, tq), lambda i, j: (i, 0, 0, j)),
                  whole(nkc, kc, D_IDX), whole(nkc, kc, 128), whole(nkc, 128, kc)],
        out_specs=pl.BlockSpec((tq, H_B * HD_B), lambda i, j: (i * nq + j, 0)),
        scratch_shapes=[pltpu.VMEM((nkc, kc, tq), F32), pltpu.VMEM((nkc, kc, tq), BF16),
                        pltpu.VMEM((H_B, tq), F32), pltpu.VMEM((H_B, _ACC_ROWS, tq), F32)],
        compiler_params=_cparams(2),
        name="dsa_prompt",
    )(qit, wit, qtp, kibf.reshape(b, nkc, kc, D_IDX), kbf.reshape(b, nkc, kc, 128), vt)


def _merge_body(x_ref, oa_ref, ob_ref, g1_ref, wg_ref, woa_ref, wob_ref, wo_ref, h_ref):
    x = x_ref[...]
    xn = _rms_rows(x, g1_ref[...]).astype(BF16)
    gates = jax.nn.sigmoid(jnp.dot(xn, wg_ref[...], preferred_element_type=F32))
    a = jnp.dot(oa_ref[...], woa_ref[...], preferred_element_type=F32)
    bb = jnp.dot(ob_ref[...], wob_ref[...], preferred_element_type=F32)
    mix = gates[:, :D_MODEL] * a + gates[:, D_MODEL:] * bb
    h_ref[...] = x + jnp.dot(mix.astype(BF16), wo_ref[...], preferred_element_type=F32)


def _merge(x, oa, ob, wob, w, *, tm):
    m, d = x.shape
    nb = ob.shape[1]
    row = lambda n: pl.BlockSpec((tm, n), lambda i: (i, 0))
    return pl.pallas_call(
        _merge_body,
        out_shape=jax.ShapeDtypeStruct((m, d), F32),
        grid=(m // tm,),
        in_specs=[row(d), row(512), row(nb), _const_spec((1, d)), _const_spec((d, 2 * d)),
                  _const_spec((512, d)), _const_spec((nb, d)), _const_spec((d, d))],
        out_specs=row(d),
        compiler_params=_cparams(1),
        name="merge",
    )(x, oa, ob, w["g1"], w["wg"], w["woa"], wob, w["wo"])


def _mlp_body(h_ref, g2_ref, wup_ref, wdn_ref, y_ref):
    h = h_ref[...]
    hn = _rms_rows(h, g2_ref[...]).astype(BF16)
    up = jnp.dot(hn, wup_ref[...], preferred_element_type=F32)
    act = jnp.square(jnp.maximum(up, 0.0)).astype(BF16)
    y_ref[...] = h + jnp.dot(act, wdn_ref[...], preferred_element_type=F32)


def _mlp(h, w, *, tm):
    m, d = h.shape
    row = pl.BlockSpec((tm, d), lambda i: (i, 0))
    return pl.pallas_call(
        _mlp_body,
        out_shape=jax.ShapeDtypeStruct((m, d), F32),
        grid=(m // tm,),
        in_specs=[row, _const_spec((1, d)), _const_spec((d, D_FF)), _const_spec((D_FF, d))],
        out_specs=row,
        compiler_params=_cparams(1),
        name="mlp",
    )(h, w["g2"], w["wup"], w["wdn"])


def _proj_decode_body(x_ref, st_ref, g1_ref, wqkv_ref, wz_ref, wbq_ref, wkv_ref, wiq_ref, wsm_ref,
                      convw_ref, qng_ref, kng_ref, alog_ref, dtb_ref, ones_ref, seg_ref, cos_ref, sin_ref,
                      u_out, q_out, k_out, v_out, z_out, qb_out, kv_out, qi_out, sm_out, ss_out):
    x = x_ref[...]
    xn = _rms_rows(x, g1_ref[...]).astype(BF16)
    u = jnp.dot(xn, wqkv_ref[...], preferred_element_type=F32)
    u_out[...] = u
    cw = convw_ref[...]
    acc = st_ref[0] * cw[0:1] + st_ref[1] * cw[1:2] + st_ref[2] * cw[2:3] + u * cw[3:4]
    c = _silu(acc)
    for h in range(H_A):
        qh = c[:, h * DK_A:(h + 1) * DK_A]
        kh = c[:, (H_A + h) * DK_A:(H_A + h + 1) * DK_A]
        q_out[:, h * DK_A:(h + 1) * DK_A] = qh * (lax.rsqrt(jnp.sum(qh * qh, -1, keepdims=True) + EPS)
                                                 * (DK_A ** -0.5))
        k_out[:, h * DK_A:(h + 1) * DK_A] = kh * lax.rsqrt(jnp.sum(kh * kh, -1, keepdims=True) + EPS)
    v_out[...] = c[:, 2 * H_A * DK_A:]
    z_out[...] = jnp.dot(xn, wz_ref[...], preferred_element_type=F32)
    cos, sin = cos_ref[...], sin_ref[...]
    ones_bd = ones_ref[...]
    bq = jnp.dot(xn, wbq_ref[...], preferred_element_type=F32)
    qb = _rope_rows(_seg_rms(bq, ones_bd, qng_ref[...]), cos, sin)
    qb_out[...] = (qb * (HD_B ** -0.5)).astype(BF16)
    kv = jnp.dot(xn, wkv_ref[...], preferred_element_type=F32)
    kb = _rope_rows(_seg_rms(kv[:, :128], ones_bd[:128, :128], kng_ref[...]), cos, sin)
    kv_out[:, 0:128] = kb
    kv_out[:, 128:256] = kv[:, 128:]
    iq = jnp.dot(xn, wiq_ref[...], preferred_element_type=F32)
    qi = (_rope_rows(iq, cos, sin) * (D_IDX ** -0.5)).astype(BF16)
    qi_out[...] = qi
    sm = jnp.dot(xn, wsm_ref[...], preferred_element_type=F32)
    smo = _small_block(sm, cos, sin, alog_ref[...], dtb_ref[...])
    sm_out[...] = smo
    lane = lax.broadcasted_iota(I32, smo.shape, 1)
    ki = smo.astype(BF16).astype(F32)
    ki2 = jnp.where(lane < D_IDX, ki, pltpu.roll(ki, D_IDX, 1))
    prod = qi.astype(F32) * jnp.concatenate([ki2] * 4, axis=-1)
    sh = _mm_f32(prod, seg_ref[...])
    contrib = jnp.where(jnp.logical_and(lane >= _SM_WI, lane < _SM_WI + H_IDX),
                        jnp.maximum(sh, 0.0) * smo, 0.0)
    ss_out[...] = jnp.broadcast_to(jnp.sum(contrib, axis=-1, keepdims=True), ss_out.shape)


def _proj_decode(x, st, w, tabs):
    r, d = x.shape
    outs = [(r, CONV_CH, F32), (r, 512, F32), (r, 512, F32), (r, 512, F32), (r, 512, F32), (r, 512, BF16),
            (r, 256, F32), (r, 512, BF16), (r, 128, F32), (r, 128, F32)]
    args = (x, st, w["g1"], w["wqkv"], w["wz"], w["wbq"], w["wkv"], w["wiq"], w["wsm"], w["convw"],
            w["qng"], w["kng"], w["alog_row"], w["dtb_row"], w["ones_bd"], w["seg_sum"], tabs["cos"], tabs["sin"])
    return pl.pallas_call(
        _proj_decode_body,
        out_shape=[jax.ShapeDtypeStruct((a, n), dt) for a, n, dt in outs],
        compiler_params=pltpu.CompilerParams(vmem_limit_bytes=VMEM_LIMIT),
        name="proj_decode",
    )(*args)


def _gdn_decode_body(qc_ref, kc_ref, v_ref, z_ref, sm_ref, gg_ref, s_ref, o_ref, so_ref):
    sm = sm_ref[0]
    gg = gg_ref[...]
    for h in range(H_A):
        g = sm[:, _SM_G + h:_SM_G + h + 1]
        beta = sm[:, _SM_BETA + h:_SM_BETA + h + 1]
        kcol, qcol = kc_ref[0, h], qc_ref[0, h]
        v = v_ref[0, h]
        s = s_ref[0, h] * jnp.exp(g)
        vn = beta * (v - jnp.sum(s * kcol, axis=0, keepdims=True))
        s = s + kcol * vn
        so_ref[0, h] = s
        o = jnp.sum(s * qcol, axis=0, keepdims=True)
        o_ref[0, h] = (_rms_rows(o, gg) * _silu(z_ref[0, h])).astype(BF16)


def _gdn_decode(q, k, v, z, sm, gg, state):
    r = q.shape[0]
    col = lambda a: a.reshape(r, H_A, DK_A, 1)
    rowv = lambda a: a.reshape(r, H_A, 1, DV_A)
    cspec = pl.BlockSpec((1, H_A, DK_A, 1), lambda i: (i, 0, 0, 0))
    rspec = pl.BlockSpec((1, H_A, 1, DV_A), lambda i: (i, 0, 0, 0))
    sspec = pl.BlockSpec((1, H_A, DK_A, DV_A), lambda i: (i, 0, 0, 0))
    o, s_new = pl.pallas_call(
        _gdn_decode_body,
        out_shape=[jax.ShapeDtypeStruct((r, H_A, 1, DV_A), BF16),
                   jax.ShapeDtypeStruct((r, H_A, DK_A, DV_A), F32)],
        grid=(r,),
        in_specs=[cspec, cspec, rspec, rspec, pl.BlockSpec((1, 1, 128), lambda i: (i, 0, 0)),
                  _const_spec((1, DV_A)), sspec],
        out_specs=[rspec, sspec],
        compiler_params=_cparams(1),
        name="gdn_decode",
    )(col(q), col(k), rowv(v), rowv(z), sm.reshape(r, 1, 128), gg, state)
    return o.reshape(r, H_A * DV_A), s_new


def _page_copies(pt_ref, pages_hbm, buf, sem, row, first, count, slot):
    return [pltpu.make_async_copy(pages_hbm.at[pt_ref[row, first + p]], buf.at[slot, p], sem.at[slot])
            for p in range(count)]


def _idx_decode_body(pt_ref, qi_ref, w_ref, pages_hbm, out_ref, buf, sem, *, n_pages):
    i = pl.program_id(0)
    slot = i % 2
    copies = functools.partial(_page_copies, pt_ref, pages_hbm, buf, sem)

    @pl.when(i == 0)
    def _():
        for cp in copies(0, 0, n_pages, 0):
            cp.start()

    @pl.when(i + 1 < pl.num_programs(0))
    def _():
        for cp in copies(i + 1, 0, n_pages, 1 - slot):
            cp.start()

    for cp in copies(i, 0, n_pages, slot):
        cp.wait()
    qi = qi_ref[0]
    w = w_ref[0]
    for p in range(n_pages):
        s = _mm(qi, buf[slot, p])
        out_ref[0, p:p + 1, :] = jnp.sum(jnp.maximum(s, 0.0) * w, axis=0, keepdims=True)


def _idx_decode(page_table, qi, wi_col, cache_kidx):
    r, n_pages = page_table.shape
    grid_spec = pltpu.PrefetchScalarGridSpec(
        num_scalar_prefetch=1,
        grid=(r,),
        in_specs=[pl.BlockSpec((1, H_IDX, D_IDX), lambda i, pt: (i, 0, 0)),
                  pl.BlockSpec((1, H_IDX, 1), lambda i, pt: (i, 0, 0)),
                  pl.BlockSpec(memory_space=pl.ANY)],
        out_specs=pl.BlockSpec((1, n_pages, PAGE_SIZE), lambda i, pt: (i, 0, 0)),
        scratch_shapes=[pltpu.VMEM((2, n_pages, D_IDX, PAGE_SIZE), F32), pltpu.SemaphoreType.DMA((2,))],
    )
    return pl.pallas_call(
        functools.partial(_idx_decode_body, n_pages=n_pages),
        out_shape=jax.ShapeDtypeStruct((r, n_pages, PAGE_SIZE), F32),
        grid_spec=grid_spec,
        compiler_params=_cparams(1),
        name="idx_decode",
    )(page_table, qi.reshape(r, H_IDX, D_IDX), wi_col, cache_kidx)


def _select_decode_body(sc_ref, ss_ref, bias_ref, bself_ref, *, k_top, cw):
    r, n = sc_ref.shape
    nch = n // cw
    key_ref = sc_ref
    skey = ss_ref[:, 0:1]

    def lane_sum(f):
        def body(c, acc):
            off = pl.multiple_of(c * cw, cw)
            kk = key_ref[:, pl.ds(off, cw)]
            for j in range(cw // 128):
                acc = acc + f(kk[:, 128 * j:128 * (j + 1)], off + 128 * j)
            return acc
        acc = lax.fori_loop(0, nch, body, jnp.zeros((r, 128), F32))
        return jnp.sum(acc, axis=-1, keepdims=True).astype(I32)

    def count_ge(cand):
        cb = jnp.broadcast_to(cand, (r, 128))
        return lane_sum(lambda kk, off: jnp.where(kk >= cb, 1.0, 0.0)) + (skey >= cand).astype(I32)

    def count_gt_eq(t):
        tb = jnp.broadcast_to(t, (r, 128))
        return (lane_sum(lambda kk, off: jnp.where(kk > tb, 1.0, 0.0)) + (skey > t).astype(I32),
                lane_sum(lambda kk, off: jnp.where(kk == tb, 1.0, 0.0)) + (skey == t).astype(I32))

    def count_eq_below(t, j):
        tb = jnp.broadcast_to(t, (r, 128))
        jb = jnp.broadcast_to(j, (r, 128))
        lane = lax.broadcasted_iota(I32, (r, 128), 1)
        return (lane_sum(lambda kk, off: jnp.where(kk == tb, jnp.where(lane + off < jb, 1.0, 0.0), 0.0))
                + jnp.logical_and(skey == t, j > n).astype(I32))

    t, x = _topk_threshold(count_ge, count_gt_eq, count_eq_below, (r, 1), k_top, 15)
    tb = jnp.broadcast_to(t, (r, cw))
    xb = jnp.broadcast_to(x, (r, cw))

    def bias_chunk(c, carry):
        off = pl.multiple_of(c * cw, cw)
        kk = key_ref[:, pl.ds(off, cw)]
        idx = lax.broadcasted_iota(I32, (r, cw), 1) + off
        tie = jnp.where(kk == tb, jnp.where(idx <= xb, 0.0, NEG_BIG), NEG_BIG)
        bias_ref[:, pl.ds(off, cw)] = jnp.where(kk > tb, 0.0, tie)
        return carry

    lax.fori_loop(0, nch, bias_chunk, 0)
    tie_self = jnp.where(skey == t, jnp.where(x >= n, 0.0, NEG_BIG), NEG_BIG)
    bself_ref[...] = jnp.broadcast_to(jnp.where(skey > t, 0.0, tie_self), bself_ref.shape)


def _select_decode(scores, sself, *, k_top):
    r, n = scores.shape
    return pl.pallas_call(
        functools.partial(_select_decode_body, k_top=k_top, cw=512),
        out_shape=[jax.ShapeDtypeStruct((r, n), F32), jax.ShapeDtypeStruct((r, 128), F32)],
        compiler_params=pltpu.CompilerParams(vmem_limit_bytes=VMEM_LIMIT),
        name="select_decode",
    )(scores, sself)


def _attn_decode_body(pt_ref, qp_ref, bias_ref, kvs_ref, bself_ref, pages_hbm, o_ref,
                      m_ref, l_ref, acc_ref, buf, sem, *, npg, nsteps):
    i, j = pl.program_id(0), pl.program_id(1)
    step = i * nsteps + j
    slot = step % 2
    copies = functools.partial(_page_copies, pt_ref, pages_hbm, buf, sem)

    @pl.when(step == 0)
    def _():
        for cp in copies(0, 0, npg, 0):
            cp.start()

    @pl.when(step + 1 < pl.num_programs(0) * nsteps)
    def _():
        nxt = step + 1
        for cp in copies(nxt // nsteps, (nxt % nsteps) * npg, npg, 1 - slot):
            cp.start()

    @pl.when(j == 0)
    def _():
        m_ref[...] = jnp.full(m_ref.shape, NEG_BIG, F32)
        l_ref[...] = jnp.zeros(l_ref.shape, F32)
        acc_ref[...] = jnp.zeros(acc_ref.shape, F32)

    for cp in copies(i, j * npg, npg, slot):
        cp.wait()
    qp = qp_ref[0]

    ss = [_mm(qp, buf[slot, p, 0:128, :]) + bias_ref[0, :, PAGE_SIZE * p:PAGE_SIZE * (p + 1)]
          for p in range(npg)]
    m_old = m_ref[...]
    m_new = m_old
    for s in ss:
        m_new = jnp.maximum(m_new, jnp.max(s, axis=-1, keepdims=True))
    alpha = jnp.exp(m_old - m_new)
    l_new = l_ref[...] * alpha
    acc_new = acc_ref[...] * alpha
    for p in range(npg):
        pe = jnp.exp(ss[p] - m_new)
        l_new = l_new + jnp.sum(pe, axis=-1, keepdims=True)
        acc_new = acc_new + _mm_nt(pe, buf[slot, p, 128:256, :])
    m_ref[...] = m_new
    l_ref[...] = l_new
    acc_ref[...] = acc_new

    @pl.when(j == nsteps - 1)
    def _():
        kvs = kvs_ref[0]
        krow = kvs[:, :128].astype(BF16).astype(F32)
        s = jnp.sum(qp.astype(F32) * krow, axis=-1, keepdims=True) + bself_ref[0][:, 0:1]
        m_old = m_ref[...]
        m_new = jnp.maximum(m_old, s)
        alpha = jnp.exp(m_old - m_new)
        p = jnp.exp(s - m_new)
        l = l_ref[...] * alpha + p
        vrow = kvs[:, 128:].astype(BF16).astype(F32)
        acc = acc_ref[...] * alpha + p.astype(BF16).astype(F32) * vrow
        out = acc / l
        rowi = lax.broadcasted_iota(I32, out.shape, 0)
        out = jnp.where(rowi < 4, out, pltpu.roll(out, HD_B, 1))
        o_ref[0] = out.astype(BF16)


def _attn_decode(page_table, qpad, bias, kv_self, bself, cache_kv, *, npg):
    r, n_pages = page_table.shape
    nsteps = n_pages // npg
    page_rows = 2 * N_KV_B * HD_B
    grid_spec = pltpu.PrefetchScalarGridSpec(
        num_scalar_prefetch=1,
        grid=(r, nsteps),
        in_specs=[pl.BlockSpec((1, H_B, 128), lambda i, j, pt: (i, 0, 0)),
                  pl.BlockSpec((1, 1, PAGE_SIZE * npg), lambda i, j, pt: (i, 0, j)),
                  pl.BlockSpec((1, 1, 256), lambda i, j, pt: (i, 0, 0)),
                  pl.BlockSpec((1, 1, 128), lambda i, j, pt: (i, 0, 0)),
                  pl.BlockSpec(memory_space=pl.ANY)],
        out_specs=pl.BlockSpec((1, H_B, 128), lambda i, j, pt: (i, 0, 0)),
        scratch_shapes=[pltpu.VMEM((H_B, 1), F32), pltpu.VMEM((H_B, 1), F32), pltpu.VMEM((H_B, 128), F32),
                        pltpu.VMEM((2, npg, page_rows, PAGE_SIZE), F32), pltpu.SemaphoreType.DMA((2,))],
    )
    return pl.pallas_call(
        functools.partial(_attn_decode_body, npg=npg, nsteps=nsteps),
        out_shape=jax.ShapeDtypeStruct((r, H_B, 128), BF16),
        grid_spec=grid_spec,
        compiler_params=_cparams(2),
        name="attn_decode",
    )(page_table, qpad, bias.reshape(r, 1, -1), kv_self.reshape(r, 1, 256), bself.reshape(r, 1, 128), cache_kv)


def _prep_weights(norm1_g, w_in, conv_w, a_log, dt_bias, gdn_norm_g, q_norm_g, k_norm_g,
                  w_out_a, w_out_b, w_o, norm2_g, w_up, w_down):
    o = _OFF
    wb = w_in.astype(BF16)
    d = w_in.shape[0]
    wsm = jnp.concatenate([wb[:, o["ik"]:o["iw"]], wb[:, o["ab"]:o["bq"]], wb[:, o["iw"]:o["ga"]],
                           jnp.zeros((d, 128 - 80), BF16)], axis=1)

    def lanes(vals, start):
        return jnp.zeros((1, 128), F32).at[0, start:start + vals.shape[0]].set(vals.astype(F32))

    seg = np.zeros((512, 128), np.float32)
    for h in range(H_IDX):
        seg[64 * h:64 * (h + 1), _SM_WI + h] = 1.0
    ones_bd = np.kron(np.eye(8, dtype=np.float32), np.ones((64, 64), np.float32))
    wob = w_out_b.astype(BF16).reshape(H_B, HD_B, d)
    wob_pad = jnp.concatenate([wob, jnp.zeros_like(wob)], axis=1).reshape(H_B * 128, d)
    col8 = lambda v: jnp.concatenate([jnp.zeros((4,), F32), v.astype(F32)]).reshape(8, 1)
    return dict(
        g1=norm1_g.reshape(1, d).astype(F32),
        wqkv=wb[:, o["aq"]:o["az"]], wz=wb[:, o["az"]:o["ab"]], wbq=wb[:, o["bq"]:o["bk"]],
        wkv=wb[:, o["bk"]:o["iq"]], wiq=wb[:, o["iq"]:o["ik"]], wsm=wsm,
        wbqt=wb[:, o["bq"]:o["bk"]].T, wiqt=wb[:, o["iq"]:o["ik"]].T, wvt=wb[:, o["bv"]:o["iq"]].T,
        wsmt=wsm.T, wg=wb[:, o["ga"]:o["end"]],
        convw=conv_w.astype(F32),
        qng=jnp.tile(q_norm_g.astype(F32), H_B).reshape(1, 512),
        kng=jnp.tile(k_norm_g.astype(F32), N_KV_B).reshape(1, 128),
        qngt=jnp.tile(q_norm_g.astype(F32).reshape(HD_B, 1), (1, 128)),
        alog_row=lanes(a_log, _SM_G), dtb_row=lanes(dt_bias, _SM_G),
        alog_col=col8(a_log), dtb_col=col8(dt_bias),
        ones_bd=jnp.asarray(ones_bd, BF16), ones_bd128=jnp.asarray(ones_bd[:128, :128], BF16),
        seg_sum=jnp.asarray(seg),
        gg=gdn_norm_g.reshape(1, DV_A).astype(F32),
        woa=w_out_a.astype(BF16), wob=w_out_b.astype(BF16), wob_pad=wob_pad, wo=w_o.astype(BF16),
        g2=norm2_g.reshape(1, d).astype(F32), wup=w_up.astype(BF16), wdn=w_down.astype(BF16),
    )


def _rope_tables(pos):
    half = HD_B // 2
    inv = ROPE_THETA ** (-jnp.arange(half, dtype=F32) / half)
    ang = pos.astype(F32)[:, None] * inv[None, :]
    cos, sin = jnp.cos(ang), jnp.sin(ang)
    return dict(cos=jnp.tile(cos, (1, 4)), sin=jnp.tile(jnp.concatenate([-sin, sin], axis=1), (1, 2)),
                cos_t=cos.T, sin_t=sin.T)


def _layer_prompt(x, w, *, tm=512, tq=256, nb=8):
    b, t, d = x.shape
    tabs = _rope_tables(jnp.arange(t))
    (q, k, v, z, kv, kbf, kibf, sm, kidx, qtp, qit, vt, gt, wit, conv_new) = _proj_prompt(x, w, tabs, tm=tm, kc=tq)
    oa, ssm = _gdn_prompt(q, k, v, z, sm, gt, w["gg"], b=b, t=t, nb=nb)
    ob = _dsa_prompt(qit, wit, qtp, kibf, kbf, vt, b=b, t=t, tq=tq, k_top=min(TOPK_MAX, t // 4))
    h = _merge(x.reshape(b * t, d), oa.reshape(b * t, 512), ob, w["wob"], w, tm=tm)
    y = _mlp(h, w, tm=tm)
    return (y.reshape(b, t, d), kv.reshape(b, t, 2, N_KV_B, HD_B), kidx.reshape(b, t, D_IDX), ssm, conv_new)


def _layer_decode(x, w, state_ssm, state_conv, cache_kv, cache_kidx, page_table, *, npg=32):
    r, d = x.shape
    n_pages = page_table.shape[1]
    past = n_pages * PAGE_SIZE
    tabs = _rope_tables(jnp.full((1,), past))
    st = jnp.swapaxes(state_conv, 0, 1)
    (u, q, k, v, z, qb, kv, qi, sm, sself) = _proj_decode(x, st, w, tabs)
    oa, ssm = _gdn_decode(q, k, v, z, sm, w["gg"], state_ssm)
    wi_col = sm[:, _SM_WI:_SM_WI + H_IDX].reshape(r, H_IDX, 1)
    kidx_pages = jnp.swapaxes(cache_kidx, 1, 2)
    kv_pages = jnp.transpose(cache_kv, (0, 2, 3, 4, 1)).reshape(-1, 2 * N_KV_B * HD_B, PAGE_SIZE)
    scores = _idx_decode(page_table, qi, wi_col, kidx_pages).reshape(r, past)
    bias, bself = _select_decode(scores, sself, k_top=min(TOPK_MAX, (past + 1) // 4))
    qh = qb.reshape(r, H_B, HD_B)
    zq = jnp.zeros_like(qh[:, :4])
    qpad = jnp.concatenate([jnp.concatenate([qh[:, :4], zq], axis=-1),
                            jnp.concatenate([zq, qh[:, 4:]], axis=-1)], axis=1)
    ob = _attn_decode(page_table, qpad, bias, kv, bself, kv_pages, npg=npg)
    h = _merge(x, oa, ob.reshape(r, H_B * 128), w["wob_pad"], w, tm=r)
    y = _mlp(h, w, tm=r)
    conv_new = jnp.concatenate([state_conv[:, 1:], u[:, None, :]], axis=1)
    return (y, kv.reshape(r, 1, 2, N_KV_B, HD_B), sm[:, :D_IDX].reshape(r, 1, D_IDX), ssm, conv_new)


def kernel(x_prompt, x_sample, cache_kv, cache_kidx, page_table, state_ssm, state_conv, norm1_g, w_in, conv_w,
           a_log, dt_bias, gdn_norm_g, q_norm_g, k_norm_g, w_out_a, w_out_b, w_o, norm2_g, w_up, w_down):
    assert w_in.shape[0] == 1, "single-layer trunk"
    w = _prep_weights(norm1_g[0], w_in[0], conv_w[0], a_log[0], dt_bias[0], gdn_norm_g[0], q_norm_g[0],
                      k_norm_g[0], w_out_a[0], w_out_b[0], w_o[0], norm2_g[0], w_up[0], w_down[0])
    yp, kv_p, ki_p, ssm_p, cv_p = _layer_prompt(x_prompt, w)
    ys, kv_s, ki_s, ssm_s, cv_s = _layer_decode(x_sample[:, 0], w, state_ssm[0], state_conv[0],
                                                cache_kv[0], cache_kidx[0], page_table)
    return (yp, ys[:, None, :], kv_p[None], ki_p[None], ssm_p[None], cv_p[None],
            kv_s[None], ki_s[None], ssm_s[None], cv_s[None])
```

```python
import functools
import math

import jax
import jax.numpy as jnp
import numpy as np
from jax import lax
from jax.experimental import pallas as pl
from jax.experimental.pallas import tpu as pltpu

F32 = jnp.float32
BF16 = jnp.bfloat16
I32 = jnp.int32

D_MODEL = 1024
H_A, DK_A, DV_A = 4, 128, 128
CONV_W = 4
CONV_CH = H_A * (2 * DK_A + DV_A)
CHUNK = 64
HD_B, H_B, N_KV_B = 64, 8, 2
H_IDX, D_IDX = 8, 64
TOPK_MAX = 256
PAGE_SIZE = 128
ROPE_THETA = 10000.0
D_FF = 4 * D_MODEL
EPS = 1e-6

_OFF = dict(aq=0, ak=512, av=1024, az=1536, ab=2048, aa=2052, bq=2056, bk=2568, bv=2696,
            iq=2824, ik=3336, iw=3400, ga=3408, gb=4432, end=5456)
_SM_BETA, _SM_G, _SM_WI = 64, 68, 72

LOG2_E = math.log2(math.e)
NEG_BIG = -1e30
INT_MIN = -(2 ** 31)
NEG_INF_KEY = (0xFF800000 ^ 0x7FFFFFFF) - 2 ** 32

VMEM_LIMIT = 56 * 1024 * 1024
HIGHEST = lax.Precision.HIGHEST


def _cparams(n_axes):
    return pltpu.CompilerParams(dimension_semantics=("arbitrary",) * n_axes,
                                vmem_limit_bytes=VMEM_LIMIT)


def _const_spec(shape):
    nd = len(shape)
    return pl.BlockSpec(shape, lambda *a: (0,) * nd, pipeline_mode=pl.Buffered(1))


def _mm(a, b):
    return jnp.dot(a.astype(BF16), b.astype(BF16), preferred_element_type=F32)


def _mm_nt(a, b):
    return lax.dot_general(a.astype(BF16), b.astype(BF16), (((1,), (1,)), ((), ())),
                           preferred_element_type=F32)


def _mm_tn(a, b):
    return lax.dot_general(a.astype(BF16), b.astype(BF16), (((0,), (0,)), ((), ())),
                           preferred_element_type=F32)


def _mm_f32(a, b):
    return jnp.dot(a, b, preferred_element_type=F32, precision=HIGHEST)


def _rms_rows(x, g):
    return x * lax.rsqrt(jnp.mean(x * x, axis=-1, keepdims=True) + EPS) * g


def _silu(x):
    return x * jax.nn.sigmoid(x)


def _softplus(x):
    return jnp.maximum(x, 0.0) + jnp.log1p(jnp.exp(-jnp.abs(x)))


def _rope_rows(x, cos, sin_signed):
    n = x.shape[-1]
    reps = n // 128
    if reps > 1:
        cos = jnp.concatenate([cos] * reps, axis=-1)
        sin_signed = jnp.concatenate([sin_signed] * reps, axis=-1)
    lane = lax.broadcasted_iota(I32, x.shape, x.ndim - 1)
    first_half = (lane % HD_B) < (HD_B // 2)
    partner = jnp.where(first_half, pltpu.roll(x, n - HD_B // 2, x.ndim - 1),
                        pltpu.roll(x, HD_B // 2, x.ndim - 1))
    return x * cos + partner * sin_signed


def _rope_cols(x, cos_t, sin_t):
    x1, x2 = x[:32], x[32:]
    return jnp.concatenate([x1 * cos_t - x2 * sin_t, x2 * cos_t + x1 * sin_t], axis=0)


def _seg_rms(x, ones_bd, g):
    sq = x * x
    hi = sq.astype(BF16)
    lo = (sq - hi.astype(F32)).astype(BF16)
    ss = (jnp.dot(hi, ones_bd, preferred_element_type=F32)
          + jnp.dot(lo, ones_bd, preferred_element_type=F32))
    return x * lax.rsqrt(ss * (1.0 / HD_B) + EPS) * g


def _small_block(sm, cos, sin_signed, alog_row, dtb_row):
    lane = lax.broadcasted_iota(I32, sm.shape, 1)
    roped = _rope_rows(sm, cos, sin_signed)
    beta = jax.nn.sigmoid(sm)
    g = -jnp.exp(alog_row) * _softplus(sm + dtb_row)
    wi = sm * (H_IDX ** -0.5)
    out = jnp.where(lane < _SM_BETA, roped,
                    jnp.where(lane < _SM_G, beta, jnp.where(lane < _SM_WI, g, wi)))
    return out


def _proj_prompt_body(x_ref, g1_ref, wqkv_ref, wz_ref, wkv_ref, wsm_ref,
                      wbqt_ref, wiqt_ref, wvt_ref, wsmt_ref, convw_ref, kng_ref, qngt_ref, alog_ref, dtb_ref,
                      alogt_ref, dtbt_ref, ones_ref, cos_ref, sin_ref, cost_ref, sint_ref,
                      q_out, k_out, v_out, z_out, kv_out, kbf_out, kibf_out, sm_out, kidx_out,
                      qtp_out, qit_out, vt_out, gt_out, wit_out, conv_out, ubuf, *, tm, kc):
    @pl.when(pl.program_id(1) == 0)
    def _():
        ubuf[0:8, :] = jnp.zeros((8, CONV_CH), F32)

    x = x_ref[0]
    xn = _rms_rows(x, g1_ref[...]).astype(BF16)

    nt_dims = (((1,), (1,)), ((), ()))
    u = jnp.dot(xn, wqkv_ref[...], preferred_element_type=F32)
    z = jnp.dot(xn, wz_ref[...], preferred_element_type=F32)
    kv = jnp.dot(xn, wkv_ref[...], preferred_element_type=F32)
    sm = jnp.dot(xn, wsm_ref[...], preferred_element_type=F32)
    qt = lax.dot_general(wbqt_ref[...], xn, nt_dims, preferred_element_type=F32)
    qit = lax.dot_general(wiqt_ref[...], xn, nt_dims, preferred_element_type=F32)
    vt = lax.dot_general(wvt_ref[...], xn, nt_dims, preferred_element_type=F32)
    smt = lax.dot_general(wsmt_ref[...], xn, nt_dims, preferred_element_type=F32)

    ubuf[8:tm + 8, :] = u
    cw = convw_ref[...]
    acc = (ubuf[5:5 + tm, :] * cw[0:1] + ubuf[6:6 + tm, :] * cw[1:2]
           + ubuf[7:7 + tm, :] * cw[2:3] + u * cw[3:4])
    conv_out[0] = ubuf[tm + 5:tm + 8, :]
    ubuf[0:8, :] = ubuf[tm:tm + 8, :]
    c = _silu(acc)
    for h in range(H_A):
        qh = c[:, h * DK_A:(h + 1) * DK_A]
        kh = c[:, (H_A + h) * DK_A:(H_A + h + 1) * DK_A]
        q_out[:, h * DK_A:(h + 1) * DK_A] = qh * (lax.rsqrt(jnp.sum(qh * qh, -1, keepdims=True) + EPS)
                                                 * (DK_A ** -0.5))
        k_out[:, h * DK_A:(h + 1) * DK_A] = kh * lax.rsqrt(jnp.sum(kh * kh, -1, keepdims=True) + EPS)
    v_out[...] = c[:, 2 * H_A * DK_A:]
    z_out[...] = z

    cos, sin = cos_ref[...], sin_ref[...]
    ones_bd = ones_ref[...]

    kb =_rope_rows(_seg_rms(kv[:, :128], ones_bd, kng_ref[...]), cos, sin)
    kv_out[:, 0:128] = kb
    kv_out[:, 128:256] = kv[:, 128:]
    kbf_out[...] = kb.astype(BF16)

    smo = _small_block(sm, cos, sin, alog_ref[...], dtb_ref[...])
    sm_out[...] = smo
    kidx_out[...] = smo[:, :D_IDX]
    kibf_out[...] = smo[:, :D_IDX].astype(BF16)

    cos_t, sin_t = cost_ref[...], sint_ref[...]
    qg =jnp.concatenate([qngt_ref[...]] * (tm // 128), axis=1)
    zero64 = jnp.zeros((HD_B, tm), BF16)
    for h in range(H_B):
        qh = qt[HD_B * h:HD_B * (h + 1)]
        qh = qh * lax.rsqrt(jnp.mean(qh * qh, axis=0, keepdims=True) + EPS) * qg
        qh = (_rope_cols(qh, cos_t, sin_t) * (HD_B ** -0.5 * LOG2_E)).astype(BF16)
        n = h // (H_B // N_KV_B)
        qtp_out[0, h, HD_B * n:HD_B * (n + 1), :] = qh
        qtp_out[0, h, HD_B * (1 - n):HD_B * (2 - n), :] = zero64
        qit_out[0, h] = (_rope_cols(qit[D_IDX * h:D_IDX * (h + 1)], cos_t, sin_t) * (D_IDX ** -0.5)).astype(BF16)
    vtb = vt.astype(BF16)
    for j in range(tm // kc):
        vt_out[0, j] = vtb[:, j * kc:(j + 1) * kc]
    sm8 = smt[64:72]
    row = lax.broadcasted_iota(I32, sm8.shape, 0)
    g8 = -jnp.exp(alogt_ref[...]) * _softplus(sm8 + dtbt_ref[...])
    gt_out[0] = jnp.where(row < 4, jax.nn.sigmoid(sm8), g8)
    wit_out[0] = smt[72:80] * (H_IDX ** -0.5)


def _proj_prompt(x, w, tabs, *, tm, kc):
    b, t, d = x.shape
    nt = t // tm
    m = b * t
    row = lambda n: pl.BlockSpec((tm, n), lambda i, j: (i * nt + j, 0))
    ins = [
        pl.BlockSpec((1, tm, d), lambda i, j: (i, j, 0)),
        _const_spec((1, d)),
        _const_spec((d, CONV_CH)), _const_spec((d, 512)), _const_spec((d, 256)), _const_spec((d, 128)),
        _const_spec((512, d)), _const_spec((512, d)), _const_spec((128, d)), _const_spec((128, d)),
        _const_spec((CONV_W, CONV_CH)), _const_spec((1, 128)), _const_spec((HD_B, 128)),
        _const_spec((1, 128)), _const_spec((1, 128)), _const_spec((8, 1)), _const_spec((8, 1)),
        _const_spec((128, 128)),
        pl.BlockSpec((tm, 128), lambda i, j: (j, 0)), pl.BlockSpec((tm, 128), lambda i, j: (j, 0)),
        pl.BlockSpec((32, tm), lambda i, j: (0, j)), pl.BlockSpec((32, tm), lambda i, j: (0, j)),
    ]
    nkc = t // kc
    outs = [
        (jax.ShapeDtypeStruct((m, 512), F32), row(512)),
        (jax.ShapeDtypeStruct((m, 512), F32), row(512)),
        (jax.ShapeDtypeStruct((m, 512), F32), row(512)),
        (jax.ShapeDtypeStruct((m, 512), F32), row(512)),
        (jax.ShapeDtypeStruct((m, 256), F32), row(256)),
        (jax.ShapeDtypeStruct((m, 128), BF16), row(128)),
        (jax.ShapeDtypeStruct((m, D_IDX), BF16), row(D_IDX)),
        (jax.ShapeDtypeStruct((m, 128), F32), row(128)),
        (jax.ShapeDtypeStruct((m, D_IDX), F32), row(D_IDX)),
        (jax.ShapeDtypeStruct((b, H_B, 128, t), BF16),
         pl.BlockSpec((1, H_B, 128, tm), lambda i, j: (i, 0, 0, j))),
        (jax.ShapeDtypeStruct((b, H_IDX, D_IDX, t), BF16),
         pl.BlockSpec((1, H_IDX, D_IDX, tm), lambda i, j: (i, 0, 0, j))),
        (jax.ShapeDtypeStruct((b, nkc, 128, kc), BF16),
         pl.BlockSpec((1, tm // kc, 128, kc), lambda i, j: (i, j, 0, 0))),
        (jax.ShapeDtypeStruct((b, 8, t), F32), pl.BlockSpec((1, 8, tm), lambda i, j: (i, 0, j))),
        (jax.ShapeDtypeStruct((b, H_IDX, t), F32), pl.BlockSpec((1, H_IDX, tm), lambda i, j: (i, 0, j))),
        (jax.ShapeDtypeStruct((b, CONV_W - 1, CONV_CH), F32),
         pl.BlockSpec((1, CONV_W - 1, CONV_CH), lambda i, j: (i, 0, 0))),
    ]
    return pl.pallas_call(
        functools.partial(_proj_prompt_body, tm=tm, kc=kc),
        out_shape=[o[0] for o in outs],
        grid=(b, nt),
        in_specs=ins,
        out_specs=[o[1] for o in outs],
        scratch_shapes=[pltpu.VMEM((tm + 8, CONV_CH), F32)],
        compiler_params=_cparams(2),
        name="proj_prompt",
    )(x, w["g1"], w["wqkv"], w["wz"], w["wkv"], w["wsm"], w["wbqt"], w["wiqt"], w["wvt"], w["wsmt"],
      w["convw"], w["kng"], w["qngt"], w["alog_row"], w["dtb_row"], w["alog_col"], w["dtb_col"],
      w["ones_bd128"], tabs["cos"], tabs["sin"], tabs["cos_t"], tabs["sin_t"])


def _split3(x):
    hi = x.astype(BF16)
    r1 = x - hi.astype(F32)
    mid = r1.astype(BF16)
    lo = (r1 - mid.astype(F32)).astype(BF16)
    return hi, mid, lo


def _gdn_body(q_ref, k_ref, v_ref, z_ref, sm_ref, gt_ref, gg_ref, bdk_ref, bdp_ref, o_ref, s_ref, *, nb):
    c = pl.program_id(1)

    @pl.when(c == 0)
    def _():
        s_ref[...] = jnp.zeros_like(s_ref)

    n, nh = CHUNK, H_A
    w4 = nh * n
    ri4 = lax.broadcasted_iota(I32, (n, w4), 0)
    ci4 = lax.broadcasted_iota(I32, (n, w4), 1) % n
    strict4, incl4 = ri4 > ci4, ri4 >= ci4
    eye4 = jnp.where(ri4 == ci4, 1.0, 0.0)
    triu4 = jnp.where(ri4 <= ci4, 1.0, 0.0).astype(BF16)
    ri = lax.broadcasted_iota(I32, (n, n), 0)
    tril = jnp.where(ri >= lax.broadcasted_iota(I32, (n, n), 1), 1.0, 0.0).astype(BF16)
    blk_row = lax.broadcasted_iota(I32, (1, w4), 1) // n
    low_half = lax.broadcasted_iota(I32, (n, 128), 1) < n
    zeros_pad = jnp.zeros((n, 2 * DV_A), BF16)
    gg = gg_ref[...]
    dot = functools.partial(jnp.dot, preferred_element_type=F32)

    def pad_rows(x, e):
        z = zeros_pad[:, :x.shape[1]]
        return jnp.concatenate([x, z] if e == 0 else [z, x], axis=0)

    seqs = range(nb)
    st = [dict() for _ in seqs]

    for bi in seqs:
        hi, mid, lo = _split3(sm_ref[bi])
        st[bi]["gc_all"] = dot(tril, hi) + dot(tril, mid) + dot(tril, lo)
        hi, mid, lo = _split3(gt_ref[bi, 0])
        st[bi]["grows"] = dot(hi, triu4) + dot(mid, triu4) + dot(lo, triu4)

    for bi in seqs:
        d = st[bi]
        sm, k, grows = sm_ref[bi], k_ref[bi], d["grows"]
        grow = jnp.where(blk_row == 0, grows[4:5], jnp.where(blk_row == 1, grows[5:6],
                         jnp.where(blk_row == 2, grows[6:7], grows[7:8])))
        gcols = [jnp.broadcast_to(d["gc_all"][:, _SM_G + h:_SM_G + h + 1], (n, 128)) for h in range(nh)]
        betas = [jnp.broadcast_to(sm[:, _SM_BETA + h:_SM_BETA + h + 1], (n, 128)) for h in range(nh)]
        d["gc512"] = jnp.concatenate(gcols, axis=1)
        d["beta512"] = jnp.concatenate(betas, axis=1)
        gc256 = jnp.concatenate([jnp.where(low_half, gcols[0], gcols[1]),
                                 jnp.where(low_half, gcols[2], gcols[3])], axis=1)
        d["dec"] = jnp.exp(jnp.minimum(gc256 - grow, 0.0))
        d["kb"] = k * d["beta512"]
        kbd = jnp.concatenate([k.astype(BF16)] * nh, axis=0) * bdk_ref[...]
        d["aq"] = lax.dot_general(jnp.concatenate([d["kb"], q_ref[bi]], axis=0).astype(BF16), kbd,
                                  (((1,), (1,)), ((), ())), preferred_element_type=F32)

    for bi in seqs:
        d = st[bi]
        x = jnp.where(strict4, -d["aq"][:n] * d["dec"], 0.0)
        d["qkm"] = jnp.where(incl4, d["aq"][n:] * d["dec"], 0.0).astype(BF16)
        d["inv"], d["p"] = eye4 + x, x
    for lvl in range(6):
        for bi in seqs:
            d = st[bi]
            p, inv = d["p"], d["inv"]
            wbd = jnp.concatenate([p.astype(BF16)] * nh, axis=0) * bdp_ref[...]
            if lvl == 0:
                d["p"] = dot(p.astype(BF16), wbd)
            elif lvl < 5:
                r = dot(jnp.concatenate([p, inv], axis=0).astype(BF16), wbd)
                d["p"], d["inv"] = r[:n], inv + r[n:]
            else:
                d["inv"] = inv + dot(inv.astype(BF16), wbd)

    heads = [(bi, h) for bi in seqs for h in range(nh)]
    hsl = lambda h: slice(h * DK_A, (h + 1) * DK_A)
    psl = lambda h: slice(128 * (h // 2), 128 * (h // 2 + 1))
    for bi in seqs:
        d = st[bi]
        k, gc512 = k_ref[bi], d["gc512"]
        d["tinv"] = d["inv"].astype(BF16)
        egc = jnp.exp(gc512)
        d["rhs_v"] = (v_ref[bi] * d["beta512"]).astype(BF16)
        d["rhs_k"] = (d["kb"] * egc).astype(BF16)
        g_last = gc512[n - 1:n, :]
        d["q_dec"] = q_ref[bi] * egc
        d["k_dec"] = (k * jnp.exp(g_last - gc512)).astype(BF16)
        d["c_dec"] = jnp.exp(g_last)
    sol, ws, vn = {}, {}, {}
    for bi, h in heads:
        d = st[bi]
        rhs = pad_rows(jnp.concatenate([d["rhs_v"][:, hsl(h)], d["rhs_k"][:, hsl(h)]], axis=1), h % 2)
        sol[bi, h] = dot(d["tinv"][:, psl(h)], rhs)
    for bi, h in heads:
        lhs = jnp.concatenate([sol[bi, h][:, DV_A:], st[bi]["q_dec"][:, hsl(h)]], axis=0)
        ws[bi, h] = dot(lhs.astype(BF16), s_ref[bi, h].astype(BF16))
    for bi, h in heads:
        d = st[bi]
        vn[bi, h] = (sol[bi, h][:, :DV_A] - ws[bi, h][:n]).astype(BF16)
        s_ref[bi, h] = d["c_dec"][:, hsl(h)] * s_ref[bi, h] + lax.dot_general(
            d["k_dec"][:, hsl(h)], vn[bi, h], (((0,), (0,)), ((), ())), preferred_element_type=F32)
    for bi, h in heads:
        o = ws[bi, h][n:] + dot(st[bi]["qkm"][:, psl(h)], pad_rows(vn[bi, h], h % 2))
        o_ref[bi, :, hsl(h)] = (_rms_rows(o, gg) * _silu(z_ref[bi, :, hsl(h)])).astype(BF16)


def _gdn_prompt(q, k, v, z, sm, gt, gg, *, b, t, nb):
    nc = t // CHUNK
    r3 = lambda a, n: a.reshape(b, t, n)
    blk = lambda n: pl.BlockSpec((nb, CHUNK, n), lambda i, j: (i, j, 0))
    w4 = H_A * CHUNK
    bdk = np.kron(np.eye(H_A, dtype=np.float32), np.ones((CHUNK, DK_A), np.float32))
    bdp = np.kron(np.eye(H_A, dtype=np.float32), np.ones((CHUNK, CHUNK), np.float32))
    return pl.pallas_call(
        functools.partial(_gdn_body, nb=nb),
        out_shape=[jax.ShapeDtypeStruct((b, t, 512), BF16),
                   jax.ShapeDtypeStruct((b, H_A, DK_A, DV_A), F32)],
        grid=(b // nb, nc),
        in_specs=[blk(512), blk(512), blk(512), blk(512), blk(128),
                  pl.BlockSpec((nb, 1, 8, CHUNK), lambda i, j: (i, j, 0, 0)),
                  _const_spec((1, DV_A)), _const_spec((w4, H_A * DK_A)), _const_spec((w4, w4))],
        out_specs=[blk(512), pl.BlockSpec((nb, H_A, DK_A, DV_A), lambda i, j: (i, 0, 0, 0))],
        compiler_params=_cparams(2),
        name="gdn_prompt",
    )(r3(q, 512), r3(k, 512), r3(v, 512), r3(z, 512), r3(sm, 128),
      jnp.swapaxes(gt.reshape(b, 8, nc, CHUNK), 1, 2), gg, jnp.asarray(bdk, BF16), jnp.asarray(bdp, BF16))


def _ordinal_to_float(k):
    k = jnp.maximum(k, NEG_INF_KEY)
    return pltpu.bitcast(k ^ ((k >> 31) & 0x7FFFFFFF), F32)


def _kth_largest(count_ge, shape, k_top):
    def bit_step(i, o):
        cand = o + jnp.left_shift(jnp.int32(1), 31 - i)
        return jnp.where(count_ge(_ordinal_to_float(cand)) >= k_top, cand, o)

    o = lax.fori_loop(0, 32, bit_step, jnp.full(shape, INT_MIN, I32))
    return _ordinal_to_float(o), o


def _kth_largest_two_stage(count_ge, count_ge_rounded, shape, k_top):
    def bf16_ordinal_as_f32_ordinal(h):
        return jnp.where(h >= 0, h << 16, (h << 16) | 0xFFFF)

    def coarse_step(i, h):
        cand = h + jnp.left_shift(jnp.int32(1), 15 - i)
        c = _ordinal_to_float(bf16_ordinal_as_f32_ordinal(cand)).astype(BF16)
        return jnp.where(count_ge_rounded(c) >= k_top, cand, h)

    h = lax.fori_loop(0, 16, coarse_step, jnp.full(shape, -(2 ** 15), I32))
    o = bf16_ordinal_as_f32_ordinal(h)
    o = jnp.where(count_ge(_ordinal_to_float(o)) >= k_top, o, bf16_ordinal_as_f32_ordinal(h - 1))

    def fine_step(i, o):
        cand = o + jnp.left_shift(jnp.int32(1), 15 - i)
        return jnp.where(count_ge(_ordinal_to_float(cand)) >= k_top, cand, o)

    o = lax.fori_loop(0, 16, fine_step, o)
    return _ordinal_to_float(o), o


def _topk_threshold(count_ge, count_gt_eq, count_eq_below, shape, k_top, idx_bits):
    t, o = _kth_largest(count_ge, shape, k_top)
    c_gt, c_eq = count_gt_eq(t)
    need = k_top - c_gt
    untied = jnp.logical_or(c_eq == need, o <= NEG_INF_KEY)
    all_untied = jnp.min(jnp.where(untied, 1.0, 0.0)) > 0.5

    def tie_search(_):
        def idx_step(i, x):
            cand = x + jnp.left_shift(jnp.int32(1), idx_bits - 1 - i)
            return jnp.where(count_eq_below(t, cand) < need, cand, x)
        return lax.fori_loop(0, idx_bits, idx_step, jnp.zeros(shape, I32))

    x = lax.cond(all_untied, lambda _: jnp.full(shape, 2 ** 30, I32), tie_search, 0)
    return t, x


_ACC_ROWS = HD_B + 16
_LOGITS_AHEAD = 5


def _dsa_body(qit_ref, wit_ref, qtp_ref, ki_ref, k_ref, vt_ref, o_ref,
              sc_ref, sb_ref, m_ref, acc_ref, *, tq, kc, k_top):
    i = pl.program_id(1)
    nkc = i + 1
    rowi = lax.broadcasted_iota(I32, (kc, tq), 0)
    coli = lax.broadcasted_iota(I32, (kc, tq), 1)
    ng = kc // 32

    wit = wit_ref[0]

    def score_chunks(chunks):
        for c in chunks:
            kic = ki_ref[0, c]
            score = jnp.zeros((kc, tq), F32)
            for h in range(H_IDX):
                s = jnp.dot(kic, qit_ref[0, h], preferred_element_type=F32)
                score = score + jnp.maximum(s, 0.0) * wit[h:h + 1, :]
            causal = rowi + (c - i) * kc <= coli
            score = jnp.where(causal, score, -jnp.inf)
            sc_ref[c] = score
            sb_ref[c] = score.astype(BF16)

    nquads = nkc // 4
    tail = 4 * nquads

    def score_quad(p, carry):
        score_chunks([4 * p, 4 * p + 1, 4 * p + 2, 4 * p + 3])
        return carry

    lax.fori_loop(0, nquads, score_quad, 0)

    @pl.when(nkc - tail >= 2)
    def _():
        score_chunks([tail, tail + 1])

    @pl.when(nkc % 2 == 1)
    def _():
        score_chunks([nkc - 1])

    def key_sum(f):
        def body(c, acc):
            ind = f(sc_ref[c], c * kc)
            return acc + jnp.sum(ind.reshape(ng, 32, tq), axis=0)
        acc = lax.fori_loop(0, nkc, body, jnp.zeros((32, tq), F32))
        return jnp.sum(acc, axis=0, keepdims=True).astype(I32)

    def count_ge_rounded(cand):
        one, zero = jnp.ones((), BF16), jnp.zeros((), BF16)

        def body(c, acc):
            ind = jnp.where(sb_ref[c] >= cand, one, zero)
            for g in range(ng):
                acc = acc + ind[32 * g:32 * (g + 1)]
            return acc
        acc = lax.fori_loop(0, nkc, body, jnp.zeros((32, tq), BF16))
        return jnp.sum(acc.astype(F32), axis=0, keepdims=True).astype(I32)

    t, _ = _kth_largest_two_stage(lambda cand: key_sum(lambda s, off: jnp.where(s >= cand, 1.0, 0.0)),
                                  count_ge_rounded, (1, tq), k_top)
    ties_wanted = (k_top - key_sum(lambda s, off: jnp.where(s > t, 1.0, 0.0))).astype(F32)

    m_ref[...] = jnp.full(m_ref.shape, NEG_BIG, F32)
    acc_ref[...] = jnp.zeros(acc_ref.shape, F32)
    ones_rows = jnp.ones((_ACC_ROWS - HD_B, kc), BF16)
    prefix = jnp.where(lax.broadcasted_iota(I32, (kc, kc), 0) >= lax.broadcasted_iota(I32, (kc, kc), 1),
                       1.0, 0.0).astype(BF16)

    def attn_chunks(chunks, ties):
        kks, biases, vaugs = [], [], []
        for c in chunks:
            sc = sc_ref[c]
            tied = jnp.where(sc == t, 1.0, 0.0).astype(BF16)
            rank = ties + jnp.dot(prefix, tied, preferred_element_type=F32)
            tie = jnp.where(sc == t, jnp.where(rank <= ties_wanted, 0.0, NEG_BIG), NEG_BIG)
            biases.append(jnp.where(sc > -jnp.inf, jnp.where(sc > t, 0.0, tie), NEG_BIG))
            ties = rank[kc - 1:kc, :]
            kks.append(k_ref[0, c])
            vt = vt_ref[0, c]
            vaugs.append([jnp.concatenate([vt[HD_B * n:HD_B * (n + 1)], ones_rows], axis=0)
                          for n in range(N_KV_B)])
        units = [(j, h) for j in range(len(chunks)) for h in range(H_B)]
        logits = {}

        def issue(u):
            j, h = units[u]
            logits[u] = jnp.dot(kks[j], qtp_ref[0, h], preferred_element_type=F32)

        for u in range(_LOGITS_AHEAD):
            issue(u)
        for u, (j, h) in enumerate(units):
            if u + _LOGITS_AHEAD < len(units):
                issue(u + _LOGITS_AHEAD)
            n = h // (H_B // N_KV_B)
            s = biases[j] + logits.pop(u)
            m_old = m_ref[h:h + 1, :]
            m_new = jnp.maximum(m_old, jnp.max(s, axis=0, keepdims=True))
            alpha = jnp.exp2(m_old - m_new)
            pexp = jnp.exp2(s - m_new).astype(BF16)
            acc_ref[h] = acc_ref[h] * alpha + jnp.dot(vaugs[j][n], pexp, preferred_element_type=F32)
            m_ref[h:h + 1, :] = m_new
        return ties

    ties = lax.fori_loop(0, nquads, lambda p, tb: attn_chunks([4 * p, 4 * p + 1, 4 * p + 2, 4 * p + 3], tb),
                         jnp.zeros((1, tq), F32))
    ties = lax.cond(nkc - tail >= 2, lambda tb: attn_chunks([tail, tail + 1], tb), lambda tb: tb, ties)

    @pl.when(nkc % 2 == 1)
    def _():
        attn_chunks([nkc - 1], ties)

    for p in range(H_B // 2):
        halves = []
        for h in (2 * p, 2 * p + 1):
            acc = acc_ref[h]
            halves.append(acc[:HD_B] / acc[HD_B:HD_B + 1])
        o_ref[:, 128 * p:128 * (p + 1)] = jnp.concatenate(halves, axis=0).T.astype(BF16)


def _dsa_prompt(qit, wit, qtp, kibf, kbf, vt, *, b, t, tq, k_top):
    kc = tq
    nq = t // tq
    nkc = t // kc
    whole = lambda *s: pl.BlockSpec((1,) + s, lambda i, j: (i,) + (0,) * len(s))
    return pl.pallas_call(
        functools.partial(_dsa_body, tq=tq, kc=kc, k_top=k_top),
        out_shape=jax.ShapeDtypeStruct((b * t, H_B * HD_B), BF16),
        grid=(b, nq),
        in_specs=[pl.BlockSpec((1, H_IDX, D_IDX, tq), lambda i, j: (i, 0, 0, j)),
                  pl.BlockSpec((1, H_IDX, tq), lambda i, j: (i, 0, j)),
                  pl.BlockSpec((1, H_B, 128, tq), lambda i, j: (i, 0, 0, j)),
                  whole(nkc, kc, D_IDX), whole(nkc, kc, 128), whole(nkc, 128, kc)],
        out_specs=pl.BlockSpec((tq, H_B * HD_B), lambda i, j: (i * nq + j, 0)),
        scratch_shapes=[pltpu.VMEM((nkc, kc, tq), F32), pltpu.VMEM((nkc, kc, tq), BF16),
                        pltpu.VMEM((H_B, tq), F32), pltpu.VMEM((H_B, _ACC_ROWS, tq), F32)],
        compiler_params=_cparams(2),
        name="dsa_prompt",
    )(qit, wit, qtp, kibf.reshape(b, nkc, kc, D_IDX), kbf.reshape(b, nkc, kc, 128), vt)


def _merge_body(x_ref, oa_ref, ob_ref, g1_ref, wg_ref, woa_ref, wob_ref, wo_ref, h_ref):
    x = x_ref[...]
    xn = _rms_rows(x, g1_ref[...]).astype(BF16)
    gates = jax.nn.sigmoid(jnp.dot(xn, wg_ref[...], preferred_element_type=F32))
    a = jnp.dot(oa_ref[...], woa_ref[...], preferred_element_type=F32)
    bb = jnp.dot(ob_ref[...], wob_ref[...], preferred_element_type=F32)
    mix = gates[:, :D_MODEL] * a + gates[:, D_MODEL:] * bb
    h_ref[...] = x + jnp.dot(mix.astype(BF16), wo_ref[...], preferred_element_type=F32)


def _merge(x, oa, ob, wob, w, *, tm):
    m, d = x.shape
    nb = ob.shape[1]
    row = lambda n: pl.BlockSpec((tm, n), lambda i: (i, 0))
    return pl.pallas_call(
        _merge_body,
        out_shape=jax.ShapeDtypeStruct((m, d), F32),
        grid=(m // tm,),
        in_specs=[row(d), row(512), row(nb), _const_spec((1, d)), _const_spec((d, 2 * d)),
                  _const_spec((512, d)), _const_spec((nb, d)), _const_spec((d, d))],
        out_specs=row(d),
        compiler_params=_cparams(1),
        name="merge",
    )(x, oa, ob, w["g1"], w["wg"], w["woa"], wob, w["wo"])


def _mlp_body(h_ref, g2_ref, wup_ref, wdn_ref, y_ref):
    h = h_ref[...]
    hn = _rms_rows(h, g2_ref[...]).astype(BF16)
    up = jnp.dot(hn, wup_ref[...], preferred_element_type=F32)
    act = jnp.square(jnp.maximum(up, 0.0)).astype(BF16)
    y_ref[...] = h + jnp.dot(act, wdn_ref[...], preferred_element_type=F32)


def _mlp(h, w, *, tm):
    m, d = h.shape
    row = pl.BlockSpec((tm, d), lambda i: (i, 0))
    return pl.pallas_call(
        _mlp_body,
        out_shape=jax.ShapeDtypeStruct((m, d), F32),
        grid=(m // tm,),
        in_specs=[row, _const_spec((1, d)), _const_spec((d, D_FF)), _const_spec((D_FF, d))],
        out_specs=row,
        compiler_params=_cparams(1),
        name="mlp",
    )(h, w["g2"], w["wup"], w["wdn"])


def _proj_decode_body(x_ref, st_ref, g1_ref, wqkv_ref, wz_ref, wbq_ref, wkv_ref, wiq_ref, wsm_ref,
                      convw_ref, qng_ref, kng_ref, alog_ref, dtb_ref, ones_ref, seg_ref, cos_ref, sin_ref,
                      u_out, q_out, k_out, v_out, z_out, qb_out, kv_out, qi_out, sm_out, ss_out):
    x = x_ref[...]
    xn = _rms_rows(x, g1_ref[...]).astype(BF16)
    u = jnp.dot(xn, wqkv_ref[...], preferred_element_type=F32)
    u_out[...] = u
    cw = convw_ref[...]
    acc = st_ref[0] * cw[0:1] + st_ref[1] * cw[1:2] + st_ref[2] * cw[2:3] + u * cw[3:4]
    c = _silu(acc)
    for h in range(H_A):
        qh = c[:, h * DK_A:(h + 1) * DK_A]
        kh = c[:, (H_A + h) * DK_A:(H_A + h + 1) * DK_A]
        q_out[:, h * DK_A:(h + 1) * DK_A] = qh * (lax.rsqrt(jnp.sum(qh * qh, -1, keepdims=True) + EPS)
                                                 * (DK_A ** -0.5))
        k_out[:, h * DK_A:(h + 1) * DK_A] = kh * lax.rsqrt(jnp.sum(kh * kh, -1, keepdims=True) + EPS)
    v_out[...] = c[:, 2 * H_A * DK_A:]
    z_out[...] = jnp.dot(xn, wz_ref[...], preferred_element_type=F32)
    cos, sin = cos_ref[...], sin_ref[...]
    ones_bd = ones_ref[...]
    bq = jnp.dot(xn, wbq_ref[...], preferred_element_type=F32)
    qb = _rope_rows(_seg_rms(bq, ones_bd, qng_ref[...]), cos, sin)
    qb_out[...] = (qb * (HD_B ** -0.5)).astype(BF16)
    kv = jnp.dot(xn, wkv_ref[...], preferred_element_type=F32)
    kb = _rope_rows(_seg_rms(kv[:, :128], ones_bd[:128, :128], kng_ref[...]), cos, sin)
    kv_out[:, 0:128] = kb
    kv_out[:, 128:256] = kv[:, 128:]
    iq = jnp.dot(xn, wiq_ref[...], preferred_element_type=F32)
    qi = (_rope_rows(iq, cos, sin) * (D_IDX ** -0.5)).astype(BF16)
    qi_out[...] = qi
    sm = jnp.dot(xn, wsm_ref[...], preferred_element_type=F32)
    smo = _small_block(sm, cos, sin, alog_ref[...], dtb_ref[...])
    sm_out[...] = smo
    lane = lax.broadcasted_iota(I32, smo.shape, 1)
    ki = smo.astype(BF16).astype(F32)
    ki2 = jnp.where(lane < D_IDX, ki, pltpu.roll(ki, D_IDX, 1))
    prod = qi.astype(F32) * jnp.concatenate([ki2] * 4, axis=-1)
    sh = _mm_f32(prod, seg_ref[...])
    contrib = jnp.where(jnp.logical_and(lane >= _SM_WI, lane < _SM_WI + H_IDX),
                        jnp.maximum(sh, 0.0) * smo, 0.0)
    ss_out[...] = jnp.broadcast_to(jnp.sum(contrib, axis=-1, keepdims=True), ss_out.shape)


def _proj_decode(x, st, w, tabs):
    r, d = x.shape
    outs = [(r, CONV_CH, F32), (r, 512, F32), (r, 512, F32), (r, 512, F32), (r, 512, F32), (r, 512, BF16),
            (r, 256, F32), (r, 512, BF16), (r, 128, F32), (r, 128, F32)]
    args = (x, st, w["g1"], w["wqkv"], w["wz"], w["wbq"], w["wkv"], w["wiq"], w["wsm"], w["convw"],
            w["qng"], w["kng"], w["alog_row"], w["dtb_row"], w["ones_bd"], w["seg_sum"], tabs["cos"], tabs["sin"])
    return pl.pallas_call(
        _proj_decode_body,
        out_shape=[jax.ShapeDtypeStruct((a, n), dt) for a, n, dt in outs],
        compiler_params=pltpu.CompilerParams(vmem_limit_bytes=VMEM_LIMIT),
        name="proj_decode",
    )(*args)


def _gdn_decode_body(qc_ref, kc_ref, v_ref, z_ref, sm_ref, gg_ref, s_ref, o_ref, so_ref):
    sm = sm_ref[0]
    gg = gg_ref[...]
    for h in range(H_A):
        g = sm[:, _SM_G + h:_SM_G + h + 1]
        beta = sm[:, _SM_BETA + h:_SM_BETA + h + 1]
        kcol, qcol = kc_ref[0, h], qc_ref[0, h]
        v = v_ref[0, h]
        s = s_ref[0, h] * jnp.exp(g)
        vn = beta * (v - jnp.sum(s * kcol, axis=0, keepdims=True))
        s = s + kcol * vn
        so_ref[0, h] = s
        o = jnp.sum(s * qcol, axis=0, keepdims=True)
        o_ref[0, h] = (_rms_rows(o, gg) * _silu(z_ref[0, h])).astype(BF16)


def _gdn_decode(q, k, v, z, sm, gg, state):
    r = q.shape[0]
    col = lambda a: a.reshape(r, H_A, DK_A, 1)
    rowv = lambda a: a.reshape(r, H_A, 1, DV_A)
    cspec = pl.BlockSpec((1, H_A, DK_A, 1), lambda i: (i, 0, 0, 0))
    rspec = pl.BlockSpec((1, H_A, 1, DV_A), lambda i: (i, 0, 0, 0))
    sspec = pl.BlockSpec((1, H_A, DK_A, DV_A), lambda i: (i, 0, 0, 0))
    o, s_new = pl.pallas_call(
        _gdn_decode_body,
        out_shape=[jax.ShapeDtypeStruct((r, H_A, 1, DV_A), BF16),
                   jax.ShapeDtypeStruct((r, H_A, DK_A, DV_A), F32)],
        grid=(r,),
        in_specs=[cspec, cspec, rspec, rspec, pl.BlockSpec((1, 1, 128), lambda i: (i, 0, 0)),
                  _const_spec((1, DV_A)), sspec],
        out_specs=[rspec, sspec],
        compiler_params=_cparams(1),
        name="gdn_decode",
    )(col(q), col(k), rowv(v), rowv(z), sm.reshape(r, 1, 128), gg, state)
    return o.reshape(r, H_A * DV_A), s_new


def _page_copies(pt_ref, pages_hbm, buf, sem, row, first, count, slot):
    return [pltpu.make_async_copy(pages_hbm.at[pt_ref[row, first + p]], buf.at[slot, p], sem.at[slot])
            for p in range(count)]


def _idx_decode_body(pt_ref, qi_ref, w_ref, pages_hbm, out_ref, buf, sem, *, n_pages):
    i = pl.program_id(0)
    slot = i % 2
    copies = functools.partial(_page_copies, pt_ref, pages_hbm, buf, sem)

    @pl.when(i == 0)
    def _():
        for cp in copies(0, 0, n_pages, 0):
            cp.start()

    @pl.when(i + 1 < pl.num_programs(0))
    def _():
        for cp in copies(i + 1, 0, n_pages, 1 - slot):
            cp.start()

    for cp in copies(i, 0, n_pages, slot):
        cp.wait()
    qi = qi_ref[0]
    w = w_ref[0]
    for p in range(n_pages):
        s = _mm(qi, buf[slot, p])
        out_ref[0, p:p + 1, :] = jnp.sum(jnp.maximum(s, 0.0) * w, axis=0, keepdims=True)


def _idx_decode(page_table, qi, wi_col, cache_kidx):
    r, n_pages = page_table.shape
    grid_spec = pltpu.PrefetchScalarGridSpec(
        num_scalar_prefetch=1,
        grid=(r,),
        in_specs=[pl.BlockSpec((1, H_IDX, D_IDX), lambda i, pt: (i, 0, 0)),
                  pl.BlockSpec((1, H_IDX, 1), lambda i, pt: (i, 0, 0)),
                  pl.BlockSpec(memory_space=pl.ANY)],
        out_specs=pl.BlockSpec((1, n_pages, PAGE_SIZE), lambda i, pt: (i, 0, 0)),
        scratch_shapes=[pltpu.VMEM((2, n_pages, D_IDX, PAGE_SIZE), F32), pltpu.SemaphoreType.DMA((2,))],
    )
    return pl.pallas_call(
        functools.partial(_idx_decode_body, n_pages=n_pages),
        out_shape=jax.ShapeDtypeStruct((r, n_pages, PAGE_SIZE), F32),
        grid_spec=grid_spec,
        compiler_params=_cparams(1),
        name="idx_decode",
    )(page_table, qi.reshape(r, H_IDX, D_IDX), wi_col, cache_kidx)


def _select_decode_body(sc_ref, ss_ref, bias_ref, bself_ref, *, k_top, cw):
    r, n = sc_ref.shape
    nch = n // cw
    key_ref = sc_ref
    skey = ss_ref[:, 0:1]

    def lane_sum(f):
        def body(c, acc):
            off = pl.multiple_of(c * cw, cw)
            kk = key_ref[:, pl.ds(off, cw)]
            for j in range(cw // 128):
                acc = acc + f(kk[:, 128 * j:128 * (j + 1)], off + 128 * j)
            return acc
        acc = lax.fori_loop(0, nch, body, jnp.zeros((r, 128), F32))
        return jnp.sum(acc, axis=-1, keepdims=True).astype(I32)

    def count_ge(cand):
        cb = jnp.broadcast_to(cand, (r, 128))
        return lane_sum(lambda kk, off: jnp.where(kk >= cb, 1.0, 0.0)) + (skey >= cand).astype(I32)

    def count_gt_eq(t):
        tb = jnp.broadcast_to(t, (r, 128))
        return (lane_sum(lambda kk, off: jnp.where(kk > tb, 1.0, 0.0)) + (skey > t).astype(I32),
                lane_sum(lambda kk, off: jnp.where(kk == tb, 1.0, 0.0)) + (skey == t).astype(I32))

    def count_eq_below(t, j):
        tb = jnp.broadcast_to(t, (r, 128))
        jb = jnp.broadcast_to(j, (r, 128))
        lane = lax.broadcasted_iota(I32, (r, 128), 1)
        return (lane_sum(lambda kk, off: jnp.where(kk == tb, jnp.where(lane + off < jb, 1.0, 0.0), 0.0))
                + jnp.logical_and(skey == t, j > n).astype(I32))

    t, x = _topk_threshold(count_ge, count_gt_eq, count_eq_below, (r, 1), k_top, 15)
    tb = jnp.broadcast_to(t, (r, cw))
    xb = jnp.broadcast_to(x, (r, cw))

    def bias_chunk(c, carry):
        off = pl.multiple_of(c * cw, cw)
        kk = key_ref[:, pl.ds(off, cw)]
        idx = lax.broadcasted_iota(I32, (r, cw), 1) + off
        tie = jnp.where(kk == tb, jnp.where(idx <= xb, 0.0, NEG_BIG), NEG_BIG)
        bias_ref[:, pl.ds(off, cw)] = jnp.where(kk > tb, 0.0, tie)
        return carry

    lax.fori_loop(0, nch, bias_chunk, 0)
    tie_self = jnp.where(skey == t, jnp.where(x >= n, 0.0, NEG_BIG), NEG_BIG)
    bself_ref[...] = jnp.broadcast_to(jnp.where(skey > t, 0.0, tie_self), bself_ref.shape)


def _select_decode(scores, sself, *, k_top):
    r, n = scores.shape
    return pl.pallas_call(
        functools.partial(_select_decode_body, k_top=k_top, cw=512),
        out_shape=[jax.ShapeDtypeStruct((r, n), F32), jax.ShapeDtypeStruct((r, 128), F32)],
        compiler_params=pltpu.CompilerParams(vmem_limit_bytes=VMEM_LIMIT),
        name="select_decode",
    )(scores, sself)


def _attn_decode_body(pt_ref, qp_ref, bias_ref, kvs_ref, bself_ref, pages_hbm, o_ref,
                      m_ref, l_ref, acc_ref, buf, sem, *, npg, nsteps):
    i, j = pl.program_id(0), pl.program_id(1)
    step = i * nsteps + j
    slot = step % 2
    copies = functools.partial(_page_copies, pt_ref, pages_hbm, buf, sem)

    @pl.when(step == 0)
    def _():
        for cp in copies(0, 0, npg, 0):
            cp.start()

    @pl.when(step + 1 < pl.num_programs(0) * nsteps)
    def _():
        nxt = step + 1
        for cp in copies(nxt // nsteps, (nxt % nsteps) * npg, npg, 1 - slot):
            cp.start()

    @pl.when(j == 0)
    def _():
        m_ref[...] = jnp.full(m_ref.shape, NEG_BIG, F32)
        l_ref[...] = jnp.zeros(l_ref.shape, F32)
        acc_ref[...] = jnp.zeros(acc_ref.shape, F32)

    for cp in copies(i, j * npg, npg, slot):
        cp.wait()
    qp = qp_ref[0]

    ss = [_mm(qp, buf[slot, p, 0:128, :]) + bias_ref[0, :, PAGE_SIZE * p:PAGE_SIZE * (p + 1)]
          for p in range(npg)]
    m_old = m_ref[...]
    m_new = m_old
    for s in ss:
        m_new = jnp.maximum(m_new, jnp.max(s, axis=-1, keepdims=True))
    alpha = jnp.exp(m_old - m_new)
    l_new = l_ref[...] * alpha
    acc_new = acc_ref[...] * alpha
    for p in range(npg):
        pe = jnp.exp(ss[p] - m_new)
        l_new = l_new + jnp.sum(pe, axis=-1, keepdims=True)
        acc_new = acc_new + _mm_nt(pe, buf[slot, p, 128:256, :])
    m_ref[...] = m_new
    l_ref[...] = l_new
    acc_ref[...] = acc_new

    @pl.when(j == nsteps - 1)
    def _():
        kvs = kvs_ref[0]
        krow = kvs[:, :128].astype(BF16).astype(F32)
        s = jnp.sum(qp.astype(F32) * krow, axis=-1, keepdims=True) + bself_ref[0][:, 0:1]
        m_old = m_ref[...]
        m_new = jnp.maximum(m_old, s)
        alpha = jnp.exp(m_old - m_new)
        p = jnp.exp(s - m_new)
        l = l_ref[...] * alpha + p
        vrow = kvs[:, 128:].astype(BF16).astype(F32)
        acc = acc_ref[...] * alpha + p.astype(BF16).astype(F32) * vrow
        out = acc / l
        rowi = lax.broadcasted_iota(I32, out.shape, 0)
        out = jnp.where(rowi < 4, out, pltpu.roll(out, HD_B, 1))
        o_ref[0] = out.astype(BF16)


def _attn_decode(page_table, qpad, bias, kv_self, bself, cache_kv, *, npg):
    r, n_pages = page_table.shape
    nsteps = n_pages // npg
    page_rows = 2 * N_KV_B * HD_B
    grid_spec = pltpu.PrefetchScalarGridSpec(
        num_scalar_prefetch=1,
        grid=(r, nsteps),
        in_specs=[pl.BlockSpec((1, H_B, 128), lambda i, j, pt: (i, 0, 0)),
                  pl.BlockSpec((1, 1, PAGE_SIZE * npg), lambda i, j, pt: (i, 0, j)),
                  pl.BlockSpec((1, 1, 256), lambda i, j, pt: (i, 0, 0)),
                  pl.BlockSpec((1, 1, 128), lambda i, j, pt: (i, 0, 0)),
                  pl.BlockSpec(memory_space=pl.ANY)],
        out_specs=pl.BlockSpec((1, H_B, 128), lambda i, j, pt: (i, 0, 0)),
        scratch_shapes=[pltpu.VMEM((H_B, 1), F32), pltpu.VMEM((H_B, 1), F32), pltpu.VMEM((H_B, 128), F32),
                        pltpu.VMEM((2, npg, page_rows, PAGE_SIZE), F32), pltpu.SemaphoreType.DMA((2,))],
    )
    return pl.pallas_call(
        functools.partial(_attn_decode_body, npg=npg, nsteps=nsteps),
        out_shape=jax.ShapeDtypeStruct((r, H_B, 128), BF16),
        grid_spec=grid_spec,
        compiler_params=_cparams(2),
        name="attn_decode",
    )(page_table, qpad, bias.reshape(r, 1, -1), kv_self.reshape(r, 1, 256), bself.reshape(r, 1, 128), cache_kv)


def _prep_weights(norm1_g, w_in, conv_w, a_log, dt_bias, gdn_norm_g, q_norm_g, k_norm_g,
                  w_out_a, w_out_b, w_o, norm2_g, w_up, w_down):
    o = _OFF
    wb = w_in.astype(BF16)
    d = w_in.shape[0]
    wsm = jnp.concatenate([wb[:, o["ik"]:o["iw"]], wb[:, o["ab"]:o["bq"]], wb[:, o["iw"]:o["ga"]],
                           jnp.zeros((d, 128 - 80), BF16)], axis=1)

    def lanes(vals, start):
        return jnp.zeros((1, 128), F32).at[0, start:start + vals.shape[0]].set(vals.astype(F32))

    seg = np.zeros((512, 128), np.float32)
    for h in range(H_IDX):
        seg[64 * h:64 * (h + 1), _SM_WI + h] = 1.0
    ones_bd = np.kron(np.eye(8, dtype=np.float32), np.ones((64, 64), np.float32))
    wob = w_out_b.astype(BF16).reshape(H_B, HD_B, d)
    wob_pad = jnp.concatenate([wob, jnp.zeros_like(wob)], axis=1).reshape(H_B * 128, d)
    col8 = lambda v: jnp.concatenate([jnp.zeros((4,), F32), v.astype(F32)]).reshape(8, 1)
    return dict(
        g1=norm1_g.reshape(1, d).astype(F32),
        wqkv=wb[:, o["aq"]:o["az"]], wz=wb[:, o["az"]:o["ab"]], wbq=wb[:, o["bq"]:o["bk"]],
        wkv=wb[:, o["bk"]:o["iq"]], wiq=wb[:, o["iq"]:o["ik"]], wsm=wsm,
        wbqt=wb[:, o["bq"]:o["bk"]].T, wiqt=wb[:, o["iq"]:o["ik"]].T, wvt=wb[:, o["bv"]:o["iq"]].T,
        wsmt=wsm.T, wg=wb[:, o["ga"]:o["end"]],
        convw=conv_w.astype(F32),
        qng=jnp.tile(q_norm_g.astype(F32), H_B).reshape(1, 512),
        kng=jnp.tile(k_norm_g.astype(F32), N_KV_B).reshape(1, 128),
        qngt=jnp.tile(q_norm_g.astype(F32).reshape(HD_B, 1), (1, 128)),
        alog_row=lanes(a_log, _SM_G), dtb_row=lanes(dt_bias, _SM_G),
        alog_col=col8(a_log), dtb_col=col8(dt_bias),
        ones_bd=jnp.asarray(ones_bd, BF16), ones_bd128=jnp.asarray(ones_bd[:128, :128], BF16),
        seg_sum=jnp.asarray(seg),
        gg=gdn_norm_g.reshape(1, DV_A).astype(F32),
        woa=w_out_a.astype(BF16), wob=w_out_b.astype(BF16), wob_pad=wob_pad, wo=w_o.astype(BF16),
        g2=norm2_g.reshape(1, d).astype(F32), wup=w_up.astype(BF16), wdn=w_down.astype(BF16),
    )


def _rope_tables(pos):
    half = HD_B // 2
    inv = ROPE_THETA ** (-jnp.arange(half, dtype=F32) / half)
    ang = pos.astype(F32)[:, None] * inv[None, :]
    cos, sin = jnp.cos(ang), jnp.sin(ang)
    return dict(cos=jnp.tile(cos, (1, 4)), sin=jnp.tile(jnp.concatenate([-sin, sin], axis=1), (1, 2)),
                cos_t=cos.T, sin_t=sin.T)


def _layer_prompt(x, w, *, tm=512, tq=256, nb=8):
    b, t, d = x.shape
    tabs = _rope_tables(jnp.arange(t))
    (q, k, v, z, kv, kbf, kibf, sm, kidx, qtp, qit, vt, gt, wit, conv_new) = _proj_prompt(x, w, tabs, tm=tm, kc=tq)
    oa, ssm = _gdn_prompt(q, k, v, z, sm, gt, w["gg"], b=b, t=t, nb=nb)
    ob = _dsa_prompt(qit, wit, qtp, kibf, kbf, vt, b=b, t=t, tq=tq, k_top=min(TOPK_MAX, t // 4))
    h = _merge(x.reshape(b * t, d), oa.reshape(b * t, 512), ob, w["wob"], w, tm=tm)
    y = _mlp(h, w, tm=tm)
    return (y.reshape(b, t, d), kv.reshape(b, t, 2, N_KV_B, HD_B), kidx.reshape(b, t, D_IDX), ssm, conv_new)


def _layer_decode(x, w, state_ssm, state_conv, cache_kv, cache_kidx, page_table, *, npg=32):
    r, d = x.shape
    n_pages = page_table.shape[1]
    past = n_pages * PAGE_SIZE
    tabs = _rope_tables(jnp.full((1,), past))
    st = jnp.swapaxes(state_conv, 0, 1)
    (u, q, k, v, z, qb, kv, qi, sm, sself) = _proj_decode(x, st, w, tabs)
    oa, ssm = _gdn_decode(q, k, v, z, sm, w["gg"], state_ssm)
    wi_col = sm[:, _SM_WI:_SM_WI + H_IDX].reshape(r, H_IDX, 1)
    kidx_pages = jnp.swapaxes(cache_kidx, 1, 2)
    kv_pages = jnp.transpose(cache_kv, (0, 2, 3, 4, 1)).reshape(-1, 2 * N_KV_B * HD_B, PAGE_SIZE)
    scores = _idx_decode(page_table, qi, wi_col, kidx_pages).reshape(r, past)
    bias, bself = _select_decode(scores, sself, k_top=min(TOPK_MAX, (past + 1) // 4))
    qh = qb.reshape(r, H_B, HD_B)
    zq = jnp.zeros_like(qh[:, :4])
    qpad = jnp.concatenate([jnp.concatenate([qh[:, :4], zq], axis=-1),
                            jnp.concatenate([zq, qh[:, 4:]], axis=-1)], axis=1)
    ob = _attn_decode(page_table, qpad, bias, kv, bself, kv_pages, npg=npg)
    h = _merge(x, oa, ob.reshape(r, H_B * 128), w["wob_pad"], w, tm=r)
    y = _mlp(h, w, tm=r)
    conv_new = jnp.concatenate([state_conv[:, 1:], u[:, None, :]], axis=1)
    return (y, kv.reshape(r, 1, 2, N_KV_B, HD_B), sm[:, :D_IDX].reshape(r, 1, D_IDX), ssm, conv_new)


def kernel(x_prompt, x_sample, cache_kv, cache_kidx, page_table, state_ssm, state_conv, norm1_g, w_in, conv_w,
           a_log, dt_bias, gdn_norm_g, q_norm_g, k_norm_g, w_out_a, w_out_b, w_o, norm2_g, w_up, w_down):
    assert w_in.shape[0] == 1, "single-layer trunk"
    w = _prep_weights(norm1_g[0], w_in[0], conv_w[0], a_log[0], dt_bias[0], gdn_norm_g[0], q_norm_g[0],
                      k_norm_g[0], w_out_a[0], w_out_b[0], w_o[0], norm2_g[0], w_up[0], w_down[0])
    yp, kv_p, ki_p, ssm_p, cv_p = _layer_prompt(x_prompt, w)
    ys, kv_s, ki_s, ssm_s, cv_s = _layer_decode(x_sample[:, 0], w, state_ssm[0], state_conv[0],
                                                cache_kv[0], cache_kidx[0], page_table)
    return (yp, ys[:, None, :], kv_p[None], ki_p[None], ssm_p[None], cv_p[None],
            kv_s[None], ki_s[None], ssm_s[None], cv_s[None])
```

```python
import functools
import math

import jax
import jax.numpy as jnp
import numpy as np
from jax import lax
from jax.experimental import pallas as pl
from jax.experimental.pallas import tpu as pltpu

F32 = jnp.float32
BF16 = jnp.bfloat16
I32 = jnp.int32

D_MODEL = 1024
H_A, DK_A, DV_A = 4, 128, 128
CONV_W = 4
CONV_CH = H_A * (2 * DK_A + DV_A)
CHUNK = 64
HD_B, H_B, N_KV_B = 64, 8, 2
H_IDX, D_IDX = 8, 64
TOPK_MAX = 256
PAGE_SIZE = 128
ROPE_THETA = 10000.0
D_FF = 4 * D_MODEL
EPS = 1e-6

_OFF = dict(aq=0, ak=512, av=1024, az=1536, ab=2048, aa=2052, bq=2056, bk=2568, bv=2696,
            iq=2824, ik=3336, iw=3400, ga=3408, gb=4432, end=5456)
_SM_BETA, _SM_G, _SM_WI = 64, 68, 72

LOG2_E = math.log2(math.e)
NEG_BIG = -1e30
INT_MIN = -(2 ** 31)
NEG_INF_KEY = (0xFF800000 ^ 0x7FFFFFFF) - 2 ** 32

VMEM_LIMIT = 56 * 1024 * 1024
HIGHEST = lax.Precision.HIGHEST


def _cparams(n_axes):
    return pltpu.CompilerParams(dimension_semantics=("arbitrary",) * n_axes,
                                vmem_limit_bytes=VMEM_LIMIT)


def _const_spec(shape):
    nd = len(shape)
    return pl.BlockSpec(shape, lambda *a: (0,) * nd, pipeline_mode=pl.Buffered(1))


def _mm(a, b):
    return jnp.dot(a.astype(BF16), b.astype(BF16), preferred_element_type=F32)


def _mm_nt(a, b):
    return lax.dot_general(a.astype(BF16), b.astype(BF16), (((1,), (1,)), ((), ())),
                           preferred_element_type=F32)


def _mm_tn(a, b):
    return lax.dot_general(a.astype(BF16), b.astype(BF16), (((0,), (0,)), ((), ())),
                           preferred_element_type=F32)


def _mm_f32(a, b):
    return jnp.dot(a, b, preferred_element_type=F32, precision=HIGHEST)


def _rms_rows(x, g):
    return x * lax.rsqrt(jnp.mean(x * x, axis=-1, keepdims=True) + EPS) * g


def _silu(x):
    return x * jax.nn.sigmoid(x)


def _softplus(x):
    return jnp.maximum(x, 0.0) + jnp.log1p(jnp.exp(-jnp.abs(x)))


def _rope_rows(x, cos, sin_signed):
    n = x.shape[-1]
    reps = n // 128
    if reps > 1:
        cos = jnp.concatenate([cos] * reps, axis=-1)
        sin_signed = jnp.concatenate([sin_signed] * reps, axis=-1)
    lane = lax.broadcasted_iota(I32, x.shape, x.ndim - 1)
    first_half = (lane % HD_B) < (HD_B // 2)
    partner = jnp.where(first_half, pltpu.roll(x, n - HD_B // 2, x.ndim - 1),
                        pltpu.roll(x, HD_B // 2, x.ndim - 1))
    return x * cos + partner * sin_signed


def _rope_cols(x, cos_t, sin_t):
    x1, x2 = x[:32], x[32:]
    return jnp.concatenate([x1 * cos_t - x2 * sin_t, x2 * cos_t + x1 * sin_t], axis=0)


def _seg_rms(x, ones_bd, g):
    sq = x * x
    hi = sq.astype(BF16)
    lo = (sq - hi.astype(F32)).astype(BF16)
    ss = (jnp.dot(hi, ones_bd, preferred_element_type=F32)
          + jnp.dot(lo, ones_bd, preferred_element_type=F32))
    return x * lax.rsqrt(ss * (1.0 / HD_B) + EPS) * g


def _small_block(sm, cos, sin_signed, alog_row, dtb_row):
    lane = lax.broadcasted_iota(I32, sm.shape, 1)
    roped = _rope_rows(sm, cos, sin_signed)
    beta = jax.nn.sigmoid(sm)
    g = -jnp.exp(alog_row) * _softplus(sm + dtb_row)
    wi = sm * (H_IDX ** -0.5)
    out = jnp.where(lane < _SM_BETA, roped,
                    jnp.where(lane < _SM_G, beta, jnp.where(lane < _SM_WI, g, wi)))
    return out


def _proj_prompt_body(x_ref, g1_ref, wqkv_ref, wz_ref, wkv_ref, wsm_ref,
                      wbqt_ref, wiqt_ref, wvt_ref, wsmt_ref, convw_ref, kng_ref, qngt_ref, alog_ref, dtb_ref,
                      alogt_ref, dtbt_ref, ones_ref, cos_ref, sin_ref, cost_ref, sint_ref,
                      q_out, k_out, v_out, z_out, kv_out, kbf_out, kibf_out, sm_out, kidx_out,
                      qtp_out, qit_out, vt_out, gt_out, wit_out, conv_out, ubuf, *, tm, kc):
    @pl.when(pl.program_id(1) == 0)
    def _():
        ubuf[0:8, :] = jnp.zeros((8, CONV_CH), F32)

    x = x_ref[0]
    xn = _rms_rows(x, g1_ref[...]).astype(BF16)

    nt_dims = (((1,), (1,)), ((), ()))
    u = jnp.dot(xn, wqkv_ref[...], preferred_element_type=F32)
    z = jnp.dot(xn, wz_ref[...], preferred_element_type=F32)
    kv = jnp.dot(xn, wkv_ref[...], preferred_element_type=F32)
    sm = jnp.dot(xn, wsm_ref[...], preferred_element_type=F32)
    qt = lax.dot_general(wbqt_ref[...], xn, nt_dims, preferred_element_type=F32)
    qit = lax.dot_general(wiqt_ref[...], xn, nt_dims, preferred_element_type=F32)
    vt = lax.dot_general(wvt_ref[...], xn, nt_dims, preferred_element_type=F32)
    smt = lax.dot_general(wsmt_ref[...], xn, nt_dims, preferred_element_type=F32)

    ubuf[8:tm + 8, :] = u
    cw = convw_ref[...]
    acc = (ubuf[5:5 + tm, :] * cw[0:1] + ubuf[6:6 + tm, :] * cw[1:2]
           + ubuf[7:7 + tm, :] * cw[2:3] + u * cw[3:4])
    conv_out[0] = ubuf[tm + 5:tm + 8, :]
    ubuf[0:8, :] = ubuf[tm:tm + 8, :]
    c = _silu(acc)
    for h in range(H_A):
        qh = c[:, h * DK_A:(h + 1) * DK_A]
        kh = c[:, (H_A + h) * DK_A:(H_A + h + 1) * DK_A]
        q_out[:, h * DK_A:(h + 1) * DK_A] = qh * (lax.rsqrt(jnp.sum(qh * qh, -1, keepdims=True) + EPS)
                                                 * (DK_A ** -0.5))
        k_out[:, h * DK_A:(h + 1) * DK_A] = kh * lax.rsqrt(jnp.sum(kh * kh, -1, keepdims=True) + EPS)
    v_out[...] = c[:, 2 * H_A * DK_A:]
    z_out[...] = z

    cos, sin = cos_ref[...], sin_ref[...]
    ones_bd = ones_ref[...]

    kb =_rope_rows(_seg_rms(kv[:, :128], ones_bd, kng_ref[...]), cos, sin)
    kv_out[:, 0:128] = kb
    kv_out[:, 128:256] = kv[:, 128:]
    kbf_out[...] = kb.astype(BF16)

    smo = _small_block(sm, cos, sin, alog_ref[...], dtb_ref[...])
    sm_out[...] = smo
    kidx_out[...] = smo[:, :D_IDX]
    kibf_out[...] = smo[:, :D_IDX].astype(BF16)

    cos_t, sin_t = cost_ref[...], sint_ref[...]
    qg =jnp.concatenate([qngt_ref[...]] * (tm // 128), axis=1)
    zero64 = jnp.zeros((HD_B, tm), BF16)
    for h in range(H_B):
        qh = qt[HD_B * h:HD_B * (h + 1)]
        qh = qh * lax.rsqrt(jnp.mean(qh * qh, axis=0, keepdims=True) + EPS) * qg
        qh = (_rope_cols(qh, cos_t, sin_t) * (HD_B ** -0.5 * LOG2_E)).astype(BF16)
        n = h // (H_B // N_KV_B)
        qtp_out[0, h, HD_B * n:HD_B * (n + 1), :] = qh
        qtp_out[0, h, HD_B * (1 - n):HD_B * (2 - n), :] = zero64
        qit_out[0, h] = (_rope_cols(qit[D_IDX * h:D_IDX * (h + 1)], cos_t, sin_t) * (D_IDX ** -0.5)).astype(BF16)
    vtb = vt.astype(BF16)
    for j in range(tm // kc):
        vt_out[0, j] = vtb[:, j * kc:(j + 1) * kc]
    sm8 = smt[64:72]
    row = lax.broadcasted_iota(I32, sm8.shape, 0)
    g8 = -jnp.exp(alogt_ref[...]) * _softplus(sm8 + dtbt_ref[...])
    gt_out[0] = jnp.where(row < 4, jax.nn.sigmoid(sm8), g8)
    wit_out[0] = smt[72:80] * (H_IDX ** -0.5)


def _proj_prompt(x, w, tabs, *, tm, kc):
    b, t, d = x.shape
    nt = t // tm
    m = b * t
    row = lambda n: pl.BlockSpec((tm, n), lambda i, j: (i * nt + j, 0))
    ins = [
        pl.BlockSpec((1, tm, d), lambda i, j: (i, j, 0)),
        _const_spec((1, d)),
        _const_spec((d, CONV_CH)), _const_spec((d, 512)), _const_spec((d, 256)), _const_spec((d, 128)),
        _const_spec((512, d)), _const_spec((512, d)), _const_spec((128, d)), _const_spec((128, d)),
        _const_spec((CONV_W, CONV_CH)), _const_spec((1, 128)), _const_spec((HD_B, 128)),
        _const_spec((1, 128)), _const_spec((1, 128)), _const_spec((8, 1)), _const_spec((8, 1)),
        _const_spec((128, 128)),
        pl.BlockSpec((tm, 128), lambda i, j: (j, 0)), pl.BlockSpec((tm, 128), lambda i, j: (j, 0)),
        pl.BlockSpec((32, tm), lambda i, j: (0, j)), pl.BlockSpec((32, tm), lambda i, j: (0, j)),
    ]
    nkc = t // kc
    outs = [
        (jax.ShapeDtypeStruct((m, 512), F32), row(512)),
        (jax.ShapeDtypeStruct((m, 512), F32), row(512)),
        (jax.ShapeDtypeStruct((m, 512), F32), row(512)),
        (jax.ShapeDtypeStruct((m, 512), F32), row(512)),
        (jax.ShapeDtypeStruct((m, 256), F32), row(256)),
        (jax.ShapeDtypeStruct((m, 128), BF16), row(128)),
        (jax.ShapeDtypeStruct((m, D_IDX), BF16), row(D_IDX)),
        (jax.ShapeDtypeStruct((m, 128), F32), row(128)),
        (jax.ShapeDtypeStruct((m, D_IDX), F32), row(D_IDX)),
        (jax.ShapeDtypeStruct((b, H_B, 128, t), BF16),
         pl.BlockSpec((1, H_B, 128, tm), lambda i, j: (i, 0, 0, j))),
        (jax.ShapeDtypeStruct((b, H_IDX, D_IDX, t), BF16),
         pl.BlockSpec((1, H_IDX, D_IDX, tm), lambda i, j: (i, 0, 0, j))),
        (jax.ShapeDtypeStruct((b, nkc, 128, kc), BF16),
         pl.BlockSpec((1, tm // kc, 128, kc), lambda i, j: (i, j, 0, 0))),
        (jax.ShapeDtypeStruct((b, 8, t), F32), pl.BlockSpec((1, 8, tm), lambda i, j: (i, 0, j))),
        (jax.ShapeDtypeStruct((b, H_IDX, t), F32), pl.BlockSpec((1, H_IDX, tm), lambda i, j: (i, 0, j))),
        (jax.ShapeDtypeStruct((b, CONV_W - 1, CONV_CH), F32),
         pl.BlockSpec((1, CONV_W - 1, CONV_CH), lambda i, j: (i, 0, 0))),
    ]
    return pl.pallas_call(
        functools.partial(_proj_prompt_body, tm=tm, kc=kc),
        out_shape=[o[0] for o in outs],
        grid=(b, nt),
        in_specs=ins,
        out_specs=[o[1] for o in outs],
        scratch_shapes=[pltpu.VMEM((tm + 8, CONV_CH), F32)],
        compiler_params=_cparams(2),
        name="proj_prompt",
    )(x, w["g1"], w["wqkv"], w["wz"], w["wkv"], w["wsm"], w["wbqt"], w["wiqt"], w["wvt"], w["wsmt"],
      w["convw"], w["kng"], w["qngt"], w["alog_row"], w["dtb_row"], w["alog_col"], w["dtb_col"],
      w["ones_bd128"], tabs["cos"], tabs["sin"], tabs["cos_t"], tabs["sin_t"])


def _split3(x):
    hi = x.astype(BF16)
    r1 = x - hi.astype(F32)
    mid = r1.astype(BF16)
    lo = (r1 - mid.astype(F32)).astype(BF16)
    return hi, mid, lo


def _gdn_body(q_ref, k_ref, v_ref, z_ref, sm_ref, gt_ref, gg_ref, bdk_ref, bdp_ref, o_ref, s_ref, *, nb):
    c = pl.program_id(1)

    @pl.when(c == 0)
    def _():
        s_ref[...] = jnp.zeros_like(s_ref)

    n, nh = CHUNK, H_A
    w4 = nh * n
    ri4 = lax.broadcasted_iota(I32, (n, w4), 0)
    ci4 = lax.broadcasted_iota(I32, (n, w4), 1) % n
    strict4, incl4 = ri4 > ci4, ri4 >= ci4
    eye4 = jnp.where(ri4 == ci4, 1.0, 0.0)
    triu4 = jnp.where(ri4 <= ci4, 1.0, 0.0).astype(BF16)
    ri = lax.broadcasted_iota(I32, (n, n), 0)
    tril = jnp.where(ri >= lax.broadcasted_iota(I32, (n, n), 1), 1.0, 0.0).astype(BF16)
    blk_row = lax.broadcasted_iota(I32, (1, w4), 1) // n
    low_half = lax.broadcasted_iota(I32, (n, 128), 1) < n
    zeros_pad = jnp.zeros((n, 2 * DV_A), BF16)
    gg = gg_ref[...]
    dot = functools.partial(jnp.dot, preferred_element_type=F32)

    def pad_rows(x, e):
        z = zeros_pad[:, :x.shape[1]]
        return jnp.concatenate([x, z] if e == 0 else [z, x], axis=0)

    seqs = range(nb)
    st = [dict() for _ in seqs]

    for bi in seqs:
        hi, mid, lo = _split3(sm_ref[bi])
        st[bi]["gc_all"] = dot(tril, hi) + dot(tril, mid) + dot(tril, lo)
        hi, mid, lo = _split3(gt_ref[bi, 0])
        st[bi]["grows"] = dot(hi, triu4) + dot(mid, triu4) + dot(lo, triu4)

    for bi in seqs:
        d = st[bi]
        sm, k, grows = sm_ref[bi], k_ref[bi], d["grows"]
        grow = jnp.where(blk_row == 0, grows[4:5], jnp.where(blk_row == 1, grows[5:6],
                         jnp.where(blk_row == 2, grows[6:7], grows[7:8])))
        gcols = [jnp.broadcast_to(d["gc_all"][:, _SM_G + h:_SM_G + h + 1], (n, 128)) for h in range(nh)]
        betas = [jnp.broadcast_to(sm[:, _SM_BETA + h:_SM_BETA + h + 1], (n, 128)) for h in range(nh)]
        d["gc512"] = jnp.concatenate(gcols, axis=1)
        d["beta512"] = jnp.concatenate(betas, axis=1)
        gc256 = jnp.concatenate([jnp.where(low_half, gcols[0], gcols[1]),
                                 jnp.where(low_half, gcols[2], gcols[3])], axis=1)
        d["dec"] = jnp.exp(jnp.minimum(gc256 - grow, 0.0))
        d["kb"] = k * d["beta512"]
        kbd = jnp.concatenate([k.astype(BF16)] * nh, axis=0) * bdk_ref[...]
        d["aq"] = lax.dot_general(jnp.concatenate([d["kb"], q_ref[bi]], axis=0).astype(BF16), kbd,
                                  (((1,), (1,)), ((), ())), preferred_element_type=F32)

    for bi in seqs:
        d = st[bi]
        x = jnp.where(strict4, -d["aq"][:n] * d["dec"], 0.0)
        d["qkm"] = jnp.where(incl4, d["aq"][n:] * d["dec"], 0.0).astype(BF16)
        d["inv"], d["p"] = eye4 + x, x
    for lvl in range(6):
        for bi in seqs:
            d = st[bi]
            p, inv = d["p"], d["inv"]
            wbd = jnp.concatenate([p.astype(BF16)] * nh, axis=0) * bdp_ref[...]
            if lvl == 0:
                d["p"] = dot(p.astype(BF16), wbd)
            elif lvl < 5:
                r = dot(jnp.concatenate([p, inv], axis=0).astype(BF16), wbd)
                d["p"], d["inv"] = r[:n], inv + r[n:]
            else:
                d["inv"] = inv + dot(inv.astype(BF16), wbd)

    heads = [(bi, h) for bi in seqs for h in range(nh)]
    hsl = lambda h: slice(h * DK_A, (h + 1) * DK_A)
    psl = lambda h: slice(128 * (h // 2), 128 * (h // 2 + 1))
    for bi in seqs:
        d = st[bi]
        k, gc512 = k_ref[bi], d["gc512"]
        d["tinv"] = d["inv"].astype(BF16)
        egc = jnp.exp(gc512)
        d["rhs_v"] = (v_ref[bi] * d["beta512"]).astype(BF16)
        d["rhs_k"] = (d["kb"] * egc).astype(BF16)
        g_last = gc512[n - 1:n, :]
        d["q_dec"] = q_ref[bi] * egc
        d["k_dec"] = (k * jnp.exp(g_last - gc512)).astype(BF16)
        d["c_dec"] = jnp.exp(g_last)
    sol, ws, vn = {}, {}, {}
    for bi, h in heads:
        d = st[bi]
        rhs = pad_rows(jnp.concatenate([d["rhs_v"][:, hsl(h)], d["rhs_k"][:, hsl(h)]], axis=1), h % 2)
        sol[bi, h] = dot(d["tinv"][:, psl(h)], rhs)
    for bi, h in heads:
        lhs = jnp.concatenate([sol[bi, h][:, DV_A:], st[bi]["q_dec"][:, hsl(h)]], axis=0)
        ws[bi, h] = dot(lhs.astype(BF16), s_ref[bi, h].astype(BF16))
    for bi, h in heads:
        d = st[bi]
        vn[bi, h] = (sol[bi, h][:, :DV_A] - ws[bi, h][:n]).astype(BF16)
        s_ref[bi, h] = d["c_dec"][:, hsl(h)] * s_ref[bi, h] + lax.dot_general(
            d["k_dec"][:, hsl(h)], vn[bi, h], (((0,), (0,)), ((), ())), preferred_element_type=F32)
    for bi, h in heads:
        o = ws[bi, h][n:] + dot(st[bi]["qkm"][:, psl(h)], pad_rows(vn[bi, h], h % 2))
        o_ref[bi, :, hsl(h)] = (_rms_rows(o, gg) * _silu(z_ref[bi, :, hsl(h)])).astype(BF16)


def _gdn_prompt(q, k, v, z, sm, gt, gg, *, b, t, nb):
    nc = t // CHUNK
    r3 = lambda a, n: a.reshape(b, t, n)
    blk = lambda n: pl.BlockSpec((nb, CHUNK, n), lambda i, j: (i, j, 0))
    w4 = H_A * CHUNK
    bdk = np.kron(np.eye(H_A, dtype=np.float32), np.ones((CHUNK, DK_A), np.float32))
    bdp = np.kron(np.eye(H_A, dtype=np.float32), np.ones((CHUNK, CHUNK), np.float32))
    return pl.pallas_call(
        functools.partial(_gdn_body, nb=nb),
        out_shape=[jax.ShapeDtypeStruct((b, t, 512), BF16),
                   jax.ShapeDtypeStruct((b, H_A, DK_A, DV_A), F32)],
        grid=(b // nb, nc),
        in_specs=[blk(512), blk(512), blk(512), blk(512), blk(128),
                  pl.BlockSpec((nb, 1, 8, CHUNK), lambda i, j: (i, j, 0, 0)),
                  _const_spec((1, DV_A)), _const_spec((w4, H_A * DK_A)), _const_spec((w4, w4))],
        out_specs=[blk(512), pl.BlockSpec((nb, H_A, DK_A, DV_A), lambda i, j: (i, 0, 0, 0))],
        compiler_params=_cparams(2),
        name="gdn_prompt",
    )(r3(q, 512), r3(k, 512), r3(v, 512), r3(z, 512), r3(sm, 128),
      jnp.swapaxes(gt.reshape(b, 8, nc, CHUNK), 1, 2), gg, jnp.asarray(bdk, BF16), jnp.asarray(bdp, BF16))


def _ordinal_to_float(k):
    k = jnp.maximum(k, NEG_INF_KEY)
    return pltpu.bitcast(k ^ ((k >> 31) & 0x7FFFFFFF), F32)


def _kth_largest(count_ge, shape, k_top):
    def bit_step(i, o):
        cand = o + jnp.left_shift(jnp.int32(1), 31 - i)
        return jnp.where(count_ge(_ordinal_to_float(cand)) >= k_top, cand, o)

    o = lax.fori_loop(0, 32, bit_step, jnp.full(shape, INT_MIN, I32))
    return _ordinal_to_float(o), o


def _kth_largest_two_stage(count_ge, count_ge_rounded, shape, k_top):
    def bf16_ordinal_as_f32_ordinal(h):
        return jnp.where(h >= 0, h << 16, (h << 16) | 0xFFFF)

    def coarse_step(i, h):
        cand = h + jnp.left_shift(jnp.int32(1), 15 - i)
        c = _ordinal_to_float(bf16_ordinal_as_f32_ordinal(cand)).astype(BF16)
        return jnp.where(count_ge_rounded(c) >= k_top, cand, h)

    h = lax.fori_loop(0, 16, coarse_step, jnp.full(shape, -(2 ** 15), I32))
    o = bf16_ordinal_as_f32_ordinal(h)
    o = jnp.where(count_ge(_ordinal_to_float(o)) >= k_top, o, bf16_ordinal_as_f32_ordinal(h - 1))

    def fine_step(i, o):
        cand = o + jnp.left_shift(jnp.int32(1), 15 - i)
        return jnp.where(count_ge(_ordinal_to_float(cand)) >= k_top, cand, o)

    o = lax.fori_loop(0, 16, fine_step, o)
    return _ordinal_to_float(o), o


def _topk_threshold(count_ge, count_gt_eq, count_eq_below, shape, k_top, idx_bits):
    t, o = _kth_largest(count_ge, shape, k_top)
    c_gt, c_eq = count_gt_eq(t)
    need = k_top - c_gt
    untied = jnp.logical_or(c_eq == need, o <= NEG_INF_KEY)
    all_untied = jnp.min(jnp.where(untied, 1.0, 0.0)) > 0.5

    def tie_search(_):
        def idx_step(i, x):
            cand = x + jnp.left_shift(jnp.int32(1), idx_bits - 1 - i)
            return jnp.where(count_eq_below(t, cand) < need, cand, x)
        return lax.fori_loop(0, idx_bits, idx_step, jnp.zeros(shape, I32))

    x = lax.cond(all_untied, lambda _: jnp.full(shape, 2 ** 30, I32), tie_search, 0)
    return t, x


_ACC_ROWS = HD_B + 16
_LOGITS_AHEAD = 5


def _dsa_body(qit_ref, wit_ref, qtp_ref, ki_ref, k_ref, vt_ref, o_ref,
              sc_ref, sb_ref, m_ref, acc_ref, *, tq, kc, k_top):
    i = pl.program_id(1)
    nkc = i + 1
    rowi = lax.broadcasted_iota(I32, (kc, tq), 0)
    coli = lax.broadcasted_iota(I32, (kc, tq), 1)
    ng = kc // 32

    wit = wit_ref[0]

    def score_chunks(chunks):
        for c in chunks:
            kic = ki_ref[0, c]
            score = jnp.zeros((kc, tq), F32)
            for h in range(H_IDX):
                s = jnp.dot(kic, qit_ref[0, h], preferred_element_type=F32)
                score = score + jnp.maximum(s, 0.0) * wit[h:h + 1, :]
            causal = rowi + (c - i) * kc <= coli
            score = jnp.where(causal, score, -jnp.inf)
            sc_ref[c] = score
            sb_ref[c] = score.astype(BF16)

    nquads = nkc // 4
    tail = 4 * nquads

    def score_quad(p, carry):
        score_chunks([4 * p, 4 * p + 1, 4 * p + 2, 4 * p + 3])
        return carry

    lax.fori_loop(0, nquads, score_quad, 0)

    @pl.when(nkc - tail >= 2)
    def _():
        score_chunks([tail, tail + 1])

    @pl.when(nkc % 2 == 1)
    def _():
        score_chunks([nkc - 1])

    def over_chunks(add_chunk, acc0):
        acc = lax.fori_loop(0, nkc // 2, lambda p, a: add_chunk(2 * p + 1, add_chunk(2 * p, a)), acc0)
        return lax.cond(nkc % 2 == 1, lambda a: add_chunk(nkc - 1, a), lambda a: a, acc)

    def key_sum(f):
        def add_chunk(c, acc):
            ind = f(sc_ref[c], c * kc)
            return acc + jnp.sum(ind.reshape(ng, 32, tq), axis=0)
        acc = over_chunks(add_chunk, jnp.zeros((32, tq), F32))
        return jnp.sum(acc, axis=0, keepdims=True).astype(I32)

    def count_ge_rounded(cand):
        one, zero = jnp.ones((), BF16), jnp.zeros((), BF16)

        def add_chunk(c, acc):
            ind = jnp.where(sb_ref[c] >= cand, one, zero)
            for g in range(ng):
                acc = acc + ind[32 * g:32 * (g + 1)]
            return acc
        acc = over_chunks(add_chunk, jnp.zeros((32, tq), BF16))
        return jnp.sum(acc.astype(F32), axis=0, keepdims=True).astype(I32)

    t, _ = _kth_largest_two_stage(lambda cand: key_sum(lambda s, off: jnp.where(s >= cand, 1.0, 0.0)),
                                  count_ge_rounded, (1, tq), k_top)
    ties_wanted = (k_top - key_sum(lambda s, off: jnp.where(s > t, 1.0, 0.0))).astype(F32)

    m_ref[...] = jnp.full(m_ref.shape, NEG_BIG, F32)
    acc_ref[...] = jnp.zeros(acc_ref.shape, F32)
    ones_rows = jnp.ones((_ACC_ROWS - HD_B, kc), BF16)
    prefix = jnp.where(lax.broadcasted_iota(I32, (kc, kc), 0) >= lax.broadcasted_iota(I32, (kc, kc), 1),
                       1.0, 0.0).astype(BF16)

    def attn_chunks(chunks, ties):
        kks, biases, vaugs = [], [], []
        for c in chunks:
            sc = sc_ref[c]
            tied = jnp.where(sc == t, 1.0, 0.0).astype(BF16)
            rank = ties + jnp.dot(prefix, tied, preferred_element_type=F32)
            tie = jnp.where(sc == t, jnp.where(rank <= ties_wanted, 0.0, NEG_BIG), NEG_BIG)
            biases.append(jnp.where(sc > -jnp.inf, jnp.where(sc > t, 0.0, tie), NEG_BIG))
            ties = rank[kc - 1:kc, :]
            kks.append(k_ref[0, c])
            vt = vt_ref[0, c]
            vaugs.append([jnp.concatenate([vt[HD_B * n:HD_B * (n + 1)], ones_rows], axis=0)
                          for n in range(N_KV_B)])
        units = [(j, h) for j in range(len(chunks)) for h in range(H_B)]
        logits = {}

        def issue(u):
            j, h = units[u]
            logits[u] = jnp.dot(kks[j], qtp_ref[0, h], preferred_element_type=F32)

        for u in range(_LOGITS_AHEAD):
            issue(u)
        for u, (j, h) in enumerate(units):
            if u + _LOGITS_AHEAD < len(units):
                issue(u + _LOGITS_AHEAD)
            n = h // (H_B // N_KV_B)
            s = biases[j] + logits.pop(u)
            m_old = m_ref[h:h + 1, :]
            m_new = jnp.maximum(m_old, jnp.max(s, axis=0, keepdims=True))
            alpha = jnp.exp2(m_old - m_new)
            pexp = jnp.exp2(s - m_new).astype(BF16)
            acc_ref[h] = acc_ref[h] * alpha + jnp.dot(vaugs[j][n], pexp, preferred_element_type=F32)
            m_ref[h:h + 1, :] = m_new
        return ties

    ties = lax.fori_loop(0, nquads, lambda p, tb: attn_chunks([4 * p, 4 * p + 1, 4 * p + 2, 4 * p + 3], tb),
                         jnp.zeros((1, tq), F32))
    ties = lax.cond(nkc - tail >= 2, lambda tb: attn_chunks([tail, tail + 1], tb), lambda tb: tb, ties)

    @pl.when(nkc % 2 == 1)
    def _():
        attn_chunks([nkc - 1], ties)

    for p in range(H_B // 2):
        halves = []
        for h in (2 * p, 2 * p + 1):
            acc = acc_ref[h]
            halves.append(acc[:HD_B] / acc[HD_B:HD_B + 1])
        o_ref[:, 128 * p:128 * (p + 1)] = jnp.concatenate(halves, axis=0).T.astype(BF16)


def _dsa_prompt(qit, wit, qtp, kibf, kbf, vt, *, b, t, tq, k_top):
    kc = tq
    nq = t // tq
    nkc = t // kc
    whole = lambda *s: pl.BlockSpec((1,) + s, lambda i, j: (i,) + (0,) * len(s))
    return pl.pallas_call(
        functools.partial(_dsa_body, tq=tq, kc=kc, k_top=k_top),
        out_shape=jax.ShapeDtypeStruct((b * t, H_B * HD_B), BF16),
        grid=(b, nq),
        in_specs=[pl.BlockSpec((1, H_IDX, D_IDX, tq), lambda i, j: (i, 0, 0, j)),
                  pl.BlockSpec((1, H_IDX, tq), lambda i, j: (i, 0, j)),
                  pl.BlockSpec((1, H_B, 128, tq), lambda i, j: (i, 0, 0, j)),
                  whole(nkc, kc, D_IDX), whole(nkc, kc, 128), whole(nkc, 128, kc)],
        out_specs=pl.BlockSpec((tq, H_B * HD_B), lambda i, j: (i * nq + j, 0)),
        scratch_shapes=[pltpu.VMEM((nkc, kc, tq), F32), pltpu.VMEM((nkc, kc, tq), BF16),
                        pltpu.VMEM((H_B, tq), F32), pltpu.VMEM((H_B, _ACC_ROWS, tq), F32)],
        compiler_params=_cparams(2),
        name="dsa_prompt",
    )(qit, wit, qtp, kibf.reshape(b, nkc, kc, D_IDX), kbf.reshape(b, nkc, kc, 128), vt)


def _merge_body(x_ref, oa_ref, ob_ref, g1_ref, wg_ref, woa_ref, wob_ref, wo_ref, h_ref):
    x = x_ref[...]
    xn = _rms_rows(x, g1_ref[...]).astype(BF16)
    gates = jax.nn.sigmoid(jnp.dot(xn, wg_ref[...], preferred_element_type=F32))
    a = jnp.dot(oa_ref[...], woa_ref[...], preferred_element_type=F32)
    bb = jnp.dot(ob_ref[...], wob_ref[...], preferred_element_type=F32)
    mix = gates[:, :D_MODEL] * a + gates[:, D_MODEL:] * bb
    h_ref[...] = x + jnp.dot(mix.astype(BF16), wo_ref[...], preferred_element_type=F32)


def _merge(x, oa, ob, wob, w, *, tm):
    m, d = x.shape
    nb = ob.shape[1]
    row = lambda n: pl.BlockSpec((tm, n), lambda i: (i, 0))
    return pl.pallas_call(
        _merge_body,
        out_shape=jax.ShapeDtypeStruct((m, d), F32),
        grid=(m // tm,),
        in_specs=[row(d), row(512), row(nb), _const_spec((1, d)), _const_spec((d, 2 * d)),
                  _const_spec((512, d)), _const_spec((nb, d)), _const_spec((d, d))],
        out_specs=row(d),
        compiler_params=_cparams(1),
        name="merge",
    )(x, oa, ob, w["g1"], w["wg"], w["woa"], wob, w["wo"])


def _mlp_body(h_ref, g2_ref, wup_ref, wdn_ref, y_ref):
    h = h_ref[...]
    hn = _rms_rows(h, g2_ref[...]).astype(BF16)
    up = jnp.dot(hn, wup_ref[...], preferred_element_type=F32)
    act = jnp.square(jnp.maximum(up, 0.0)).astype(BF16)
    y_ref[...] = h + jnp.dot(act, wdn_ref[...], preferred_element_type=F32)


def _mlp(h, w, *, tm):
    m, d = h.shape
    row = pl.BlockSpec((tm, d), lambda i: (i, 0))
    return pl.pallas_call(
        _mlp_body,
        out_shape=jax.ShapeDtypeStruct((m, d), F32),
        grid=(m // tm,),
        in_specs=[row, _const_spec((1, d)), _const_spec((d, D_FF)), _const_spec((D_FF, d))],
        out_specs=row,
        compiler_params=_cparams(1),
        name="mlp",
    )(h, w["g2"], w["wup"], w["wdn"])


def _proj_decode_body(x_ref, st_ref, g1_ref, wqkv_ref, wz_ref, wbq_ref, wkv_ref, wiq_ref, wsm_ref,
                      convw_ref, qng_ref, kng_ref, alog_ref, dtb_ref, ones_ref, seg_ref, cos_ref, sin_ref,
                      u_out, q_out, k_out, v_out, z_out, qb_out, kv_out, qi_out, sm_out, ss_out):
    x = x_ref[...]
    xn = _rms_rows(x, g1_ref[...]).astype(BF16)
    u = jnp.dot(xn, wqkv_ref[...], preferred_element_type=F32)
    u_out[...] = u
    cw = convw_ref[...]
    acc = st_ref[0] * cw[0:1] + st_ref[1] * cw[1:2] + st_ref[2] * cw[2:3] + u * cw[3:4]
    c = _silu(acc)
    for h in range(H_A):
        qh = c[:, h * DK_A:(h + 1) * DK_A]
        kh = c[:, (H_A + h) * DK_A:(H_A + h + 1) * DK_A]
        q_out[:, h * DK_A:(h + 1) * DK_A] = qh * (lax.rsqrt(jnp.sum(qh * qh, -1, keepdims=True) + EPS)
                                                 * (DK_A ** -0.5))
        k_out[:, h * DK_A:(h + 1) * DK_A] = kh * lax.rsqrt(jnp.sum(kh * kh, -1, keepdims=True) + EPS)
    v_out[...] = c[:, 2 * H_A * DK_A:]
    z_out[...] = jnp.dot(xn, wz_ref[...], preferred_element_type=F32)
    cos, sin = cos_ref[...], sin_ref[...]
    ones_bd = ones_ref[...]
    bq = jnp.dot(xn, wbq_ref[...], preferred_element_type=F32)
    qb = _rope_rows(_seg_rms(bq, ones_bd, qng_ref[...]), cos, sin)
    qb_out[...] = (qb * (HD_B ** -0.5)).astype(BF16)
    kv = jnp.dot(xn, wkv_ref[...], preferred_element_type=F32)
    kb = _rope_rows(_seg_rms(kv[:, :128], ones_bd[:128, :128], kng_ref[...]), cos, sin)
    kv_out[:, 0:128] = kb
    kv_out[:, 128:256] = kv[:, 128:]
    iq = jnp.dot(xn, wiq_ref[...], preferred_element_type=F32)
    qi = (_rope_rows(iq, cos, sin) * (D_IDX ** -0.5)).astype(BF16)
    qi_out[...] = qi
    sm = jnp.dot(xn, wsm_ref[...], preferred_element_type=F32)
    smo = _small_block(sm, cos, sin, alog_ref[...], dtb_ref[...])
    sm_out[...] = smo
    lane = lax.broadcasted_iota(I32, smo.shape, 1)
    ki = smo.astype(BF16).astype(F32)
    ki2 = jnp.where(lane < D_IDX, ki, pltpu.roll(ki, D_IDX, 1))
    prod = qi.astype(F32) * jnp.concatenate([ki2] * 4, axis=-1)
    sh = _mm_f32(prod, seg_ref[...])
    contrib = jnp.where(jnp.logical_and(lane >= _SM_WI, lane < _SM_WI + H_IDX),
                        jnp.maximum(sh, 0.0) * smo, 0.0)
    ss_out[...] = jnp.broadcast_to(jnp.sum(contrib, axis=-1, keepdims=True), ss_out.shape)


def _proj_decode(x, st, w, tabs):
    r, d = x.shape
    outs = [(r, CONV_CH, F32), (r, 512, F32), (r, 512, F32), (r, 512, F32), (r, 512, F32), (r, 512, BF16),
            (r, 256, F32), (r, 512, BF16), (r, 128, F32), (r, 128, F32)]
    args = (x, st, w["g1"], w["wqkv"], w["wz"], w["wbq"], w["wkv"], w["wiq"], w["wsm"], w["convw"],
            w["qng"], w["kng"], w["alog_row"], w["dtb_row"], w["ones_bd"], w["seg_sum"], tabs["cos"], tabs["sin"])
    return pl.pallas_call(
        _proj_decode_body,
        out_shape=[jax.ShapeDtypeStruct((a, n), dt) for a, n, dt in outs],
        compiler_params=pltpu.CompilerParams(vmem_limit_bytes=VMEM_LIMIT),
        name="proj_decode",
    )(*args)


def _gdn_decode_body(qc_ref, kc_ref, v_ref, z_ref, sm_ref, gg_ref, s_ref, o_ref, so_ref):
    sm = sm_ref[0]
    gg = gg_ref[...]
    for h in range(H_A):
        g = sm[:, _SM_G + h:_SM_G + h + 1]
        beta = sm[:, _SM_BETA + h:_SM_BETA + h + 1]
        kcol, qcol = kc_ref[0, h], qc_ref[0, h]
        v = v_ref[0, h]
        s = s_ref[0, h] * jnp.exp(g)
        vn = beta * (v - jnp.sum(s * kcol, axis=0, keepdims=True))
        s = s + kcol * vn
        so_ref[0, h] = s
        o = jnp.sum(s * qcol, axis=0, keepdims=True)
        o_ref[0, h] = (_rms_rows(o, gg) * _silu(z_ref[0, h])).astype(BF16)


def _gdn_decode(q, k, v, z, sm, gg, state):
    r = q.shape[0]
    col = lambda a: a.reshape(r, H_A, DK_A, 1)
    rowv = lambda a: a.reshape(r, H_A, 1, DV_A)
    cspec = pl.BlockSpec((1, H_A, DK_A, 1), lambda i: (i, 0, 0, 0))
    rspec = pl.BlockSpec((1, H_A, 1, DV_A), lambda i: (i, 0, 0, 0))
    sspec = pl.BlockSpec((1, H_A, DK_A, DV_A), lambda i: (i, 0, 0, 0))
    o, s_new = pl.pallas_call(
        _gdn_decode_body,
        out_shape=[jax.ShapeDtypeStruct((r, H_A, 1, DV_A), BF16),
                   jax.ShapeDtypeStruct((r, H_A, DK_A, DV_A), F32)],
        grid=(r,),
        in_specs=[cspec, cspec, rspec, rspec, pl.BlockSpec((1, 1, 128), lambda i: (i, 0, 0)),
                  _const_spec((1, DV_A)), sspec],
        out_specs=[rspec, sspec],
        compiler_params=_cparams(1),
        name="gdn_decode",
    )(col(q), col(k), rowv(v), rowv(z), sm.reshape(r, 1, 128), gg, state)
    return o.reshape(r, H_A * DV_A), s_new


def _page_copies(pt_ref, pages_hbm, buf, sem, row, first, count, slot):
    return [pltpu.make_async_copy(pages_hbm.at[pt_ref[row, first + p]], buf.at[slot, p], sem.at[slot])
            for p in range(count)]


def _idx_decode_body(pt_ref, qi_ref, w_ref, pages_hbm, out_ref, buf, sem, *, n_pages):
    i = pl.program_id(0)
    slot = i % 2
    copies = functools.partial(_page_copies, pt_ref, pages_hbm, buf, sem)

    @pl.when(i == 0)
    def _():
        for cp in copies(0, 0, n_pages, 0):
            cp.start()

    @pl.when(i + 1 < pl.num_programs(0))
    def _():
        for cp in copies(i + 1, 0, n_pages, 1 - slot):
            cp.start()

    for cp in copies(i, 0, n_pages, slot):
        cp.wait()
    qi = qi_ref[0]
    w = w_ref[0]
    for p in range(n_pages):
        s = _mm(qi, buf[slot, p])
        out_ref[0, p:p + 1, :] = jnp.sum(jnp.maximum(s, 0.0) * w, axis=0, keepdims=True)


def _idx_decode(page_table, qi, wi_col, cache_kidx):
    r, n_pages = page_table.shape
    grid_spec = pltpu.PrefetchScalarGridSpec(
        num_scalar_prefetch=1,
        grid=(r,),
        in_specs=[pl.BlockSpec((1, H_IDX, D_IDX), lambda i, pt: (i, 0, 0)),
                  pl.BlockSpec((1, H_IDX, 1), lambda i, pt: (i, 0, 0)),
                  pl.BlockSpec(memory_space=pl.ANY)],
        out_specs=pl.BlockSpec((1, n_pages, PAGE_SIZE), lambda i, pt: (i, 0, 0)),
        scratch_shapes=[pltpu.VMEM((2, n_pages, D_IDX, PAGE_SIZE), F32), pltpu.SemaphoreType.DMA((2,))],
    )
    return pl.pallas_call(
        functools.partial(_idx_decode_body, n_pages=n_pages),
        out_shape=jax.ShapeDtypeStruct((r, n_pages, PAGE_SIZE), F32),
        grid_spec=grid_spec,
        compiler_params=_cparams(1),
        name="idx_decode",
    )(page_table, qi.reshape(r, H_IDX, D_IDX), wi_col, cache_kidx)


def _select_decode_body(sc_ref, ss_ref, bias_ref, bself_ref, *, k_top, cw):
    r, n = sc_ref.shape
    nch = n // cw
    key_ref = sc_ref
    skey = ss_ref[:, 0:1]

    def lane_sum(f):
        def body(c, acc):
            off = pl.multiple_of(c * cw, cw)
            kk = key_ref[:, pl.ds(off, cw)]
            for j in range(cw // 128):
                acc = acc + f(kk[:, 128 * j:128 * (j + 1)], off + 128 * j)
            return acc
        acc = lax.fori_loop(0, nch, body, jnp.zeros((r, 128), F32))
        return jnp.sum(acc, axis=-1, keepdims=True).astype(I32)

    def count_ge(cand):
        cb = jnp.broadcast_to(cand, (r, 128))
        return lane_sum(lambda kk, off: jnp.where(kk >= cb, 1.0, 0.0)) + (skey >= cand).astype(I32)

    def count_gt_eq(t):
        tb = jnp.broadcast_to(t, (r, 128))
        return (lane_sum(lambda kk, off: jnp.where(kk > tb, 1.0, 0.0)) + (skey > t).astype(I32),
                lane_sum(lambda kk, off: jnp.where(kk == tb, 1.0, 0.0)) + (skey == t).astype(I32))

    def count_eq_below(t, j):
        tb = jnp.broadcast_to(t, (r, 128))
        jb = jnp.broadcast_to(j, (r, 128))
        lane = lax.broadcasted_iota(I32, (r, 128), 1)
        return (lane_sum(lambda kk, off: jnp.where(kk == tb, jnp.where(lane + off < jb, 1.0, 0.0), 0.0))
                + jnp.logical_and(skey == t, j > n).astype(I32))

    t, x = _topk_threshold(count_ge, count_gt_eq, count_eq_below, (r, 1), k_top, 15)
    tb = jnp.broadcast_to(t, (r, cw))
    xb = jnp.broadcast_to(x, (r, cw))

    def bias_chunk(c, carry):
        off = pl.multiple_of(c * cw, cw)
        kk = key_ref[:, pl.ds(off, cw)]
        idx = lax.broadcasted_iota(I32, (r, cw), 1) + off
        tie = jnp.where(kk == tb, jnp.where(idx <= xb, 0.0, NEG_BIG), NEG_BIG)
        bias_ref[:, pl.ds(off, cw)] = jnp.where(kk > tb, 0.0, tie)
        return carry

    lax.fori_loop(0, nch, bias_chunk, 0)
    tie_self = jnp.where(skey == t, jnp.where(x >= n, 0.0, NEG_BIG), NEG_BIG)
    bself_ref[...] = jnp.broadcast_to(jnp.where(skey > t, 0.0, tie_self), bself_ref.shape)


def _select_decode(scores, sself, *, k_top):
    r, n = scores.shape
    return pl.pallas_call(
        functools.partial(_select_decode_body, k_top=k_top, cw=512),
        out_shape=[jax.ShapeDtypeStruct((r, n), F32), jax.ShapeDtypeStruct((r, 128), F32)],
        compiler_params=pltpu.CompilerParams(vmem_limit_bytes=VMEM_LIMIT),
        name="select_decode",
    )(scores, sself)


def _attn_decode_body(pt_ref, qp_ref, bias_ref, kvs_ref, bself_ref, pages_hbm, o_ref,
                      m_ref, l_ref, acc_ref, buf, sem, *, npg, nsteps):
    i, j = pl.program_id(0), pl.program_id(1)
    step = i * nsteps + j
    slot = step % 2
    copies = functools.partial(_page_copies, pt_ref, pages_hbm, buf, sem)

    @pl.when(step == 0)
    def _():
        for cp in copies(0, 0, npg, 0):
            cp.start()

    @pl.when(step + 1 < pl.num_programs(0) * nsteps)
    def _():
        nxt = step + 1
        for cp in copies(nxt // nsteps, (nxt % nsteps) * npg, npg, 1 - slot):
            cp.start()

    @pl.when(j == 0)
    def _():
        m_ref[...] = jnp.full(m_ref.shape, NEG_BIG, F32)
        l_ref[...] = jnp.zeros(l_ref.shape, F32)
        acc_ref[...] = jnp.zeros(acc_ref.shape, F32)

    for cp in copies(i, j * npg, npg, slot):
        cp.wait()
    qp = qp_ref[0]

    ss = [_mm(qp, buf[slot, p, 0:128, :]) + bias_ref[0, :, PAGE_SIZE * p:PAGE_SIZE * (p + 1)]
          for p in range(npg)]
    m_old = m_ref[...]
    m_new = m_old
    for s in ss:
        m_new = jnp.maximum(m_new, jnp.max(s, axis=-1, keepdims=True))
    alpha = jnp.exp(m_old - m_new)
    l_new = l_ref[...] * alpha
    acc_new = acc_ref[...] * alpha
    for p in range(npg):
        pe = jnp.exp(ss[p] - m_new)
        l_new = l_new + jnp.sum(pe, axis=-1, keepdims=True)
        acc_new = acc_new + _mm_nt(pe, buf[slot, p, 128:256, :])
    m_ref[...] = m_new
    l_ref[...] = l_new
    acc_ref[...] = acc_new

    @pl.when(j == nsteps - 1)
    def _():
        kvs = kvs_ref[0]
        krow = kvs[:, :128].astype(BF16).astype(F32)
        s = jnp.sum(qp.astype(F32) * krow, axis=-1, keepdims=True) + bself_ref[0][:, 0:1]
        m_old = m_ref[...]
        m_new = jnp.maximum(m_old, s)
        alpha = jnp.exp(m_old - m_new)
        p = jnp.exp(s - m_new)
        l = l_ref[...] * alpha + p
        vrow = kvs[:, 128:].astype(BF16).astype(F32)
        acc = acc_ref[...] * alpha + p.astype(BF16).astype(F32) * vrow
        out = acc / l
        rowi = lax.broadcasted_iota(I32, out.shape, 0)
        out = jnp.where(rowi < 4, out, pltpu.roll(out, HD_B, 1))
        o_ref[0] = out.astype(BF16)


def _attn_decode(page_table, qpad, bias, kv_self, bself, cache_kv, *, npg):
    r, n_pages = page_table.shape
    nsteps = n_pages // npg
    page_rows = 2 * N_KV_B * HD_B
    grid_spec = pltpu.PrefetchScalarGridSpec(
        num_scalar_prefetch=1,
        grid=(r, nsteps),
        in_specs=[pl.BlockSpec((1, H_B, 128), lambda i, j, pt: (i, 0, 0)),
                  pl.BlockSpec((1, 1, PAGE_SIZE * npg), lambda i, j, pt: (i, 0, j)),
                  pl.BlockSpec((1, 1, 256), lambda i, j, pt: (i, 0, 0)),
                  pl.BlockSpec((1, 1, 128), lambda i, j, pt: (i, 0, 0)),
                  pl.BlockSpec(memory_space=pl.ANY)],
        out_specs=pl.BlockSpec((1, H_B, 128), lambda i, j, pt: (i, 0, 0)),
        scratch_shapes=[pltpu.VMEM((H_B, 1), F32), pltpu.VMEM((H_B, 1), F32), pltpu.VMEM((H_B, 128), F32),
                        pltpu.VMEM((2, npg, page_rows, PAGE_SIZE), F32), pltpu.SemaphoreType.DMA((2,))],
    )
    return pl.pallas_call(
        functools.partial(_attn_decode_body, npg=npg, nsteps=nsteps),
        out_shape=jax.ShapeDtypeStruct((r, H_B, 128), BF16),
        grid_spec=grid_spec,
        compiler_params=_cparams(2),
        name="attn_decode",
    )(page_table, qpad, bias.reshape(r, 1, -1), kv_self.reshape(r, 1, 256), bself.reshape(r, 1, 128), cache_kv)


def _prep_weights(norm1_g, w_in, conv_w, a_log, dt_bias, gdn_norm_g, q_norm_g, k_norm_g,
                  w_out_a, w_out_b, w_o, norm2_g, w_up, w_down):
    o = _OFF
    wb = w_in.astype(BF16)
    d = w_in.shape[0]
    wsm = jnp.concatenate([wb[:, o["ik"]:o["iw"]], wb[:, o["ab"]:o["bq"]], wb[:, o["iw"]:o["ga"]],
                           jnp.zeros((d, 128 - 80), BF16)], axis=1)

    def lanes(vals, start):
        return jnp.zeros((1, 128), F32).at[0, start:start + vals.shape[0]].set(vals.astype(F32))

    seg = np.zeros((512, 128), np.float32)
    for h in range(H_IDX):
        seg[64 * h:64 * (h + 1), _SM_WI + h] = 1.0
    ones_bd = np.kron(np.eye(8, dtype=np.float32), np.ones((64, 64), np.float32))
    wob = w_out_b.astype(BF16).reshape(H_B, HD_B, d)
    wob_pad = jnp.concatenate([wob, jnp.zeros_like(wob)], axis=1).reshape(H_B * 128, d)
    col8 = lambda v: jnp.concatenate([jnp.zeros((4,), F32), v.astype(F32)]).reshape(8, 1)
    return dict(
        g1=norm1_g.reshape(1, d).astype(F32),
        wqkv=wb[:, o["aq"]:o["az"]], wz=wb[:, o["az"]:o["ab"]], wbq=wb[:, o["bq"]:o["bk"]],
        wkv=wb[:, o["bk"]:o["iq"]], wiq=wb[:, o["iq"]:o["ik"]], wsm=wsm,
        wbqt=wb[:, o["bq"]:o["bk"]].T, wiqt=wb[:, o["iq"]:o["ik"]].T, wvt=wb[:, o["bv"]:o["iq"]].T,
        wsmt=wsm.T, wg=wb[:, o["ga"]:o["end"]],
        convw=conv_w.astype(F32),
        qng=jnp.tile(q_norm_g.astype(F32), H_B).reshape(1, 512),
        kng=jnp.tile(k_norm_g.astype(F32), N_KV_B).reshape(1, 128),
        qngt=jnp.tile(q_norm_g.astype(F32).reshape(HD_B, 1), (1, 128)),
        alog_row=lanes(a_log, _SM_G), dtb_row=lanes(dt_bias, _SM_G),
        alog_col=col8(a_log), dtb_col=col8(dt_bias),
        ones_bd=jnp.asarray(ones_bd, BF16), ones_bd128=jnp.asarray(ones_bd[:128, :128], BF16),
        seg_sum=jnp.asarray(seg),
        gg=gdn_norm_g.reshape(1, DV_A).astype(F32),
        woa=w_out_a.astype(BF16), wob=w_out_b.astype(BF16), wob_pad=wob_pad, wo=w_o.astype(BF16),
        g2=norm2_g.reshape(1, d).astype(F32), wup=w_up.astype(BF16), wdn=w_down.astype(BF16),
    )


def _rope_tables(pos):
    half = HD_B // 2
    inv = ROPE_THETA ** (-jnp.arange(half, dtype=F32) / half)
    ang = pos.astype(F32)[:, None] * inv[None, :]
    cos, sin = jnp.cos(ang), jnp.sin(ang)
    return dict(cos=jnp.tile(cos, (1, 4)), sin=jnp.tile(jnp.concatenate([-sin, sin], axis=1), (1, 2)),
                cos_t=cos.T, sin_t=sin.T)


def _layer_prompt(x, w, *, tm=512, tq=256, nb=8):
    b, t, d = x.shape
    tabs = _rope_tables(jnp.arange(t))
    (q, k, v, z, kv, kbf, kibf, sm, kidx, qtp, qit, vt, gt, wit, conv_new) = _proj_prompt(x, w, tabs, tm=tm, kc=tq)
    oa, ssm = _gdn_prompt(q, k, v, z, sm, gt, w["gg"], b=b, t=t, nb=nb)
    ob = _dsa_prompt(qit, wit, qtp, kibf, kbf, vt, b=b, t=t, tq=tq, k_top=min(TOPK_MAX, t // 4))
    h = _merge(x.reshape(b * t, d), oa.reshape(b * t, 512), ob, w["wob"], w, tm=tm)
    y = _mlp(h, w, tm=tm)
    return (y.reshape(b, t, d), kv.reshape(b, t, 2, N_KV_B, HD_B), kidx.reshape(b, t, D_IDX), ssm, conv_new)


def _layer_decode(x, w, state_ssm, state_conv, cache_kv, cache_kidx, page_table, *, npg=32):
    r, d = x.shape
    n_pages = page_table.shape[1]
    past = n_pages * PAGE_SIZE
    tabs = _rope_tables(jnp.full((1,), past))
    st = jnp.swapaxes(state_conv, 0, 1)
    (u, q, k, v, z, qb, kv, qi, sm, sself) = _proj_decode(x, st, w, tabs)
    oa, ssm = _gdn_decode(q, k, v, z, sm, w["gg"], state_ssm)
    wi_col = sm[:, _SM_WI:_SM_WI + H_IDX].reshape(r, H_IDX, 1)
    kidx_pages = jnp.swapaxes(cache_kidx, 1, 2)
    kv_pages = jnp.transpose(cache_kv, (0, 2, 3, 4, 1)).reshape(-1, 2 * N_KV_B * HD_B, PAGE_SIZE)
    scores = _idx_decode(page_table, qi, wi_col, kidx_pages).reshape(r, past)
    bias, bself = _select_decode(scores, sself, k_top=min(TOPK_MAX, (past + 1) // 4))
    qh = qb.reshape(r, H_B, HD_B)
    zq = jnp.zeros_like(qh[:, :4])
    qpad = jnp.concatenate([jnp.concatenate([qh[:, :4], zq], axis=-1),
                            jnp.concatenate([zq, qh[:, 4:]], axis=-1)], axis=1)
    ob = _attn_decode(page_table, qpad, bias, kv, bself, kv_pages, npg=npg)
    h = _merge(x, oa, ob.reshape(r, H_B * 128), w["wob_pad"], w, tm=r)
    y = _mlp(h, w, tm=r)
    conv_new = jnp.concatenate([state_conv[:, 1:], u[:, None, :]], axis=1)
    return (y, kv.reshape(r, 1, 2, N_KV_B, HD_B), sm[:, :D_IDX].reshape(r, 1, D_IDX), ssm, conv_new)


def kernel(x_prompt, x_sample, cache_kv, cache_kidx, page_table, state_ssm, state_conv, norm1_g, w_in, conv_w,
           a_log, dt_bias, gdn_norm_g, q_norm_g, k_norm_g, w_out_a, w_out_b, w_o, norm2_g, w_up, w_down):
    assert w_in.shape[0] == 1, "single-layer trunk"
    w = _prep_weights(norm1_g[0], w_in[0], conv_w[0], a_log[0], dt_bias[0], gdn_norm_g[0], q_norm_g[0],
                      k_norm_g[0], w_out_a[0], w_out_b[0], w_o[0], norm2_g[0], w_up[0], w_down[0])
    yp, kv_p, ki_p, ssm_p, cv_p = _layer_prompt(x_prompt, w)
    ys, kv_s, ki_s, ssm_s, cv_s = _layer_decode(x_sample[:, 0], w, state_ssm[0], state_conv[0],
                                                cache_kv[0], cache_kidx[0], page_table)
    return (yp, ys[:, None, :], kv_p[None], ki_p[None], ssm_p[None], cv_p[None],
            kv_s[None], ki_s[None], ssm_s[None], cv_s[None])
```

```python
import functools
import math

import jax
import jax.numpy as jnp
import numpy as np
from jax import lax
from jax.experimental import pallas as pl
from jax.experimental.pallas import tpu as pltpu

F32 = jnp.float32
BF16 = jnp.bfloat16
I32 = jnp.int32

D_MODEL = 1024
H_A, DK_A, DV_A = 4, 128, 128
CONV_W = 4
CONV_CH = H_A * (2 * DK_A + DV_A)
CHUNK = 64
HD_B, H_B, N_KV_B = 64, 8, 2
H_IDX, D_IDX = 8, 64
TOPK_MAX = 256
PAGE_SIZE = 128
ROPE_THETA = 10000.0
D_FF = 4 * D_MODEL
EPS = 1e-6

_OFF = dict(aq=0, ak=512, av=1024, az=1536, ab=2048, aa=2052, bq=2056, bk=2568, bv=2696,
            iq=2824, ik=3336, iw=3400, ga=3408, gb=4432, end=5456)
_SM_BETA, _SM_G, _SM_WI = 64, 68, 72

LOG2_E = math.log2(math.e)
NEG_BIG = -1e30
INT_MIN = -(2 ** 31)
NEG_INF_KEY = (0xFF800000 ^ 0x7FFFFFFF) - 2 ** 32

VMEM_LIMIT = 56 * 1024 * 1024
HIGHEST = lax.Precision.HIGHEST


def _cparams(n_axes):
    return pltpu.CompilerParams(dimension_semantics=("arbitrary",) * n_axes,
                                vmem_limit_bytes=VMEM_LIMIT)


def _const_spec(shape):
    nd = len(shape)
    return pl.BlockSpec(shape, lambda *a: (0,) * nd, pipeline_mode=pl.Buffered(1))


def _mm(a, b):
    return jnp.dot(a.astype(BF16), b.astype(BF16), preferred_element_type=F32)


def _mm_nt(a, b):
    return lax.dot_general(a.astype(BF16), b.astype(BF16), (((1,), (1,)), ((), ())),
                           preferred_element_type=F32)


def _mm_tn(a, b):
    return lax.dot_general(a.astype(BF16), b.astype(BF16), (((0,), (0,)), ((), ())),
                           preferred_element_type=F32)


def _mm_f32(a, b):
    return jnp.dot(a, b, preferred_element_type=F32, precision=HIGHEST)


def _rms_rows(x, g):
    return x * lax.rsqrt(jnp.mean(x * x, axis=-1, keepdims=True) + EPS) * g


def _silu(x):
    return x * jax.nn.sigmoid(x)


def _softplus(x):
    return jnp.maximum(x, 0.0) + jnp.log1p(jnp.exp(-jnp.abs(x)))


def _rope_rows(x, cos, sin_signed):
    n = x.shape[-1]
    reps = n // 128
    if reps > 1:
        cos = jnp.concatenate([cos] * reps, axis=-1)
        sin_signed = jnp.concatenate([sin_signed] * reps, axis=-1)
    lane = lax.broadcasted_iota(I32, x.shape, x.ndim - 1)
    first_half = (lane % HD_B) < (HD_B // 2)
    partner = jnp.where(first_half, pltpu.roll(x, n - HD_B // 2, x.ndim - 1),
                        pltpu.roll(x, HD_B // 2, x.ndim - 1))
    return x * cos + partner * sin_signed


def _rope_cols(x, cos_t, sin_t):
    x1, x2 = x[:32], x[32:]
    return jnp.concatenate([x1 * cos_t - x2 * sin_t, x2 * cos_t + x1 * sin_t], axis=0)


def _seg_rms(x, ones_bd, g):
    sq = x * x
    hi = sq.astype(BF16)
    lo = (sq - hi.astype(F32)).astype(BF16)
    ss = (jnp.dot(hi, ones_bd, preferred_element_type=F32)
          + jnp.dot(lo, ones_bd, preferred_element_type=F32))
    return x * lax.rsqrt(ss * (1.0 / HD_B) + EPS) * g


def _small_block(sm, cos, sin_signed, alog_row, dtb_row):
    lane = lax.broadcasted_iota(I32, sm.shape, 1)
    roped = _rope_rows(sm, cos, sin_signed)
    beta = jax.nn.sigmoid(sm)
    g = -jnp.exp(alog_row) * _softplus(sm + dtb_row)
    wi = sm * (H_IDX ** -0.5)
    out = jnp.where(lane < _SM_BETA, roped,
                    jnp.where(lane < _SM_G, beta, jnp.where(lane < _SM_WI, g, wi)))
    return out


def _proj_prompt_body(x_ref, g1_ref, wqkv_ref, wz_ref, wkv_ref, wsm_ref,
                      wbqt_ref, wiqt_ref, wvt_ref, wsmt_ref, convw_ref, kng_ref, qngt_ref, alog_ref, dtb_ref,
                      alogt_ref, dtbt_ref, ones_ref, cos_ref, sin_ref, cost_ref, sint_ref,
                      q_out, k_out, v_out, z_out, kv_out, kbf_out, kibf_out, sm_out, kidx_out,
                      qtp_out, qit_out, vt_out, gt_out, wit_out, conv_out, ubuf, *, tm, kc):
    @pl.when(pl.program_id(1) == 0)
    def _():
        ubuf[0:8, :] = jnp.zeros((8, CONV_CH), F32)

    x = x_ref[0]
    xn = _rms_rows(x, g1_ref[...]).astype(BF16)

    nt_dims = (((1,), (1,)), ((), ()))
    u = jnp.dot(xn, wqkv_ref[...], preferred_element_type=F32)
    z = jnp.dot(xn, wz_ref[...], preferred_element_type=F32)
    kv = jnp.dot(xn, wkv_ref[...], preferred_element_type=F32)
    sm = jnp.dot(xn, wsm_ref[...], preferred_element_type=F32)
    qt = lax.dot_general(wbqt_ref[...], xn, nt_dims, preferred_element_type=F32)
    qit = lax.dot_general(wiqt_ref[...], xn, nt_dims, preferred_element_type=F32)
    vt = lax.dot_general(wvt_ref[...], xn, nt_dims, preferred_element_type=F32)
    smt = lax.dot_general(wsmt_ref[...], xn, nt_dims, preferred_element_type=F32)

    ubuf[8:tm + 8, :] = u
    cw = convw_ref[...]
    for blk in range(CONV_CH // DK_A):
        cs = slice(blk * DK_A, (blk + 1) * DK_A)
        acc = (ubuf[5:5 + tm, cs] * cw[0:1, cs] + ubuf[6:6 + tm, cs] * cw[1:2, cs]
               + ubuf[7:7 + tm, cs] * cw[2:3, cs] + ubuf[8:8 + tm, cs] * cw[3:4, cs])
        c = _silu(acc)
        if blk < H_A:
            q_out[:, cs] = c * (lax.rsqrt(jnp.sum(c * c, -1, keepdims=True) + EPS) * (DK_A ** -0.5))
        elif blk < 2 * H_A:
            hs = slice((blk - H_A) * DK_A, (blk - H_A + 1) * DK_A)
            k_out[:, hs] = c * lax.rsqrt(jnp.sum(c * c, -1, keepdims=True) + EPS)
        else:
            hs = slice((blk - 2 * H_A) * DV_A, (blk - 2 * H_A + 1) * DV_A)
            v_out[:, hs] = c
    conv_out[0] = ubuf[tm + 5:tm + 8, :]
    ubuf[0:8, :] = ubuf[tm:tm + 8, :]
    z_out[...] = z

    cos, sin = cos_ref[...], sin_ref[...]
    ones_bd = ones_ref[...]

    kb =_rope_rows(_seg_rms(kv[:, :128], ones_bd, kng_ref[...]), cos, sin)
    kv_out[:, 0:128] = kb
    kv_out[:, 128:256] = kv[:, 128:]
    kbf_out[...] = kb.astype(BF16)

    smo = _small_block(sm, cos, sin, alog_ref[...], dtb_ref[...])
    sm_out[...] = smo
    kidx_out[...] = smo[:, :D_IDX]
    kibf_out[...] = smo[:, :D_IDX].astype(BF16)

    cos_t, sin_t = cost_ref[...], sint_ref[...]
    qg =jnp.concatenate([qngt_ref[...]] * (tm // 128), axis=1)
    zero64 = jnp.zeros((HD_B, tm), BF16)
    for h in range(H_B):
        qh = qt[HD_B * h:HD_B * (h + 1)]
        qh = qh * lax.rsqrt(jnp.mean(qh * qh, axis=0, keepdims=True) + EPS) * qg
        qh = (_rope_cols(qh, cos_t, sin_t) * (HD_B ** -0.5 * LOG2_E)).astype(BF16)
        n = h // (H_B // N_KV_B)
        qtp_out[0, h, HD_B * n:HD_B * (n + 1), :] = qh
        qtp_out[0, h, HD_B * (1 - n):HD_B * (2 - n), :] = zero64
        qit_out[0, h] = (_rope_cols(qit[D_IDX * h:D_IDX * (h + 1)], cos_t, sin_t) * (D_IDX ** -0.5)).astype(BF16)
    vtb = vt.astype(BF16)
    for j in range(tm // kc):
        vt_out[0, j] = vtb[:, j * kc:(j + 1) * kc]
    sm8 = smt[64:72]
    row = lax.broadcasted_iota(I32, sm8.shape, 0)
    g8 = -jnp.exp(alogt_ref[...]) * _softplus(sm8 + dtbt_ref[...])
    gt_out[0] = jnp.where(row < 4, jax.nn.sigmoid(sm8), g8)
    wit_out[0] = smt[72:80] * (H_IDX ** -0.5)


def _proj_prompt(x, w, tabs, *, tm, kc):
    b, t, d = x.shape
    nt = t // tm
    m = b * t
    row = lambda n: pl.BlockSpec((tm, n), lambda i, j: (i * nt + j, 0))
    ins = [
        pl.BlockSpec((1, tm, d), lambda i, j: (i, j, 0)),
        _const_spec((1, d)),
        _const_spec((d, CONV_CH)), _const_spec((d, 512)), _const_spec((d, 256)), _const_spec((d, 128)),
        _const_spec((512, d)), _const_spec((512, d)), _const_spec((128, d)), _const_spec((128, d)),
        _const_spec((CONV_W, CONV_CH)), _const_spec((1, 128)), _const_spec((HD_B, 128)),
        _const_spec((1, 128)), _const_spec((1, 128)), _const_spec((8, 1)), _const_spec((8, 1)),
        _const_spec((128, 128)),
        pl.BlockSpec((tm, 128), lambda i, j: (j, 0)), pl.BlockSpec((tm, 128), lambda i, j: (j, 0)),
        pl.BlockSpec((32, tm), lambda i, j: (0, j)), pl.BlockSpec((32, tm), lambda i, j: (0, j)),
    ]
    nkc = t // kc
    outs = [
        (jax.ShapeDtypeStruct((m, 512), F32), row(512)),
        (jax.ShapeDtypeStruct((m, 512), F32), row(512)),
        (jax.ShapeDtypeStruct((m, 512), F32), row(512)),
        (jax.ShapeDtypeStruct((m, 512), F32), row(512)),
        (jax.ShapeDtypeStruct((m, 256), F32), row(256)),
        (jax.ShapeDtypeStruct((m, 128), BF16), row(128)),
        (jax.ShapeDtypeStruct((m, D_IDX), BF16), row(D_IDX)),
        (jax.ShapeDtypeStruct((m, 128), F32), row(128)),
        (jax.ShapeDtypeStruct((m, D_IDX), F32), row(D_IDX)),
        (jax.ShapeDtypeStruct((b, H_B, 128, t), BF16),
         pl.BlockSpec((1, H_B, 128, tm), lambda i, j: (i, 0, 0, j))),
        (jax.ShapeDtypeStruct((b, H_IDX, D_IDX, t), BF16),
         pl.BlockSpec((1, H_IDX, D_IDX, tm), lambda i, j: (i, 0, 0, j))),
        (jax.ShapeDtypeStruct((b, nkc, 128, kc), BF16),
         pl.BlockSpec((1, tm // kc, 128, kc), lambda i, j: (i, j, 0, 0))),
        (jax.ShapeDtypeStruct((b, 8, t), F32), pl.BlockSpec((1, 8, tm), lambda i, j: (i, 0, j))),
        (jax.ShapeDtypeStruct((b, H_IDX, t), F32), pl.BlockSpec((1, H_IDX, tm), lambda i, j: (i, 0, j))),
        (jax.ShapeDtypeStruct((b, CONV_W - 1, CONV_CH), F32),
         pl.BlockSpec((1, CONV_W - 1, CONV_CH), lambda i, j: (i, 0, 0))),
    ]
    return pl.pallas_call(
        functools.partial(_proj_prompt_body, tm=tm, kc=kc),
        out_shape=[o[0] for o in outs],
        grid=(b, nt),
        in_specs=ins,
        out_specs=[o[1] for o in outs],
        scratch_shapes=[pltpu.VMEM((tm + 8, CONV_CH), F32)],
        compiler_params=_cparams(2),
        name="proj_prompt",
    )(x, w["g1"], w["wqkv"], w["wz"], w["wkv"], w["wsm"], w["wbqt"], w["wiqt"], w["wvt"], w["wsmt"],
      w["convw"], w["kng"], w["qngt"], w["alog_row"], w["dtb_row"], w["alog_col"], w["dtb_col"],
      w["ones_bd128"], tabs["cos"], tabs["sin"], tabs["cos_t"], tabs["sin_t"])


def _split3(x):
    hi = x.astype(BF16)
    r1 = x - hi.astype(F32)
    mid = r1.astype(BF16)
    lo = (r1 - mid.astype(F32)).astype(BF16)
    return hi, mid, lo


def _gdn_body(q_ref, k_ref, v_ref, z_ref, sm_ref, gt_ref, gg_ref, bdk_ref, bdp_ref, o_ref, s_ref, *, nb):
    c = pl.program_id(1)

    @pl.when(c == 0)
    def _():
        s_ref[...] = jnp.zeros_like(s_ref)

    n, nh = CHUNK, H_A
    w4 = nh * n
    ri4 = lax.broadcasted_iota(I32, (n, w4), 0)
    ci4 = lax.broadcasted_iota(I32, (n, w4), 1) % n
    strict4, incl4 = ri4 > ci4, ri4 >= ci4
    eye4 = jnp.where(ri4 == ci4, 1.0, 0.0)
    triu4 = jnp.where(ri4 <= ci4, 1.0, 0.0).astype(BF16)
    ri = lax.broadcasted_iota(I32, (n, n), 0)
    tril = jnp.where(ri >= lax.broadcasted_iota(I32, (n, n), 1), 1.0, 0.0).astype(BF16)
    blk_row = lax.broadcasted_iota(I32, (1, w4), 1) // n
    low_half = lax.broadcasted_iota(I32, (n, 128), 1) < n
    zeros_pad = jnp.zeros((n, 2 * DV_A), BF16)
    gg = gg_ref[...]
    dot = functools.partial(jnp.dot, preferred_element_type=F32)

    def pad_rows(x, e):
        z = zeros_pad[:, :x.shape[1]]
        return jnp.concatenate([x, z] if e == 0 else [z, x], axis=0)

    seqs = range(nb)
    st = [dict() for _ in seqs]

    for bi in seqs:
        hi, mid, lo = _split3(sm_ref[bi])
        st[bi]["gc_all"] = dot(tril, hi) + dot(tril, mid) + dot(tril, lo)
        hi, mid, lo = _split3(gt_ref[bi, 0])
        st[bi]["grows"] = dot(hi, triu4) + dot(mid, triu4) + dot(lo, triu4)

    for bi in seqs:
        d = st[bi]
        sm, k, grows = sm_ref[bi], k_ref[bi], d["grows"]
        grow = jnp.where(blk_row == 0, grows[4:5], jnp.where(blk_row == 1, grows[5:6],
                         jnp.where(blk_row == 2, grows[6:7], grows[7:8])))
        gcols = [jnp.broadcast_to(d["gc_all"][:, _SM_G + h:_SM_G + h + 1], (n, 128)) for h in range(nh)]
        betas = [jnp.broadcast_to(sm[:, _SM_BETA + h:_SM_BETA + h + 1], (n, 128)) for h in range(nh)]
        d["gc512"] = jnp.concatenate(gcols, axis=1)
        d["beta512"] = jnp.concatenate(betas, axis=1)
        gc256 = jnp.concatenate([jnp.where(low_half, gcols[0], gcols[1]),
                                 jnp.where(low_half, gcols[2], gcols[3])], axis=1)
        d["dec"] = jnp.exp(jnp.minimum(gc256 - grow, 0.0))
        d["kb"] = k * d["beta512"]
        kbd = jnp.concatenate([k.astype(BF16)] * nh, axis=0) * bdk_ref[...]
        d["aq"] = lax.dot_general(jnp.concatenate([d["kb"], q_ref[bi]], axis=0).astype(BF16), kbd,
                                  (((1,), (1,)), ((), ())), preferred_element_type=F32)

    for bi in seqs:
        d = st[bi]
        x = jnp.where(strict4, -d["aq"][:n] * d["dec"], 0.0)
        d["qkm"] = jnp.where(incl4, d["aq"][n:] * d["dec"], 0.0).astype(BF16)
        d["inv"], d["p"] = eye4 + x, x
    for lvl in range(6):
        for bi in seqs:
            d = st[bi]
            p, inv = d["p"], d["inv"]
            wbd = jnp.concatenate([p.astype(BF16)] * nh, axis=0) * bdp_ref[...]
            if lvl == 0:
                d["p"] = dot(p.astype(BF16), wbd)
            elif lvl < 5:
                r = dot(jnp.concatenate([p, inv], axis=0).astype(BF16), wbd)
                d["p"], d["inv"] = r[:n], inv + r[n:]
            else:
                d["inv"] = inv + dot(inv.astype(BF16), wbd)

    heads = [(bi, h) for bi in seqs for h in range(nh)]
    hsl = lambda h: slice(h * DK_A, (h + 1) * DK_A)
    psl = lambda h: slice(128 * (h // 2), 128 * (h // 2 + 1))
    for bi in seqs:
        d = st[bi]
        k, gc512 = k_ref[bi], d["gc512"]
        d["tinv"] = d["inv"].astype(BF16)
        egc = jnp.exp(gc512)
        d["rhs_v"] = (v_ref[bi] * d["beta512"]).astype(BF16)
        d["rhs_k"] = (d["kb"] * egc).astype(BF16)
        g_last = gc512[n - 1:n, :]
        d["q_dec"] = q_ref[bi] * egc
        d["k_dec"] = (k * jnp.exp(g_last - gc512)).astype(BF16)
        d["c_dec"] = jnp.exp(g_last)
    sol, ws, vn = {}, {}, {}
    for bi, h in heads:
        d = st[bi]
        rhs = pad_rows(jnp.concatenate([d["rhs_v"][:, hsl(h)], d["rhs_k"][:, hsl(h)]], axis=1), h % 2)
        sol[bi, h] = dot(d["tinv"][:, psl(h)], rhs)
    for bi, h in heads:
        lhs = jnp.concatenate([sol[bi, h][:, DV_A:], st[bi]["q_dec"][:, hsl(h)]], axis=0)
        ws[bi, h] = dot(lhs.astype(BF16), s_ref[bi, h].astype(BF16))
    for bi, h in heads:
        d = st[bi]
        vn[bi, h] = (sol[bi, h][:, :DV_A] - ws[bi, h][:n]).astype(BF16)
        s_ref[bi, h] = d["c_dec"][:, hsl(h)] * s_ref[bi, h] + lax.dot_general(
            d["k_dec"][:, hsl(h)], vn[bi, h], (((0,), (0,)), ((), ())), preferred_element_type=F32)
    for bi, h in heads:
        o = ws[bi, h][n:] + dot(st[bi]["qkm"][:, psl(h)], pad_rows(vn[bi, h], h % 2))
        o_ref[bi, :, hsl(h)] = (_rms_rows(o, gg) * _silu(z_ref[bi, :, hsl(h)])).astype(BF16)


def _gdn_prompt(q, k, v, z, sm, gt, gg, *, b, t, nb):
    nc = t // CHUNK
    r3 = lambda a, n: a.reshape(b, t, n)
    blk = lambda n: pl.BlockSpec((nb, CHUNK, n), lambda i, j: (i, j, 0))
    w4 = H_A * CHUNK
    bdk = np.kron(np.eye(H_A, dtype=np.float32), np.ones((CHUNK, DK_A), np.float32))
    bdp = np.kron(np.eye(H_A, dtype=np.float32), np.ones((CHUNK, CHUNK), np.float32))
    return pl.pallas_call(
        functools.partial(_gdn_body, nb=nb),
        out_shape=[jax.ShapeDtypeStruct((b, t, 512), BF16),
                   jax.ShapeDtypeStruct((b, H_A, DK_A, DV_A), F32)],
        grid=(b // nb, nc),
        in_specs=[blk(512), blk(512), blk(512), blk(512), blk(128),
                  pl.BlockSpec((nb, 1, 8, CHUNK), lambda i, j: (i, j, 0, 0)),
                  _const_spec((1, DV_A)), _const_spec((w4, H_A * DK_A)), _const_spec((w4, w4))],
        out_specs=[blk(512), pl.BlockSpec((nb, H_A, DK_A, DV_A), lambda i, j: (i, 0, 0, 0))],
        compiler_params=_cparams(2),
        name="gdn_prompt",
    )(r3(q, 512), r3(k, 512), r3(v, 512), r3(z, 512), r3(sm, 128),
      jnp.swapaxes(gt.reshape(b, 8, nc, CHUNK), 1, 2), gg, jnp.asarray(bdk, BF16), jnp.asarray(bdp, BF16))


def _ordinal_to_float(k):
    k = jnp.maximum(k, NEG_INF_KEY)
    return pltpu.bitcast(k ^ ((k >> 31) & 0x7FFFFFFF), F32)


def _kth_largest(count_ge, shape, k_top):
    def bit_step(i, o):
        cand = o + jnp.left_shift(jnp.int32(1), 31 - i)
        return jnp.where(count_ge(_ordinal_to_float(cand)) >= k_top, cand, o)

    o = lax.fori_loop(0, 32, bit_step, jnp.full(shape, INT_MIN, I32))
    return _ordinal_to_float(o), o


def _kth_largest_two_stage(count_ge, count_ge_rounded, shape, k_top):
    def bf16_ordinal_as_f32_ordinal(h):
        return jnp.where(h >= 0, h << 16, (h << 16) | 0xFFFF)

    def coarse_step(i, h):
        cand = h + jnp.left_shift(jnp.int32(1), 15 - i)
        c = _ordinal_to_float(bf16_ordinal_as_f32_ordinal(cand)).astype(BF16)
        return jnp.where(count_ge_rounded(c) >= k_top, cand, h)

    h = lax.fori_loop(0, 16, coarse_step, jnp.full(shape, -(2 ** 15), I32))
    o = bf16_ordinal_as_f32_ordinal(h)
    o = jnp.where(count_ge(_ordinal_to_float(o)) >= k_top, o, bf16_ordinal_as_f32_ordinal(h - 1))

    def fine_step(i, o):
        cand = o + jnp.left_shift(jnp.int32(1), 15 - i)
        return jnp.where(count_ge(_ordinal_to_float(cand)) >= k_top, cand, o)

    o = lax.fori_loop(0, 16, fine_step, o)
    return _ordinal_to_float(o), o


def _topk_threshold(count_ge, count_gt_eq, count_eq_below, shape, k_top, idx_bits):
    t, o = _kth_largest(count_ge, shape, k_top)
    c_gt, c_eq = count_gt_eq(t)
    need = k_top - c_gt
    untied = jnp.logical_or(c_eq == need, o <= NEG_INF_KEY)
    all_untied = jnp.min(jnp.where(untied, 1.0, 0.0)) > 0.5

    def tie_search(_):
        def idx_step(i, x):
            cand = x + jnp.left_shift(jnp.int32(1), idx_bits - 1 - i)
            return jnp.where(count_eq_below(t, cand) < need, cand, x)
        return lax.fori_loop(0, idx_bits, idx_step, jnp.zeros(shape, I32))

    x = lax.cond(all_untied, lambda _: jnp.full(shape, 2 ** 30, I32), tie_search, 0)
    return t, x


_ACC_ROWS = HD_B + 16
_LOGITS_AHEAD = 5


def _dsa_body(qit_ref, wit_ref, qtp_ref, ki_ref, k_ref, vt_ref, o_ref,
              sc_ref, sb_ref, m_ref, acc_ref, *, tq, kc, k_top):
    i = pl.program_id(1)
    nkc = i + 1
    rowi = lax.broadcasted_iota(I32, (kc, tq), 0)
    coli = lax.broadcasted_iota(I32, (kc, tq), 1)
    ng = kc // 32

    wit = wit_ref[0]

    def score_chunks(chunks):
        for c in chunks:
            kic = ki_ref[0, c]
            score = jnp.zeros((kc, tq), F32)
            for h in range(H_IDX):
                s = jnp.dot(kic, qit_ref[0, h], preferred_element_type=F32)
                score = score + jnp.maximum(s, 0.0) * wit[h:h + 1, :]
            causal = rowi + (c - i) * kc <= coli
            score = jnp.where(causal, score, -jnp.inf)
            sc_ref[c] = score
            sb_ref[c] = score.astype(BF16)

    nquads = nkc // 4
    tail = 4 * nquads

    def score_quad(p, carry):
        score_chunks([4 * p, 4 * p + 1, 4 * p + 2, 4 * p + 3])
        return carry

    lax.fori_loop(0, nquads, score_quad, 0)

    @pl.when(nkc - tail >= 2)
    def _():
        score_chunks([tail, tail + 1])

    @pl.when(nkc % 2 == 1)
    def _():
        score_chunks([nkc - 1])

    def over_chunks(add_chunk, acc0):
        def quad(p, a):
            for k in range(4):
                a = add_chunk(4 * p + k, a)
            return a
        acc = lax.fori_loop(0, nquads, quad, acc0)
        acc = lax.cond(nkc - tail >= 2, lambda a: add_chunk(tail + 1, add_chunk(tail, a)), lambda a: a, acc)
        return lax.cond(nkc % 2 == 1, lambda a: add_chunk(nkc - 1, a), lambda a: a, acc)

    def key_sum(f):
        def add_chunk(c, acc):
            ind = f(sc_ref[c], c * kc)
            return acc + jnp.sum(ind.reshape(ng, 32, tq), axis=0)
        acc = over_chunks(add_chunk, jnp.zeros((32, tq), F32))
        return jnp.sum(acc, axis=0, keepdims=True).astype(I32)

    def count_ge_rounded(cand):
        one, zero = jnp.ones((), BF16), jnp.zeros((), BF16)

        def add_chunk(c, acc):
            ind = jnp.where(sb_ref[c] >= cand, one, zero)
            for g in range(ng):
                acc = acc + ind[32 * g:32 * (g + 1)]
            return acc
        acc = over_chunks(add_chunk, jnp.zeros((32, tq), BF16))
        return jnp.sum(acc.astype(F32), axis=0, keepdims=True).astype(I32)

    t, _ = _kth_largest_two_stage(lambda cand: key_sum(lambda s, off: jnp.where(s >= cand, 1.0, 0.0)),
                                  count_ge_rounded, (1, tq), k_top)
    ties_wanted = (k_top - key_sum(lambda s, off: jnp.where(s > t, 1.0, 0.0))).astype(F32)

    m_ref[...] = jnp.full(m_ref.shape, NEG_BIG, F32)
    acc_ref[...] = jnp.zeros(acc_ref.shape, F32)
    ones_rows = jnp.ones((_ACC_ROWS - HD_B, kc), BF16)
    prefix = jnp.where(lax.broadcasted_iota(I32, (kc, kc), 0) >= lax.broadcasted_iota(I32, (kc, kc), 1),
                       1.0, 0.0).astype(BF16)

    def attn_chunks(chunks, ties):
        kks, biases, vaugs = [], [], []
        for c in chunks:
            sc = sc_ref[c]
            tied = jnp.where(sc == t, 1.0, 0.0).astype(BF16)
            rank = ties + jnp.dot(prefix, tied, preferred_element_type=F32)
            tie = jnp.where(sc == t, jnp.where(rank <= ties_wanted, 0.0, NEG_BIG), NEG_BIG)
            biases.append(jnp.where(sc > -jnp.inf, jnp.where(sc > t, 0.0, tie), NEG_BIG))
            ties = rank[kc - 1:kc, :]
            kks.append(k_ref[0, c])
            vt = vt_ref[0, c]
            vaugs.append([jnp.concatenate([vt[HD_B * n:HD_B * (n + 1)], ones_rows], axis=0)
                          for n in range(N_KV_B)])
        units = [(j, h) for j in range(len(chunks)) for h in range(H_B)]
        logits = {}

        def issue(u):
            j, h = units[u]
            logits[u] = jnp.dot(kks[j], qtp_ref[0, h], preferred_element_type=F32)

        for u in range(_LOGITS_AHEAD):
            issue(u)
        for u, (j, h) in enumerate(units):
            if u + _LOGITS_AHEAD < len(units):
                issue(u + _LOGITS_AHEAD)
            n = h // (H_B // N_KV_B)
            s = biases[j] + logits.pop(u)
            m_old = m_ref[h:h + 1, :]
            m_new = jnp.maximum(m_old, jnp.max(s, axis=0, keepdims=True))
            alpha = jnp.exp2(m_old - m_new)
            pexp = jnp.exp2(s - m_new).astype(BF16)
            acc_ref[h] = acc_ref[h] * alpha + jnp.dot(vaugs[j][n], pexp, preferred_element_type=F32)
            m_ref[h:h + 1, :] = m_new
        return ties

    ties = lax.fori_loop(0, nquads, lambda p, tb: attn_chunks([4 * p, 4 * p + 1, 4 * p + 2, 4 * p + 3], tb),
                         jnp.zeros((1, tq), F32))
    ties = lax.cond(nkc - tail >= 2, lambda tb: attn_chunks([tail, tail + 1], tb), lambda tb: tb, ties)

    @pl.when(nkc % 2 == 1)
    def _():
        attn_chunks([nkc - 1], ties)

    for p in range(H_B // 2):
        halves = []
        for h in (2 * p, 2 * p + 1):
            acc = acc_ref[h]
            halves.append(acc[:HD_B] / acc[HD_B:HD_B + 1])
        o_ref[:, 128 * p:128 * (p + 1)] = jnp.concatenate(halves, axis=0).T.astype(BF16)


def _dsa_prompt(qit, wit, qtp, kibf, kbf, vt, *, b, t, tq, k_top):
    kc = tq
    nq = t // tq
    nkc = t // kc
    whole = lambda *s: pl.BlockSpec((1,) + s, lambda i, j: (i,) + (0,) * len(s))
    return pl.pallas_call(
        functools.partial(_dsa_body, tq=tq, kc=kc, k_top=k_top),
        out_shape=jax.ShapeDtypeStruct((b * t, H_B * HD_B), BF16),
        grid=(b, nq),
        in_specs=[pl.BlockSpec((1, H_IDX, D_IDX, tq), lambda i, j: (i, 0, 0, j)),
                  pl.BlockSpec((1, H_IDX, tq), lambda i, j: (i, 0, j)),
                  pl.BlockSpec((1, H_B, 128, tq), lambda i, j: (i, 0, 0, j)),
                  whole(nkc, kc, D_IDX), whole(nkc, kc, 128), whole(nkc, 128, kc)],
        out_specs=pl.BlockSpec((tq, H_B * HD_B), lambda i, j: (i * nq + j, 0)),
        scratch_shapes=[pltpu.VMEM((nkc, kc, tq), F32), pltpu.VMEM((nkc, kc, tq), BF16),
                        pltpu.VMEM((H_B, tq), F32), pltpu.VMEM((H_B, _ACC_ROWS, tq), F32)],
        compiler_params=_cparams(2),
        name="dsa_prompt",
    )(qit, wit, qtp, kibf.reshape(b, nkc, kc, D_IDX), kbf.reshape(b, nkc, kc, 128), vt)


def _merge_body(x_ref, oa_ref, ob_ref, g1_ref, wg_ref, woa_ref, wob_ref, wo_ref, h_ref):
    x = x_ref[...]
    xn = _rms_rows(x, g1_ref[...]).astype(BF16)
    gates = jax.nn.sigmoid(jnp.dot(xn, wg_ref[...], preferred_element_type=F32))
    a = jnp.dot(oa_ref[...], woa_ref[...], preferred_element_type=F32)
    bb = jnp.dot(ob_ref[...], wob_ref[...], preferred_element_type=F32)
    mix = gates[:, :D_MODEL] * a + gates[:, D_MODEL:] * bb
    h_ref[...] = x + jnp.dot(mix.astype(BF16), wo_ref[...], preferred_element_type=F32)


def _merge(x, oa, ob, wob, w, *, tm):
    m, d = x.shape
    nb = ob.shape[1]
    row = lambda n: pl.BlockSpec((tm, n), lambda i: (i, 0))
    return pl.pallas_call(
        _merge_body,
        out_shape=jax.ShapeDtypeStruct((m, d), F32),
        grid=(m // tm,),
        in_specs=[row(d), row(512), row(nb), _const_spec((1, d)), _const_spec((d, 2 * d)),
                  _const_spec((512, d)), _const_spec((nb, d)), _const_spec((d, d))],
        out_specs=row(d),
        compiler_params=_cparams(1),
        name="merge",
    )(x, oa, ob, w["g1"], w["wg"], w["woa"], wob, w["wo"])


def _mlp_body(h_ref, g2_ref, wup_ref, wdn_ref, y_ref):
    h = h_ref[...]
    hn = _rms_rows(h, g2_ref[...]).astype(BF16)
    up = jnp.dot(hn, wup_ref[...], preferred_element_type=F32)
    act = jnp.square(jnp.maximum(up, 0.0)).astype(BF16)
    y_ref[...] = h + jnp.dot(act, wdn_ref[...], preferred_element_type=F32)


def _mlp(h, w, *, tm):
    m, d = h.shape
    row = pl.BlockSpec((tm, d), lambda i: (i, 0))
    return pl.pallas_call(
        _mlp_body,
        out_shape=jax.ShapeDtypeStruct((m, d), F32),
        grid=(m // tm,),
        in_specs=[row, _const_spec((1, d)), _const_spec((d, D_FF)), _const_spec((D_FF, d))],
        out_specs=row,
        compiler_params=_cparams(1),
        name="mlp",
    )(h, w["g2"], w["wup"], w["wdn"])


def _proj_decode_body(x_ref, st_ref, g1_ref, wqkv_ref, wz_ref, wbq_ref, wkv_ref, wiq_ref, wsm_ref,
                      convw_ref, qng_ref, kng_ref, alog_ref, dtb_ref, ones_ref, seg_ref, cos_ref, sin_ref,
                      u_out, q_out, k_out, v_out, z_out, qb_out, kv_out, qi_out, sm_out, ss_out):
    x = x_ref[...]
    xn = _rms_rows(x, g1_ref[...]).astype(BF16)
    u = jnp.dot(xn, wqkv_ref[...], preferred_element_type=F32)
    u_out[...] = u
    cw = convw_ref[...]
    acc = st_ref[0] * cw[0:1] + st_ref[1] * cw[1:2] + st_ref[2] * cw[2:3] + u * cw[3:4]
    c = _silu(acc)
    for h in range(H_A):
        qh = c[:, h * DK_A:(h + 1) * DK_A]
        kh = c[:, (H_A + h) * DK_A:(H_A + h + 1) * DK_A]
        q_out[:, h * DK_A:(h + 1) * DK_A] = qh * (lax.rsqrt(jnp.sum(qh * qh, -1, keepdims=True) + EPS)
                                                 * (DK_A ** -0.5))
        k_out[:, h * DK_A:(h + 1) * DK_A] = kh * lax.rsqrt(jnp.sum(kh * kh, -1, keepdims=True) + EPS)
    v_out[...] = c[:, 2 * H_A * DK_A:]
    z_out[...] = jnp.dot(xn, wz_ref[...], preferred_element_type=F32)
    cos, sin = cos_ref[...], sin_ref[...]
    ones_bd = ones_ref[...]
    bq = jnp.dot(xn, wbq_ref[...], preferred_element_type=F32)
    qb = _rope_rows(_seg_rms(bq, ones_bd, qng_ref[...]), cos, sin)
    qb_out[...] = (qb * (HD_B ** -0.5)).astype(BF16)
    kv = jnp.dot(xn, wkv_ref[...], preferred_element_type=F32)
    kb = _rope_rows(_seg_rms(kv[:, :128], ones_bd[:128, :128], kng_ref[...]), cos, sin)
    kv_out[:, 0:128] = kb
    kv_out[:, 128:256] = kv[:, 128:]
    iq = jnp.dot(xn, wiq_ref[...], preferred_element_type=F32)
    qi = (_rope_rows(iq, cos, sin) * (D_IDX ** -0.5)).astype(BF16)
    qi_out[...] = qi
    sm = jnp.dot(xn, wsm_ref[...], preferred_element_type=F32)
    smo = _small_block(sm, cos, sin, alog_ref[...], dtb_ref[...])
    sm_out[...] = smo
    lane = lax.broadcasted_iota(I32, smo.shape, 1)
    ki = smo.astype(BF16).astype(F32)
    ki2 = jnp.where(lane < D_IDX, ki, pltpu.roll(ki, D_IDX, 1))
    prod = qi.astype(F32) * jnp.concatenate([ki2] * 4, axis=-1)
    sh = _mm_f32(prod, seg_ref[...])
    contrib = jnp.where(jnp.logical_and(lane >= _SM_WI, lane < _SM_WI + H_IDX),
                        jnp.maximum(sh, 0.0) * smo, 0.0)
    ss_out[...] = jnp.broadcast_to(jnp.sum(contrib, axis=-1, keepdims=True), ss_out.shape)


def _proj_decode(x, st, w, tabs):
    r, d = x.shape
    outs = [(r, CONV_CH, F32), (r, 512, F32), (r, 512, F32), (r, 512, F32), (r, 512, F32), (r, 512, BF16),
            (r, 256, F32), (r, 512, BF16), (r, 128, F32), (r, 128, F32)]
    args = (x, st, w["g1"], w["wqkv"], w["wz"], w["wbq"], w["wkv"], w["wiq"], w["wsm"], w["convw"],
            w["qng"], w["kng"], w["alog_row"], w["dtb_row"], w["ones_bd"], w["seg_sum"], tabs["cos"], tabs["sin"])
    return pl.pallas_call(
        _proj_decode_body,
        out_shape=[jax.ShapeDtypeStruct((a, n), dt) for a, n, dt in outs],
        compiler_params=pltpu.CompilerParams(vmem_limit_bytes=VMEM_LIMIT),
        name="proj_decode",
    )(*args)


def _gdn_decode_body(qc_ref, kc_ref, v_ref, z_ref, sm_ref, gg_ref, s_ref, o_ref, so_ref):
    sm = sm_ref[0]
    gg = gg_ref[...]
    for h in range(H_A):
        g = sm[:, _SM_G + h:_SM_G + h + 1]
        beta = sm[:, _SM_BETA + h:_SM_BETA + h + 1]
        kcol, qcol = kc_ref[0, h], qc_ref[0, h]
        v = v_ref[0, h]
        s = s_ref[0, h] * jnp.exp(g)
        vn = beta * (v - jnp.sum(s * kcol, axis=0, keepdims=True))
        s = s + kcol * vn
        so_ref[0, h] = s
        o = jnp.sum(s * qcol, axis=0, keepdims=True)
        o_ref[0, h] = (_rms_rows(o, gg) * _silu(z_ref[0, h])).astype(BF16)


def _gdn_decode(q, k, v, z, sm, gg, state):
    r = q.shape[0]
    col = lambda a: a.reshape(r, H_A, DK_A, 1)
    rowv = lambda a: a.reshape(r, H_A, 1, DV_A)
    cspec = pl.BlockSpec((1, H_A, DK_A, 1), lambda i: (i, 0, 0, 0))
    rspec = pl.BlockSpec((1, H_A, 1, DV_A), lambda i: (i, 0, 0, 0))
    sspec = pl.BlockSpec((1, H_A, DK_A, DV_A), lambda i: (i, 0, 0, 0))
    o, s_new = pl.pallas_call(
        _gdn_decode_body,
        out_shape=[jax.ShapeDtypeStruct((r, H_A, 1, DV_A), BF16),
                   jax.ShapeDtypeStruct((r, H_A, DK_A, DV_A), F32)],
        grid=(r,),
        in_specs=[cspec, cspec, rspec, rspec, pl.BlockSpec((1, 1, 128), lambda i: (i, 0, 0)),
                  _const_spec((1, DV_A)), sspec],
        out_specs=[rspec, sspec],
        compiler_params=_cparams(1),
        name="gdn_decode",
    )(col(q), col(k), rowv(v), rowv(z), sm.reshape(r, 1, 128), gg, state)
    return o.reshape(r, H_A * DV_A), s_new


def _page_copies(pt_ref, pages_hbm, buf, sem, row, first, count, slot):
    return [pltpu.make_async_copy(pages_hbm.at[pt_ref[row, first + p]], buf.at[slot, p], sem.at[slot])
            for p in range(count)]


def _idx_decode_body(pt_ref, qi_ref, w_ref, pages_hbm, out_ref, buf, sem, *, n_pages):
    i = pl.program_id(0)
    slot = i % 2
    copies = functools.partial(_page_copies, pt_ref, pages_hbm, buf, sem)

    @pl.when(i == 0)
    def _():
        for cp in copies(0, 0, n_pages, 0):
            cp.start()

    @pl.when(i + 1 < pl.num_programs(0))
    def _():
        for cp in copies(i + 1, 0, n_pages, 1 - slot):
            cp.start()

    for cp in copies(i, 0, n_pages, slot):
        cp.wait()
    qi = qi_ref[0]
    w = w_ref[0]
    for p in range(n_pages):
        s = _mm(qi, buf[slot, p])
        out_ref[0, p:p + 1, :] = jnp.sum(jnp.maximum(s, 0.0) * w, axis=0, keepdims=True)


def _idx_decode(page_table, qi, wi_col, cache_kidx):
    r, n_pages = page_table.shape
    grid_spec = pltpu.PrefetchScalarGridSpec(
        num_scalar_prefetch=1,
        grid=(r,),
        in_specs=[pl.BlockSpec((1, H_IDX, D_IDX), lambda i, pt: (i, 0, 0)),
                  pl.BlockSpec((1, H_IDX, 1), lambda i, pt: (i, 0, 0)),
                  pl.BlockSpec(memory_space=pl.ANY)],
        out_specs=pl.BlockSpec((1, n_pages, PAGE_SIZE), lambda i, pt: (i, 0, 0)),
        scratch_shapes=[pltpu.VMEM((2, n_pages, D_IDX, PAGE_SIZE), F32), pltpu.SemaphoreType.DMA((2,))],
    )
    return pl.pallas_call(
        functools.partial(_idx_decode_body, n_pages=n_pages),
        out_shape=jax.ShapeDtypeStruct((r, n_pages, PAGE_SIZE), F32),
        grid_spec=grid_spec,
        compiler_params=_cparams(1),
        name="idx_decode",
    )(page_table, qi.reshape(r, H_IDX, D_IDX), wi_col, cache_kidx)


def _select_decode_body(sc_ref, ss_ref, bias_ref, bself_ref, *, k_top, cw):
    r, n = sc_ref.shape
    nch = n // cw
    key_ref = sc_ref
    skey = ss_ref[:, 0:1]

    def lane_sum(f):
        def body(c, acc):
            off = pl.multiple_of(c * cw, cw)
            kk = key_ref[:, pl.ds(off, cw)]
            for j in range(cw // 128):
                acc = acc + f(kk[:, 128 * j:128 * (j + 1)], off + 128 * j)
            return acc
        acc = lax.fori_loop(0, nch, body, jnp.zeros((r, 128), F32))
        return jnp.sum(acc, axis=-1, keepdims=True).astype(I32)

    def count_ge(cand):
        cb = jnp.broadcast_to(cand, (r, 128))
        return lane_sum(lambda kk, off: jnp.where(kk >= cb, 1.0, 0.0)) + (skey >= cand).astype(I32)

    def count_gt_eq(t):
        tb = jnp.broadcast_to(t, (r, 128))
        return (lane_sum(lambda kk, off: jnp.where(kk > tb, 1.0, 0.0)) + (skey > t).astype(I32),
                lane_sum(lambda kk, off: jnp.where(kk == tb, 1.0, 0.0)) + (skey == t).astype(I32))

    def count_eq_below(t, j):
        tb = jnp.broadcast_to(t, (r, 128))
        jb = jnp.broadcast_to(j, (r, 128))
        lane = lax.broadcasted_iota(I32, (r, 128), 1)
        return (lane_sum(lambda kk, off: jnp.where(kk == tb, jnp.where(lane + off < jb, 1.0, 0.0), 0.0))
                + jnp.logical_and(skey == t, j > n).astype(I32))

    t, x = _topk_threshold(count_ge, count_gt_eq, count_eq_below, (r, 1), k_top, 15)
    tb = jnp.broadcast_to(t, (r, cw))
    xb = jnp.broadcast_to(x, (r, cw))

    def bias_chunk(c, carry):
        off = pl.multiple_of(c * cw, cw)
        kk = key_ref[:, pl.ds(off, cw)]
        idx = lax.broadcasted_iota(I32, (r, cw), 1) + off
        tie = jnp.where(kk == tb, jnp.where(idx <= xb, 0.0, NEG_BIG), NEG_BIG)
        bias_ref[:, pl.ds(off, cw)] = jnp.where(kk > tb, 0.0, tie)
        return carry

    lax.fori_loop(0, nch, bias_chunk, 0)
    tie_self = jnp.where(skey == t, jnp.where(x >= n, 0.0, NEG_BIG), NEG_BIG)
    bself_ref[...] = jnp.broadcast_to(jnp.where(skey > t, 0.0, tie_self), bself_ref.shape)


def _select_decode(scores, sself, *, k_top):
    r, n = scores.shape
    return pl.pallas_call(
        functools.partial(_select_decode_body, k_top=k_top, cw=512),
        out_shape=[jax.ShapeDtypeStruct((r, n), F32), jax.ShapeDtypeStruct((r, 128), F32)],
        compiler_params=pltpu.CompilerParams(vmem_limit_bytes=VMEM_LIMIT),
        name="select_decode",
    )(scores, sself)


def _attn_decode_body(pt_ref, qp_ref, bias_ref, kvs_ref, bself_ref, pages_hbm, o_ref,
                      m_ref, l_ref, acc_ref, buf, sem, *, npg, nsteps):
    i, j = pl.program_id(0), pl.program_id(1)
    step = i * nsteps + j
    slot = step % 2
    copies = functools.partial(_page_copies, pt_ref, pages_hbm, buf, sem)

    @pl.when(step == 0)
    def _():
        for cp in copies(0, 0, npg, 0):
            cp.start()

    @pl.when(step + 1 < pl.num_programs(0) * nsteps)
    def _():
        nxt = step + 1
        for cp in copies(nxt // nsteps, (nxt % nsteps) * npg, npg, 1 - slot):
            cp.start()

    @pl.when(j == 0)
    def _():
        m_ref[...] = jnp.full(m_ref.shape, NEG_BIG, F32)
        l_ref[...] = jnp.zeros(l_ref.shape, F32)
        acc_ref[...] = jnp.zeros(acc_ref.shape, F32)

    for cp in copies(i, j * npg, npg, slot):
        cp.wait()
    qp = qp_ref[0]

    ss = [_mm(qp, buf[slot, p, 0:128, :]) + bias_ref[0, :, PAGE_SIZE * p:PAGE_SIZE * (p + 1)]
          for p in range(npg)]
    m_old = m_ref[...]
    m_new = m_old
    for s in ss:
        m_new = jnp.maximum(m_new, jnp.max(s, axis=-1, keepdims=True))
    alpha = jnp.exp(m_old - m_new)
    l_new = l_ref[...] * alpha
    acc_new = acc_ref[...] * alpha
    for p in range(npg):
        pe = jnp.exp(ss[p] - m_new)
        l_new = l_new + jnp.sum(pe, axis=-1, keepdims=True)
        acc_new = acc_new + _mm_nt(pe, buf[slot, p, 128:256, :])
    m_ref[...] = m_new
    l_ref[...] = l_new
    acc_ref[...] = acc_new

    @pl.when(j == nsteps - 1)
    def _():
        kvs = kvs_ref[0]
        krow = kvs[:, :128].astype(BF16).astype(F32)
        s = jnp.sum(qp.astype(F32) * krow, axis=-1, keepdims=True) + bself_ref[0][:, 0:1]
        m_old = m_ref[...]
        m_new = jnp.maximum(m_old, s)
        alpha = jnp.exp(m_old - m_new)
        p = jnp.exp(s - m_new)
        l = l_ref[...] * alpha + p
        vrow = kvs[:, 128:].astype(BF16).astype(F32)
        acc = acc_ref[...] * alpha + p.astype(BF16).astype(F32) * vrow
        out = acc / l
        rowi = lax.broadcasted_iota(I32, out.shape, 0)
        out = jnp.where(rowi < 4, out, pltpu.roll(out, HD_B, 1))
        o_ref[0] = out.astype(BF16)


def _attn_decode(page_table, qpad, bias, kv_self, bself, cache_kv, *, npg):
    r, n_pages = page_table.shape
    nsteps = n_pages // npg
    page_rows = 2 * N_KV_B * HD_B
    grid_spec = pltpu.PrefetchScalarGridSpec(
        num_scalar_prefetch=1,
        grid=(r, nsteps),
        in_specs=[pl.BlockSpec((1, H_B, 128), lambda i, j, pt: (i, 0, 0)),
                  pl.BlockSpec((1, 1, PAGE_SIZE * npg), lambda i, j, pt: (i, 0, j)),
                  pl.BlockSpec((1, 1, 256), lambda i, j, pt: (i, 0, 0)),
                  pl.BlockSpec((1, 1, 128), lambda i, j, pt: (i, 0, 0)),
                  pl.BlockSpec(memory_space=pl.ANY)],
        out_specs=pl.BlockSpec((1, H_B, 128), lambda i, j, pt: (i, 0, 0)),
        scratch_shapes=[pltpu.VMEM((H_B, 1), F32), pltpu.VMEM((H_B, 1), F32), pltpu.VMEM((H_B, 128), F32),
                        pltpu.VMEM((2, npg, page_rows, PAGE_SIZE), F32), pltpu.SemaphoreType.DMA((2,))],
    )
    return pl.pallas_call(
        functools.partial(_attn_decode_body, npg=npg, nsteps=nsteps),
        out_shape=jax.ShapeDtypeStruct((r, H_B, 128), BF16),
        grid_spec=grid_spec,
        compiler_params=_cparams(2),
        name="attn_decode",
    )(page_table, qpad, bias.reshape(r, 1, -1), kv_self.reshape(r, 1, 256), bself.reshape(r, 1, 128), cache_kv)


def _prep_weights(norm1_g, w_in, conv_w, a_log, dt_bias, gdn_norm_g, q_norm_g, k_norm_g,
                  w_out_a, w_out_b, w_o, norm2_g, w_up, w_down):
    o = _OFF
    wb = w_in.astype(BF16)
    d = w_in.shape[0]
    wsm = jnp.concatenate([wb[:, o["ik"]:o["iw"]], wb[:, o["ab"]:o["bq"]], wb[:, o["iw"]:o["ga"]],
                           jnp.zeros((d, 128 - 80), BF16)], axis=1)

    def lanes(vals, start):
        return jnp.zeros((1, 128), F32).at[0, start:start + vals.shape[0]].set(vals.astype(F32))

    seg = np.zeros((512, 128), np.float32)
    for h in range(H_IDX):
        seg[64 * h:64 * (h + 1), _SM_WI + h] = 1.0
    ones_bd = np.kron(np.eye(8, dtype=np.float32), np.ones((64, 64), np.float32))
    wob = w_out_b.astype(BF16).reshape(H_B, HD_B, d)
    wob_pad = jnp.concatenate([wob, jnp.zeros_like(wob)], axis=1).reshape(H_B * 128, d)
    col8 = lambda v: jnp.concatenate([jnp.zeros((4,), F32), v.astype(F32)]).reshape(8, 1)
    return dict(
        g1=norm1_g.reshape(1, d).astype(F32),
        wqkv=wb[:, o["aq"]:o["az"]], wz=wb[:, o["az"]:o["ab"]], wbq=wb[:, o["bq"]:o["bk"]],
        wkv=wb[:, o["bk"]:o["iq"]], wiq=wb[:, o["iq"]:o["ik"]], wsm=wsm,
        wbqt=wb[:, o["bq"]:o["bk"]].T, wiqt=wb[:, o["iq"]:o["ik"]].T, wvt=wb[:, o["bv"]:o["iq"]].T,
        wsmt=wsm.T, wg=wb[:, o["ga"]:o["end"]],
        convw=conv_w.astype(F32),
        qng=jnp.tile(q_norm_g.astype(F32), H_B).reshape(1, 512),
        kng=jnp.tile(k_norm_g.astype(F32), N_KV_B).reshape(1, 128),
        qngt=jnp.tile(q_norm_g.astype(F32).reshape(HD_B, 1), (1, 128)),
        alog_row=lanes(a_log, _SM_G), dtb_row=lanes(dt_bias, _SM_G),
        alog_col=col8(a_log), dtb_col=col8(dt_bias),
        ones_bd=jnp.asarray(ones_bd, BF16), ones_bd128=jnp.asarray(ones_bd[:128, :128], BF16),
        seg_sum=jnp.asarray(seg),
        gg=gdn_norm_g.reshape(1, DV_A).astype(F32),
        woa=w_out_a.astype(BF16), wob=w_out_b.astype(BF16), wob_pad=wob_pad, wo=w_o.astype(BF16),
        g2=norm2_g.reshape(1, d).astype(F32), wup=w_up.astype(BF16), wdn=w_down.astype(BF16),
    )


def _rope_tables(pos):
    half = HD_B // 2
    inv = ROPE_THETA ** (-jnp.arange(half, dtype=F32) / half)
    ang = pos.astype(F32)[:, None] * inv[None, :]
    cos, sin = jnp.cos(ang), jnp.sin(ang)
    return dict(cos=jnp.tile(cos, (1, 4)), sin=jnp.tile(jnp.concatenate([-sin, sin], axis=1), (1, 2)),
                cos_t=cos.T, sin_t=sin.T)


def _layer_prompt(x, w, *, tm=512, tq=256, nb=8):
    b, t, d = x.shape
    tabs = _rope_tables(jnp.arange(t))
    (q, k, v, z, kv, kbf, kibf, sm, kidx, qtp, qit, vt, gt, wit, conv_new) = _proj_prompt(x, w, tabs, tm=tm, kc=tq)
    oa, ssm = _gdn_prompt(q, k, v, z, sm, gt, w["gg"], b=b, t=t, nb=nb)
    ob = _dsa_prompt(qit, wit, qtp, kibf, kbf, vt, b=b, t=t, tq=tq, k_top=min(TOPK_MAX, t // 4))
    h = _merge(x.reshape(b * t, d), oa.reshape(b * t, 512), ob, w["wob"], w, tm=tm)
    y = _mlp(h, w, tm=tm)
    return (y.reshape(b, t, d), kv.reshape(b, t, 2, N_KV_B, HD_B), kidx.reshape(b, t, D_IDX), ssm, conv_new)


def _layer_decode(x, w, state_ssm, state_conv, cache_kv, cache_kidx, page_table, *, npg=32):
    r, d = x.shape
    n_pages = page_table.shape[1]
    past = n_pages * PAGE_SIZE
    tabs = _rope_tables(jnp.full((1,), past))
    st = jnp.swapaxes(state_conv, 0, 1)
    (u, q, k, v, z, qb, kv, qi, sm, sself) = _proj_decode(x, st, w, tabs)
    oa, ssm = _gdn_decode(q, k, v, z, sm, w["gg"], state_ssm)
    wi_col = sm[:, _SM_WI:_SM_WI + H_IDX].reshape(r, H_IDX, 1)
    kidx_pages = jnp.swapaxes(cache_kidx, 1, 2)
    kv_pages = jnp.transpose(cache_kv, (0, 2, 3, 4, 1)).reshape(-1, 2 * N_KV_B * HD_B, PAGE_SIZE)
    scores = _idx_decode(page_table, qi, wi_col, kidx_pages).reshape(r, past)
    bias, bself = _select_decode(scores, sself, k_top=min(TOPK_MAX, (past + 1) // 4))
    qh = qb.reshape(r, H_B, HD_B)
    zq = jnp.zeros_like(qh[:, :4])
    qpad = jnp.concatenate([jnp.concatenate([qh[:, :4], zq], axis=-1),
                            jnp.concatenate([zq, qh[:, 4:]], axis=-1)], axis=1)
    ob = _attn_decode(page_table, qpad, bias, kv, bself, kv_pages, npg=npg)
    h = _merge(x, oa, ob.reshape(r, H_B * 128), w["wob_pad"], w, tm=r)
    y = _mlp(h, w, tm=r)
    conv_new = jnp.concatenate([state_conv[:, 1:], u[:, None, :]], axis=1)
    return (y, kv.reshape(r, 1, 2, N_KV_B, HD_B), sm[:, :D_IDX].reshape(r, 1, D_IDX), ssm, conv_new)


def kernel(x_prompt, x_sample, cache_kv, cache_kidx, page_table, state_ssm, state_conv, norm1_g, w_in, conv_w,
           a_log, dt_bias, gdn_norm_g, q_norm_g, k_norm_g, w_out_a, w_out_b, w_o, norm2_g, w_up, w_down):
    assert w_in.shape[0] == 1, "single-layer trunk"
    w = _prep_weights(norm1_g[0], w_in[0], conv_w[0], a_log[0], dt_bias[0], gdn_norm_g[0], q_norm_g[0],
                      k_norm_g[0], w_out_a[0], w_out_b[0], w_o[0], norm2_g[0], w_up[0], w_down[0])
    yp, kv_p, ki_p, ssm_p, cv_p = _layer_prompt(x_prompt, w)
    ys, kv_s, ki_s, ssm_s, cv_s = _layer_decode(x_sample[:, 0], w, state_ssm[0], state_conv[0],
                                                cache_kv[0], cache_kidx[0], page_table)
    return (yp, ys[:, None, :], kv_p[None], ki_p[None], ssm_p[None], cv_p[None],
            kv_s[None], ki_s[None], ssm_s[None], cv_s[None])
```
